```python
import math
import jax, jax.numpy as jnp
from jax import lax
import numpy as np

D_MODEL = 1024
BATCH = 2
SEQ = 8192
DEPTH = 4

N_MEM = 256
HEAD_DIM = 64
DSA_HEADS = 6
GDN_HEADS = 6
MEM_HEADS = 4
DSA_W = DSA_HEADS * HEAD_DIM
GDN_W = GDN_HEADS * HEAD_DIM
MEM_W = MEM_HEADS * HEAD_DIM
MIX_W = DSA_W + GDN_W + MEM_W
IDX_HEADS = 8
IDX_DIM = 32
TOPK_MAX = 256
Q_BLOCK = 128
GDN_CHUNK = 64
CONV_WIDTH = 4
NUM_BUCKETS = 32
MAX_DISTANCE = 128
D_FF = 2816
EPS = 1e-6
IN_SPLIT_SIZES = (
    DSA_W, DSA_W, DSA_W,
    IDX_HEADS * IDX_DIM, IDX_DIM, IDX_HEADS,
    GDN_W, GDN_W, GDN_W, GDN_W,
    GDN_HEADS, GDN_HEADS,
    MEM_W,
)
IN_W = sum(IN_SPLIT_SIZES)

kernel_name = 'hybrid_dsa_gdn_memory_macaron'


def rms_norm(x, gain):
    xf = x.astype(jnp.float32)
    y = xf * lax.rsqrt(jnp.mean(xf * xf, axis=-1, keepdims=True) + EPS)
    return (y * gain.astype(jnp.float32)).astype(x.dtype)


def l2_norm(x):
    return x * lax.rsqrt(jnp.sum(x * x, axis=-1, keepdims=True) + EPS)


def swiglu(h, w_gate, w_up, w_down):
    return (jax.nn.silu(h @ w_gate) * (h @ w_up)) @ w_down


def split_columns(p):
    offsets, acc = [], 0
    for s in IN_SPLIT_SIZES[:-1]:
        acc += s
        offsets.append(acc)
    return jnp.split(p, offsets, axis=-1)


def t5_bucket(dist):
    n = jnp.maximum(dist, 0)
    max_exact = NUM_BUCKETS // 2
    nf = jnp.maximum(n, 1).astype(jnp.float32)
    large = max_exact + (jnp.log(nf / max_exact) / math.log(MAX_DISTANCE / max_exact)
                         * (NUM_BUCKETS - max_exact)).astype(jnp.int32)
    large = jnp.minimum(large, NUM_BUCKETS - 1)
    return jnp.where(n < max_exact, n, large)


def dsa_attention(q, k, v, q_idx, k_idx, w_idx, rel_bias):
    B, T, H, dh = q.shape
    top_k = min(TOPK_MAX, T // 4)
    nb = T // Q_BLOCK
    key_pos = jnp.arange(T, dtype=jnp.int32)
    gather = jax.vmap(lambda src, idx: src[idx])

    def blocks(a):
        return jnp.moveaxis(a.reshape(B, nb, Q_BLOCK, *a.shape[2:]), 1, 0)

    def one_block(args):
        blk, qb, qib, wb = args
        q_pos = blk * Q_BLOCK + jnp.arange(Q_BLOCK, dtype=jnp.int32)
        admissible = key_pos[None, :] <= q_pos[:, None]
        head_scores = jax.nn.relu(jnp.einsum('bqhd,bsd->bqhs', qib, k_idx).astype(jnp.float32))
        score = jnp.einsum('bqhs,bqh->bqs', head_scores, wb.astype(jnp.float32))
        score = jnp.where(admissible[None], score, -jnp.inf)
        _, sel = lax.top_k(score, top_k)
        k_sel = gather(k, sel)
        v_sel = gather(v, sel)
        logits = jnp.einsum('bqhd,bqkhd->bhqk', qb, k_sel).astype(jnp.float32) * dh ** -0.5
        dist = q_pos[None, :, None] - sel
        bias = rel_bias.astype(jnp.float32)[t5_bucket(dist)]
        logits = logits + jnp.moveaxis(bias, 3, 1)
        logits = jnp.where((dist >= 0)[:, None], logits, -jnp.inf)
        p = jax.nn.softmax(logits, axis=-1).astype(v.dtype)
        return jnp.einsum('bhqk,bqkhd->bqhd', p, v_sel)

    out = lax.map(one_block, (jnp.arange(nb, dtype=jnp.int32), blocks(q), blocks(q_idx), blocks(w_idx)))
    return jnp.moveaxis(out, 0, 1).reshape(B, T, H, dh)


def causal_depthwise_conv(x, w):
    C = x.shape[-1]
    return lax.conv_general_dilated(
        x, w[:, None, :].astype(x.dtype), window_strides=(1,),
        padding=[(CONV_WIDTH - 1, 0)], dimension_numbers=('NWC', 'WIO', 'NWC'),
        feature_group_count=C)


def gated_delta_rule_chunked(q, k, v, g, beta):
    B, T, H, dk = q.shape
    dv = v.shape[-1]
    C = GDN_CHUNK
    N = T // C

    def to_chunks(a):
        return jnp.moveaxis(a.reshape(B, N, C, H, *a.shape[3:]), 3, 1)

    q, k, v, g, beta = (to_chunks(a) for a in (q, k, v, g, beta))
    g_cum = jnp.cumsum(g, axis=-1)
    causal = jnp.tril(jnp.ones((C, C), dtype=bool))
    strict = jnp.tril(jnp.ones((C, C), dtype=bool), -1)
    decay = jnp.exp(jnp.where(causal, g_cum[..., :, None] - g_cum[..., None, :], -jnp.inf))
    k_beta = k * beta[..., None]
    v_beta = v * beta[..., None]
    lower = jnp.where(strict, jnp.einsum('bhncd,bhnsd->bhncs', k_beta, k) * decay, 0.0)
    t_mat = jnp.eye(C, dtype=jnp.float32) + lower
    u = lax.linalg.triangular_solve(t_mat, v_beta, left_side=True, lower=True, unit_diagonal=True)
    w = lax.linalg.triangular_solve(t_mat, k_beta * jnp.exp(g_cum)[..., None],
                                    left_side=True, lower=True, unit_diagonal=True)
    attn_intra = jnp.where(causal, jnp.einsum('bhncd,bhnsd->bhncs', q, k) * decay, 0.0)

    def step(S, xs):
        q_c, k_c, u_c, w_c, g_c, a_c = xs
        v_new = u_c - jnp.einsum('bhcd,bhde->bhce', w_c, S)
        o = jnp.einsum('bhcd,bhde->bhce', q_c * jnp.exp(g_c)[..., None], S) \
            + jnp.einsum('bhcs,bhse->bhce', a_c, v_new)
        g_last = g_c[..., -1]
        S = S * jnp.exp(g_last)[..., None, None] + jnp.einsum(
            'bhcd,bhce->bhde', k_c * jnp.exp(g_last[..., None] - g_c)[..., None], v_new)
        return S, o

    xs = tuple(jnp.moveaxis(a, 2, 0) for a in (q, k, u, w, g_cum, attn_intra))
    S0 = jnp.zeros((B, H, dk, dv), jnp.float32)
    _, o = lax.scan(step, S0, xs)
    return jnp.transpose(o, (1, 0, 3, 2, 4)).reshape(B, T, H, dv)


def gdn_mixer(q, k, v, z, a, b_in, conv_w, A_log, dt_bias, out_gain):
    B, T, _ = q.shape
    qkv = jax.nn.silu(causal_depthwise_conv(jnp.concatenate([q, k, v], axis=-1), conv_w))
    q, k, v = jnp.split(qkv, 3, axis=-1)
    heads = lambda t: t.reshape(B, T, GDN_HEADS, HEAD_DIM).astype(jnp.float32)
    qh = l2_norm(heads(q)) * HEAD_DIM ** -0.5
    kh = l2_norm(heads(k))
    vh = heads(v)
    beta = jax.nn.sigmoid(b_in.astype(jnp.float32))
    g = -jnp.exp(A_log.astype(jnp.float32)) * jax.nn.softplus(a.astype(jnp.float32) + dt_bias.astype(jnp.float32))
    o = gated_delta_rule_chunked(qh, kh, vh, g, beta)
    o = rms_norm(o, out_gain) * jax.nn.silu(heads(z))
    return o.reshape(B, T, GDN_W).astype(z.dtype)


def memory_attention(q, mem_kv, q_gain, k_gain):
    B, T, _ = q.shape
    M = mem_kv.shape[1]
    qh = rms_norm(q.reshape(B, T, MEM_HEADS, HEAD_DIM), q_gain)
    km, vm = jnp.split(mem_kv, 2, axis=-1)
    kh = rms_norm(km.reshape(B, M, MEM_HEADS, HEAD_DIM), k_gain)
    vh = vm.reshape(B, M, MEM_HEADS, HEAD_DIM)
    logits = jnp.einsum('bthd,bmhd->bhtm', qh, kh).astype(jnp.float32) * HEAD_DIM ** -0.5
    p = jax.nn.softmax(logits, axis=-1).astype(vh.dtype)
    return jnp.einsum('bhtm,bmhd->bthd', p, vh).reshape(B, T, MEM_W)


def setup_inputs(seed: int = 0) -> dict:
    key = jax.random.key(seed)
    ks = jax.random.split(key, 24)
    nrm = lambda k, shape, scale: jax.random.normal(k, shape, jnp.float32) * scale
    gain = lambda k, shape: 1.0 + 0.05 * jax.random.normal(k, shape, jnp.float32)
    return {
        'x': nrm(ks[0], (BATCH, SEQ, D_MODEL), 1.0),
        'mem': nrm(ks[1], (BATCH, N_MEM, D_MODEL), 1.0),
        'ffn1_norm': gain(ks[2], (DEPTH, D_MODEL)),
        'ffn1_w_gate': nrm(ks[3], (DEPTH, D_MODEL, D_FF), D_MODEL ** -0.5),
        'ffn1_w_up': nrm(ks[4], (DEPTH, D_MODEL, D_FF), D_MODEL ** -0.5),
        'ffn1_w_down': nrm(ks[5], (DEPTH, D_FF, D_MODEL), D_FF ** -0.5),
        'mix_norm': gain(ks[6], (DEPTH, D_MODEL)),
        'w_in': nrm(ks[7], (DEPTH, D_MODEL, IN_W), D_MODEL ** -0.5),
        'dsa_q_norm': gain(ks[8], (DEPTH, HEAD_DIM)),
        'dsa_k_norm': gain(ks[9], (DEPTH, HEAD_DIM)),
        'rel_bias': nrm(ks[10], (NUM_BUCKETS, DSA_HEADS), 0.5),
        'gdn_conv': nrm(ks[11], (DEPTH, CONV_WIDTH, 3 * GDN_W), CONV_WIDTH ** -0.5),
        'gdn_A_log': jnp.log(jax.random.uniform(ks[12], (DEPTH, GDN_HEADS), jnp.float32, 1.0, 16.0)),
        'gdn_dt_bias': nrm(ks[13], (DEPTH, GDN_HEADS), 0.1),
        'gdn_out_norm': gain(ks[14], (DEPTH, HEAD_DIM)),
        'mem_norm': gain(ks[15], (DEPTH, D_MODEL)),
        'w_mem_kv': nrm(ks[16], (DEPTH, D_MODEL, 2 * MEM_W), D_MODEL ** -0.5),
        'mem_q_norm': gain(ks[17], (DEPTH, HEAD_DIM)),
        'mem_k_norm': gain(ks[18], (DEPTH, HEAD_DIM)),
        'w_out': nrm(ks[19], (DEPTH, MIX_W, D_MODEL), MIX_W ** -0.5),
        'ffn2_norm': gain(ks[20], (DEPTH, D_MODEL)),
        'ffn2_w_gate': nrm(ks[21], (DEPTH, D_MODEL, D_FF), D_MODEL ** -0.5),
        'ffn2_w_up': nrm(ks[22], (DEPTH, D_MODEL, D_FF), D_MODEL ** -0.5),
        'ffn2_w_down': nrm(ks[23], (DEPTH, D_FF, D_MODEL), D_FF ** -0.5),
    }


def reference(x, mem, ffn1_norm, ffn1_w_gate, ffn1_w_up, ffn1_w_down, mix_norm, w_in,
              dsa_q_norm, dsa_k_norm, rel_bias, gdn_conv, gdn_A_log, gdn_dt_bias,
              gdn_out_norm, mem_norm, w_mem_kv, mem_q_norm, mem_k_norm, w_out,
              ffn2_norm, ffn2_w_gate, ffn2_w_up, ffn2_w_down):
    B, T, _ = x.shape
    for l in range(DEPTH):
        x = x + 0.5 * swiglu(rms_norm(x, ffn1_norm[l]), ffn1_w_gate[l], ffn1_w_up[l], ffn1_w_down[l])
        h = rms_norm(x, mix_norm[l])
        (dq, dk, dv, iq, ik, iw, gq, gk, gv, gz, ga, gb, mq) = split_columns(h @ w_in[l])
        qa = rms_norm(dq.reshape(B, T, DSA_HEADS, HEAD_DIM), dsa_q_norm[l])
        ka = rms_norm(dk.reshape(B, T, DSA_HEADS, HEAD_DIM), dsa_k_norm[l])
        va = dv.reshape(B, T, DSA_HEADS, HEAD_DIM)
        q_idx = iq.reshape(B, T, IDX_HEADS, IDX_DIM) * IDX_DIM ** -0.5
        w_idx = iw * IDX_HEADS ** -0.5
        out_a = dsa_attention(qa, ka, va, q_idx, ik, w_idx, rel_bias).reshape(B, T, DSA_W)
        out_b = gdn_mixer(gq, gk, gv, gz, ga, gb, gdn_conv[l], gdn_A_log[l], gdn_dt_bias[l], gdn_out_norm[l])
        mem_kv = rms_norm(mem, mem_norm[l]) @ w_mem_kv[l]
        out_c = memory_attention(mq, mem_kv, mem_q_norm[l], mem_k_norm[l])
        x = x + jnp.concatenate([out_a, out_b, out_c], axis=-1) @ w_out[l]
        x = x + 0.5 * swiglu(rms_norm(x, ffn2_norm[l]), ffn2_w_gate[l], ffn2_w_up[l], ffn2_w_down[l])
    return x
```

```python
import functools
import math

import jax
import jax.numpy as jnp
import numpy as np
from jax import lax
from jax.experimental import pallas as pl
from jax.experimental.pallas import tpu as pltpu

F32 = jnp.float32
BF16 = jnp.bfloat16

HEAD_DIM = 64
DSA_HEADS = 6
GDN_HEADS = 6
MEM_HEADS = 4
DSA_W = DSA_HEADS * HEAD_DIM
GDN_W = GDN_HEADS * HEAD_DIM
MEM_W = MEM_HEADS * HEAD_DIM
IDX_HEADS = 8
IDX_DIM = 32
TOPK_MAX = 256
Q_BLOCK = 128
GDN_CHUNK = 64
CONV_WIDTH = 4
NUM_BUCKETS = 32
MAX_DISTANCE = 128
EPS = 1e-6

LANES = 128
VMEM_LIMIT = 52 * 1024 * 1024
NEG = -1e30
HI = lax.Precision.HIGHEST

SEG_A = 0
SEG_IQ = SEG_A + 3 * DSA_W
SEG_KW = SEG_IQ + IDX_HEADS * IDX_DIM
SEG_G = SEG_KW + LANES
SEG_AB = SEG_G + 4 * GDN_W
SEG_MQ = SEG_AB + LANES
IN_PACKED = SEG_MQ + MEM_W


def _hdot(a, b):
    return jnp.dot(a, b, precision=HI, preferred_element_type=F32)


def _bdot(a, b):
    return jnp.dot(a.astype(BF16), b.astype(BF16), preferred_element_type=F32)


def _sigmoid(x):
    return 1.0 / (1.0 + jnp.exp(-x))


def _params(*sem):
    return pltpu.CompilerParams(dimension_semantics=sem, vmem_limit_bytes=VMEM_LIMIT)


def _ffn_kernel(x_ref, g_ref, wg_ref, wu_ref, wd_ref, o_ref, h_scr, acc_scr):
    f = pl.program_id(1)

    @pl.when(f == 0)
    def _():
        x = x_ref[...]
        ms = jnp.mean(x * x, axis=-1, keepdims=True)
        h_scr[...] = (x * lax.rsqrt(ms + EPS) * g_ref[...]).astype(BF16)
        acc_scr[...] = jnp.zeros_like(acc_scr)

    h = h_scr[...]
    a = jnp.dot(h, wg_ref[...], preferred_element_type=F32)
    u = jnp.dot(h, wu_ref[...], preferred_element_type=F32)
    z = (a * _sigmoid(a)) * u
    acc_scr[...] += jnp.dot(z.astype(BF16), wd_ref[...], preferred_element_type=F32)

    @pl.when(f == pl.num_programs(1) - 1)
    def _():
        o_ref[...] = x_ref[...] + 0.5 * acc_scr[...]


def _ffn(x, gain, wg, wu, wd, *, tm, tf):
    m, d = x.shape
    dff = wg.shape[1]
    return pl.pallas_call(
        _ffn_kernel,
        grid=(m // tm, dff // tf),
        in_specs=[
            pl.BlockSpec((tm, d), lambda i, f: (i, 0)),
            pl.BlockSpec((1, d), lambda i, f: (0, 0)),
            pl.BlockSpec((d, tf), lambda i, f: (0, f)),
            pl.BlockSpec((d, tf), lambda i, f: (0, f)),
            pl.BlockSpec((tf, d), lambda i, f: (f, 0)),
        ],
        out_specs=pl.BlockSpec((tm, d), lambda i, f: (i, 0)),
        out_shape=jax.ShapeDtypeStruct((m, d), F32),
        scratch_shapes=[pltpu.VMEM((tm, d), BF16), pltpu.VMEM((tm, d), F32)],
        compiler_params=_params("parallel", "arbitrary"),
        name="ffn",
    )(x, gain.reshape(1, d), wg, wu, wd)


def _inproj_kernel(x_ref, g_ref, w_ref, grp_ref, qg_ref, kg_ref,
                   q_ref, k_ref, vt_ref, iq_ref, ikb_ref, kw_ref, gx_ref, ab_ref, mq_ref):
    x = x_ref[...]
    ms = jnp.mean(x * x, axis=-1, keepdims=True)
    h = (x * lax.rsqrt(ms + EPS) * g_ref[...]).astype(BF16)
    p = jnp.dot(h, w_ref[...], preferred_element_type=F32)

    grp = grp_ref[...]
    dq = p[:, SEG_A:SEG_A + DSA_W]
    dk = p[:, SEG_A + DSA_W:SEG_A + 2 * DSA_W]
    dv = p[:, SEG_A + 2 * DSA_W:SEG_A + 3 * DSA_W]
    qms = _hdot(dq * dq, grp) * (1.0 / HEAD_DIM)
    kms = _hdot(dk * dk, grp) * (1.0 / HEAD_DIM)
    q_ref[...] = (dq * lax.rsqrt(qms + EPS) * qg_ref[...] * (HEAD_DIM ** -0.5)).astype(BF16)
    k_ref[...] = (dk * lax.rsqrt(kms + EPS) * kg_ref[...]).astype(BF16)
    vt_ref[...] = dv.T.astype(BF16)

    iq_ref[...] = (p[:, SEG_IQ:SEG_IQ + IDX_HEADS * IDX_DIM] * (IDX_DIM ** -0.5)).astype(BF16)
    kw = p[:, SEG_KW:SEG_KW + LANES]
    kw_ref[...] = kw
    ikb_ref[...] = kw.astype(BF16)
    gx_ref[...] = p[:, SEG_G:SEG_G + 4 * GDN_W]
    ab_ref[...] = p[:, SEG_AB:SEG_AB + LANES]
    mq_ref[...] = p[:, SEG_MQ:SEG_MQ + MEM_W]


def _inproj(x, gain, w_packed, grp, qg, kg, *, tm):
    m, d = x.shape
    row = lambda w: pl.BlockSpec((tm, w), lambda i: (i, 0))
    full = lambda a: pl.BlockSpec(a.shape, lambda i: (0,) * a.ndim)
    out_shape = (
        jax.ShapeDtypeStruct((m, DSA_W), BF16),
        jax.ShapeDtypeStruct((m, DSA_W), BF16),
        jax.ShapeDtypeStruct((DSA_W, m), BF16),
        jax.ShapeDtypeStruct((m, IDX_HEADS * IDX_DIM), BF16),
        jax.ShapeDtypeStruct((m, LANES), BF16),
        jax.ShapeDtypeStruct((m, LANES), F32),
        jax.ShapeDtypeStruct((m, 4 * GDN_W), F32),
        jax.ShapeDtypeStruct((m, LANES), F32),
        jax.ShapeDtypeStruct((m, MEM_W), F32),
    )
    out_specs = (row(DSA_W), row(DSA_W), pl.BlockSpec((DSA_W, tm), lambda i: (0, i)),
                 row(IDX_HEADS * IDX_DIM), row(LANES), row(LANES), row(4 * GDN_W),
                 row(LANES), row(MEM_W))
    return pl.pallas_call(
        _inproj_kernel,
        grid=(m // tm,),
        in_specs=[row(d), full(gain), full(w_packed), full(grp), full(qg), full(kg)],
        out_specs=out_specs,
        out_shape=out_shape,
        compiler_params=_params("parallel"),
        name="inproj",
    )(x, gain, w_packed, grp, qg, kg)


def _key_to_float(u):
    key = u ^ jnp.int32(-2 ** 31)
    bits = jnp.where(key >= 0, key, key ^ jnp.int32(0x7FFFFFFF))
    return lax.bitcast_convert_type(bits, F32)


def _dsa_kernel(q_ref, iq_ref, kwq_ref, k_ref, vt_ref, ik_ref, bias_ref, ltri_ref, o_ref,
                s_scr, l_scr, iqt_scr, qpad_scr, *, kc, top_k):
    j = pl.program_id(1)
    per = kc // Q_BLOCK
    nch = (j + per) // per
    q_pos = j * Q_BLOCK + lax.broadcasted_iota(jnp.int32, (1, LANES), 1)
    row_iota = lax.broadcasted_iota(jnp.int32, (kc, LANES), 0)

    iqt = iq_ref[...].astype(F32).T
    zpad = jnp.zeros((LANES - IDX_DIM, LANES), F32)
    for h in range(IDX_HEADS):
        iqt_scr[h] = jnp.concatenate([iqt[h * IDX_DIM:(h + 1) * IDX_DIM], zpad], axis=0).astype(BF16)
    wt = kwq_ref[...].T[IDX_DIM:IDX_DIM + IDX_HEADS, :] * (IDX_HEADS ** -0.5)
    qt = q_ref[...].astype(F32).T
    half = lax.broadcasted_iota(jnp.int32, (LANES, LANES), 0) < HEAD_DIM
    for h in range(DSA_HEADS):
        pair = qt[(h // 2) * LANES:(h // 2 + 1) * LANES]
        keep = half if h % 2 == 0 else jnp.logical_not(half)
        qpad_scr[h] = jnp.where(keep, pair, 0.0).astype(BF16)

    def chunk(c):
        return pl.ds(pl.multiple_of(c * kc, kc), kc)

    def score_chunk(c, carry):
        kch = ik_ref[0, chunk(c), :]
        acc = jnp.zeros((kc, LANES), F32)
        for h in range(IDX_HEADS):
            d = jnp.dot(kch, iqt_scr[h], preferred_element_type=F32)
            acc = acc + jnp.maximum(d, 0.0) * wt[h:h + 1, :]
        adm = (row_iota + c * kc) <= q_pos
        s_scr[chunk(c), :] = jnp.where(adm, acc, -jnp.inf)
        return carry

    lax.fori_loop(0, nch, score_chunk, 0)

    def count(pred):
        def body(c, cnt):
            m = jnp.where(pred(s_scr[chunk(c), :]), 1.0, 0.0)
            return cnt + jnp.sum(m.reshape(kc // 8, 8, LANES), axis=0)
        cnt = lax.fori_loop(0, nch, body, jnp.zeros((8, LANES), F32))
        return jnp.sum(cnt, axis=0, keepdims=True)

    def bit_body(i, u):
        u_try = u | lax.shift_left(jnp.int32(1), 31 - i)
        t = _key_to_float(u_try)
        cnt = count(lambda s: s >= t)
        return jnp.where(cnt >= top_k, u_try, u)

    u = lax.fori_loop(0, 32, bit_body, jnp.zeros((1, LANES), jnp.int32))
    thr = jnp.where(u == 0, jnp.float32(-3.0e38), _key_to_float(u))
    need = top_k - count(lambda s: s > thr)

    def mask_chunk(c, run):
        s = s_scr[chunk(c), :]
        tie = jnp.where(s == thr, 1.0, 0.0)
        pref = jnp.dot(ltri_ref[...], tie.astype(BF16), preferred_element_type=F32) + run
        tie_sel = jnp.where(pref <= need, tie, 0.0)
        sel = jnp.where(s > thr, 1.0, tie_sel)
        s_scr[chunk(c), :] = jnp.where(sel > 0.5, 0.0, NEG)
        return run + jnp.sum(tie, axis=0, keepdims=True)

    lax.fori_loop(0, nch, mask_chunk, jnp.zeros((1, LANES), F32))

    diag = pl.ds(pl.multiple_of(j * Q_BLOCK, Q_BLOCK), Q_BLOCK)
    prev = pl.ds(pl.multiple_of(jnp.maximum(j - 1, 0) * Q_BLOCK, Q_BLOCK), Q_BLOCK)
    outs = []
    for h in range(DSA_HEADS):
        lanes = slice((h // 2) * LANES, (h // 2 + 1) * LANES)

        def logit_chunk(c, carry, h=h, lanes=lanes):
            l = jnp.dot(k_ref[0, chunk(c), lanes], qpad_scr[h], preferred_element_type=F32)
            l_scr[chunk(c), :] = l + s_scr[chunk(c), :]
            return carry

        lax.fori_loop(0, nch, logit_chunk, 0)
        l_scr[diag, :] = l_scr[diag, :] + bias_ref[h, 0]

        @pl.when(j >= 1)
        def _(h=h):
            l_scr[prev, :] = l_scr[prev, :] + bias_ref[h, 1]

        def max_chunk(c, m):
            return jnp.maximum(m, jnp.max(l_scr[chunk(c), :].reshape(kc // 8, 8, LANES), axis=0))

        m8 = lax.fori_loop(0, nch, max_chunk, jnp.full((8, LANES), NEG, F32))
        mx = jnp.max(m8, axis=0, keepdims=True)

        def pv_chunk(c, carry, h=h):
            acc, den = carry
            p = jnp.exp(l_scr[chunk(c), :] - mx)
            den = den + jnp.sum(p.reshape(kc // 8, 8, LANES), axis=0)
            vt = vt_ref[h * HEAD_DIM:(h + 1) * HEAD_DIM, chunk(c)]
            acc = acc + jnp.dot(vt, p.astype(BF16), preferred_element_type=F32)
            return acc, den

        acc, den = lax.fori_loop(
            0, nch, pv_chunk, (jnp.zeros((HEAD_DIM, LANES), F32), jnp.zeros((8, LANES), F32)))
        outs.append(acc / jnp.sum(den, axis=0, keepdims=True))
    o_ref[...] = jnp.concatenate(outs, axis=0).T


def _dsa(q, iq, kw, k, vt, ikb, bias, ltri, *, batch, seq, kc):
    nb = seq // Q_BLOCK
    top_k = min(TOPK_MAX, seq // 4)
    qrow = lambda w: pl.BlockSpec((Q_BLOCK, w), lambda b, j: (b * nb + j, 0))
    return pl.pallas_call(
        functools.partial(_dsa_kernel, kc=kc, top_k=top_k),
        grid=(batch, nb),
        in_specs=[
            qrow(DSA_W), qrow(IDX_HEADS * IDX_DIM), qrow(LANES),
            pl.BlockSpec((1, seq, DSA_W), lambda b, j: (b, 0, 0)),
            pl.BlockSpec((DSA_W, seq), lambda b, j: (0, b)),
            pl.BlockSpec((1, seq, LANES), lambda b, j: (b, 0, 0)),
            pl.BlockSpec(bias.shape, lambda b, j: (0, 0, 0, 0)),
            pl.BlockSpec(ltri.shape, lambda b, j: (0, 0)),
        ],
        out_specs=qrow(DSA_W),
        out_shape=jax.ShapeDtypeStruct((batch * seq, DSA_W), F32),
        scratch_shapes=[
            pltpu.VMEM((seq, LANES), F32),
            pltpu.VMEM((seq, LANES), F32),
            pltpu.VMEM((IDX_HEADS, LANES, LANES), BF16),
            pltpu.VMEM((DSA_HEADS, LANES, LANES), BF16),
        ],
        compiler_params=_params("arbitrary", "arbitrary"),
        name="dsa",
    )(q, iq, kw, k.reshape(batch, seq, DSA_W), vt, ikb.reshape(batch, seq, LANES), bias, ltri)


def _gdn_kernel(x_ref, ab_ref, cw_ref, ea_ref, eb_ref, alog_ref, dtb_ref, gain_ref, grp_ref,
                o_ref, xpad_scr, qkv_scr, gb_scr, st_scr, o_scr, *, tb):
    t = pl.program_id(1)
    cin = 3 * GDN_W

    @pl.when(t == 0)
    def _():
        xpad_scr[0:8, :] = jnp.zeros((8, cin), F32)
        st_scr[...] = jnp.zeros_like(st_scr)

    x = x_ref[...]
    xpad_scr[8:8 + tb, :] = x[:, :cin]
    conv = jnp.zeros((tb, cin), F32)
    for jj in range(CONV_WIDTH):
        conv = conv + cw_ref[jj:jj + 1, :] * xpad_scr[pl.ds(8 - (CONV_WIDTH - 1) + jj, tb), :]
    xpad_scr[0:8, :] = x[tb - 8:tb, :cin]
    qkv = conv * _sigmoid(conv)
    grp = grp_ref[...]
    q = qkv[:, :GDN_W]
    k = qkv[:, GDN_W:2 * GDN_W]
    q = q * lax.rsqrt(_hdot(q * q, grp) + EPS) * (HEAD_DIM ** -0.5)
    k = k * lax.rsqrt(_hdot(k * k, grp) + EPS)
    qkv_scr[:, :GDN_W] = q
    qkv_scr[:, GDN_W:2 * GDN_W] = k
    qkv_scr[:, 2 * GDN_W:] = qkv[:, 2 * GDN_W:]

    ab = ab_ref[...]
    a_e = _hdot(ab, ea_ref[...]) + dtb_ref[...]
    b_e = _hdot(ab, eb_ref[...])
    softplus = jnp.maximum(a_e, 0.0) + jnp.log(1.0 + jnp.exp(-jnp.abs(a_e)))
    gb_scr[:, :GDN_W] = -jnp.exp(alog_ref[...]) * softplus
    gb_scr[:, GDN_W:] = _sigmoid(b_e)

    r64 = lax.broadcasted_iota(jnp.int32, (GDN_CHUNK, LANES), 0)
    c64 = lax.broadcasted_iota(jnp.int32, (GDN_CHUNK, LANES), 1) % GDN_CHUNK
    causal = c64 <= r64
    strict = c64 < r64
    eye = jnp.where(c64 == r64, 1.0, 0.0)
    upper = jnp.where(r64 <= c64, 1.0, 0.0)
    r128 = lax.broadcasted_iota(jnp.int32, (LANES, LANES), 0)
    c128 = lax.broadcasted_iota(jnp.int32, (LANES, LANES), 1)
    bdmask = (r128 // HEAD_DIM) == (c128 // HEAD_DIM)
    lt_r = lax.broadcasted_iota(jnp.int32, (GDN_CHUNK, GDN_CHUNK), 0)
    lt_c = lax.broadcasted_iota(jnp.int32, (GDN_CHUNK, GDN_CHUNK), 1)
    ltri = jnp.where(lt_c <= lt_r, 1.0, 0.0)
    ones = jnp.ones((GDN_CHUNK, GDN_CHUNK), F32)

    def bd(m):
        return jnp.where(bdmask, jnp.concatenate([m, m], axis=0), 0.0)

    def chunk_body(ci, carry):
        rows = pl.ds(pl.multiple_of(ci * GDN_CHUNK, GDN_CHUNK), GDN_CHUNK)
        for p in range(GDN_HEADS // 2):
            lanes = slice(p * LANES, (p + 1) * LANES)
            off = lambda s: slice(s * GDN_W + p * LANES, s * GDN_W + (p + 1) * LANES)
            qp = qkv_scr[rows, off(0)]
            kp = qkv_scr[rows, off(1)]
            vp = qkv_scr[rows, off(2)]
            g = gb_scr[rows, off(0)]
            beta = gb_scr[rows, off(1)]
            gc = _hdot(ltri, g)
            gr = _hdot(ones, g * upper)
            dmat = jnp.exp(jnp.where(causal, gc - gr, NEG))
            kbd = jnp.where(bdmask, jnp.concatenate([kp, kp], axis=0).T, 0.0)
            kb = kp * beta
            lm = jnp.where(strict, _hdot(kb, kbd) * dmat, 0.0)
            aintra = _hdot(qp, kbd) * dmat
            tinv = eye - lm
            lpow = lm
            for _ in range(5):
                lpow = _hdot(lpow, bd(lpow))
                tinv = tinv + _hdot(tinv, bd(lpow))
            egc = jnp.exp(gc)
            u = _hdot(tinv, bd(vp * beta))
            w = _hdot(tinv, bd(kb * egc))
            sbd = st_scr[p]
            v_new = u - _hdot(w, sbd)
            o = _hdot(qp * egc, sbd) + _hdot(aintra, bd(v_new))
            g_last = gc[GDN_CHUNK - 1:GDN_CHUNK, :]
            kd = kp * jnp.exp(g_last - gc)
            st_scr[p] = sbd * jnp.exp(g_last) + jnp.where(bdmask, _hdot(kd.T, v_new), 0.0)
            o_scr[rows, lanes] = o
        return carry

    lax.fori_loop(0, tb // GDN_CHUNK, chunk_body, 0)

    o = o_scr[...]
    oms = _hdot(o * o, grp) * (1.0 / HEAD_DIM)
    z = x[:, cin:]
    o_ref[...] = o * lax.rsqrt(oms + EPS) * gain_ref[...] * (z * _sigmoid(z))


def _gdn(gx, ab, conv_w, ea, eb, alog, dtb, gain, grp, *, batch, seq, tb):
    nt = seq // tb
    cin = 3 * GDN_W
    row = lambda w: pl.BlockSpec((tb, w), lambda b, t: (b * nt + t, 0))
    full = lambda a: pl.BlockSpec(a.shape, lambda b, t: (0,) * a.ndim)
    return pl.pallas_call(
        functools.partial(_gdn_kernel, tb=tb),
        grid=(batch, nt),
        in_specs=[row(4 * GDN_W), row(LANES), full(conv_w), full(ea), full(eb), full(alog),
                  full(dtb), full(gain), full(grp)],
        out_specs=row(GDN_W),
        out_shape=jax.ShapeDtypeStruct((batch * seq, GDN_W), F32),
        scratch_shapes=[
            pltpu.VMEM((tb + 8, cin), F32),
            pltpu.VMEM((tb, cin), F32),
            pltpu.VMEM((tb, 2 * GDN_W), F32),
            pltpu.VMEM((GDN_HEADS // 2, LANES, LANES), F32),
            pltpu.VMEM((tb, GDN_W), F32),
        ],
        compiler_params=_params("arbitrary", "arbitrary"),
        name="gdn",
    )(gx, ab, conv_w, ea, eb, alog, dtb, gain, grp)


def _memkv_kernel(mem_ref, g_ref, w_ref, kg_ref, grp_ref, kt_ref, vp_ref):
    x = mem_ref[0]
    ms = jnp.mean(x * x, axis=-1, keepdims=True)
    h = (x * lax.rsqrt(ms + EPS) * g_ref[...]).astype(BF16)
    kv = jnp.dot(h, w_ref[...], preferred_element_type=F32)
    km = kv[:, :MEM_W]
    vm = kv[:, MEM_W:]
    kms = _hdot(km * km, grp_ref[...]) * (1.0 / HEAD_DIM)
    kt = (km * lax.rsqrt(kms + EPS) * kg_ref[...]).T
    n_mem = x.shape[0]
    top = lax.broadcasted_iota(jnp.int32, (LANES, n_mem), 0) < HEAD_DIM
    left = lax.broadcasted_iota(jnp.int32, (n_mem, LANES), 1) < HEAD_DIM
    for hh in range(MEM_HEADS):
        pr = slice((hh // 2) * LANES, (hh // 2 + 1) * LANES)
        keep_r = top if hh % 2 == 0 else jnp.logical_not(top)
        keep_c = left if hh % 2 == 0 else jnp.logical_not(left)
        kt_ref[0, hh] = jnp.where(keep_r, kt[pr, :], 0.0).astype(BF16)
        vp_ref[0, hh] = jnp.where(keep_c, vm[:, pr], 0.0).astype(BF16)


def _memkv(mem, gain, w, kg, grp):
    b, n_mem, d = mem.shape
    full = lambda a: pl.BlockSpec(a.shape, lambda i: (0,) * a.ndim)
    return pl.pallas_call(
        _memkv_kernel,
        grid=(b,),
        in_specs=[pl.BlockSpec((1, n_mem, d), lambda i: (i, 0, 0)), full(gain), full(w), full(kg),
                  full(grp)],
        out_specs=(pl.BlockSpec((1, MEM_HEADS, LANES, n_mem), lambda i: (i, 0, 0, 0)),
                   pl.BlockSpec((1, MEM_HEADS, n_mem, LANES), lambda i: (i, 0, 0, 0))),
        out_shape=(jax.ShapeDtypeStruct((b, MEM_HEADS, LANES, n_mem), BF16),
                   jax.ShapeDtypeStruct((b, MEM_HEADS, n_mem, LANES), BF16)),
        compiler_params=_params("parallel"),
        name="memkv",
    )(mem, gain, w, kg, grp)


def _memattn_kernel(q_ref, qg_ref, grp_ref, kt_ref, vp_ref, o_ref):
    q = q_ref[...]
    qms = _hdot(q * q, grp_ref[...]) * (1.0 / HEAD_DIM)
    qn = (q * lax.rsqrt(qms + EPS) * qg_ref[...] * (HEAD_DIM ** -0.5)).astype(BF16)
    outs = []
    for pr in range(MEM_HEADS // 2):
        qp = qn[:, pr * LANES:(pr + 1) * LANES]
        acc = None
        for e in range(2):
            hh = 2 * pr + e
            l = jnp.dot(qp, kt_ref[0, hh], preferred_element_type=F32)
            l = l - jnp.max(l, axis=-1, keepdims=True)
            p = jnp.exp(l)
            p = p / jnp.sum(p, axis=-1, keepdims=True)
            o = jnp.dot(p.astype(BF16), vp_ref[0, hh], preferred_element_type=F32)
            acc = o if acc is None else acc + o
        outs.append(acc)
    o_ref[...] = jnp.concatenate(outs, axis=-1)


def _memattn(mq, qg, grp, kt, vp, *, batch, seq, tm):
    nt = seq // tm
    n_mem = kt.shape[-1]
    full = lambda a: pl.BlockSpec(a.shape, lambda b, t: (0,) * a.ndim)
    return pl.pallas_call(
        _memattn_kernel,
        grid=(batch, nt),
        in_specs=[
            pl.BlockSpec((tm, MEM_W), lambda b, t: (b * nt + t, 0)), full(qg), full(grp),
            pl.BlockSpec((1, MEM_HEADS, LANES, n_mem), lambda b, t: (b, 0, 0, 0)),
            pl.BlockSpec((1, MEM_HEADS, n_mem, LANES), lambda b, t: (b, 0, 0, 0)),
        ],
        out_specs=pl.BlockSpec((tm, MEM_W), lambda b, t: (b * nt + t, 0)),
        out_shape=jax.ShapeDtypeStruct((batch * seq, MEM_W), F32),
        compiler_params=_params("parallel", "parallel"),
        name="memattn",
    )(mq, qg, grp, kt, vp)


def _outproj_kernel(x_ref, a_ref, b_ref, c_ref, wa_ref, wb_ref, wc_ref, o_ref):
    y = _bdot(a_ref[...], wa_ref[...]) + _bdot(b_ref[...], wb_ref[...]) + _bdot(c_ref[...], wc_ref[...])
    o_ref[...] = x_ref[...] + y


def _outproj(x, a, b, c, wa, wb, wc, *, tm):
    m, d = x.shape
    row = lambda w: pl.BlockSpec((tm, w), lambda i: (i, 0))
    full = lambda arr: pl.BlockSpec(arr.shape, lambda i: (0,) * arr.ndim)
    return pl.pallas_call(
        _outproj_kernel,
        grid=(m // tm,),
        in_specs=[row(d), row(DSA_W), row(GDN_W), row(MEM_W), full(wa), full(wb), full(wc)],
        out_specs=row(d),
        out_shape=jax.ShapeDtypeStruct((m, d), F32),
        compiler_params=_params("parallel"),
        name="outproj",
    )(x, a, b, c, wa, wb, wc)


def _group_ones(width):
    idx = np.arange(width) // HEAD_DIM
    return jnp.asarray((idx[:, None] == idx[None, :]).astype(np.float32))


def _t5_bucket_np(n):
    max_exact = NUM_BUCKETS // 2
    nf = np.maximum(n, 1).astype(np.float32)
    large = max_exact + (np.log(nf / np.float32(max_exact)) / np.float32(math.log(MAX_DISTANCE / max_exact))
                         * (NUM_BUCKETS - max_exact)).astype(np.int32)
    large = np.minimum(large, NUM_BUCKETS - 1)
    return np.where(n < max_exact, n, large)


def _bias_tiles(rel_bias):
    s = np.arange(Q_BLOCK)[:, None]
    q = np.arange(Q_BLOCK)[None, :]
    d0 = q - s
    d1 = Q_BLOCK + q - s
    b0 = _t5_bucket_np(np.maximum(d0, 0))
    b1 = _t5_bucket_np(d1)
    table = rel_bias.astype(F32)
    far = table[NUM_BUCKETS - 1]
    t0 = jnp.where(jnp.asarray(d0 >= 0)[..., None], table[b0] - far, 0.0)
    t1 = table[b1] - far
    return jnp.transpose(jnp.stack([t0, t1], axis=0), (3, 0, 1, 2))


def _pack_w_in(w):
    d = w.shape[0]
    o = 3 * DSA_W
    iq = w[:, o:o + IDX_HEADS * IDX_DIM]
    o += IDX_HEADS * IDX_DIM
    kw = w[:, o:o + IDX_DIM + IDX_HEADS]
    o += IDX_DIM + IDX_HEADS
    g = w[:, o:o + 4 * GDN_W]
    o += 4 * GDN_W
    ab = w[:, o:o + 2 * GDN_HEADS]
    o += 2 * GDN_HEADS
    mq = w[:, o:o + MEM_W]
    pad = lambda a: jnp.pad(a, ((0, 0), (0, LANES - a.shape[1])))
    return jnp.concatenate([w[:, :3 * DSA_W], iq, pad(kw), g, pad(ab), mq], axis=1).astype(BF16)


def _tile_heads(v, heads):
    return jnp.tile(v.astype(F32), heads).reshape(1, heads * HEAD_DIM)


def kernel(x, mem, ffn1_norm, ffn1_w_gate, ffn1_w_up, ffn1_w_down, mix_norm, w_in, dsa_q_norm, dsa_k_norm, rel_bias, gdn_conv, gdn_A_log, gdn_dt_bias, gdn_out_norm, mem_norm, w_mem_kv, mem_q_norm, mem_k_norm, w_out, ffn2_norm, ffn2_w_gate, ffn2_w_up, ffn2_w_down):
    batch, seq, d = x.shape
    depth = w_in.shape[0]
    m = batch * seq
    tm = min(512, seq)
    kc = min(512, seq)
    dff = ffn1_w_gate.shape[-1]
    tf = dff // 2 if (dff // 2) % LANES == 0 else dff

    grp_a = _group_ones(DSA_W)
    grp_m = _group_ones(MEM_W)
    bias = _bias_tiles(rel_bias)
    ltri = jnp.asarray(np.tril(np.ones((kc, kc), np.float32))).astype(BF16)
    heads_of = np.arange(GDN_W) // HEAD_DIM
    ea = jnp.asarray((np.arange(LANES)[:, None] == heads_of[None, :]).astype(np.float32))
    eb = jnp.asarray((np.arange(LANES)[:, None] == heads_of[None, :] + GDN_HEADS).astype(np.float32))

    xf = x.reshape(m, d)
    for l in range(depth):
        xf = _ffn(xf, ffn1_norm[l], ffn1_w_gate[l].astype(BF16), ffn1_w_up[l].astype(BF16),
                  ffn1_w_down[l].astype(BF16), tm=tm, tf=tf)
        (q, k, vt, iq, ikb, kw, gx, ab, mq) = _inproj(
            xf, mix_norm[l].reshape(1, d), _pack_w_in(w_in[l]), grp_a,
            _tile_heads(dsa_q_norm[l], DSA_HEADS), _tile_heads(dsa_k_norm[l], DSA_HEADS), tm=tm)
        out_a = _dsa(q, iq, kw, k, vt, ikb, bias, ltri, batch=batch, seq=seq, kc=kc)
        out_b = _gdn(gx, ab, gdn_conv[l], ea, eb,
                     jnp.repeat(gdn_A_log[l].astype(F32), HEAD_DIM).reshape(1, GDN_W),
                     jnp.repeat(gdn_dt_bias[l].astype(F32), HEAD_DIM).reshape(1, GDN_W),
                     _tile_heads(gdn_out_norm[l], GDN_HEADS), grp_a, batch=batch, seq=seq, tb=tm)
        kt, vp = _memkv(mem, mem_norm[l].reshape(1, d), w_mem_kv[l].astype(BF16),
                        _tile_heads(mem_k_norm[l], MEM_HEADS), grp_m)
        out_c = _memattn(mq, _tile_heads(mem_q_norm[l], MEM_HEADS), grp_m, kt, vp,
                         batch=batch, seq=seq, tm=tm)
        wo = w_out[l].astype(BF16)
        xf = _outproj(xf, out_a, out_b, out_c, wo[:DSA_W], wo[DSA_W:DSA_W + GDN_W],
                      wo[DSA_W + GDN_W:], tm=tm)
        xf = _ffn(xf, ffn2_norm[l], ffn2_w_gate[l].astype(BF16), ffn2_w_up[l].astype(BF16),
                  ffn2_w_down[l].astype(BF16), tm=tm, tf=tf)
    return xf.reshape(batch, seq, d)
```

```python
import functools
import math

import jax
import jax.numpy as jnp
import numpy as np
from jax import lax
from jax.experimental import pallas as pl
from jax.experimental.pallas import tpu as pltpu

F32 = jnp.float32
BF16 = jnp.bfloat16

HEAD_DIM = 64
DSA_HEADS = 6
GDN_HEADS = 6
MEM_HEADS = 4
DSA_W = DSA_HEADS * HEAD_DIM
GDN_W = GDN_HEADS * HEAD_DIM
MEM_W = MEM_HEADS * HEAD_DIM
IDX_HEADS = 8
IDX_DIM = 32
TOPK_MAX = 256
Q_BLOCK = 128
GDN_CHUNK = 64
CONV_WIDTH = 4
NUM_BUCKETS = 32
MAX_DISTANCE = 128
EPS = 1e-6

LANES = 128
VMEM_LIMIT = 52 * 1024 * 1024
NEG = -1e30
HI = lax.Precision.HIGHEST

SEG_A = 0
SEG_IQ = SEG_A + 3 * DSA_W
SEG_KW = SEG_IQ + IDX_HEADS * IDX_DIM
SEG_G = SEG_KW + LANES
SEG_AB = SEG_G + 4 * GDN_W
SEG_MQ = SEG_AB + LANES
IN_PACKED = SEG_MQ + MEM_W


def _hdot(a, b):
    return jnp.dot(a, b, precision=HI, preferred_element_type=F32)


def _bdot(a, b):
    return jnp.dot(a.astype(BF16), b.astype(BF16), preferred_element_type=F32)


def _sigmoid(x):
    return 1.0 / (1.0 + jnp.exp(-x))


def _params(*sem):
    return pltpu.CompilerParams(dimension_semantics=sem, vmem_limit_bytes=VMEM_LIMIT)


def _ffn_kernel(x_ref, g_ref, wg_ref, wu_ref, wd_ref, o_ref, h_scr, acc_scr):
    f = pl.program_id(1)

    @pl.when(f == 0)
    def _():
        x = x_ref[...]
        ms = jnp.mean(x * x, axis=-1, keepdims=True)
        h_scr[...] = (x * lax.rsqrt(ms + EPS) * g_ref[...]).astype(BF16)
        acc_scr[...] = jnp.zeros_like(acc_scr)

    h = h_scr[...]
    a = jnp.dot(h, wg_ref[...], preferred_element_type=F32)
    u = jnp.dot(h, wu_ref[...], preferred_element_type=F32)
    z = (a * _sigmoid(a)) * u
    acc_scr[...] += jnp.dot(z.astype(BF16), wd_ref[...], preferred_element_type=F32)

    @pl.when(f == pl.num_programs(1) - 1)
    def _():
        o_ref[...] = x_ref[...] + 0.5 * acc_scr[...]


def _ffn(x, gain, wg, wu, wd, *, tm, tf):
    m, d = x.shape
    dff = wg.shape[1]
    return pl.pallas_call(
        _ffn_kernel,
        grid=(m // tm, dff // tf),
        in_specs=[
            pl.BlockSpec((tm, d), lambda i, f: (i, 0)),
            pl.BlockSpec((1, d), lambda i, f: (0, 0)),
            pl.BlockSpec((d, tf), lambda i, f: (0, f)),
            pl.BlockSpec((d, tf), lambda i, f: (0, f)),
            pl.BlockSpec((tf, d), lambda i, f: (f, 0)),
        ],
        out_specs=pl.BlockSpec((tm, d), lambda i, f: (i, 0)),
        out_shape=jax.ShapeDtypeStruct((m, d), F32),
        scratch_shapes=[pltpu.VMEM((tm, d), BF16), pltpu.VMEM((tm, d), F32)],
        compiler_params=_params("parallel", "arbitrary"),
        name="ffn",
    )(x, gain.reshape(1, d), wg, wu, wd)


def _inproj_kernel(x_ref, g_ref, w_ref, grp_ref, qg_ref, kg_ref,
                   q_ref, k_ref, vt_ref, iq_ref, ikb_ref, kw_ref, gx_ref, ab_ref, mq_ref):
    x = x_ref[...]
    ms = jnp.mean(x * x, axis=-1, keepdims=True)
    h = (x * lax.rsqrt(ms + EPS) * g_ref[...]).astype(BF16)
    p = jnp.dot(h, w_ref[...], preferred_element_type=F32)

    grp = grp_ref[...]
    dq = p[:, SEG_A:SEG_A + DSA_W]
    dk = p[:, SEG_A + DSA_W:SEG_A + 2 * DSA_W]
    dv = p[:, SEG_A + 2 * DSA_W:SEG_A + 3 * DSA_W]
    qms = _hdot(dq * dq, grp) * (1.0 / HEAD_DIM)
    kms = _hdot(dk * dk, grp) * (1.0 / HEAD_DIM)
    q_ref[...] = (dq * lax.rsqrt(qms + EPS) * qg_ref[...] * (HEAD_DIM ** -0.5)).astype(BF16)
    k_ref[...] = (dk * lax.rsqrt(kms + EPS) * kg_ref[...]).astype(BF16)
    vt_ref[...] = dv.T.astype(BF16)

    iq_ref[...] = (p[:, SEG_IQ:SEG_IQ + IDX_HEADS * IDX_DIM] * (IDX_DIM ** -0.5)).astype(BF16)
    kw = p[:, SEG_KW:SEG_KW + LANES]
    kw_ref[...] = kw
    ikb_ref[...] = kw.astype(BF16)
    gx_ref[...] = p[:, SEG_G:SEG_G + 4 * GDN_W]
    ab_ref[...] = p[:, SEG_AB:SEG_AB + LANES]
    mq_ref[...] = p[:, SEG_MQ:SEG_MQ + MEM_W]


def _inproj(x, gain, w_packed, grp, qg, kg, *, tm):
    m, d = x.shape
    row = lambda w: pl.BlockSpec((tm, w), lambda i: (i, 0))
    full = lambda a: pl.BlockSpec(a.shape, lambda i: (0,) * a.ndim)
    out_shape = (
        jax.ShapeDtypeStruct((m, DSA_W), BF16),
        jax.ShapeDtypeStruct((m, DSA_W), BF16),
        jax.ShapeDtypeStruct((DSA_W, m), BF16),
        jax.ShapeDtypeStruct((m, IDX_HEADS * IDX_DIM), BF16),
        jax.ShapeDtypeStruct((m, LANES), BF16),
        jax.ShapeDtypeStruct((m, LANES), F32),
        jax.ShapeDtypeStruct((m, 4 * GDN_W), F32),
        jax.ShapeDtypeStruct((m, LANES), F32),
        jax.ShapeDtypeStruct((m, MEM_W), F32),
    )
    out_specs = (row(DSA_W), row(DSA_W), pl.BlockSpec((DSA_W, tm), lambda i: (0, i)),
                 row(IDX_HEADS * IDX_DIM), row(LANES), row(LANES), row(4 * GDN_W),
                 row(LANES), row(MEM_W))
    return pl.pallas_call(
        _inproj_kernel,
        grid=(m // tm,),
        in_specs=[row(d), full(gain), full(w_packed), full(grp), full(qg), full(kg)],
        out_specs=out_specs,
        out_shape=out_shape,
        compiler_params=_params("parallel"),
        name="inproj",
    )(x, gain, w_packed, grp, qg, kg)


def _key_to_float(u):
    key = u ^ jnp.int32(-2 ** 31)
    bits = jnp.where(key >= 0, key, key ^ jnp.int32(0x7FFFFFFF))
    return lax.bitcast_convert_type(bits, F32)


def _tree(op, x):
    parts = x.reshape(x.shape[0] // 8, 8, LANES)
    k = parts.shape[0]
    while k > 1:
        k //= 2
        parts = op(parts[:k], parts[k:2 * k])
    return parts[0]


def _dsa_kernel(q_ref, iq_ref, kwq_ref, k_ref, vt_ref, ik_ref, bias_ref, ltri_ref, o_ref,
                s_scr, iqt_scr, qpad_scr, *, kc, top_k):
    j = pl.program_id(1)
    per = kc // Q_BLOCK
    pad = kc - Q_BLOCK
    nch = j // per + 1
    q_pos = j * Q_BLOCK + lax.broadcasted_iota(jnp.int32, (1, LANES), 1)
    row_iota = lax.broadcasted_iota(jnp.int32, (kc, LANES), 0)

    def rows(i):
        return pl.ds(pl.multiple_of(j * Q_BLOCK - i * kc, Q_BLOCK), kc)

    iqt = iq_ref[...].astype(F32).T
    zpad = jnp.zeros((LANES - IDX_DIM, LANES), F32)
    for h in range(IDX_HEADS):
        iqt_scr[:, h * LANES:(h + 1) * LANES] = jnp.concatenate(
            [iqt[h * IDX_DIM:(h + 1) * IDX_DIM], zpad], axis=0).astype(BF16)
    wt = kwq_ref[...].T[IDX_DIM:IDX_DIM + IDX_HEADS, :] * (IDX_HEADS ** -0.5)
    qt = q_ref[...].astype(F32).T
    half = lax.broadcasted_iota(jnp.int32, (LANES, LANES), 0) < HEAD_DIM
    for h in range(DSA_HEADS):
        pair = qt[(h // 2) * LANES:(h // 2 + 1) * LANES]
        keep = half if h % 2 == 0 else jnp.logical_not(half)
        qpad_scr[:, h * LANES:(h + 1) * LANES] = jnp.where(keep, pair, 0.0).astype(BF16)

    def score_chunk(i, carry):
        d = jnp.dot(ik_ref[0, rows(i), :], iqt_scr[...], preferred_element_type=F32)
        acc = jnp.maximum(d[:, :LANES], 0.0) * wt[0:1, :]
        for h in range(1, IDX_HEADS):
            acc = acc + jnp.maximum(d[:, h * LANES:(h + 1) * LANES], 0.0) * wt[h:h + 1, :]
        key = row_iota + (j * Q_BLOCK - i * kc - pad)
        adm = jnp.where(key >= 0, key, q_pos + 1) <= q_pos
        s_scr[rows(i), :] = jnp.where(adm, acc, -jnp.inf)
        return carry

    lax.fori_loop(0, nch, score_chunk, 0)

    def count(pred):
        def body(i, cnt):
            return cnt + _tree(jnp.add, jnp.where(pred(s_scr[rows(i), :]), 1.0, 0.0))
        cnt = lax.fori_loop(0, nch, body, jnp.zeros((8, LANES), F32))
        return jnp.sum(cnt, axis=0, keepdims=True)

    def bit_body(b, u):
        u_try = u | lax.shift_left(jnp.int32(1), 31 - b)
        t = _key_to_float(u_try)
        cnt = count(lambda s: s >= t)
        return jnp.where(cnt >= top_k, u_try, u)

    u = lax.fori_loop(0, 32, bit_body, jnp.zeros((1, LANES), jnp.int32))
    thr = jnp.where(u == 0, jnp.float32(-3.0e38), _key_to_float(u))
    n_ge = count(lambda s: s >= thr)
    need = top_k - count(lambda s: s > thr)
    has_ties = jnp.max(n_ge) > top_k

    @pl.when(jnp.logical_not(has_ties))
    def _():
        def mask_chunk(i, carry):
            s_scr[rows(i), :] = jnp.where(s_scr[rows(i), :] >= thr, 0.0, NEG)
            return carry
        lax.fori_loop(0, nch, mask_chunk, 0)

    @pl.when(has_ties)
    def _():
        def mask_chunk(t, run):
            i = nch - 1 - t
            s = s_scr[rows(i), :]
            tie = jnp.where(s == thr, 1.0, 0.0)
            pref = jnp.dot(ltri_ref[...], tie.astype(BF16), preferred_element_type=F32) + run
            tie_sel = jnp.where(pref <= need, tie, 0.0)
            sel = jnp.where(s > thr, 1.0, tie_sel)
            s_scr[rows(i), :] = jnp.where(sel > 0.5, 0.0, NEG)
            return run + jnp.sum(tie, axis=0, keepdims=True)
        lax.fori_loop(0, nch, mask_chunk, jnp.zeros((1, LANES), F32))

    def logits(i, with_bias):
        msk = s_scr[rows(i), :]
        out = []
        for pr in range(DSA_HEADS // 2):
            l2 = jnp.dot(k_ref[0, rows(i), pr * LANES:(pr + 1) * LANES],
                         qpad_scr[:, 2 * pr * LANES:(2 * pr + 2) * LANES],
                         preferred_element_type=F32)
            for e in range(2):
                l = l2[:, e * LANES:(e + 1) * LANES] + msk
                out.append(l + bias_ref[2 * pr + e] if with_bias else l)
        return out

    def max_step(i, ms, with_bias):
        return tuple(jnp.maximum(m, _tree(jnp.maximum, l)) for m, l in zip(ms, logits(i, with_bias)))

    ms = max_step(0, tuple(jnp.full((8, LANES), NEG, F32) for _ in range(DSA_HEADS)), True)
    ms = lax.fori_loop(1, nch, lambda i, m: max_step(i, m, False), ms)
    mx = [jnp.max(m, axis=0, keepdims=True) for m in ms]

    def pv_step(i, state, with_bias):
        new = []
        for h, l in enumerate(logits(i, with_bias)):
            den, acc = state[h]
            p = jnp.exp(l - mx[h])
            vt = vt_ref[h * HEAD_DIM:(h + 1) * HEAD_DIM, rows(i)]
            new.append((den + _tree(jnp.add, p),
                        acc + jnp.dot(vt, p.astype(BF16), preferred_element_type=F32)))
        return tuple(new)

    state = pv_step(0, tuple((jnp.zeros((8, LANES), F32), jnp.zeros((HEAD_DIM, LANES), F32))
                             for _ in range(DSA_HEADS)), True)
    state = lax.fori_loop(1, nch, lambda i, st: pv_step(i, st, False), state)
    outs = [acc / jnp.sum(den, axis=0, keepdims=True) for (den, acc) in state]
    o_ref[...] = jnp.concatenate(outs, axis=0).T


def _dsa(q, iq, kw, k, vt, ikb, bias, ltri, *, batch, seq, kc):
    nb = seq // Q_BLOCK
    top_k = min(TOPK_MAX, seq // 4)
    pad = kc - Q_BLOCK
    seqp = seq + pad
    kp = jnp.pad(k.reshape(batch, seq, DSA_W), ((0, 0), (pad, 0), (0, 0)))
    ikp = jnp.pad(ikb.reshape(batch, seq, LANES), ((0, 0), (pad, 0), (0, 0)))
    vtp = jnp.pad(vt.reshape(DSA_W, batch, seq), ((0, 0), (0, 0), (pad, 0))).reshape(DSA_W, batch * seqp)
    qrow = lambda w: pl.BlockSpec((Q_BLOCK, w), lambda b, j: (b * nb + j, 0))
    return pl.pallas_call(
        functools.partial(_dsa_kernel, kc=kc, top_k=top_k),
        grid=(batch, nb),
        in_specs=[
            qrow(DSA_W), qrow(IDX_HEADS * IDX_DIM), qrow(LANES),
            pl.BlockSpec((1, seqp, DSA_W), lambda b, j: (b, 0, 0)),
            pl.BlockSpec((DSA_W, seqp), lambda b, j: (0, b)),
            pl.BlockSpec((1, seqp, LANES), lambda b, j: (b, 0, 0)),
            pl.BlockSpec(bias.shape, lambda b, j: (0, 0, 0)),
            pl.BlockSpec(ltri.shape, lambda b, j: (0, 0)),
        ],
        out_specs=qrow(DSA_W),
        out_shape=jax.ShapeDtypeStruct((batch * seq, DSA_W), F32),
        scratch_shapes=[
            pltpu.VMEM((seqp, LANES), F32),
            pltpu.VMEM((LANES, IDX_HEADS * LANES), BF16),
            pltpu.VMEM((LANES, DSA_HEADS * LANES), BF16),
        ],
        compiler_params=_params("arbitrary", "arbitrary"),
        name="dsa",
    )(q, iq, kw, kp, vtp, ikp, bias, ltri)


def _gdn_kernel(x_ref, ab_ref, cw_ref, ea_ref, eb_ref, alog_ref, dtb_ref, gain_ref, grp_ref,
                o_ref, xpad_scr, qkv_scr, gb_scr, st_scr, o_scr, *, tb):
    t = pl.program_id(1)
    cin = 3 * GDN_W

    @pl.when(t == 0)
    def _():
        xpad_scr[0:8, :] = jnp.zeros((8, cin), F32)
        st_scr[...] = jnp.zeros_like(st_scr)

    x = x_ref[...]
    xpad_scr[8:8 + tb, :] = x[:, :cin]
    conv = jnp.zeros((tb, cin), F32)
    for jj in range(CONV_WIDTH):
        conv = conv + cw_ref[jj:jj + 1, :] * xpad_scr[pl.ds(8 - (CONV_WIDTH - 1) + jj, tb), :]
    xpad_scr[0:8, :] = x[tb - 8:tb, :cin]
    qkv = conv * _sigmoid(conv)
    grp = grp_ref[...]
    q = qkv[:, :GDN_W]
    k = qkv[:, GDN_W:2 * GDN_W]
    q = q * lax.rsqrt(_hdot(q * q, grp) + EPS) * (HEAD_DIM ** -0.5)
    k = k * lax.rsqrt(_hdot(k * k, grp) + EPS)
    qkv_scr[:, :GDN_W] = q
    qkv_scr[:, GDN_W:2 * GDN_W] = k
    qkv_scr[:, 2 * GDN_W:] = qkv[:, 2 * GDN_W:]

    ab = ab_ref[...]
    a_e = _hdot(ab, ea_ref[...]) + dtb_ref[...]
    b_e = _hdot(ab, eb_ref[...])
    softplus = jnp.maximum(a_e, 0.0) + jnp.log(1.0 + jnp.exp(-jnp.abs(a_e)))
    gb_scr[:, :GDN_W] = -jnp.exp(alog_ref[...]) * softplus
    gb_scr[:, GDN_W:] = _sigmoid(b_e)

    r64 = lax.broadcasted_iota(jnp.int32, (GDN_CHUNK, LANES), 0)
    c64 = lax.broadcasted_iota(jnp.int32, (GDN_CHUNK, LANES), 1) % GDN_CHUNK
    causal = c64 <= r64
    strict = c64 < r64
    eye = jnp.where(c64 == r64, 1.0, 0.0)
    upper = jnp.where(r64 <= c64, 1.0, 0.0)
    r128 = lax.broadcasted_iota(jnp.int32, (LANES, LANES), 0)
    c128 = lax.broadcasted_iota(jnp.int32, (LANES, LANES), 1)
    bdmask = (r128 // HEAD_DIM) == (c128 // HEAD_DIM)
    lt_r = lax.broadcasted_iota(jnp.int32, (GDN_CHUNK, GDN_CHUNK), 0)
    lt_c = lax.broadcasted_iota(jnp.int32, (GDN_CHUNK, GDN_CHUNK), 1)
    ltri = jnp.where(lt_c <= lt_r, 1.0, 0.0)
    ones = jnp.ones((GDN_CHUNK, GDN_CHUNK), F32)

    def bd(m):
        return jnp.where(bdmask, jnp.concatenate([m, m], axis=0), 0.0)

    def chunk_body(ci, carry):
        rows = pl.ds(pl.multiple_of(ci * GDN_CHUNK, GDN_CHUNK), GDN_CHUNK)
        for p in range(GDN_HEADS // 2):
            lanes = slice(p * LANES, (p + 1) * LANES)
            off = lambda s: slice(s * GDN_W + p * LANES, s * GDN_W + (p + 1) * LANES)
            qp = qkv_scr[rows, off(0)]
            kp = qkv_scr[rows, off(1)]
            vp = qkv_scr[rows, off(2)]
            g = gb_scr[rows, off(0)]
            beta = gb_scr[rows, off(1)]
            gc = _hdot(ltri, g)
            gr = _hdot(ones, g * upper)
            dmat = jnp.exp(jnp.where(causal, gc - gr, NEG))
            kbd = jnp.where(bdmask, jnp.concatenate([kp, kp], axis=0).T, 0.0)
            kb = kp * beta
            lm = jnp.where(strict, _hdot(kb, kbd) * dmat, 0.0)
            aintra = _hdot(qp, kbd) * dmat
            tinv = eye - lm
            lpow = lm
            for _ in range(5):
                lpow = _hdot(lpow, bd(lpow))
                tinv = tinv + _hdot(tinv, bd(lpow))
            egc = jnp.exp(gc)
            u = _hdot(tinv, bd(vp * beta))
            w = _hdot(tinv, bd(kb * egc))
            sbd = st_scr[p]
            v_new = u - _hdot(w, sbd)
            o = _hdot(qp * egc, sbd) + _hdot(aintra, bd(v_new))
            g_last = gc[GDN_CHUNK - 1:GDN_CHUNK, :]
            kd = kp * jnp.exp(g_last - gc)
            st_scr[p] = sbd * jnp.exp(g_last) + jnp.where(bdmask, _hdot(kd.T, v_new), 0.0)
            o_scr[rows, lanes] = o
        return carry

    lax.fori_loop(0, tb // GDN_CHUNK, chunk_body, 0)

    o = o_scr[...]
    oms = _hdot(o * o, grp) * (1.0 / HEAD_DIM)
    z = x[:, cin:]
    o_ref[...] = o * lax.rsqrt(oms + EPS) * gain_ref[...] * (z * _sigmoid(z))


def _gdn(gx, ab, conv_w, ea, eb, alog, dtb, gain, grp, *, batch, seq, tb):
    nt = seq // tb
    cin = 3 * GDN_W
    row = lambda w: pl.BlockSpec((tb, w), lambda b, t: (b * nt + t, 0))
    full = lambda a: pl.BlockSpec(a.shape, lambda b, t: (0,) * a.ndim)
    return pl.pallas_call(
        functools.partial(_gdn_kernel, tb=tb),
        grid=(batch, nt),
        in_specs=[row(4 * GDN_W), row(LANES), full(conv_w), full(ea), full(eb), full(alog),
                  full(dtb), full(gain), full(grp)],
        out_specs=row(GDN_W),
        out_shape=jax.ShapeDtypeStruct((batch * seq, GDN_W), F32),
        scratch_shapes=[
            pltpu.VMEM((tb + 8, cin), F32),
            pltpu.VMEM((tb, cin), F32),
            pltpu.VMEM((tb, 2 * GDN_W), F32),
            pltpu.VMEM((GDN_HEADS // 2, LANES, LANES), F32),
            pltpu.VMEM((tb, GDN_W), F32),
        ],
        compiler_params=_params("arbitrary", "arbitrary"),
        name="gdn",
    )(gx, ab, conv_w, ea, eb, alog, dtb, gain, grp)


def _memkv_kernel(mem_ref, g_ref, w_ref, kg_ref, grp_ref, kt_ref, vp_ref):
    x = mem_ref[0]
    ms = jnp.mean(x * x, axis=-1, keepdims=True)
    h = (x * lax.rsqrt(ms + EPS) * g_ref[...]).astype(BF16)
    kv = jnp.dot(h, w_ref[...], preferred_element_type=F32)
    km = kv[:, :MEM_W]
    vm = kv[:, MEM_W:]
    kms = _hdot(km * km, grp_ref[...]) * (1.0 / HEAD_DIM)
    kt = (km * lax.rsqrt(kms + EPS) * kg_ref[...]).T
    n_mem = x.shape[0]
    top = lax.broadcasted_iota(jnp.int32, (LANES, n_mem), 0) < HEAD_DIM
    left = lax.broadcasted_iota(jnp.int32, (n_mem, LANES), 1) < HEAD_DIM
    for hh in range(MEM_HEADS):
        pr = slice((hh // 2) * LANES, (hh // 2 + 1) * LANES)
        keep_r = top if hh % 2 == 0 else jnp.logical_not(top)
        keep_c = left if hh % 2 == 0 else jnp.logical_not(left)
        kt_ref[0, hh] = jnp.where(keep_r, kt[pr, :], 0.0).astype(BF16)
        vp_ref[0, hh] = jnp.where(keep_c, vm[:, pr], 0.0).astype(BF16)


def _memkv(mem, gain, w, kg, grp):
    b, n_mem, d = mem.shape
    full = lambda a: pl.BlockSpec(a.shape, lambda i: (0,) * a.ndim)
    return pl.pallas_call(
        _memkv_kernel,
        grid=(b,),
        in_specs=[pl.BlockSpec((1, n_mem, d), lambda i: (i, 0, 0)), full(gain), full(w), full(kg),
                  full(grp)],
        out_specs=(pl.BlockSpec((1, MEM_HEADS, LANES, n_mem), lambda i: (i, 0, 0, 0)),
                   pl.BlockSpec((1, MEM_HEADS, n_mem, LANES), lambda i: (i, 0, 0, 0))),
        out_shape=(jax.ShapeDtypeStruct((b, MEM_HEADS, LANES, n_mem), BF16),
                   jax.ShapeDtypeStruct((b, MEM_HEADS, n_mem, LANES), BF16)),
        compiler_params=_params("parallel"),
        name="memkv",
    )(mem, gain, w, kg, grp)


def _memattn_kernel(q_ref, qg_ref, grp_ref, kt_ref, vp_ref, o_ref):
    q = q_ref[...]
    qms = _hdot(q * q, grp_ref[...]) * (1.0 / HEAD_DIM)
    qn = (q * lax.rsqrt(qms + EPS) * qg_ref[...] * (HEAD_DIM ** -0.5)).astype(BF16)
    outs = []
    for pr in range(MEM_HEADS // 2):
        qp = qn[:, pr * LANES:(pr + 1) * LANES]
        acc = None
        for e in range(2):
            hh = 2 * pr + e
            l = jnp.dot(qp, kt_ref[0, hh], preferred_element_type=F32)
            l = l - jnp.max(l, axis=-1, keepdims=True)
            p = jnp.exp(l)
            p = p / jnp.sum(p, axis=-1, keepdims=True)
            o = jnp.dot(p.astype(BF16), vp_ref[0, hh], preferred_element_type=F32)
            acc = o if acc is None else acc + o
        outs.append(acc)
    o_ref[...] = jnp.concatenate(outs, axis=-1)


def _memattn(mq, qg, grp, kt, vp, *, batch, seq, tm):
    nt = seq // tm
    n_mem = kt.shape[-1]
    full = lambda a: pl.BlockSpec(a.shape, lambda b, t: (0,) * a.ndim)
    return pl.pallas_call(
        _memattn_kernel,
        grid=(batch, nt),
        in_specs=[
            pl.BlockSpec((tm, MEM_W), lambda b, t: (b * nt + t, 0)), full(qg), full(grp),
            pl.BlockSpec((1, MEM_HEADS, LANES, n_mem), lambda b, t: (b, 0, 0, 0)),
            pl.BlockSpec((1, MEM_HEADS, n_mem, LANES), lambda b, t: (b, 0, 0, 0)),
        ],
        out_specs=pl.BlockSpec((tm, MEM_W), lambda b, t: (b * nt + t, 0)),
        out_shape=jax.ShapeDtypeStruct((batch * seq, MEM_W), F32),
        compiler_params=_params("parallel", "parallel"),
        name="memattn",
    )(mq, qg, grp, kt, vp)


def _outproj_kernel(x_ref, a_ref, b_ref, c_ref, wa_ref, wb_ref, wc_ref, o_ref):
    y = _bdot(a_ref[...], wa_ref[...]) + _bdot(b_ref[...], wb_ref[...]) + _bdot(c_ref[...], wc_ref[...])
    o_ref[...] = x_ref[...] + y


def _outproj(x, a, b, c, wa, wb, wc, *, tm):
    m, d = x.shape
    row = lambda w: pl.BlockSpec((tm, w), lambda i: (i, 0))
    full = lambda arr: pl.BlockSpec(arr.shape, lambda i: (0,) * arr.ndim)
    return pl.pallas_call(
        _outproj_kernel,
        grid=(m // tm,),
        in_specs=[row(d), row(DSA_W), row(GDN_W), row(MEM_W), full(wa), full(wb), full(wc)],
        out_specs=row(d),
        out_shape=jax.ShapeDtypeStruct((m, d), F32),
        compiler_params=_params("parallel"),
        name="outproj",
    )(x, a, b, c, wa, wb, wc)


def _group_ones(width):
    idx = np.arange(width) // HEAD_DIM
    return jnp.asarray((idx[:, None] == idx[None, :]).astype(np.float32))


def _t5_bucket_np(n):
    max_exact = NUM_BUCKETS // 2
    nf = np.maximum(n, 1).astype(np.float32)
    large = max_exact + (np.log(nf / np.float32(max_exact)) / np.float32(math.log(MAX_DISTANCE / max_exact))
                         * (NUM_BUCKETS - max_exact)).astype(np.int32)
    large = np.minimum(large, NUM_BUCKETS - 1)
    return np.where(n < max_exact, n, large)


def _bias_tiles(rel_bias, kc):
    s = np.arange(kc)[:, None]
    q = np.arange(Q_BLOCK)[None, :]
    dist = (kc - Q_BLOCK) + q - s
    bucket = _t5_bucket_np(np.maximum(dist, 0))
    assert (bucket[dist >= 2 * Q_BLOCK] == NUM_BUCKETS - 1).all()
    table = rel_bias.astype(F32)
    far = table[NUM_BUCKETS - 1]
    near = jnp.asarray((dist >= 0) & (dist < 2 * Q_BLOCK))
    return jnp.transpose(jnp.where(near[..., None], table[bucket] - far, 0.0), (2, 0, 1))


def _pack_w_in(w):
    d = w.shape[0]
    o = 3 * DSA_W
    iq = w[:, o:o + IDX_HEADS * IDX_DIM]
    o += IDX_HEADS * IDX_DIM
    kw = w[:, o:o + IDX_DIM + IDX_HEADS]
    o += IDX_DIM + IDX_HEADS
    g = w[:, o:o + 4 * GDN_W]
    o += 4 * GDN_W
    ab = w[:, o:o + 2 * GDN_HEADS]
    o += 2 * GDN_HEADS
    mq = w[:, o:o + MEM_W]
    pad = lambda a: jnp.pad(a, ((0, 0), (0, LANES - a.shape[1])))
    return jnp.concatenate([w[:, :3 * DSA_W], iq, pad(kw), g, pad(ab), mq], axis=1).astype(BF16)


def _tile_heads(v, heads):
    return jnp.tile(v.astype(F32), heads).reshape(1, heads * HEAD_DIM)


def kernel(x, mem, ffn1_norm, ffn1_w_gate, ffn1_w_up, ffn1_w_down, mix_norm, w_in, dsa_q_norm, dsa_k_norm, rel_bias, gdn_conv, gdn_A_log, gdn_dt_bias, gdn_out_norm, mem_norm, w_mem_kv, mem_q_norm, mem_k_norm, w_out, ffn2_norm, ffn2_w_gate, ffn2_w_up, ffn2_w_down):
    batch, seq, d = x.shape
    depth = w_in.shape[0]
    m = batch * seq
    tm = min(512, seq)
    kc = min(512, seq)
    dff = ffn1_w_gate.shape[-1]
    tf = dff // 2 if (dff // 2) % LANES == 0 else dff

    grp_a = _group_ones(DSA_W)
    grp_m = _group_ones(MEM_W)
    bias = _bias_tiles(rel_bias, kc)
    ltri = jnp.asarray(np.tril(np.ones((kc, kc), np.float32))).astype(BF16)
    heads_of = np.arange(GDN_W) // HEAD_DIM
    ea = jnp.asarray((np.arange(LANES)[:, None] == heads_of[None, :]).astype(np.float32))
    eb = jnp.asarray((np.arange(LANES)[:, None] == heads_of[None, :] + GDN_HEADS).astype(np.float32))

    xf = x.reshape(m, d)
    for l in range(depth):
        xf = _ffn(xf, ffn1_norm[l], ffn1_w_gate[l].astype(BF16), ffn1_w_up[l].astype(BF16),
                  ffn1_w_down[l].astype(BF16), tm=tm, tf=tf)
        (q, k, vt, iq, ikb, kw, gx, ab, mq) = _inproj(
            xf, mix_norm[l].reshape(1, d), _pack_w_in(w_in[l]), grp_a,
            _tile_heads(dsa_q_norm[l], DSA_HEADS), _tile_heads(dsa_k_norm[l], DSA_HEADS), tm=tm)
        out_a = _dsa(q, iq, kw, k, vt, ikb, bias, ltri, batch=batch, seq=seq, kc=kc)
        out_b = _gdn(gx, ab, gdn_conv[l], ea, eb,
                     jnp.repeat(gdn_A_log[l].astype(F32), HEAD_DIM).reshape(1, GDN_W),
                     jnp.repeat(gdn_dt_bias[l].astype(F32), HEAD_DIM).reshape(1, GDN_W),
                     _tile_heads(gdn_out_norm[l], GDN_HEADS), grp_a, batch=batch, seq=seq, tb=tm)
        kt, vp = _memkv(mem, mem_norm[l].reshape(1, d), w_mem_kv[l].astype(BF16),
                        _tile_heads(mem_k_norm[l], MEM_HEADS), grp_m)
        out_c = _memattn(mq, _tile_heads(mem_q_norm[l], MEM_HEADS), grp_m, kt, vp,
                         batch=batch, seq=seq, tm=tm)
        wo = w_out[l].astype(BF16)
        xf = _outproj(xf, out_a, out_b, out_c, wo[:DSA_W], wo[DSA_W:DSA_W + GDN_W],
                      wo[DSA_W + GDN_W:], tm=tm)
        xf = _ffn(xf, ffn2_norm[l], ffn2_w_gate[l].astype(BF16), ffn2_w_up[l].astype(BF16),
                  ffn2_w_down[l].astype(BF16), tm=tm, tf=tf)
    return xf.reshape(batch, seq, d)
```

```python
import functools
import math

import jax
import jax.numpy as jnp
import numpy as np
from jax import lax
from jax.experimental import pallas as pl
from jax.experimental.pallas import tpu as pltpu

F32 = jnp.float32
BF16 = jnp.bfloat16

HEAD_DIM = 64
DSA_HEADS = 6
GDN_HEADS = 6
MEM_HEADS = 4
DSA_W = DSA_HEADS * HEAD_DIM
GDN_W = GDN_HEADS * HEAD_DIM
MEM_W = MEM_HEADS * HEAD_DIM
IDX_HEADS = 8
IDX_DIM = 32
TOPK_MAX = 256
Q_BLOCK = 128
GDN_CHUNK = 64
CONV_WIDTH = 4
NUM_BUCKETS = 32
MAX_DISTANCE = 128
EPS = 1e-6

LANES = 128
VMEM_LIMIT = 52 * 1024 * 1024
NEG = -1e30

SEG_A = 0
SEG_IQ = SEG_A + 3 * DSA_W
SEG_KW = SEG_IQ + IDX_HEADS * IDX_DIM
SEG_G = SEG_KW + LANES
SEG_AB = SEG_G + 4 * GDN_W
SEG_MQ = SEG_AB + LANES
IN_PACKED = SEG_MQ + MEM_W


def _bdot(a, b):
    return jnp.dot(a.astype(BF16), b.astype(BF16), preferred_element_type=F32)


def _split(x, terms):
    out = []
    for _ in range(terms - 1):
        hi = x.astype(BF16)
        out.append(hi)
        x = x - hi.astype(F32)
    out.append(x.astype(BF16))
    return out


def _dot3(a, b):
    ah, al = _split(a, 2)
    bh, bl = _split(b, 2)
    mm = lambda x, y: jnp.dot(x, y, preferred_element_type=F32)
    return mm(ah, bh) + (mm(ah, bl) + mm(al, bh))


def _dot_sel(a, sel, terms):
    selb = sel.astype(BF16)
    acc = None
    for piece in _split(a, terms):
        d = jnp.dot(piece, selb, preferred_element_type=F32)
        acc = d if acc is None else acc + d
    return acc


def _sel_dot(sel, b, terms):
    selb = sel.astype(BF16)
    acc = None
    for piece in _split(b, terms):
        d = jnp.dot(selb, piece, preferred_element_type=F32)
        acc = d if acc is None else acc + d
    return acc


def _group_sum(x, grp):
    return _dot_sel(x, grp, 2)


def _sigmoid(x):
    return 1.0 / (1.0 + jnp.exp(-x))


def _params(*sem):
    return pltpu.CompilerParams(dimension_semantics=sem, vmem_limit_bytes=VMEM_LIMIT)


def _ffn_kernel(x_ref, g_ref, wg_ref, wu_ref, wd_ref, o_ref, h_scr, acc_scr):
    f = pl.program_id(1)

    @pl.when(f == 0)
    def _():
        x = x_ref[...]
        ms = jnp.mean(x * x, axis=-1, keepdims=True)
        h_scr[...] = (x * lax.rsqrt(ms + EPS) * g_ref[...]).astype(BF16)
        acc_scr[...] = jnp.zeros_like(acc_scr)

    h = h_scr[...]
    a = jnp.dot(h, wg_ref[...], preferred_element_type=F32)
    u = jnp.dot(h, wu_ref[...], preferred_element_type=F32)
    z = (a * _sigmoid(a)) * u
    acc_scr[...] += jnp.dot(z.astype(BF16), wd_ref[...], preferred_element_type=F32)

    @pl.when(f == pl.num_programs(1) - 1)
    def _():
        o_ref[...] = x_ref[...] + 0.5 * acc_scr[...]


def _ffn(x, gain, wg, wu, wd, *, tm, tf):
    m, d = x.shape
    dff = wg.shape[1]
    return pl.pallas_call(
        _ffn_kernel,
        grid=(m // tm, dff // tf),
        in_specs=[
            pl.BlockSpec((tm, d), lambda i, f: (i, 0)),
            pl.BlockSpec((1, d), lambda i, f: (0, 0)),
            pl.BlockSpec((d, tf), lambda i, f: (0, f)),
            pl.BlockSpec((d, tf), lambda i, f: (0, f)),
            pl.BlockSpec((tf, d), lambda i, f: (f, 0)),
        ],
        out_specs=pl.BlockSpec((tm, d), lambda i, f: (i, 0)),
        out_shape=jax.ShapeDtypeStruct((m, d), F32),
        scratch_shapes=[pltpu.VMEM((tm, d), BF16), pltpu.VMEM((tm, d), F32)],
        compiler_params=_params("parallel", "arbitrary"),
        name="ffn",
    )(x, gain.reshape(1, d), wg, wu, wd)


def _inproj_kernel(x_ref, g_ref, w_ref, grp_ref, qg_ref, kg_ref,
                   q_ref, k_ref, vt_ref, iq_ref, ikb_ref, kw_ref, gx_ref, ab_ref, mq_ref):
    x = x_ref[...]
    ms = jnp.mean(x * x, axis=-1, keepdims=True)
    h = (x * lax.rsqrt(ms + EPS) * g_ref[...]).astype(BF16)
    p = jnp.dot(h, w_ref[...], preferred_element_type=F32)

    grp = grp_ref[...]
    dq = p[:, SEG_A:SEG_A + DSA_W]
    dk = p[:, SEG_A + DSA_W:SEG_A + 2 * DSA_W]
    dv = p[:, SEG_A + 2 * DSA_W:SEG_A + 3 * DSA_W]
    qms = _group_sum(dq * dq, grp) * (1.0 / HEAD_DIM)
    kms = _group_sum(dk * dk, grp) * (1.0 / HEAD_DIM)
    q_ref[...] = (dq * lax.rsqrt(qms + EPS) * qg_ref[...] * (HEAD_DIM ** -0.5)).astype(BF16)
    k_ref[...] = (dk * lax.rsqrt(kms + EPS) * kg_ref[...]).astype(BF16)
    vt_ref[...] = dv.T.astype(BF16)

    iq_ref[...] = (p[:, SEG_IQ:SEG_IQ + IDX_HEADS * IDX_DIM] * (IDX_DIM ** -0.5)).astype(BF16)
    kw = p[:, SEG_KW:SEG_KW + LANES]
    kw_ref[...] = kw
    ikb_ref[...] = kw.astype(BF16)
    gx_ref[...] = p[:, SEG_G:SEG_G + 4 * GDN_W]
    ab_ref[...] = p[:, SEG_AB:SEG_AB + LANES]
    mq_ref[...] = p[:, SEG_MQ:SEG_MQ + MEM_W]


def _inproj(x, gain, w_packed, grp, qg, kg, *, tm):
    m, d = x.shape
    row = lambda w: pl.BlockSpec((tm, w), lambda i: (i, 0))
    full = lambda a: pl.BlockSpec(a.shape, lambda i: (0,) * a.ndim)
    out_shape = (
        jax.ShapeDtypeStruct((m, DSA_W), BF16),
        jax.ShapeDtypeStruct((m, DSA_W), BF16),
        jax.ShapeDtypeStruct((DSA_W, m), BF16),
        jax.ShapeDtypeStruct((m, IDX_HEADS * IDX_DIM), BF16),
        jax.ShapeDtypeStruct((m, LANES), BF16),
        jax.ShapeDtypeStruct((m, LANES), F32),
        jax.ShapeDtypeStruct((m, 4 * GDN_W), F32),
        jax.ShapeDtypeStruct((m, LANES), F32),
        jax.ShapeDtypeStruct((m, MEM_W), F32),
    )
    out_specs = (row(DSA_W), row(DSA_W), pl.BlockSpec((DSA_W, tm), lambda i: (0, i)),
                 row(IDX_HEADS * IDX_DIM), row(LANES), row(LANES), row(4 * GDN_W),
                 row(LANES), row(MEM_W))
    return pl.pallas_call(
        _inproj_kernel,
        grid=(m // tm,),
        in_specs=[row(d), full(gain), full(w_packed), full(grp), full(qg), full(kg)],
        out_specs=out_specs,
        out_shape=out_shape,
        compiler_params=_params("parallel"),
        name="inproj",
    )(x, gain, w_packed, grp, qg, kg)


def _key_to_float(u):
    key = u ^ jnp.int32(-2 ** 31)
    bits = jnp.where(key >= 0, key, key ^ jnp.int32(0x7FFFFFFF))
    return lax.bitcast_convert_type(bits, F32)


def _trunc16(x):
    bits = lax.bitcast_convert_type(x, jnp.int32) & jnp.int32(-65536)
    return lax.bitcast_convert_type(bits, F32)


def _tree(op, x, group=8):
    parts = x.reshape(x.shape[0] // group, group, LANES)
    k = parts.shape[0]
    while k > 1:
        k //= 2
        parts = op(parts[:k], parts[k:2 * k])
    return parts[0]


def _dsa_kernel(q_ref, iq_ref, kwq_ref, k_ref, vt_ref, ik_ref, bias_ref, ltri_ref, o_ref,
                s_scr, sb_scr, iqt_scr, qpad_scr, *, kc, top_k):
    j = pl.program_id(1)
    per = kc // Q_BLOCK
    pad = kc - Q_BLOCK
    nch = j // per + 1
    q_pos = j * Q_BLOCK + lax.broadcasted_iota(jnp.int32, (1, LANES), 1)
    row_iota = lax.broadcasted_iota(jnp.int32, (kc, LANES), 0)

    def rows(i):
        return pl.ds(pl.multiple_of(j * Q_BLOCK - i * kc, Q_BLOCK), kc)

    iqt = iq_ref[...].astype(F32).T
    zpad = jnp.zeros((LANES - IDX_DIM, LANES), F32)
    for h in range(IDX_HEADS):
        iqt_scr[:, h * LANES:(h + 1) * LANES] = jnp.concatenate(
            [iqt[h * IDX_DIM:(h + 1) * IDX_DIM], zpad], axis=0).astype(BF16)
    wt = kwq_ref[...].T[IDX_DIM:IDX_DIM + IDX_HEADS, :] * (IDX_HEADS ** -0.5)
    qt = q_ref[...].astype(F32).T
    half = lax.broadcasted_iota(jnp.int32, (LANES, LANES), 0) < HEAD_DIM
    for h in range(DSA_HEADS):
        pair = qt[(h // 2) * LANES:(h // 2 + 1) * LANES]
        keep = half if h % 2 == 0 else jnp.logical_not(half)
        qpad_scr[:, h * LANES:(h + 1) * LANES] = jnp.where(keep, pair, 0.0).astype(BF16)

    def score_chunk(i, carry):
        d = jnp.dot(ik_ref[0, rows(i), :], iqt_scr[...], preferred_element_type=F32)
        acc = jnp.maximum(d[:, :LANES], 0.0) * wt[0:1, :]
        for h in range(1, IDX_HEADS):
            acc = acc + jnp.maximum(d[:, h * LANES:(h + 1) * LANES], 0.0) * wt[h:h + 1, :]
        key = row_iota + (j * Q_BLOCK - i * kc - pad)
        adm = jnp.where(key >= 0, key, q_pos + 1) <= q_pos
        sc = jnp.where(adm, acc, -jnp.inf)
        s_scr[rows(i), :] = sc
        sb_scr[rows(i), :] = _trunc16(sc).astype(BF16)
        return carry

    lax.fori_loop(0, nch, score_chunk, 0)

    def count(pred):
        def body(i, cnt):
            return cnt + _tree(jnp.add, jnp.where(pred(s_scr[rows(i), :]), 1.0, 0.0))
        cnt = lax.fori_loop(0, nch, body, jnp.zeros((8, LANES), F32))
        return jnp.sum(cnt, axis=0, keepdims=True)

    def count_hi(t):
        tb = jnp.broadcast_to(_trunc16(t), (16, LANES)).astype(BF16)

        def body(i, cnt):
            sb = sb_scr[rows(i), :].reshape(kc // 16, 16, LANES)
            hit = jnp.where(sb >= tb[None], jnp.ones((), BF16), jnp.zeros((), BF16))
            return cnt + _tree(jnp.add, hit.reshape(kc, LANES), 16).astype(F32)
        cnt = lax.fori_loop(0, nch, body, jnp.zeros((16, LANES), F32))
        return jnp.sum(cnt, axis=0, keepdims=True)

    def bit_body(b, u, counter):
        u_try = u | lax.shift_left(jnp.int32(1), 31 - b)
        cnt = counter(_key_to_float(u_try))
        return jnp.where(cnt >= top_k, u_try, u)

    u = lax.fori_loop(0, 16, lambda b, u: bit_body(b, u, count_hi), jnp.zeros((1, LANES), jnp.int32))
    u = lax.fori_loop(16, 32, lambda b, u: bit_body(b, u, lambda t: count(lambda s: s >= t)), u)
    thr = jnp.where(u == 0, jnp.float32(-3.0e38), _key_to_float(u))
    n_ge = count(lambda s: s >= thr)
    need = top_k - count(lambda s: s > thr)
    has_ties = jnp.max(n_ge) > top_k

    @pl.when(jnp.logical_not(has_ties))
    def _():
        def mask_chunk(i, carry):
            s_scr[rows(i), :] = jnp.where(s_scr[rows(i), :] >= thr, 0.0, NEG)
            return carry
        lax.fori_loop(0, nch, mask_chunk, 0)

    @pl.when(has_ties)
    def _():
        def mask_chunk(t, run):
            i = nch - 1 - t
            s = s_scr[rows(i), :]
            tie = jnp.where(s == thr, 1.0, 0.0)
            pref = jnp.dot(ltri_ref[...], tie.astype(BF16), preferred_element_type=F32) + run
            tie_sel = jnp.where(pref <= need, tie, 0.0)
            sel = jnp.where(s > thr, 1.0, tie_sel)
            s_scr[rows(i), :] = jnp.where(sel > 0.5, 0.0, NEG)
            return run + jnp.sum(tie, axis=0, keepdims=True)
        lax.fori_loop(0, nch, mask_chunk, jnp.zeros((1, LANES), F32))

    def logits(i, with_bias):
        msk = s_scr[rows(i), :]
        out = []
        for pr in range(DSA_HEADS // 2):
            l2 = jnp.dot(k_ref[0, rows(i), pr * LANES:(pr + 1) * LANES],
                         qpad_scr[:, 2 * pr * LANES:(2 * pr + 2) * LANES],
                         preferred_element_type=F32)
            for e in range(2):
                l = l2[:, e * LANES:(e + 1) * LANES] + msk
                out.append(l + bias_ref[2 * pr + e] if with_bias else l)
        return out

    def max_step(i, ms, with_bias):
        return tuple(jnp.maximum(m, _tree(jnp.maximum, l)) for m, l in zip(ms, logits(i, with_bias)))

    ms = max_step(0, tuple(jnp.full((8, LANES), NEG, F32) for _ in range(DSA_HEADS)), True)
    ms = lax.fori_loop(1, nch, lambda i, m: max_step(i, m, False), ms)
    mx = [jnp.max(m, axis=0, keepdims=True) for m in ms]

    def pv_step(i, state, with_bias):
        new = []
        for h, l in enumerate(logits(i, with_bias)):
            den, acc = state[h]
            p = jnp.exp(l - mx[h])
            vt = vt_ref[h * HEAD_DIM:(h + 1) * HEAD_DIM, rows(i)]
            new.append((den + _tree(jnp.add, p),
                        acc + jnp.dot(vt, p.astype(BF16), preferred_element_type=F32)))
        return tuple(new)

    state = pv_step(0, tuple((jnp.zeros((8, LANES), F32), jnp.zeros((HEAD_DIM, LANES), F32))
                             for _ in range(DSA_HEADS)), True)
    state = lax.fori_loop(1, nch, lambda i, st: pv_step(i, st, False), state)
    outs = [acc / jnp.sum(den, axis=0, keepdims=True) for (den, acc) in state]
    o_ref[...] = jnp.concatenate(outs, axis=0).T


def _dsa(q, iq, kw, k, vt, ikb, bias, ltri, *, batch, seq, kc):
    nb = seq // Q_BLOCK
    top_k = min(TOPK_MAX, seq // 4)
    pad = kc - Q_BLOCK
    seqp = seq + pad
    kp = jnp.pad(k.reshape(batch, seq, DSA_W), ((0, 0), (pad, 0), (0, 0)))
    ikp = jnp.pad(ikb.reshape(batch, seq, LANES), ((0, 0), (pad, 0), (0, 0)))
    vtp = jnp.pad(vt.reshape(DSA_W, batch, seq), ((0, 0), (0, 0), (pad, 0))).reshape(DSA_W, batch * seqp)
    qrow = lambda w: pl.BlockSpec((Q_BLOCK, w), lambda b, j: (b * nb + j, 0))
    return pl.pallas_call(
        functools.partial(_dsa_kernel, kc=kc, top_k=top_k),
        grid=(batch, nb),
        in_specs=[
            qrow(DSA_W), qrow(IDX_HEADS * IDX_DIM), qrow(LANES),
            pl.BlockSpec((1, seqp, DSA_W), lambda b, j: (b, 0, 0)),
            pl.BlockSpec((DSA_W, seqp), lambda b, j: (0, b)),
            pl.BlockSpec((1, seqp, LANES), lambda b, j: (b, 0, 0)),
            pl.BlockSpec(bias.shape, lambda b, j: (0, 0, 0)),
            pl.BlockSpec(ltri.shape, lambda b, j: (0, 0)),
        ],
        out_specs=qrow(DSA_W),
        out_shape=jax.ShapeDtypeStruct((batch * seq, DSA_W), F32),
        scratch_shapes=[
            pltpu.VMEM((seqp, LANES), F32),
            pltpu.VMEM((seqp, LANES), BF16),
            pltpu.VMEM((LANES, IDX_HEADS * LANES), BF16),
            pltpu.VMEM((LANES, DSA_HEADS * LANES), BF16),
        ],
        compiler_params=_params("arbitrary", "arbitrary"),
        name="dsa",
    )(q, iq, kw, kp, vtp, ikp, bias, ltri)


GDN_GROUP = 2


def _gdn_kernel(x_ref, ab_ref, cw_ref, ea_ref, eb_ref, alog_ref, dtb_ref, gain_ref, grp_ref,
                o_ref, xpad_scr, qkv_scr, gb_scr, gc_scr, gr_scr, pre_scr, st_scr, o_scr, *, tb):
    t = pl.program_id(1)
    cin = 3 * GDN_W
    n_chunks = tb // GDN_CHUNK
    pairs = range(GDN_HEADS // 2)
    lanes = lambda p: slice(p * LANES, (p + 1) * LANES)

    @pl.when(t == 0)
    def _():
        xpad_scr[0:8, :] = jnp.zeros((8, cin), F32)
        st_scr[...] = jnp.zeros_like(st_scr)

    x = x_ref[...]
    xpad_scr[8:8 + tb, :] = x[:, :cin]
    conv = jnp.zeros((tb, cin), F32)
    for jj in range(CONV_WIDTH):
        conv = conv + cw_ref[jj:jj + 1, :] * xpad_scr[pl.ds(8 - (CONV_WIDTH - 1) + jj, tb), :]
    xpad_scr[0:8, :] = x[tb - 8:tb, :cin]
    qkv = conv * _sigmoid(conv)
    grp = grp_ref[...]
    q = qkv[:, :GDN_W]
    k = qkv[:, GDN_W:2 * GDN_W]
    qkv_scr[:, :GDN_W] = q * lax.rsqrt(_group_sum(q * q, grp) + EPS) * (HEAD_DIM ** -0.5)
    qkv_scr[:, GDN_W:2 * GDN_W] = k * lax.rsqrt(_group_sum(k * k, grp) + EPS)
    qkv_scr[:, 2 * GDN_W:] = qkv[:, 2 * GDN_W:]

    ab = ab_ref[...]
    a_e = _dot_sel(ab, ea_ref[...], 3) + dtb_ref[...]
    b_e = _dot_sel(ab, eb_ref[...], 3)
    softplus = jnp.maximum(a_e, 0.0) + jnp.log(1.0 + jnp.exp(-jnp.abs(a_e)))
    gb_scr[:, :GDN_W] = -jnp.exp(alog_ref[...]) * softplus
    gb_scr[:, GDN_W:] = _sigmoid(b_e)

    r64 = lax.broadcasted_iota(jnp.int32, (GDN_CHUNK, LANES), 0)
    c64 = lax.broadcasted_iota(jnp.int32, (GDN_CHUNK, LANES), 1) % GDN_CHUNK
    causal = c64 <= r64
    strict = c64 < r64
    eye = jnp.where(c64 == r64, 1.0, 0.0)
    r128 = lax.broadcasted_iota(jnp.int32, (LANES, LANES), 0)
    c128 = lax.broadcasted_iota(jnp.int32, (LANES, LANES), 1)
    bdmask = (r128 // HEAD_DIM) == (c128 // HEAD_DIM)

    def bd(m):
        return jnp.where(bdmask, jnp.concatenate([m, m], axis=0), 0.0)

    lt_r = lax.broadcasted_iota(jnp.int32, (GDN_CHUNK, GDN_CHUNK), 0)
    lt_c = lax.broadcasted_iota(jnp.int32, (GDN_CHUNK, GDN_CHUNK), 1)
    ltri = jnp.where(lt_c <= lt_r, 1.0, 0.0)
    ones8 = jnp.ones((8, GDN_CHUNK), F32)
    up_r = lax.broadcasted_iota(jnp.int32, (GDN_CHUNK, GDN_W), 0)
    up_c = lax.broadcasted_iota(jnp.int32, (GDN_CHUNK, GDN_W), 1) % GDN_CHUNK
    upper = jnp.where(up_r <= up_c, 1.0, 0.0)
    for ci in range(n_chunks):
        g = gb_scr[ci * GDN_CHUNK:(ci + 1) * GDN_CHUNK, :GDN_W]
        gc_scr[ci * GDN_CHUNK:(ci + 1) * GDN_CHUNK, :] = _sel_dot(ltri, g, 3)
        gr_scr[ci * 8:(ci + 1) * 8, :] = _sel_dot(ones8, g * upper, 3)

    def par_body(gi, carry):
        inst = [(u, p) for u in range(GDN_GROUP) for p in pairs]
        rows = [pl.ds(pl.multiple_of((gi * GDN_GROUP + u) * GDN_CHUNK, GDN_CHUNK), GDN_CHUNK)
                for u in range(GDN_GROUP)]
        grow = [pl.ds(pl.multiple_of((gi * GDN_GROUP + u) * 8, 8), 8) for u in range(GDN_GROUP)]
        off = lambda s, p: slice(s * GDN_W + p * LANES, s * GDN_W + (p + 1) * LANES)
        qp = [qkv_scr[rows[u], off(0, p)] for u, p in inst]
        kp = [qkv_scr[rows[u], off(1, p)] for u, p in inst]
        vp = [qkv_scr[rows[u], off(2, p)] for u, p in inst]
        beta = [gb_scr[rows[u], off(1, p)] for u, p in inst]
        gc = [gc_scr[rows[u], lanes(p)] for u, p in inst]
        gr = [gr_scr[grow[u], lanes(p)][0:1, :] for u, p in inst]
        dmat = [jnp.exp(jnp.where(causal, a - b, NEG)) for a, b in zip(gc, gr)]
        kbd = [jnp.where(bdmask, jnp.concatenate([a, a], axis=0).T, 0.0) for a in kp]
        kb = [a * b for a, b in zip(kp, beta)]
        kk = [_dot3(a, b) for a, b in zip(kb, kbd)]
        qk = [_dot3(a, b) for a, b in zip(qp, kbd)]
        lm = [jnp.where(strict, a * d, 0.0) for a, d in zip(kk, dmat)]
        aintra = [a * d for a, d in zip(qk, dmat)]
        tinv = [eye - a for a in lm]
        lpow = lm
        for _ in range(5):
            lpow = [_dot3(a, bd(a)) for a in lpow]
            tinv = [a + _dot3(a, bd(b)) for a, b in zip(tinv, lpow)]
        egc = [jnp.exp(a) for a in gc]
        un = [_dot3(a, bd(v * b)) for a, v, b in zip(tinv, vp, beta)]
        wn = [_dot3(a, bd(b * e)) for a, b, e in zip(tinv, kb, egc)]
        for n, (u, p) in enumerate(inst):
            pre_scr[0, rows[u], lanes(p)] = un[n]
            pre_scr[1, rows[u], lanes(p)] = wn[n]
            pre_scr[2, rows[u], lanes(p)] = aintra[n]
            pre_scr[3, rows[u], lanes(p)] = qp[n] * egc[n]
            pre_scr[4, rows[u], lanes(p)] = kp[n] * jnp.exp(gc[n][GDN_CHUNK - 1:GDN_CHUNK, :] - gc[n])
        return carry

    lax.fori_loop(0, n_chunks // GDN_GROUP, par_body, 0)

    def rec_body(ci, carry):
        rows = pl.ds(pl.multiple_of(ci * GDN_CHUNK, GDN_CHUNK), GDN_CHUNK)
        tail = gc_scr[pl.ds(pl.multiple_of(ci * GDN_CHUNK + GDN_CHUNK - 8, 8), 8), :]
        egl = jnp.exp(tail[7:8, :])
        sbd = [st_scr[p] for p in pairs]
        un, wn, aintra, qe, kd = [[pre_scr[s, rows, lanes(p)] for p in pairs] for s in range(5)]
        ws = [_dot3(a, s) for a, s in zip(wn, sbd)]
        qs = [_dot3(a, s) for a, s in zip(qe, sbd)]
        v_new = [a - b for a, b in zip(un, ws)]
        o = [a + _dot3(b, bd(v)) for a, b, v in zip(qs, aintra, v_new)]
        upd = [_dot3(a.T, v) for a, v in zip(kd, v_new)]
        for p in pairs:
            st_scr[p] = sbd[p] * egl[:, lanes(p)] + jnp.where(bdmask, upd[p], 0.0)
            o_scr[rows, lanes(p)] = o[p]
        return carry

    lax.fori_loop(0, n_chunks, rec_body, 0)

    o = o_scr[...]
    oms = _group_sum(o * o, grp) * (1.0 / HEAD_DIM)
    z = x[:, cin:]
    o_ref[...] = o * lax.rsqrt(oms + EPS) * gain_ref[...] * (z * _sigmoid(z))


def _gdn(gx, ab, conv_w, ea, eb, alog, dtb, gain, grp, *, batch, seq, tb):
    nt = seq // tb
    cin = 3 * GDN_W
    row = lambda w: pl.BlockSpec((tb, w), lambda b, t: (b * nt + t, 0))
    full = lambda a: pl.BlockSpec(a.shape, lambda b, t: (0,) * a.ndim)
    return pl.pallas_call(
        functools.partial(_gdn_kernel, tb=tb),
        grid=(batch, nt),
        in_specs=[row(4 * GDN_W), row(LANES), full(conv_w), full(ea), full(eb), full(alog),
                  full(dtb), full(gain), full(grp)],
        out_specs=row(GDN_W),
        out_shape=jax.ShapeDtypeStruct((batch * seq, GDN_W), F32),
        scratch_shapes=[
            pltpu.VMEM((tb + 8, cin), F32),
            pltpu.VMEM((tb, cin), F32),
            pltpu.VMEM((tb, 2 * GDN_W), F32),
            pltpu.VMEM((tb, GDN_W), F32),
            pltpu.VMEM((tb // 8, GDN_W), F32),
            pltpu.VMEM((5, tb, GDN_W), F32),
            pltpu.VMEM((GDN_HEADS // 2, LANES, LANES), F32),
            pltpu.VMEM((tb, GDN_W), F32),
        ],
        compiler_params=_params("arbitrary", "arbitrary"),
        name="gdn",
    )(gx, ab, conv_w, ea, eb, alog, dtb, gain, grp)


def _memkv_kernel(mem_ref, g_ref, w_ref, kg_ref, grp_ref, kt_ref, vp_ref):
    x = mem_ref[0]
    ms = jnp.mean(x * x, axis=-1, keepdims=True)
    h = (x * lax.rsqrt(ms + EPS) * g_ref[...]).astype(BF16)
    kv = jnp.dot(h, w_ref[...], preferred_element_type=F32)
    km = kv[:, :MEM_W]
    vm = kv[:, MEM_W:]
    kms = _group_sum(km * km, grp_ref[...]) * (1.0 / HEAD_DIM)
    kt = (km * lax.rsqrt(kms + EPS) * kg_ref[...]).T
    n_mem = x.shape[0]
    top = lax.broadcasted_iota(jnp.int32, (LANES, n_mem), 0) < HEAD_DIM
    left = lax.broadcasted_iota(jnp.int32, (n_mem, LANES), 1) < HEAD_DIM
    for hh in range(MEM_HEADS):
        pr = slice((hh // 2) * LANES, (hh // 2 + 1) * LANES)
        keep_r = top if hh % 2 == 0 else jnp.logical_not(top)
        keep_c = left if hh % 2 == 0 else jnp.logical_not(left)
        kt_ref[0, hh] = jnp.where(keep_r, kt[pr, :], 0.0).astype(BF16)
        vp_ref[0, hh] = jnp.where(keep_c, vm[:, pr], 0.0).astype(BF16)


def _memkv(mem, gain, w, kg, grp):
    b, n_mem, d = mem.shape
    full = lambda a: pl.BlockSpec(a.shape, lambda i: (0,) * a.ndim)
    return pl.pallas_call(
        _memkv_kernel,
        grid=(b,),
        in_specs=[pl.BlockSpec((1, n_mem, d), lambda i: (i, 0, 0)), full(gain), full(w), full(kg),
                  full(grp)],
        out_specs=(pl.BlockSpec((1, MEM_HEADS, LANES, n_mem), lambda i: (i, 0, 0, 0)),
                   pl.BlockSpec((1, MEM_HEADS, n_mem, LANES), lambda i: (i, 0, 0, 0))),
        out_shape=(jax.ShapeDtypeStruct((b, MEM_HEADS, LANES, n_mem), BF16),
                   jax.ShapeDtypeStruct((b, MEM_HEADS, n_mem, LANES), BF16)),
        compiler_params=_params("parallel"),
        name="memkv",
    )(mem, gain, w, kg, grp)


def _memattn_kernel(q_ref, qg_ref, grp_ref, kt_ref, vp_ref, o_ref):
    q = q_ref[...]
    qms = _group_sum(q * q, grp_ref[...]) * (1.0 / HEAD_DIM)
    qn = (q * lax.rsqrt(qms + EPS) * qg_ref[...] * (HEAD_DIM ** -0.5)).astype(BF16)
    outs = []
    for pr in range(MEM_HEADS // 2):
        qp = qn[:, pr * LANES:(pr + 1) * LANES]
        acc = None
        for e in range(2):
            hh = 2 * pr + e
            l = jnp.dot(qp, kt_ref[0, hh], preferred_element_type=F32)
            l = l - jnp.max(l, axis=-1, keepdims=True)
            p = jnp.exp(l)
            p = p / jnp.sum(p, axis=-1, keepdims=True)
            o = jnp.dot(p.astype(BF16), vp_ref[0, hh], preferred_element_type=F32)
            acc = o if acc is None else acc + o
        outs.append(acc)
    o_ref[...] = jnp.concatenate(outs, axis=-1)


def _memattn(mq, qg, grp, kt, vp, *, batch, seq, tm):
    nt = seq // tm
    n_mem = kt.shape[-1]
    full = lambda a: pl.BlockSpec(a.shape, lambda b, t: (0,) * a.ndim)
    return pl.pallas_call(
        _memattn_kernel,
        grid=(batch, nt),
        in_specs=[
            pl.BlockSpec((tm, MEM_W), lambda b, t: (b * nt + t, 0)), full(qg), full(grp),
            pl.BlockSpec((1, MEM_HEADS, LANES, n_mem), lambda b, t: (b, 0, 0, 0)),
            pl.BlockSpec((1, MEM_HEADS, n_mem, LANES), lambda b, t: (b, 0, 0, 0)),
        ],
        out_specs=pl.BlockSpec((tm, MEM_W), lambda b, t: (b * nt + t, 0)),
        out_shape=jax.ShapeDtypeStruct((batch * seq, MEM_W), F32),
        compiler_params=_params("parallel", "parallel"),
        name="memattn",
    )(mq, qg, grp, kt, vp)


def _outproj_kernel(x_ref, a_ref, b_ref, c_ref, wa_ref, wb_ref, wc_ref, o_ref):
    y = _bdot(a_ref[...], wa_ref[...]) + _bdot(b_ref[...], wb_ref[...]) + _bdot(c_ref[...], wc_ref[...])
    o_ref[...] = x_ref[...] + y


def _outproj(x, a, b, c, wa, wb, wc, *, tm):
    m, d = x.shape
    row = lambda w: pl.BlockSpec((tm, w), lambda i: (i, 0))
    full = lambda arr: pl.BlockSpec(arr.shape, lambda i: (0,) * arr.ndim)
    return pl.pallas_call(
        _outproj_kernel,
        grid=(m // tm,),
        in_specs=[row(d), row(DSA_W), row(GDN_W), row(MEM_W), full(wa), full(wb), full(wc)],
        out_specs=row(d),
        out_shape=jax.ShapeDtypeStruct((m, d), F32),
        compiler_params=_params("parallel"),
        name="outproj",
    )(x, a, b, c, wa, wb, wc)


def _group_ones(width):
    idx = np.arange(width) // HEAD_DIM
    return jnp.asarray((idx[:, None] == idx[None, :]).astype(np.float32))


def _t5_bucket_np(n):
    max_exact = NUM_BUCKETS // 2
    nf = np.maximum(n, 1).astype(np.float32)
    large = max_exact + (np.log(nf / np.float32(max_exact)) / np.float32(math.log(MAX_DISTANCE / max_exact))
                         * (NUM_BUCKETS - max_exact)).astype(np.int32)
    large = np.minimum(large, NUM_BUCKETS - 1)
    return np.where(n < max_exact, n, large)


def _bias_tiles(rel_bias, kc):
    s = np.arange(kc)[:, None]
    q = np.arange(Q_BLOCK)[None, :]
    dist = (kc - Q_BLOCK) + q - s
    bucket = _t5_bucket_np(np.maximum(dist, 0))
    assert (bucket[dist >= 2 * Q_BLOCK] == NUM_BUCKETS - 1).all()
    table = rel_bias.astype(F32)
    far = table[NUM_BUCKETS - 1]
    near = jnp.asarray((dist >= 0) & (dist < 2 * Q_BLOCK))
    return jnp.transpose(jnp.where(near[..., None], table[bucket] - far, 0.0), (2, 0, 1))


def _pack_w_in(w):
    d = w.shape[0]
    o = 3 * DSA_W
    iq = w[:, o:o + IDX_HEADS * IDX_DIM]
    o += IDX_HEADS * IDX_DIM
    kw = w[:, o:o + IDX_DIM + IDX_HEADS]
    o += IDX_DIM + IDX_HEADS
    g = w[:, o:o + 4 * GDN_W]
    o += 4 * GDN_W
    ab = w[:, o:o + 2 * GDN_HEADS]
    o += 2 * GDN_HEADS
    mq = w[:, o:o + MEM_W]
    pad = lambda a: jnp.pad(a, ((0, 0), (0, LANES - a.shape[1])))
    return jnp.concatenate([w[:, :3 * DSA_W], iq, pad(kw), g, pad(ab), mq], axis=1).astype(BF16)


def _tile_heads(v, heads):
    return jnp.tile(v.astype(F32), heads).reshape(1, heads * HEAD_DIM)


def kernel(x, mem, ffn1_norm, ffn1_w_gate, ffn1_w_up, ffn1_w_down, mix_norm, w_in, dsa_q_norm, dsa_k_norm, rel_bias, gdn_conv, gdn_A_log, gdn_dt_bias, gdn_out_norm, mem_norm, w_mem_kv, mem_q_norm, mem_k_norm, w_out, ffn2_norm, ffn2_w_gate, ffn2_w_up, ffn2_w_down):
    batch, seq, d = x.shape
    depth = w_in.shape[0]
    m = batch * seq
    tm = min(512, seq)
    kc = min(512, seq)
    dff = ffn1_w_gate.shape[-1]
    tf = dff // 2 if (dff // 2) % LANES == 0 else dff

    grp_a = _group_ones(DSA_W)
    grp_m = _group_ones(MEM_W)
    bias = _bias_tiles(rel_bias, kc)
    ltri = jnp.asarray(np.tril(np.ones((kc, kc), np.float32))).astype(BF16)
    heads_of = np.arange(GDN_W) // HEAD_DIM
    ea = jnp.asarray((np.arange(LANES)[:, None] == heads_of[None, :]).astype(np.float32))
    eb = jnp.asarray((np.arange(LANES)[:, None] == heads_of[None, :] + GDN_HEADS).astype(np.float32))

    xf = x.reshape(m, d)
    for l in range(depth):
        xf = _ffn(xf, ffn1_norm[l], ffn1_w_gate[l].astype(BF16), ffn1_w_up[l].astype(BF16),
                  ffn1_w_down[l].astype(BF16), tm=tm, tf=tf)
        (q, k, vt, iq, ikb, kw, gx, ab, mq) = _inproj(
            xf, mix_norm[l].reshape(1, d), _pack_w_in(w_in[l]), grp_a,
            _tile_heads(dsa_q_norm[l], DSA_HEADS), _tile_heads(dsa_k_norm[l], DSA_HEADS), tm=tm)
        out_a = _dsa(q, iq, kw, k, vt, ikb, bias, ltri, batch=batch, seq=seq, kc=kc)
        out_b = _gdn(gx, ab, gdn_conv[l], ea, eb,
                     jnp.repeat(gdn_A_log[l].astype(F32), HEAD_DIM).reshape(1, GDN_W),
                     jnp.repeat(gdn_dt_bias[l].astype(F32), HEAD_DIM).reshape(1, GDN_W),
                     _tile_heads(gdn_out_norm[l], GDN_HEADS), grp_a, batch=batch, seq=seq, tb=tm)
        kt, vp = _memkv(mem, mem_norm[l].reshape(1, d), w_mem_kv[l].astype(BF16),
                        _tile_heads(mem_k_norm[l], MEM_HEADS), grp_m)
        out_c = _memattn(mq, _tile_heads(mem_q_norm[l], MEM_HEADS), grp_m, kt, vp,
                         batch=batch, seq=seq, tm=tm)
        wo = w_out[l].astype(BF16)
        xf = _outproj(xf, out_a, out_b, out_c, wo[:DSA_W], wo[DSA_W:DSA_W + GDN_W],
                      wo[DSA_W + GDN_W:], tm=tm)
        xf = _ffn(xf, ffn2_norm[l], ffn2_w_gate[l].astype(BF16), ffn2_w_up[l].astype(BF16),
                  ffn2_w_down[l].astype(BF16), tm=tm, tf=tf)
    return xf.reshape(batch, seq, d)
```

```python
import functools
import math

import jax
import jax.numpy as jnp
import numpy as np
from jax import lax
from jax.experimental import pallas as pl
from jax.experimental.pallas import tpu as pltpu

F32 = jnp.float32
BF16 = jnp.bfloat16

HEAD_DIM = 64
DSA_HEADS = 6
GDN_HEADS = 6
MEM_HEADS = 4
DSA_W = DSA_HEADS * HEAD_DIM
GDN_W = GDN_HEADS * HEAD_DIM
MEM_W = MEM_HEADS * HEAD_DIM
IDX_HEADS = 8
IDX_DIM = 32
TOPK_MAX = 256
Q_BLOCK = 128
GDN_CHUNK = 64
CONV_WIDTH = 4
NUM_BUCKETS = 32
MAX_DISTANCE = 128
EPS = 1e-6

LANES = 128
VMEM_LIMIT = 52 * 1024 * 1024
NEG = -1e30
LOG2E = math.log2(math.e)
V_ROWS = HEAD_DIM + 16
MAX_LOGIT_SPREAD = 80.0

SEG_A = 0
SEG_IQ = SEG_A + 3 * DSA_W
SEG_KW = SEG_IQ + IDX_HEADS * IDX_DIM
SEG_G = SEG_KW + LANES
SEG_AB = SEG_G + 4 * GDN_W
SEG_MQ = SEG_AB + LANES
IN_PACKED = SEG_MQ + MEM_W


def _bdot(a, b):
    return jnp.dot(a.astype(BF16), b.astype(BF16), preferred_element_type=F32)


def _split(x, terms):
    out = []
    for _ in range(terms - 1):
        hi = x.astype(BF16)
        out.append(hi)
        x = x - hi.astype(F32)
    out.append(x.astype(BF16))
    return out


def _dot3(a, b):
    ah, al = _split(a, 2)
    bh, bl = _split(b, 2)
    mm = lambda x, y: jnp.dot(x, y, preferred_element_type=F32)
    return mm(ah, bh) + (mm(ah, bl) + mm(al, bh))


def _dot_sel(a, sel, terms):
    selb = sel.astype(BF16)
    acc = None
    for piece in _split(a, terms):
        d = jnp.dot(piece, selb, preferred_element_type=F32)
        acc = d if acc is None else acc + d
    return acc


def _sel_dot(sel, b, terms):
    selb = sel.astype(BF16)
    acc = None
    for piece in _split(b, terms):
        d = jnp.dot(selb, piece, preferred_element_type=F32)
        acc = d if acc is None else acc + d
    return acc


def _group_sum(x, grp):
    return _dot_sel(x, grp, 2)


def _sigmoid(x):
    return 1.0 / (1.0 + jnp.exp(-x))


def _params(*sem):
    return pltpu.CompilerParams(dimension_semantics=sem, vmem_limit_bytes=VMEM_LIMIT)


def _ffn_kernel(x_ref, g_ref, wg_ref, wu_ref, wd_ref, o_ref, h_scr, acc_scr):
    f = pl.program_id(1)

    @pl.when(f == 0)
    def _():
        x = x_ref[...]
        ms = jnp.mean(x * x, axis=-1, keepdims=True)
        h_scr[...] = (x * lax.rsqrt(ms + EPS) * g_ref[...]).astype(BF16)
        acc_scr[...] = jnp.zeros_like(acc_scr)

    h = h_scr[...]
    a = jnp.dot(h, wg_ref[...], preferred_element_type=F32)
    u = jnp.dot(h, wu_ref[...], preferred_element_type=F32)
    z = (a * _sigmoid(a)) * u
    acc_scr[...] += jnp.dot(z.astype(BF16), wd_ref[...], preferred_element_type=F32)

    @pl.when(f == pl.num_programs(1) - 1)
    def _():
        o_ref[...] = x_ref[...] + 0.5 * acc_scr[...]


def _ffn(x, gain, wg, wu, wd, *, tm, tf):
    m, d = x.shape
    dff = wg.shape[1]
    return pl.pallas_call(
        _ffn_kernel,
        grid=(m // tm, dff // tf),
        in_specs=[
            pl.BlockSpec((tm, d), lambda i, f: (i, 0)),
            pl.BlockSpec((1, d), lambda i, f: (0, 0)),
            pl.BlockSpec((d, tf), lambda i, f: (0, f)),
            pl.BlockSpec((d, tf), lambda i, f: (0, f)),
            pl.BlockSpec((tf, d), lambda i, f: (f, 0)),
        ],
        out_specs=pl.BlockSpec((tm, d), lambda i, f: (i, 0)),
        out_shape=jax.ShapeDtypeStruct((m, d), F32),
        scratch_shapes=[pltpu.VMEM((tm, d), BF16), pltpu.VMEM((tm, d), F32)],
        compiler_params=_params("parallel", "arbitrary"),
        name="ffn",
    )(x, gain.reshape(1, d), wg, wu, wd)


def _inproj_kernel(x_ref, g_ref, w_ref, grp_ref, qg_ref, kg_ref,
                   q_ref, k_ref, vt_ref, iq_ref, ikb_ref, kw_ref, gx_ref, ab_ref, mq_ref):
    x = x_ref[...]
    ms = jnp.mean(x * x, axis=-1, keepdims=True)
    h = (x * lax.rsqrt(ms + EPS) * g_ref[...]).astype(BF16)
    p = jnp.dot(h, w_ref[...], preferred_element_type=F32)

    grp = grp_ref[...]
    dq = p[:, SEG_A:SEG_A + DSA_W]
    dk = p[:, SEG_A + DSA_W:SEG_A + 2 * DSA_W]
    dv = p[:, SEG_A + 2 * DSA_W:SEG_A + 3 * DSA_W]
    qms = _group_sum(dq * dq, grp) * (1.0 / HEAD_DIM)
    kms = _group_sum(dk * dk, grp) * (1.0 / HEAD_DIM)
    q_ref[...] = (dq * lax.rsqrt(qms + EPS) * qg_ref[...] * (HEAD_DIM ** -0.5 * LOG2E)).astype(BF16)
    k_ref[...] = (dk * lax.rsqrt(kms + EPS) * kg_ref[...]).astype(BF16)
    dvt = dv.T
    ones = jnp.ones((V_ROWS - HEAD_DIM, dvt.shape[1]), F32)
    for hd in range(DSA_HEADS):
        vt_ref[hd] = jnp.concatenate([dvt[hd * HEAD_DIM:(hd + 1) * HEAD_DIM], ones], axis=0).astype(BF16)

    iq_ref[...] = (p[:, SEG_IQ:SEG_IQ + IDX_HEADS * IDX_DIM] * (IDX_DIM ** -0.5)).astype(BF16)
    kw = p[:, SEG_KW:SEG_KW + LANES]
    kw_ref[...] = kw
    ikb_ref[...] = kw.astype(BF16)
    gx_ref[...] = p[:, SEG_G:SEG_G + 4 * GDN_W]
    ab_ref[...] = p[:, SEG_AB:SEG_AB + LANES]
    mq_ref[...] = p[:, SEG_MQ:SEG_MQ + MEM_W]


def _inproj(x, gain, w_packed, grp, qg, kg, *, tm):
    m, d = x.shape
    row = lambda w: pl.BlockSpec((tm, w), lambda i: (i, 0))
    full = lambda a: pl.BlockSpec(a.shape, lambda i: (0,) * a.ndim)
    out_shape = (
        jax.ShapeDtypeStruct((m, DSA_W), BF16),
        jax.ShapeDtypeStruct((m, DSA_W), BF16),
        jax.ShapeDtypeStruct((DSA_HEADS, V_ROWS, m), BF16),
        jax.ShapeDtypeStruct((m, IDX_HEADS * IDX_DIM), BF16),
        jax.ShapeDtypeStruct((m, LANES), BF16),
        jax.ShapeDtypeStruct((m, LANES), F32),
        jax.ShapeDtypeStruct((m, 4 * GDN_W), F32),
        jax.ShapeDtypeStruct((m, LANES), F32),
        jax.ShapeDtypeStruct((m, MEM_W), F32),
    )
    out_specs = (row(DSA_W), row(DSA_W), pl.BlockSpec((DSA_HEADS, V_ROWS, tm), lambda i: (0, 0, i)),
                 row(IDX_HEADS * IDX_DIM), row(LANES), row(LANES), row(4 * GDN_W),
                 row(LANES), row(MEM_W))
    return pl.pallas_call(
        _inproj_kernel,
        grid=(m // tm,),
        in_specs=[row(d), full(gain), full(w_packed), full(grp), full(qg), full(kg)],
        out_specs=out_specs,
        out_shape=out_shape,
        compiler_params=_params("parallel"),
        name="inproj",
    )(x, gain, w_packed, grp, qg, kg)


def _key_to_float(u):
    key = u ^ jnp.int32(-2 ** 31)
    bits = jnp.where(key >= 0, key, key ^ jnp.int32(0x7FFFFFFF))
    return lax.bitcast_convert_type(bits, F32)


def _tree(op, x, group=8):
    parts = x.reshape(x.shape[0] // group, group, LANES)
    k = parts.shape[0]
    while k > 1:
        k //= 2
        parts = op(parts[:k], parts[k:2 * k])
    return parts[0]


def _dsa_kernel(q_ref, iq_ref, kwq_ref, k_ref, vt_ref, ik_ref, tab_ref, ltri_ref, o_ref,
                s_scr, iqt_scr, qpad_scr, bias_scr, mx_scr, brange_scr, *, kc, top_k):
    j = pl.program_id(1)
    per = kc // Q_BLOCK
    pad = kc - Q_BLOCK
    nch = j // per + 1
    q_pos = j * Q_BLOCK + lax.broadcasted_iota(jnp.int32, (1, LANES), 1)
    row_iota = lax.broadcasted_iota(jnp.int32, (kc, LANES), 0)

    def rows(i):
        return pl.ds(pl.multiple_of(j * Q_BLOCK - i * kc, Q_BLOCK), kc)

    @pl.when((pl.program_id(0) == 0) & (j == 0))
    def _():
        r = lax.broadcasted_iota(jnp.int32, (2 * Q_BLOCK, LANES), 0)
        c = lax.broadcasted_iota(jnp.int32, (2 * Q_BLOCK, LANES), 1)
        dist = Q_BLOCK + c - r
        bucket = jnp.zeros_like(dist)
        for first in _BUCKET_STARTS:
            bucket = bucket + jnp.where(dist >= first, 1, 0)
        for h in range(DSA_HEADS):
            far = tab_ref[NUM_BUCKETS - 1, h]
            delta = jnp.zeros((2 * Q_BLOCK, LANES), F32)
            b_hi = jnp.float32(0.0)
            b_lo = jnp.float32(0.0)
            for b in range(NUM_BUCKETS - 1):
                delta = jnp.where(bucket == b, tab_ref[b, h] - far, delta)
                b_hi = jnp.maximum(b_hi, (tab_ref[b, h] - far) * LOG2E)
                b_lo = jnp.minimum(b_lo, (tab_ref[b, h] - far) * LOG2E)
            bias_scr[h] = jnp.where(dist >= 0, delta * LOG2E, 0.0)
            brange_scr[0, h] = b_hi
            brange_scr[1, h] = b_lo

    iqt = iq_ref[...].astype(F32).T
    zpad = jnp.zeros((LANES - IDX_DIM, LANES), F32)
    for h in range(IDX_HEADS):
        iqt_scr[:, h * LANES:(h + 1) * LANES] = jnp.concatenate(
            [iqt[h * IDX_DIM:(h + 1) * IDX_DIM], zpad], axis=0).astype(BF16)
    wt = kwq_ref[...].T[IDX_DIM:IDX_DIM + IDX_HEADS, :] * (IDX_HEADS ** -0.5)
    qt = q_ref[...].astype(F32).T
    half = lax.broadcasted_iota(jnp.int32, (LANES, LANES), 0) < HEAD_DIM
    for h in range(DSA_HEADS):
        pair = qt[(h // 2) * LANES:(h // 2 + 1) * LANES]
        keep = half if h % 2 == 0 else jnp.logical_not(half)
        qpad_scr[:, h * LANES:(h + 1) * LANES] = jnp.where(keep, pair, 0.0).astype(BF16)

    kmax = tab_ref[NUM_BUCKETS, 0]
    spread = jnp.zeros((1, LANES), F32)
    for h in range(DSA_HEADS):
        b_hi = brange_scr[0, h]
        b_lo = brange_scr[1, h]
        qh = qt[h * HEAD_DIM:(h + 1) * HEAD_DIM]
        bound = jnp.sqrt(jnp.sum(qh * qh, axis=0, keepdims=True)) * kmax
        mx_scr[h:h + 1, :] = bound + b_hi
        spread = jnp.maximum(spread, 2.0 * bound + (b_hi - b_lo))
    bound_ok = jnp.max(spread) <= MAX_LOGIT_SPREAD

    def score_chunk(i, carry):
        d = jnp.dot(ik_ref[0, rows(i), :], iqt_scr[...], preferred_element_type=F32)
        acc = jnp.maximum(d[:, :LANES], 0.0) * wt[0:1, :]
        for h in range(1, IDX_HEADS):
            acc = acc + jnp.maximum(d[:, h * LANES:(h + 1) * LANES], 0.0) * wt[h:h + 1, :]
        key = row_iota + (j * Q_BLOCK - i * kc - pad)
        adm = jnp.where(key >= 0, key, q_pos + 1) <= q_pos
        s_scr[rows(i), :] = jnp.where(adm, acc, -jnp.inf)
        return carry

    lax.fori_loop(0, nch, score_chunk, 0)

    def count(pred):
        def body(i, cnt):
            return cnt + _tree(jnp.add, jnp.where(pred(s_scr[rows(i), :]), 1.0, 0.0))
        cnt = lax.fori_loop(0, nch, body, jnp.zeros((8, LANES), F32))
        return jnp.sum(cnt, axis=0, keepdims=True)

    def bit_body(b, u):
        u_try = u | lax.shift_left(jnp.int32(1), 31 - b)
        t = _key_to_float(u_try)
        cnt = count(lambda s: s >= t)
        return jnp.where(cnt >= top_k, u_try, u)

    u = lax.fori_loop(0, 32, bit_body, jnp.zeros((1, LANES), jnp.int32))
    thr = jnp.where(u == 0, jnp.float32(-3.0e38), _key_to_float(u))
    n_ge = count(lambda s: s >= thr)
    need = top_k - count(lambda s: s > thr)
    has_ties = jnp.max(n_ge) > top_k

    @pl.when(jnp.logical_not(has_ties))
    def _():
        def mask_chunk(i, carry):
            s_scr[rows(i), :] = jnp.where(s_scr[rows(i), :] >= thr, 0.0, NEG)
            return carry
        lax.fori_loop(0, nch, mask_chunk, 0)

    @pl.when(has_ties)
    def _():
        def mask_chunk(t, run):
            i = nch - 1 - t
            s = s_scr[rows(i), :]
            tie = jnp.where(s == thr, 1.0, 0.0)
            pref = jnp.dot(ltri_ref[...], tie.astype(BF16), preferred_element_type=F32) + run
            tie_sel = jnp.where(pref <= need, tie, 0.0)
            sel = jnp.where(s > thr, 1.0, tie_sel)
            s_scr[rows(i), :] = jnp.where(sel > 0.5, 0.0, NEG)
            return run + jnp.sum(tie, axis=0, keepdims=True)
        lax.fori_loop(0, nch, mask_chunk, jnp.zeros((1, LANES), F32))

    def logits(i, with_bias):
        msk = s_scr[rows(i), :]
        out = []
        for pr in range(DSA_HEADS // 2):
            l2 = jnp.dot(k_ref[0, rows(i), pr * LANES:(pr + 1) * LANES],
                         qpad_scr[:, 2 * pr * LANES:(2 * pr + 2) * LANES],
                         preferred_element_type=F32)
            for e in range(2):
                l = l2[:, e * LANES:(e + 1) * LANES] + msk
                if with_bias:
                    far_rows = kc - 2 * Q_BLOCK
                    l = jnp.concatenate([l[:far_rows], l[far_rows:] + bias_scr[2 * pr + e]], axis=0)
                out.append(l)
        return out

    def max_step(i, ms, with_bias):
        return tuple(jnp.maximum(m, _tree(jnp.maximum, l)) for m, l in zip(ms, logits(i, with_bias)))

    @pl.when(jnp.logical_not(bound_ok))
    def _():
        ms = max_step(0, tuple(jnp.full((8, LANES), NEG, F32) for _ in range(DSA_HEADS)), True)
        ms = lax.fori_loop(1, nch, lambda i, m: max_step(i, m, False), ms)
        for h in range(DSA_HEADS):
            mx_scr[h:h + 1, :] = jnp.max(ms[h], axis=0, keepdims=True)

    mx = [mx_scr[h:h + 1, :] for h in range(DSA_HEADS)]

    def pv_step(i, accs, with_bias):
        return tuple(
            acc + jnp.dot(vt_ref[h, :, rows(i)], jnp.exp2(l - mx[h]).astype(BF16),
                          preferred_element_type=F32)
            for h, (acc, l) in enumerate(zip(accs, logits(i, with_bias))))

    accs = pv_step(0, tuple(jnp.zeros((V_ROWS, LANES), F32) for _ in range(DSA_HEADS)), True)
    accs = lax.fori_loop(1, nch, lambda i, a: pv_step(i, a, False), accs)
    outs = [acc[:HEAD_DIM] / acc[HEAD_DIM:HEAD_DIM + 1] for acc in accs]
    o_ref[...] = jnp.concatenate(outs, axis=0).T


def _dsa(q, iq, kw, k, vt, ikb, rel_bias, k_gain, ltri, *, batch, seq, kc):
    nb = seq // Q_BLOCK
    top_k = min(TOPK_MAX, seq // 4)
    pad = kc - Q_BLOCK
    seqp = seq + pad
    kp = jnp.pad(k.reshape(batch, seq, DSA_W), ((0, 0), (pad, 0), (0, 0)))
    ikp = jnp.pad(ikb.reshape(batch, seq, LANES), ((0, 0), (pad, 0), (0, 0)))
    vtp = jnp.pad(vt.reshape(DSA_HEADS, V_ROWS, batch, seq),
                  ((0, 0), (0, 0), (0, 0), (pad, 0))).reshape(DSA_HEADS, V_ROWS, batch * seqp)
    kmax = (HEAD_DIM ** 0.5 * 1.01) * jnp.max(jnp.abs(k_gain.astype(F32)))
    table = jnp.concatenate([rel_bias.astype(F32), jnp.full((1, DSA_HEADS), kmax, F32)], axis=0)
    qrow = lambda w: pl.BlockSpec((Q_BLOCK, w), lambda b, j: (b * nb + j, 0))
    return pl.pallas_call(
        functools.partial(_dsa_kernel, kc=kc, top_k=top_k),
        grid=(batch, nb),
        in_specs=[
            qrow(DSA_W), qrow(IDX_HEADS * IDX_DIM), qrow(LANES),
            pl.BlockSpec((1, seqp, DSA_W), lambda b, j: (b, 0, 0)),
            pl.BlockSpec((DSA_HEADS, V_ROWS, seqp), lambda b, j: (0, 0, b)),
            pl.BlockSpec((1, seqp, LANES), lambda b, j: (b, 0, 0)),
            pl.BlockSpec(memory_space=pltpu.SMEM),
            pl.BlockSpec(ltri.shape, lambda b, j: (0, 0)),
        ],
        out_specs=qrow(DSA_W),
        out_shape=jax.ShapeDtypeStruct((batch * seq, DSA_W), F32),
        scratch_shapes=[
            pltpu.VMEM((seqp, LANES), F32),
            pltpu.VMEM((LANES, IDX_HEADS * LANES), BF16),
            pltpu.VMEM((LANES, DSA_HEADS * LANES), BF16),
            pltpu.VMEM((DSA_HEADS, 2 * Q_BLOCK, LANES), F32),
            pltpu.VMEM((8, LANES), F32),
            pltpu.SMEM((2, DSA_HEADS), F32),
        ],
        compiler_params=_params("arbitrary", "arbitrary"),
        name="dsa",
    )(q, iq, kw, kp, vtp, ikp, table, ltri)


GDN_GROUP = 2


def _gdn_kernel(x_ref, ab_ref, cw_ref, ea_ref, eb_ref, alog_ref, dtb_ref, gain_ref, grp_ref,
                o_ref, xpad_scr, qkv_scr, gb_scr, gc_scr, gr_scr, pre_scr, st_scr, o_scr, *, tb):
    t = pl.program_id(1)
    cin = 3 * GDN_W
    n_chunks = tb // GDN_CHUNK
    pairs = range(GDN_HEADS // 2)
    lanes = lambda p: slice(p * LANES, (p + 1) * LANES)

    @pl.when(t == 0)
    def _():
        xpad_scr[0:8, :] = jnp.zeros((8, cin), F32)
        st_scr[...] = jnp.zeros_like(st_scr)

    x = x_ref[...]
    xpad_scr[8:8 + tb, :] = x[:, :cin]
    conv = jnp.zeros((tb, cin), F32)
    for jj in range(CONV_WIDTH):
        conv = conv + cw_ref[jj:jj + 1, :] * xpad_scr[pl.ds(8 - (CONV_WIDTH - 1) + jj, tb), :]
    xpad_scr[0:8, :] = x[tb - 8:tb, :cin]
    qkv = conv * _sigmoid(conv)
    grp = grp_ref[...]
    q = qkv[:, :GDN_W]
    k = qkv[:, GDN_W:2 * GDN_W]
    qkv_scr[:, :GDN_W] = q * lax.rsqrt(_group_sum(q * q, grp) + EPS) * (HEAD_DIM ** -0.5)
    qkv_scr[:, GDN_W:2 * GDN_W] = k * lax.rsqrt(_group_sum(k * k, grp) + EPS)
    qkv_scr[:, 2 * GDN_W:] = qkv[:, 2 * GDN_W:]

    ab = ab_ref[...]
    a_e = _dot_sel(ab, ea_ref[...], 3) + dtb_ref[...]
    b_e = _dot_sel(ab, eb_ref[...], 3)
    softplus = jnp.maximum(a_e, 0.0) + jnp.log(1.0 + jnp.exp(-jnp.abs(a_e)))
    gb_scr[:, :GDN_W] = -jnp.exp(alog_ref[...]) * softplus
    gb_scr[:, GDN_W:] = _sigmoid(b_e)

    r64 = lax.broadcasted_iota(jnp.int32, (GDN_CHUNK, LANES), 0)
    c64 = lax.broadcasted_iota(jnp.int32, (GDN_CHUNK, LANES), 1) % GDN_CHUNK
    causal = c64 <= r64
    strict = c64 < r64
    eye = jnp.where(c64 == r64, 1.0, 0.0)
    r128 = lax.broadcasted_iota(jnp.int32, (LANES, LANES), 0)
    c128 = lax.broadcasted_iota(jnp.int32, (LANES, LANES), 1)
    bdmask = (r128 // HEAD_DIM) == (c128 // HEAD_DIM)

    def bd(m):
        return jnp.where(bdmask, jnp.concatenate([m, m], axis=0), 0.0)

    lt_r = lax.broadcasted_iota(jnp.int32, (GDN_CHUNK, GDN_CHUNK), 0)
    lt_c = lax.broadcasted_iota(jnp.int32, (GDN_CHUNK, GDN_CHUNK), 1)
    ltri = jnp.where(lt_c <= lt_r, 1.0, 0.0)
    ones8 = jnp.ones((8, GDN_CHUNK), F32)
    up_r = lax.broadcasted_iota(jnp.int32, (GDN_CHUNK, GDN_W), 0)
    up_c = lax.broadcasted_iota(jnp.int32, (GDN_CHUNK, GDN_W), 1) % GDN_CHUNK
    upper = jnp.where(up_r <= up_c, 1.0, 0.0)
    for ci in range(n_chunks):
        g = gb_scr[ci * GDN_CHUNK:(ci + 1) * GDN_CHUNK, :GDN_W]
        gc_scr[ci * GDN_CHUNK:(ci + 1) * GDN_CHUNK, :] = _sel_dot(ltri, g, 3)
        gr_scr[ci * 8:(ci + 1) * 8, :] = _sel_dot(ones8, g * upper, 3)

    def par_body(gi, carry):
        inst = [(u, p) for u in range(GDN_GROUP) for p in pairs]
        rows = [pl.ds(pl.multiple_of((gi * GDN_GROUP + u) * GDN_CHUNK, GDN_CHUNK), GDN_CHUNK)
                for u in range(GDN_GROUP)]
        grow = [pl.ds(pl.multiple_of((gi * GDN_GROUP + u) * 8, 8), 8) for u in range(GDN_GROUP)]
        off = lambda s, p: slice(s * GDN_W + p * LANES, s * GDN_W + (p + 1) * LANES)
        qp = [qkv_scr[rows[u], off(0, p)] for u, p in inst]
        kp = [qkv_scr[rows[u], off(1, p)] for u, p in inst]
        vp = [qkv_scr[rows[u], off(2, p)] for u, p in inst]
        beta = [gb_scr[rows[u], off(1, p)] for u, p in inst]
        gc = [gc_scr[rows[u], lanes(p)] for u, p in inst]
        gr = [gr_scr[grow[u], lanes(p)][0:1, :] for u, p in inst]
        dmat = [jnp.exp(jnp.where(causal, a - b, NEG)) for a, b in zip(gc, gr)]
        kbd = [jnp.where(bdmask, jnp.concatenate([a, a], axis=0).T, 0.0) for a in kp]
        kb = [a * b for a, b in zip(kp, beta)]
        kk = [_dot3(a, b) for a, b in zip(kb, kbd)]
        qk = [_dot3(a, b) for a, b in zip(qp, kbd)]
        lm = [jnp.where(strict, a * d, 0.0) for a, d in zip(kk, dmat)]
        aintra = [a * d for a, d in zip(qk, dmat)]
        tinv = [eye - a for a in lm]
        lpow = lm
        for _ in range(5):
            lpow = [_dot3(a, bd(a)) for a in lpow]
            tinv = [a + _dot3(a, bd(b)) for a, b in zip(tinv, lpow)]
        egc = [jnp.exp(a) for a in gc]
        un = [_dot3(a, bd(v * b)) for a, v, b in zip(tinv, vp, beta)]
        wn = [_dot3(a, bd(b * e)) for a, b, e in zip(tinv, kb, egc)]
        for n, (u, p) in enumerate(inst):
            pre_scr[0, rows[u], lanes(p)] = un[n]
            pre_scr[1, rows[u], lanes(p)] = wn[n]
            pre_scr[2, rows[u], lanes(p)] = aintra[n]
            pre_scr[3, rows[u], lanes(p)] = qp[n] * egc[n]
            pre_scr[4, rows[u], lanes(p)] = kp[n] * jnp.exp(gc[n][GDN_CHUNK - 1:GDN_CHUNK, :] - gc[n])
        return carry

    lax.fori_loop(0, n_chunks // GDN_GROUP, par_body, 0)

    def rec_body(ci, carry):
        rows = pl.ds(pl.multiple_of(ci * GDN_CHUNK, GDN_CHUNK), GDN_CHUNK)
        tail = gc_scr[pl.ds(pl.multiple_of(ci * GDN_CHUNK + GDN_CHUNK - 8, 8), 8), :]
        egl = jnp.exp(tail[7:8, :])
        sbd = [st_scr[p] for p in pairs]
        un, wn, aintra, qe, kd = [[pre_scr[s, rows, lanes(p)] for p in pairs] for s in range(5)]
        ws = [_dot3(a, s) for a, s in zip(wn, sbd)]
        qs = [_dot3(a, s) for a, s in zip(qe, sbd)]
        v_new = [a - b for a, b in zip(un, ws)]
        o = [a + _dot3(b, bd(v)) for a, b, v in zip(qs, aintra, v_new)]
        upd = [_dot3(a.T, v) for a, v in zip(kd, v_new)]
        for p in pairs:
            st_scr[p] = sbd[p] * egl[:, lanes(p)] + jnp.where(bdmask, upd[p], 0.0)
            o_scr[rows, lanes(p)] = o[p]
        return carry

    lax.fori_loop(0, n_chunks, rec_body, 0)

    o = o_scr[...]
    oms = _group_sum(o * o, grp) * (1.0 / HEAD_DIM)
    z = x[:, cin:]
    o_ref[...] = o * lax.rsqrt(oms + EPS) * gain_ref[...] * (z * _sigmoid(z))


def _gdn(gx, ab, conv_w, ea, eb, alog, dtb, gain, grp, *, batch, seq, tb):
    nt = seq // tb
    cin = 3 * GDN_W
    row = lambda w: pl.BlockSpec((tb, w), lambda b, t: (b * nt + t, 0))
    full = lambda a: pl.BlockSpec(a.shape, lambda b, t: (0,) * a.ndim)
    return pl.pallas_call(
        functools.partial(_gdn_kernel, tb=tb),
        grid=(batch, nt),
        in_specs=[row(4 * GDN_W), row(LANES), full(conv_w), full(ea), full(eb), full(alog),
                  full(dtb), full(gain), full(grp)],
        out_specs=row(GDN_W),
        out_shape=jax.ShapeDtypeStruct((batch * seq, GDN_W), F32),
        scratch_shapes=[
            pltpu.VMEM((tb + 8, cin), F32),
            pltpu.VMEM((tb, cin), F32),
            pltpu.VMEM((tb, 2 * GDN_W), F32),
            pltpu.VMEM((tb, GDN_W), F32),
            pltpu.VMEM((tb // 8, GDN_W), F32),
            pltpu.VMEM((5, tb, GDN_W), F32),
            pltpu.VMEM((GDN_HEADS // 2, LANES, LANES), F32),
            pltpu.VMEM((tb, GDN_W), F32),
        ],
        compiler_params=_params("arbitrary", "arbitrary"),
        name="gdn",
    )(gx, ab, conv_w, ea, eb, alog, dtb, gain, grp)


def _memkv_kernel(mem_ref, g_ref, w_ref, kg_ref, grp_ref, kt_ref, vp_ref):
    x = mem_ref[0]
    ms = jnp.mean(x * x, axis=-1, keepdims=True)
    h = (x * lax.rsqrt(ms + EPS) * g_ref[...]).astype(BF16)
    kv = jnp.dot(h, w_ref[...], preferred_element_type=F32)
    km = kv[:, :MEM_W]
    vm = kv[:, MEM_W:]
    kms = _group_sum(km * km, grp_ref[...]) * (1.0 / HEAD_DIM)
    kt = (km * lax.rsqrt(kms + EPS) * kg_ref[...]).T
    n_mem = x.shape[0]
    top = lax.broadcasted_iota(jnp.int32, (LANES, n_mem), 0) < HEAD_DIM
    left = lax.broadcasted_iota(jnp.int32, (n_mem, LANES), 1) < HEAD_DIM
    for hh in range(MEM_HEADS):
        pr = slice((hh // 2) * LANES, (hh // 2 + 1) * LANES)
        keep_r = top if hh % 2 == 0 else jnp.logical_not(top)
        keep_c = left if hh % 2 == 0 else jnp.logical_not(left)
        kt_ref[0, hh] = jnp.where(keep_r, kt[pr, :], 0.0).astype(BF16)
        vp_ref[0, hh] = jnp.where(keep_c, vm[:, pr], 0.0).astype(BF16)


def _memkv(mem, gain, w, kg, grp):
    b, n_mem, d = mem.shape
    full = lambda a: pl.BlockSpec(a.shape, lambda i: (0,) * a.ndim)
    return pl.pallas_call(
        _memkv_kernel,
        grid=(b,),
        in_specs=[pl.BlockSpec((1, n_mem, d), lambda i: (i, 0, 0)), full(gain), full(w), full(kg),
                  full(grp)],
        out_specs=(pl.BlockSpec((1, MEM_HEADS, LANES, n_mem), lambda i: (i, 0, 0, 0)),
                   pl.BlockSpec((1, MEM_HEADS, n_mem, LANES), lambda i: (i, 0, 0, 0))),
        out_shape=(jax.ShapeDtypeStruct((b, MEM_HEADS, LANES, n_mem), BF16),
                   jax.ShapeDtypeStruct((b, MEM_HEADS, n_mem, LANES), BF16)),
        compiler_params=_params("parallel"),
        name="memkv",
    )(mem, gain, w, kg, grp)


def _memattn_kernel(q_ref, qg_ref, grp_ref, kt_ref, vp_ref, o_ref):
    q = q_ref[...]
    qms = _group_sum(q * q, grp_ref[...]) * (1.0 / HEAD_DIM)
    qn = (q * lax.rsqrt(qms + EPS) * qg_ref[...] * (HEAD_DIM ** -0.5)).astype(BF16)
    outs = []
    for pr in range(MEM_HEADS // 2):
        qp = qn[:, pr * LANES:(pr + 1) * LANES]
        acc = None
        for e in range(2):
            hh = 2 * pr + e
            l = jnp.dot(qp, kt_ref[0, hh], preferred_element_type=F32)
            l = l - jnp.max(l, axis=-1, keepdims=True)
            p = jnp.exp(l)
            p = p / jnp.sum(p, axis=-1, keepdims=True)
            o = jnp.dot(p.astype(BF16), vp_ref[0, hh], preferred_element_type=F32)
            acc = o if acc is None else acc + o
        outs.append(acc)
    o_ref[...] = jnp.concatenate(outs, axis=-1)


def _memattn(mq, qg, grp, kt, vp, *, batch, seq, tm):
    nt = seq // tm
    n_mem = kt.shape[-1]
    full = lambda a: pl.BlockSpec(a.shape, lambda b, t: (0,) * a.ndim)
    return pl.pallas_call(
        _memattn_kernel,
        grid=(batch, nt),
        in_specs=[
            pl.BlockSpec((tm, MEM_W), lambda b, t: (b * nt + t, 0)), full(qg), full(grp),
            pl.BlockSpec((1, MEM_HEADS, LANES, n_mem), lambda b, t: (b, 0, 0, 0)),
            pl.BlockSpec((1, MEM_HEADS, n_mem, LANES), lambda b, t: (b, 0, 0, 0)),
        ],
        out_specs=pl.BlockSpec((tm, MEM_W), lambda b, t: (b * nt + t, 0)),
        out_shape=jax.ShapeDtypeStruct((batch * seq, MEM_W), F32),
        compiler_params=_params("parallel", "parallel"),
        name="memattn",
    )(mq, qg, grp, kt, vp)


def _outproj_kernel(x_ref, a_ref, b_ref, c_ref, wa_ref, wb_ref, wc_ref, o_ref):
    y = _bdot(a_ref[...], wa_ref[...]) + _bdot(b_ref[...], wb_ref[...]) + _bdot(c_ref[...], wc_ref[...])
    o_ref[...] = x_ref[...] + y


def _outproj(x, a, b, c, wa, wb, wc, *, tm):
    m, d = x.shape
    row = lambda w: pl.BlockSpec((tm, w), lambda i: (i, 0))
    full = lambda arr: pl.BlockSpec(arr.shape, lambda i: (0,) * arr.ndim)
    return pl.pallas_call(
        _outproj_kernel,
        grid=(m // tm,),
        in_specs=[row(d), row(DSA_W), row(GDN_W), row(MEM_W), full(wa), full(wb), full(wc)],
        out_specs=row(d),
        out_shape=jax.ShapeDtypeStruct((m, d), F32),
        compiler_params=_params("parallel"),
        name="outproj",
    )(x, a, b, c, wa, wb, wc)


def _group_ones(width):
    idx = np.arange(width) // HEAD_DIM
    return jnp.asarray((idx[:, None] == idx[None, :]).astype(np.float32))


def _t5_bucket_np(n):
    max_exact = NUM_BUCKETS // 2
    nf = np.maximum(n, 1).astype(np.float32)
    large = max_exact + (np.log(nf / np.float32(max_exact)) / np.float32(math.log(MAX_DISTANCE / max_exact))
                         * (NUM_BUCKETS - max_exact)).astype(np.int32)
    large = np.minimum(large, NUM_BUCKETS - 1)
    return np.where(n < max_exact, n, large)


def _bucket_starts():
    dist = np.arange(2 * Q_BLOCK)
    bucket = _t5_bucket_np(dist)
    assert (np.diff(bucket) >= 0).all() and bucket[-1] == NUM_BUCKETS - 1
    return tuple(int(dist[bucket >= b].min()) for b in range(1, NUM_BUCKETS))


_BUCKET_STARTS = _bucket_starts()


def _pack_w_in(w):
    d = w.shape[0]
    o = 3 * DSA_W
    iq = w[:, o:o + IDX_HEADS * IDX_DIM]
    o += IDX_HEADS * IDX_DIM
    kw = w[:, o:o + IDX_DIM + IDX_HEADS]
    o += IDX_DIM + IDX_HEADS
    g = w[:, o:o + 4 * GDN_W]
    o += 4 * GDN_W
    ab = w[:, o:o + 2 * GDN_HEADS]
    o += 2 * GDN_HEADS
    mq = w[:, o:o + MEM_W]
    pad = lambda a: jnp.pad(a, ((0, 0), (0, LANES - a.shape[1])))
    return jnp.concatenate([w[:, :3 * DSA_W], iq, pad(kw), g, pad(ab), mq], axis=1).astype(BF16)


def _tile_heads(v, heads):
    return jnp.tile(v.astype(F32), heads).reshape(1, heads * HEAD_DIM)


def kernel(x, mem, ffn1_norm, ffn1_w_gate, ffn1_w_up, ffn1_w_down, mix_norm, w_in, dsa_q_norm, dsa_k_norm, rel_bias, gdn_conv, gdn_A_log, gdn_dt_bias, gdn_out_norm, mem_norm, w_mem_kv, mem_q_norm, mem_k_norm, w_out, ffn2_norm, ffn2_w_gate, ffn2_w_up, ffn2_w_down):
    batch, seq, d = x.shape
    depth = w_in.shape[0]
    m = batch * seq
    tm = min(512, seq)
    kc = min(512, seq)
    dff = ffn1_w_gate.shape[-1]
    tf = dff // 2 if (dff // 2) % LANES == 0 else dff

    grp_a = _group_ones(DSA_W)
    grp_m = _group_ones(MEM_W)
    ltri = jnp.asarray(np.tril(np.ones((kc, kc), np.float32))).astype(BF16)
    heads_of = np.arange(GDN_W) // HEAD_DIM
    ea = jnp.asarray((np.arange(LANES)[:, None] == heads_of[None, :]).astype(np.float32))
    eb = jnp.asarray((np.arange(LANES)[:, None] == heads_of[None, :] + GDN_HEADS).astype(np.float32))

    xf = x.reshape(m, d)
    for l in range(depth):
        xf = _ffn(xf, ffn1_norm[l], ffn1_w_gate[l].astype(BF16), ffn1_w_up[l].astype(BF16),
                  ffn1_w_down[l].astype(BF16), tm=tm, tf=tf)
        (q, k, vt, iq, ikb, kw, gx, ab, mq) = _inproj(
            xf, mix_norm[l].reshape(1, d), _pack_w_in(w_in[l]), grp_a,
            _tile_heads(dsa_q_norm[l], DSA_HEADS), _tile_heads(dsa_k_norm[l], DSA_HEADS), tm=tm)
        out_a = _dsa(q, iq, kw, k, vt, ikb, rel_bias, dsa_k_norm[l], ltri, batch=batch, seq=seq, kc=kc)
        out_b = _gdn(gx, ab, gdn_conv[l], ea, eb,
                     jnp.repeat(gdn_A_log[l].astype(F32), HEAD_DIM).reshape(1, GDN_W),
                     jnp.repeat(gdn_dt_bias[l].astype(F32), HEAD_DIM).reshape(1, GDN_W),
                     _tile_heads(gdn_out_norm[l], GDN_HEADS), grp_a, batch=batch, seq=seq, tb=tm)
        kt, vp = _memkv(mem, mem_norm[l].reshape(1, d), w_mem_kv[l].astype(BF16),
                        _tile_heads(mem_k_norm[l], MEM_HEADS), grp_m)
        out_c = _memattn(mq, _tile_heads(mem_q_norm[l], MEM_HEADS), grp_m, kt, vp,
                         batch=batch, seq=seq, tm=tm)
        wo = w_out[l].astype(BF16)
        xf = _outproj(xf, out_a, out_b, out_c, wo[:DSA_W], wo[DSA_W:DSA_W + GDN_W],
                      wo[DSA_W + GDN_W:], tm=tm)
        xf = _ffn(xf, ffn2_norm[l], ffn2_w_gate[l].astype(BF16), ffn2_w_up[l].astype(BF16),
                  ffn2_w_down[l].astype(BF16), tm=tm, tf=tf)
    return xf.reshape(batch, seq, d)
```

```python
import functools
import math

import jax
import jax.numpy as jnp
import numpy as np
from jax import lax
from jax.experimental import pallas as pl
from jax.experimental.pallas import tpu as pltpu

F32 = jnp.float32
BF16 = jnp.bfloat16

HEAD_DIM = 64
DSA_HEADS = 6
GDN_HEADS = 6
MEM_HEADS = 4
DSA_W = DSA_HEADS * HEAD_DIM
GDN_W = GDN_HEADS * HEAD_DIM
MEM_W = MEM_HEADS * HEAD_DIM
IDX_HEADS = 8
IDX_DIM = 32
TOPK_MAX = 256
Q_BLOCK = 128
GDN_CHUNK = 64
CONV_WIDTH = 4
NUM_BUCKETS = 32
MAX_DISTANCE = 128
EPS = 1e-6

LANES = 128
VMEM_LIMIT = 52 * 1024 * 1024
NEG = -1e30
LOG2E = math.log2(math.e)
V_ROWS = HEAD_DIM + 16
MAX_LOGIT_SPREAD = 80.0
PROBES_PER_ROUND = 4
FAST_ROUNDS = 4

SEG_A = 0
SEG_IQ = SEG_A + 3 * DSA_W
SEG_KW = SEG_IQ + IDX_HEADS * IDX_DIM
SEG_G = SEG_KW + LANES
SEG_AB = SEG_G + 4 * GDN_W
SEG_MQ = SEG_AB + LANES
IN_PACKED = SEG_MQ + MEM_W


def _bdot(a, b):
    return jnp.dot(a.astype(BF16), b.astype(BF16), preferred_element_type=F32)


def _split(x, terms):
    out = []
    for _ in range(terms - 1):
        hi = x.astype(BF16)
        out.append(hi)
        x = x - hi.astype(F32)
    out.append(x.astype(BF16))
    return out


def _dot3(a, b):
    ah, al = _split(a, 2)
    bh, bl = _split(b, 2)
    mm = lambda x, y: jnp.dot(x, y, preferred_element_type=F32)
    return mm(ah, bh) + (mm(ah, bl) + mm(al, bh))


def _dot_sel(a, sel, terms):
    selb = sel.astype(BF16)
    acc = None
    for piece in _split(a, terms):
        d = jnp.dot(piece, selb, preferred_element_type=F32)
        acc = d if acc is None else acc + d
    return acc


def _sel_dot(sel, b, terms):
    selb = sel.astype(BF16)
    acc = None
    for piece in _split(b, terms):
        d = jnp.dot(selb, piece, preferred_element_type=F32)
        acc = d if acc is None else acc + d
    return acc


def _group_sum(x, grp):
    return _dot_sel(x, grp, 2)


def _sigmoid(x):
    return 1.0 / (1.0 + jnp.exp(-x))


def _params(*sem):
    return pltpu.CompilerParams(dimension_semantics=sem, vmem_limit_bytes=VMEM_LIMIT)


def _ffn_kernel(x_ref, g_ref, wg_ref, wu_ref, wd_ref, o_ref, h_scr, acc_scr):
    f = pl.program_id(1)

    @pl.when(f == 0)
    def _():
        x = x_ref[...]
        ms = jnp.mean(x * x, axis=-1, keepdims=True)
        h_scr[...] = (x * lax.rsqrt(ms + EPS) * g_ref[...]).astype(BF16)
        acc_scr[...] = jnp.zeros_like(acc_scr)

    h = h_scr[...]
    a = jnp.dot(h, wg_ref[...], preferred_element_type=F32)
    u = jnp.dot(h, wu_ref[...], preferred_element_type=F32)
    z = (a * _sigmoid(a)) * u
    acc_scr[...] += jnp.dot(z.astype(BF16), wd_ref[...], preferred_element_type=F32)

    @pl.when(f == pl.num_programs(1) - 1)
    def _():
        o_ref[...] = x_ref[...] + 0.5 * acc_scr[...]


def _ffn(x, gain, wg, wu, wd, *, tm, tf):
    m, d = x.shape
    dff = wg.shape[1]
    return pl.pallas_call(
        _ffn_kernel,
        grid=(m // tm, dff // tf),
        in_specs=[
            pl.BlockSpec((tm, d), lambda i, f: (i, 0)),
            pl.BlockSpec((1, d), lambda i, f: (0, 0)),
            pl.BlockSpec((d, tf), lambda i, f: (0, f)),
            pl.BlockSpec((d, tf), lambda i, f: (0, f)),
            pl.BlockSpec((tf, d), lambda i, f: (f, 0)),
        ],
        out_specs=pl.BlockSpec((tm, d), lambda i, f: (i, 0)),
        out_shape=jax.ShapeDtypeStruct((m, d), F32),
        scratch_shapes=[pltpu.VMEM((tm, d), BF16), pltpu.VMEM((tm, d), F32)],
        compiler_params=_params("parallel", "arbitrary"),
        name="ffn",
    )(x, gain.reshape(1, d), wg, wu, wd)


def _inproj_kernel(x_ref, g_ref, w_ref, grp_ref, qg_ref, kg_ref,
                   q_ref, k_ref, vt_ref, iq_ref, ikb_ref, kw_ref, gx_ref, ab_ref, mq_ref):
    x = x_ref[...]
    ms = jnp.mean(x * x, axis=-1, keepdims=True)
    h = (x * lax.rsqrt(ms + EPS) * g_ref[...]).astype(BF16)
    p = jnp.dot(h, w_ref[...], preferred_element_type=F32)

    grp = grp_ref[...]
    dq = p[:, SEG_A:SEG_A + DSA_W]
    dk = p[:, SEG_A + DSA_W:SEG_A + 2 * DSA_W]
    dv = p[:, SEG_A + 2 * DSA_W:SEG_A + 3 * DSA_W]
    qms = _group_sum(dq * dq, grp) * (1.0 / HEAD_DIM)
    kms = _group_sum(dk * dk, grp) * (1.0 / HEAD_DIM)
    q_ref[...] = (dq * lax.rsqrt(qms + EPS) * qg_ref[...] * (HEAD_DIM ** -0.5 * LOG2E)).astype(BF16)
    k_ref[...] = (dk * lax.rsqrt(kms + EPS) * kg_ref[...]).astype(BF16)
    dvt = dv.T
    ones = jnp.ones((V_ROWS - HEAD_DIM, dvt.shape[1]), F32)
    for hd in range(DSA_HEADS):
        vt_ref[hd] = jnp.concatenate([dvt[hd * HEAD_DIM:(hd + 1) * HEAD_DIM], ones], axis=0).astype(BF16)

    iq_ref[...] = (p[:, SEG_IQ:SEG_IQ + IDX_HEADS * IDX_DIM] * (IDX_DIM ** -0.5)).astype(BF16)
    kw = p[:, SEG_KW:SEG_KW + LANES]
    kw_ref[...] = kw
    ikb_ref[...] = kw.astype(BF16)
    gx_ref[...] = p[:, SEG_G:SEG_G + 4 * GDN_W]
    ab_ref[...] = p[:, SEG_AB:SEG_AB + LANES]
    mq_ref[...] = p[:, SEG_MQ:SEG_MQ + MEM_W]


def _inproj(x, gain, w_packed, grp, qg, kg, *, tm):
    m, d = x.shape
    row = lambda w: pl.BlockSpec((tm, w), lambda i: (i, 0))
    full = lambda a: pl.BlockSpec(a.shape, lambda i: (0,) * a.ndim)
    out_shape = (
        jax.ShapeDtypeStruct((m, DSA_W), BF16),
        jax.ShapeDtypeStruct((m, DSA_W), BF16),
        jax.ShapeDtypeStruct((DSA_HEADS, V_ROWS, m), BF16),
        jax.ShapeDtypeStruct((m, IDX_HEADS * IDX_DIM), BF16),
        jax.ShapeDtypeStruct((m, LANES), BF16),
        jax.ShapeDtypeStruct((m, LANES), F32),
        jax.ShapeDtypeStruct((m, 4 * GDN_W), F32),
        jax.ShapeDtypeStruct((m, LANES), F32),
        jax.ShapeDtypeStruct((m, MEM_W), F32),
    )
    out_specs = (row(DSA_W), row(DSA_W), pl.BlockSpec((DSA_HEADS, V_ROWS, tm), lambda i: (0, 0, i)),
                 row(IDX_HEADS * IDX_DIM), row(LANES), row(LANES), row(4 * GDN_W),
                 row(LANES), row(MEM_W))
    return pl.pallas_call(
        _inproj_kernel,
        grid=(m // tm,),
        in_specs=[row(d), full(gain), full(w_packed), full(grp), full(qg), full(kg)],
        out_specs=out_specs,
        out_shape=out_shape,
        compiler_params=_params("parallel"),
        name="inproj",
    )(x, gain, w_packed, grp, qg, kg)


_DENORMAL_TOP = 0x007FFFFF


def _float_to_key(f):
    bits = lax.bitcast_convert_type(f, jnp.int32)
    mag = jnp.maximum((bits & jnp.int32(0x7FFFFFFF)) - _DENORMAL_TOP, 0)
    return jnp.where(bits >= 0, mag, -mag)


def _key_to_float(key):
    mag = jnp.abs(key)
    bits = jnp.where(mag > 0, mag + _DENORMAL_TOP, 0)
    return lax.bitcast_convert_type(jnp.where(key < 0, bits | jnp.int32(-2 ** 31), bits), F32)


def _upper_normal_quantile(p):
    pp = jnp.clip(jnp.minimum(p, 1.0 - p), 1e-30, 0.5)
    t = jnp.sqrt(-2.0 * jnp.log(pp))
    z = t - (2.515517 + t * (0.802853 + t * 0.010328)) / (1.0 + t * (1.432788 + t * (0.189269 + t * 0.001308)))
    return jnp.where(p <= 0.5, z, -z)


def _tree(op, x, group=8):
    parts = x.reshape(x.shape[0] // group, group, LANES)
    k = parts.shape[0]
    while k > 1:
        k //= 2
        parts = op(parts[:k], parts[k:2 * k])
    return parts[0]


def _dsa_kernel(q_ref, iq_ref, kwq_ref, k_ref, vt_ref, ik_ref, tab_ref, ltri_ref, o_ref,
                s_scr, iqt_scr, qpad_scr, bias_scr, mx_scr, brange_scr, *, kc, top_k):
    j = pl.program_id(1)
    per = kc // Q_BLOCK
    pad = kc - Q_BLOCK
    nch = j // per + 1
    q_pos = j * Q_BLOCK + lax.broadcasted_iota(jnp.int32, (1, LANES), 1)
    row_iota = lax.broadcasted_iota(jnp.int32, (kc, LANES), 0)

    def rows(i):
        return pl.ds(pl.multiple_of(j * Q_BLOCK - i * kc, Q_BLOCK), kc)

    @pl.when((pl.program_id(0) == 0) & (j == 0))
    def _():
        r = lax.broadcasted_iota(jnp.int32, (2 * Q_BLOCK, LANES), 0)
        c = lax.broadcasted_iota(jnp.int32, (2 * Q_BLOCK, LANES), 1)
        dist = Q_BLOCK + c - r
        bucket = jnp.zeros_like(dist)
        for first in _BUCKET_STARTS:
            bucket = bucket + jnp.where(dist >= first, 1, 0)
        for h in range(DSA_HEADS):
            far = tab_ref[NUM_BUCKETS - 1, h]
            delta = jnp.zeros((2 * Q_BLOCK, LANES), F32)
            b_hi = jnp.float32(0.0)
            b_lo = jnp.float32(0.0)
            for b in range(NUM_BUCKETS - 1):
                delta = jnp.where(bucket == b, tab_ref[b, h] - far, delta)
                b_hi = jnp.maximum(b_hi, (tab_ref[b, h] - far) * LOG2E)
                b_lo = jnp.minimum(b_lo, (tab_ref[b, h] - far) * LOG2E)
            bias_scr[h] = jnp.where(dist >= 0, delta * LOG2E, 0.0)
            brange_scr[0, h] = b_hi
            brange_scr[1, h] = b_lo

    iqt = iq_ref[...].astype(F32).T
    zpad = jnp.zeros((LANES - IDX_DIM, LANES), F32)
    for h in range(IDX_HEADS):
        iqt_scr[:, h * LANES:(h + 1) * LANES] = jnp.concatenate(
            [iqt[h * IDX_DIM:(h + 1) * IDX_DIM], zpad], axis=0).astype(BF16)
    wt = kwq_ref[...].T[IDX_DIM:IDX_DIM + IDX_HEADS, :] * (IDX_HEADS ** -0.5)
    qt = q_ref[...].astype(F32).T
    half = lax.broadcasted_iota(jnp.int32, (LANES, LANES), 0) < HEAD_DIM
    for h in range(DSA_HEADS):
        pair = qt[(h // 2) * LANES:(h // 2 + 1) * LANES]
        keep = half if h % 2 == 0 else jnp.logical_not(half)
        qpad_scr[:, h * LANES:(h + 1) * LANES] = jnp.where(keep, pair, 0.0).astype(BF16)

    kmax = tab_ref[NUM_BUCKETS, 0]
    spread = jnp.zeros((1, LANES), F32)
    for h in range(DSA_HEADS):
        b_hi = brange_scr[0, h]
        b_lo = brange_scr[1, h]
        qh = qt[h * HEAD_DIM:(h + 1) * HEAD_DIM]
        bound = jnp.sqrt(jnp.sum(qh * qh, axis=0, keepdims=True)) * kmax
        mx_scr[h:h + 1, :] = bound + b_hi
        spread = jnp.maximum(spread, 2.0 * bound + (b_hi - b_lo))
    bound_ok = jnp.max(spread) <= MAX_LOGIT_SPREAD

    def score_chunk(i, carry):
        d = jnp.dot(ik_ref[0, rows(i), :], iqt_scr[...], preferred_element_type=F32)
        acc = jnp.maximum(d[:, :LANES], 0.0) * wt[0:1, :]
        for h in range(1, IDX_HEADS):
            acc = acc + jnp.maximum(d[:, h * LANES:(h + 1) * LANES], 0.0) * wt[h:h + 1, :]
        key = row_iota + (j * Q_BLOCK - i * kc - pad)
        adm = jnp.where(key >= 0, key, q_pos + 1) <= q_pos
        s_scr[rows(i), :] = jnp.where(adm, acc, -jnp.inf)
        fin = jnp.where(adm, acc, 0.0)
        tot, sq, top = carry
        return (tot + _tree(jnp.add, fin), sq + _tree(jnp.add, fin * fin),
                jnp.maximum(top, _tree(jnp.maximum, jnp.where(adm, acc, -jnp.inf))))

    zero8 = jnp.zeros((8, LANES), F32)
    tot, sq, top = lax.fori_loop(0, nch, score_chunk, (zero8, zero8, jnp.full((8, LANES), -jnp.inf, F32)))

    def count(pred):
        def body(i, cnt):
            return cnt + _tree(jnp.add, jnp.where(pred(s_scr[rows(i), :]), 1.0, 0.0))
        cnt = lax.fori_loop(0, nch, body, zero8)
        return jnp.sum(cnt, axis=0, keepdims=True)

    def max_below(t):
        def body(i, m):
            s = s_scr[rows(i), :]
            return jnp.maximum(m, _tree(jnp.maximum, jnp.where(s < t, s, -jnp.inf)))
        m = lax.fori_loop(0, nch, body, jnp.full((8, LANES), -jnp.inf, F32))
        return jnp.max(m, axis=0, keepdims=True)

    kf = float(top_k)
    n_adm = (q_pos + 1).astype(F32)
    mean = jnp.sum(tot, axis=0, keepdims=True) / n_adm
    std = jnp.sqrt(jnp.maximum(jnp.sum(sq, axis=0, keepdims=True) / n_adm - mean * mean, 0.0))
    first_guess = mean + _upper_normal_quantile(kf / n_adm) * std
    lowest = jnp.float32(-3.0e38)
    top1 = jnp.minimum(jnp.max(top, axis=0, keepdims=True), -lowest)

    def update(st, key):
        lo, hi, c_lo, c_hi, lo_set, hi_set, done, run = st
        key = jnp.clip(key, lo + 1, hi - 1)
        t = _key_to_float(key)
        c = count(lambda s: s >= t)
        active = done < 0.5
        up = active & (c >= kf)
        dn = active & (c < kf)
        lo, c_lo, lo_set = jnp.where(up, key, lo), jnp.where(up, c, c_lo), jnp.where(up, 1.0, lo_set)
        hi, c_hi, hi_set = jnp.where(dn, key, hi), jnp.where(dn, c, c_hi), jnp.where(dn, 1.0, hi_set)
        done = jnp.where((c_lo == kf) | (hi <= lo + 1), 1.0, done)
        return lo, hi, c_lo, c_hi, lo_set, hi_set, done, run

    def guess(st, midpoint):
        lo, hi, c_lo, c_hi, lo_set, hi_set, done, run = st
        v_lo, v_hi = _key_to_float(lo), _key_to_float(hi)
        both = (lo_set > 0.5) & (hi_set > 0.5)
        step = std * 0.25 * jnp.exp2(run)
        if midpoint:
            inner = 0.5 * v_lo + 0.5 * v_hi
        else:
            l_lo, l_hi = jnp.log(c_lo), jnp.log(jnp.maximum(c_hi, 0.5))
            frac = jnp.clip((l_lo - math.log(kf - 0.5)) / jnp.maximum(l_lo - l_hi, 1e-6), 0.0, 1.0)
            inner = v_lo + (v_hi - v_lo) * frac
        t = jnp.where(both, inner,
                      jnp.where(hi_set > 0.5, v_hi - step, jnp.where(lo_set > 0.5, v_lo + step, first_guess)))
        t = jnp.where(t != t, 0.0, t)
        run = jnp.where(both, 0.0, run + 1.0)
        return _float_to_key(t), (lo, hi, c_lo, c_hi, lo_set, hi_set, done, run)

    def pending(st):
        return 1.0 - jnp.min(st[6])

    flag0 = jnp.zeros((1, LANES), F32)
    st = (jnp.full((1, LANES), _float_to_key(lowest), jnp.int32), _float_to_key(top1) + 1,
          n_adm, flag0, flag0, flag0, jnp.where(n_adm <= kf, 1.0, 0.0), flag0)

    def fast_round(carry):
        p, _, st = carry
        for r in range(PROBES_PER_ROUND):
            key, st = guess(st, midpoint=(r == PROBES_PER_ROUND - 1))
            st = update(st, key)
        return p + 1, pending(st), st

    _, _, st = lax.while_loop(lambda c: (c[0] < FAST_ROUNDS) & (c[1] > 0.5), fast_round,
                              (jnp.int32(0), pending(st), st))

    def safe_round(carry):
        p, _, st = carry
        st = update(st, st[0] + lax.shift_right_logical(st[1] - st[0], 1))
        lo, hi, done = st[0], st[1], st[6]
        below = _float_to_key(max_below(_key_to_float(hi)))
        hi = jnp.where(done < 0.5, jnp.clip(below + 1, lo + 1, hi), hi)
        done = jnp.where(hi <= lo + 1, 1.0, done)
        st = update((lo, hi) + st[2:6] + (done, st[7]), hi - 1)
        return p + 1, pending(st), st

    _, _, st = lax.while_loop(lambda c: (c[1] > 0.5) & (c[0] < 34), safe_round,
                              (jnp.int32(0), pending(st), st))

    thr = _key_to_float(st[0])
    has_ties = jnp.max(st[2]) > kf

    @pl.when(jnp.logical_not(has_ties))
    def _():
        def mask_chunk(i, carry):
            s_scr[rows(i), :] = jnp.where(s_scr[rows(i), :] >= thr, 0.0, NEG)
            return carry
        lax.fori_loop(0, nch, mask_chunk, 0)

    @pl.when(has_ties)
    def _():
        need = kf - count(lambda s: s > thr)

        def mask_chunk(t, run):
            i = nch - 1 - t
            s = s_scr[rows(i), :]
            tie = jnp.where(s == thr, 1.0, 0.0)
            pref = jnp.dot(ltri_ref[...], tie.astype(BF16), preferred_element_type=F32) + run
            tie_sel = jnp.where(pref <= need, tie, 0.0)
            sel = jnp.where(s > thr, 1.0, tie_sel)
            s_scr[rows(i), :] = jnp.where(sel > 0.5, 0.0, NEG)
            return run + jnp.sum(tie, axis=0, keepdims=True)
        lax.fori_loop(0, nch, mask_chunk, jnp.zeros((1, LANES), F32))

    def logits(i, with_bias):
        msk = s_scr[rows(i), :]
        out = []
        for pr in range(DSA_HEADS // 2):
            l2 = jnp.dot(k_ref[0, rows(i), pr * LANES:(pr + 1) * LANES],
                         qpad_scr[:, 2 * pr * LANES:(2 * pr + 2) * LANES],
                         preferred_element_type=F32)
            for e in range(2):
                l = l2[:, e * LANES:(e + 1) * LANES] + msk
                if with_bias:
                    far_rows = kc - 2 * Q_BLOCK
                    l = jnp.concatenate([l[:far_rows], l[far_rows:] + bias_scr[2 * pr + e]], axis=0)
                out.append(l)
        return out

    def max_step(i, ms, with_bias):
        return tuple(jnp.maximum(m, _tree(jnp.maximum, l)) for m, l in zip(ms, logits(i, with_bias)))

    @pl.when(jnp.logical_not(bound_ok))
    def _():
        ms = max_step(0, tuple(jnp.full((8, LANES), NEG, F32) for _ in range(DSA_HEADS)), True)
        ms = lax.fori_loop(1, nch, lambda i, m: max_step(i, m, False), ms)
        for h in range(DSA_HEADS):
            mx_scr[h:h + 1, :] = jnp.max(ms[h], axis=0, keepdims=True)

    mx = [mx_scr[h:h + 1, :] for h in range(DSA_HEADS)]

    def pv_step(i, accs, with_bias):
        return tuple(
            acc + jnp.dot(vt_ref[h, :, rows(i)], jnp.exp2(l - mx[h]).astype(BF16),
                          preferred_element_type=F32)
            for h, (acc, l) in enumerate(zip(accs, logits(i, with_bias))))

    accs = pv_step(0, tuple(jnp.zeros((V_ROWS, LANES), F32) for _ in range(DSA_HEADS)), True)
    accs = lax.fori_loop(1, nch, lambda i, a: pv_step(i, a, False), accs)
    outs = [acc[:HEAD_DIM] / acc[HEAD_DIM:HEAD_DIM + 1] for acc in accs]
    o_ref[...] = jnp.concatenate(outs, axis=0).T


def _dsa(q, iq, kw, k, vt, ikb, rel_bias, k_gain, ltri, *, batch, seq, kc):
    nb = seq // Q_BLOCK
    top_k = min(TOPK_MAX, seq // 4)
    pad = kc - Q_BLOCK
    seqp = seq + pad
    kp = jnp.pad(k.reshape(batch, seq, DSA_W), ((0, 0), (pad, 0), (0, 0)))
    ikp = jnp.pad(ikb.reshape(batch, seq, LANES), ((0, 0), (pad, 0), (0, 0)))
    vtp = jnp.pad(vt.reshape(DSA_HEADS, V_ROWS, batch, seq),
                  ((0, 0), (0, 0), (0, 0), (pad, 0))).reshape(DSA_HEADS, V_ROWS, batch * seqp)
    kmax = (HEAD_DIM ** 0.5 * 1.01) * jnp.max(jnp.abs(k_gain.astype(F32)))
    table = jnp.concatenate([rel_bias.astype(F32), jnp.full((1, DSA_HEADS), kmax, F32)], axis=0)
    qrow = lambda w: pl.BlockSpec((Q_BLOCK, w), lambda b, j: (b * nb + j, 0))
    return pl.pallas_call(
        functools.partial(_dsa_kernel, kc=kc, top_k=top_k),
        grid=(batch, nb),
        in_specs=[
            qrow(DSA_W), qrow(IDX_HEADS * IDX_DIM), qrow(LANES),
            pl.BlockSpec((1, seqp, DSA_W), lambda b, j: (b, 0, 0)),
            pl.BlockSpec((DSA_HEADS, V_ROWS, seqp), lambda b, j: (0, 0, b)),
            pl.BlockSpec((1, seqp, LANES), lambda b, j: (b, 0, 0)),
            pl.BlockSpec(memory_space=pltpu.SMEM),
            pl.BlockSpec(ltri.shape, lambda b, j: (0, 0)),
        ],
        out_specs=qrow(DSA_W),
        out_shape=jax.ShapeDtypeStruct((batch * seq, DSA_W), F32),
        scratch_shapes=[
            pltpu.VMEM((seqp, LANES), F32),
            pltpu.VMEM((LANES, IDX_HEADS * LANES), BF16),
            pltpu.VMEM((LANES, DSA_HEADS * LANES), BF16),
            pltpu.VMEM((DSA_HEADS, 2 * Q_BLOCK, LANES), F32),
            pltpu.VMEM((8, LANES), F32),
            pltpu.SMEM((2, DSA_HEADS), F32),
        ],
        compiler_params=_params("arbitrary", "arbitrary"),
        name="dsa",
    )(q, iq, kw, kp, vtp, ikp, table, ltri)


GDN_GROUP = 2


def _gdn_kernel(x_ref, ab_ref, cw_ref, ea_ref, eb_ref, alog_ref, dtb_ref, gain_ref, grp_ref,
                o_ref, xpad_scr, qkv_scr, gb_scr, gc_scr, gr_scr, pre_scr, st_scr, o_scr, *, tb):
    t = pl.program_id(1)
    cin = 3 * GDN_W
    n_chunks = tb // GDN_CHUNK
    pairs = range(GDN_HEADS // 2)
    lanes = lambda p: slice(p * LANES, (p + 1) * LANES)

    @pl.when(t == 0)
    def _():
        xpad_scr[0:8, :] = jnp.zeros((8, cin), F32)
        st_scr[...] = jnp.zeros_like(st_scr)

    x = x_ref[...]
    xpad_scr[8:8 + tb, :] = x[:, :cin]
    conv = jnp.zeros((tb, cin), F32)
    for jj in range(CONV_WIDTH):
        conv = conv + cw_ref[jj:jj + 1, :] * xpad_scr[pl.ds(8 - (CONV_WIDTH - 1) + jj, tb), :]
    xpad_scr[0:8, :] = x[tb - 8:tb, :cin]
    qkv = conv * _sigmoid(conv)
    grp = grp_ref[...]
    q = qkv[:, :GDN_W]
    k = qkv[:, GDN_W:2 * GDN_W]
    qkv_scr[:, :GDN_W] = q * lax.rsqrt(_group_sum(q * q, grp) + EPS) * (HEAD_DIM ** -0.5)
    qkv_scr[:, GDN_W:2 * GDN_W] = k * lax.rsqrt(_group_sum(k * k, grp) + EPS)
    qkv_scr[:, 2 * GDN_W:] = qkv[:, 2 * GDN_W:]

    ab = ab_ref[...]
    a_e = _dot_sel(ab, ea_ref[...], 3) + dtb_ref[...]
    b_e = _dot_sel(ab, eb_ref[...], 3)
    softplus = jnp.maximum(a_e, 0.0) + jnp.log(1.0 + jnp.exp(-jnp.abs(a_e)))
    gb_scr[:, :GDN_W] = -jnp.exp(alog_ref[...]) * softplus
    gb_scr[:, GDN_W:] = _sigmoid(b_e)

    r64 = lax.broadcasted_iota(jnp.int32, (GDN_CHUNK, LANES), 0)
    c64 = lax.broadcasted_iota(jnp.int32, (GDN_CHUNK, LANES), 1) % GDN_CHUNK
    causal = c64 <= r64
    strict = c64 < r64
    eye = jnp.where(c64 == r64, 1.0, 0.0)
    r128 = lax.broadcasted_iota(jnp.int32, (LANES, LANES), 0)
    c128 = lax.broadcasted_iota(jnp.int32, (LANES, LANES), 1)
    bdmask = (r128 // HEAD_DIM) == (c128 // HEAD_DIM)

    def bd(m):
        return jnp.where(bdmask, jnp.concatenate([m, m], axis=0), 0.0)

    lt_r = lax.broadcasted_iota(jnp.int32, (GDN_CHUNK, GDN_CHUNK), 0)
    lt_c = lax.broadcasted_iota(jnp.int32, (GDN_CHUNK, GDN_CHUNK), 1)
    ltri = jnp.where(lt_c <= lt_r, 1.0, 0.0)
    ones8 = jnp.ones((8, GDN_CHUNK), F32)
    up_r = lax.broadcasted_iota(jnp.int32, (GDN_CHUNK, GDN_W), 0)
    up_c = lax.broadcasted_iota(jnp.int32, (GDN_CHUNK, GDN_W), 1) % GDN_CHUNK
    upper = jnp.where(up_r <= up_c, 1.0, 0.0)
    for ci in range(n_chunks):
        g = gb_scr[ci * GDN_CHUNK:(ci + 1) * GDN_CHUNK, :GDN_W]
        gc_scr[ci * GDN_CHUNK:(ci + 1) * GDN_CHUNK, :] = _sel_dot(ltri, g, 3)
        gr_scr[ci * 8:(ci + 1) * 8, :] = _sel_dot(ones8, g * upper, 3)

    def par_body(gi, carry):
        inst = [(u, p) for u in range(GDN_GROUP) for p in pairs]
        rows = [pl.ds(pl.multiple_of((gi * GDN_GROUP + u) * GDN_CHUNK, GDN_CHUNK), GDN_CHUNK)
                for u in range(GDN_GROUP)]
        grow = [pl.ds(pl.multiple_of((gi * GDN_GROUP + u) * 8, 8), 8) for u in range(GDN_GROUP)]
        off = lambda s, p: slice(s * GDN_W + p * LANES, s * GDN_W + (p + 1) * LANES)
        qp = [qkv_scr[rows[u], off(0, p)] for u, p in inst]
        kp = [qkv_scr[rows[u], off(1, p)] for u, p in inst]
        vp = [qkv_scr[rows[u], off(2, p)] for u, p in inst]
        beta = [gb_scr[rows[u], off(1, p)] for u, p in inst]
        gc = [gc_scr[rows[u], lanes(p)] for u, p in inst]
        gr = [gr_scr[grow[u], lanes(p)][0:1, :] for u, p in inst]
        dmat = [jnp.exp(jnp.where(causal, a - b, NEG)) for a, b in zip(gc, gr)]
        kbd = [jnp.where(bdmask, jnp.concatenate([a, a], axis=0).T, 0.0) for a in kp]
        kb = [a * b for a, b in zip(kp, beta)]
        kk = [_dot3(a, b) for a, b in zip(kb, kbd)]
        qk = [_dot3(a, b) for a, b in zip(qp, kbd)]
        lm = [jnp.where(strict, a * d, 0.0) for a, d in zip(kk, dmat)]
        aintra = [a * d for a, d in zip(qk, dmat)]
        tinv = [eye - a for a in lm]
        lpow = lm
        for _ in range(5):
            lpow = [_dot3(a, bd(a)) for a in lpow]
            tinv = [a + _dot3(a, bd(b)) for a, b in zip(tinv, lpow)]
        egc = [jnp.exp(a) for a in gc]
        un = [_dot3(a, bd(v * b)) for a, v, b in zip(tinv, vp, beta)]
        wn = [_dot3(a, bd(b * e)) for a, b, e in zip(tinv, kb, egc)]
        for n, (u, p) in enumerate(inst):
            pre_scr[0, rows[u], lanes(p)] = un[n]
            pre_scr[1, rows[u], lanes(p)] = wn[n]
            pre_scr[2, rows[u], lanes(p)] = aintra[n]
            pre_scr[3, rows[u], lanes(p)] = qp[n] * egc[n]
            pre_scr[4, rows[u], lanes(p)] = kp[n] * jnp.exp(gc[n][GDN_CHUNK - 1:GDN_CHUNK, :] - gc[n])
        return carry

    lax.fori_loop(0, n_chunks // GDN_GROUP, par_body, 0)

    def rec_body(ci, carry):
        rows = pl.ds(pl.multiple_of(ci * GDN_CHUNK, GDN_CHUNK), GDN_CHUNK)
        tail = gc_scr[pl.ds(pl.multiple_of(ci * GDN_CHUNK + GDN_CHUNK - 8, 8), 8), :]
        egl = jnp.exp(tail[7:8, :])
        sbd = [st_scr[p] for p in pairs]
        un, wn, aintra, qe, kd = [[pre_scr[s, rows, lanes(p)] for p in pairs] for s in range(5)]
        ws = [_dot3(a, s) for a, s in zip(wn, sbd)]
        qs = [_dot3(a, s) for a, s in zip(qe, sbd)]
        v_new = [a - b for a, b in zip(un, ws)]
        o = [a + _dot3(b, bd(v)) for a, b, v in zip(qs, aintra, v_new)]
        upd = [_dot3(a.T, v) for a, v in zip(kd, v_new)]
        for p in pairs:
            st_scr[p] = sbd[p] * egl[:, lanes(p)] + jnp.where(bdmask, upd[p], 0.0)
            o_scr[rows, lanes(p)] = o[p]
        return carry

    lax.fori_loop(0, n_chunks, rec_body, 0)

    o = o_scr[...]
    oms = _group_sum(o * o, grp) * (1.0 / HEAD_DIM)
    z = x[:, cin:]
    o_ref[...] = o * lax.rsqrt(oms + EPS) * gain_ref[...] * (z * _sigmoid(z))


def _gdn(gx, ab, conv_w, ea, eb, alog, dtb, gain, grp, *, batch, seq, tb):
    nt = seq // tb
    cin = 3 * GDN_W
    row = lambda w: pl.BlockSpec((tb, w), lambda b, t: (b * nt + t, 0))
    full = lambda a: pl.BlockSpec(a.shape, lambda b, t: (0,) * a.ndim)
    return pl.pallas_call(
        functools.partial(_gdn_kernel, tb=tb),
        grid=(batch, nt),
        in_specs=[row(4 * GDN_W), row(LANES), full(conv_w), full(ea), full(eb), full(alog),
                  full(dtb), full(gain), full(grp)],
        out_specs=row(GDN_W),
        out_shape=jax.ShapeDtypeStruct((batch * seq, GDN_W), F32),
        scratch_shapes=[
            pltpu.VMEM((tb + 8, cin), F32),
            pltpu.VMEM((tb, cin), F32),
            pltpu.VMEM((tb, 2 * GDN_W), F32),
            pltpu.VMEM((tb, GDN_W), F32),
            pltpu.VMEM((tb // 8, GDN_W), F32),
            pltpu.VMEM((5, tb, GDN_W), F32),
            pltpu.VMEM((GDN_HEADS // 2, LANES, LANES), F32),
            pltpu.VMEM((tb, GDN_W), F32),
        ],
        compiler_params=_params("arbitrary", "arbitrary"),
        name="gdn",
    )(gx, ab, conv_w, ea, eb, alog, dtb, gain, grp)


def _memkv_kernel(mem_ref, g_ref, w_ref, kg_ref, grp_ref, kt_ref, vp_ref):
    x = mem_ref[0]
    ms = jnp.mean(x * x, axis=-1, keepdims=True)
    h = (x * lax.rsqrt(ms + EPS) * g_ref[...]).astype(BF16)
    kv = jnp.dot(h, w_ref[...], preferred_element_type=F32)
    km = kv[:, :MEM_W]
    vm = kv[:, MEM_W:]
    kms = _group_sum(km * km, grp_ref[...]) * (1.0 / HEAD_DIM)
    kt = (km * lax.rsqrt(kms + EPS) * kg_ref[...]).T
    n_mem = x.shape[0]
    top = lax.broadcasted_iota(jnp.int32, (LANES, n_mem), 0) < HEAD_DIM
    left = lax.broadcasted_iota(jnp.int32, (n_mem, LANES), 1) < HEAD_DIM
    for hh in range(MEM_HEADS):
        pr = slice((hh // 2) * LANES, (hh // 2 + 1) * LANES)
        keep_r = top if hh % 2 == 0 else jnp.logical_not(top)
        keep_c = left if hh % 2 == 0 else jnp.logical_not(left)
        kt_ref[0, hh] = jnp.where(keep_r, kt[pr, :], 0.0).astype(BF16)
        vp_ref[0, hh] = jnp.where(keep_c, vm[:, pr], 0.0).astype(BF16)


def _memkv(mem, gain, w, kg, grp):
    b, n_mem, d = mem.shape
    full = lambda a: pl.BlockSpec(a.shape, lambda i: (0,) * a.ndim)
    return pl.pallas_call(
        _memkv_kernel,
        grid=(b,),
        in_specs=[pl.BlockSpec((1, n_mem, d), lambda i: (i, 0, 0)), full(gain), full(w), full(kg),
                  full(grp)],
        out_specs=(pl.BlockSpec((1, MEM_HEADS, LANES, n_mem), lambda i: (i, 0, 0, 0)),
                   pl.BlockSpec((1, MEM_HEADS, n_mem, LANES), lambda i: (i, 0, 0, 0))),
        out_shape=(jax.ShapeDtypeStruct((b, MEM_HEADS, LANES, n_mem), BF16),
                   jax.ShapeDtypeStruct((b, MEM_HEADS, n_mem, LANES), BF16)),
        compiler_params=_params("parallel"),
        name="memkv",
    )(mem, gain, w, kg, grp)


def _memattn_kernel(q_ref, qg_ref, grp_ref, kt_ref, vp_ref, o_ref):
    q = q_ref[...]
    qms = _group_sum(q * q, grp_ref[...]) * (1.0 / HEAD_DIM)
    qn = (q * lax.rsqrt(qms + EPS) * qg_ref[...] * (HEAD_DIM ** -0.5)).astype(BF16)
    outs = []
    for pr in range(MEM_HEADS // 2):
        qp = qn[:, pr * LANES:(pr + 1) * LANES]
        acc = None
        for e in range(2):
            hh = 2 * pr + e
            l = jnp.dot(qp, kt_ref[0, hh], preferred_element_type=F32)
            l = l - jnp.max(l, axis=-1, keepdims=True)
            p = jnp.exp(l)
            p = p / jnp.sum(p, axis=-1, keepdims=True)
            o = jnp.dot(p.astype(BF16), vp_ref[0, hh], preferred_element_type=F32)
            acc = o if acc is None else acc + o
        outs.append(acc)
    o_ref[...] = jnp.concatenate(outs, axis=-1)


def _memattn(mq, qg, grp, kt, vp, *, batch, seq, tm):
    nt = seq // tm
    n_mem = kt.shape[-1]
    full = lambda a: pl.BlockSpec(a.shape, lambda b, t: (0,) * a.ndim)
    return pl.pallas_call(
        _memattn_kernel,
        grid=(batch, nt),
        in_specs=[
            pl.BlockSpec((tm, MEM_W), lambda b, t: (b * nt + t, 0)), full(qg), full(grp),
            pl.BlockSpec((1, MEM_HEADS, LANES, n_mem), lambda b, t: (b, 0, 0, 0)),
            pl.BlockSpec((1, MEM_HEADS, n_mem, LANES), lambda b, t: (b, 0, 0, 0)),
        ],
        out_specs=pl.BlockSpec((tm, MEM_W), lambda b, t: (b * nt + t, 0)),
        out_shape=jax.ShapeDtypeStruct((batch * seq, MEM_W), F32),
        compiler_params=_params("parallel", "parallel"),
        name="memattn",
    )(mq, qg, grp, kt, vp)


def _outproj_kernel(x_ref, a_ref, b_ref, c_ref, wa_ref, wb_ref, wc_ref, o_ref):
    y = _bdot(a_ref[...], wa_ref[...]) + _bdot(b_ref[...], wb_ref[...]) + _bdot(c_ref[...], wc_ref[...])
    o_ref[...] = x_ref[...] + y


def _outproj(x, a, b, c, wa, wb, wc, *, tm):
    m, d = x.shape
    row = lambda w: pl.BlockSpec((tm, w), lambda i: (i, 0))
    full = lambda arr: pl.BlockSpec(arr.shape, lambda i: (0,) * arr.ndim)
    return pl.pallas_call(
        _outproj_kernel,
        grid=(m // tm,),
        in_specs=[row(d), row(DSA_W), row(GDN_W), row(MEM_W), full(wa), full(wb), full(wc)],
        out_specs=row(d),
        out_shape=jax.ShapeDtypeStruct((m, d), F32),
        compiler_params=_params("parallel"),
        name="outproj",
    )(x, a, b, c, wa, wb, wc)


def _group_ones(width):
    idx = np.arange(width) // HEAD_DIM
    return jnp.asarray((idx[:, None] == idx[None, :]).astype(np.float32))


def _t5_bucket_np(n):
    max_exact = NUM_BUCKETS // 2
    nf = np.maximum(n, 1).astype(np.float32)
    large = max_exact + (np.log(nf / np.float32(max_exact)) / np.float32(math.log(MAX_DISTANCE / max_exact))
                         * (NUM_BUCKETS - max_exact)).astype(np.int32)
    large = np.minimum(large, NUM_BUCKETS - 1)
    return np.where(n < max_exact, n, large)


def _bucket_starts():
    dist = np.arange(2 * Q_BLOCK)
    bucket = _t5_bucket_np(dist)
    assert (np.diff(bucket) >= 0).all() and bucket[-1] == NUM_BUCKETS - 1
    return tuple(int(dist[bucket >= b].min()) for b in range(1, NUM_BUCKETS))


_BUCKET_STARTS = _bucket_starts()


def _cast_kernel(w_ref, o_ref):
    o_ref[...] = w_ref[...].astype(BF16)


def _to_bf16(w, *, rows):
    nl, nr, nc = w.shape
    spec = pl.BlockSpec((1, rows, nc), lambda l, r: (l, r, 0))
    return pl.pallas_call(
        _cast_kernel,
        grid=(nl, nr // rows),
        in_specs=[spec],
        out_specs=spec,
        out_shape=jax.ShapeDtypeStruct(w.shape, BF16),
        compiler_params=_params("parallel", "parallel"),
        name="cast",
    )(w)


def _pack_w_in(w):
    o = 3 * DSA_W
    iq = w[:, o:o + IDX_HEADS * IDX_DIM]
    o += IDX_HEADS * IDX_DIM
    kw = w[:, o:o + IDX_DIM + IDX_HEADS]
    o += IDX_DIM + IDX_HEADS
    g = w[:, o:o + 4 * GDN_W]
    o += 4 * GDN_W
    ab = w[:, o:o + 2 * GDN_HEADS]
    o += 2 * GDN_HEADS
    mq = w[:, o:o + MEM_W]
    pad = lambda a: jnp.pad(a, ((0, 0), (0, LANES - a.shape[1])))
    return jnp.concatenate([w[:, :3 * DSA_W], iq, pad(kw), g, pad(ab), mq], axis=1)


def _tile_heads(v, heads):
    return jnp.tile(v.astype(F32), heads).reshape(1, heads * HEAD_DIM)


def kernel(x, mem, ffn1_norm, ffn1_w_gate, ffn1_w_up, ffn1_w_down, mix_norm, w_in, dsa_q_norm, dsa_k_norm, rel_bias, gdn_conv, gdn_A_log, gdn_dt_bias, gdn_out_norm, mem_norm, w_mem_kv, mem_q_norm, mem_k_norm, w_out, ffn2_norm, ffn2_w_gate, ffn2_w_up, ffn2_w_down):
    batch, seq, d = x.shape
    depth = w_in.shape[0]
    m = batch * seq
    tm = min(512, seq)
    kc = min(512, seq)
    dff = ffn1_w_gate.shape[-1]
    tf = dff // 2 if (dff // 2) % LANES == 0 else dff

    grp_a = _group_ones(DSA_W)
    grp_m = _group_ones(MEM_W)
    ltri = jnp.asarray(np.tril(np.ones((kc, kc), np.float32))).astype(BF16)
    heads_of = np.arange(GDN_W) // HEAD_DIM
    ea = jnp.asarray((np.arange(LANES)[:, None] == heads_of[None, :]).astype(np.float32))
    eb = jnp.asarray((np.arange(LANES)[:, None] == heads_of[None, :] + GDN_HEADS).astype(np.float32))

    w_in_b = _to_bf16(w_in, rows=min(256, d))
    xf = x.reshape(m, d)
    for l in range(depth):
        xf = _ffn(xf, ffn1_norm[l], ffn1_w_gate[l].astype(BF16), ffn1_w_up[l].astype(BF16),
                  ffn1_w_down[l].astype(BF16), tm=tm, tf=tf)
        (q, k, vt, iq, ikb, kw, gx, ab, mq) = _inproj(
            xf, mix_norm[l].reshape(1, d), _pack_w_in(w_in_b[l]), grp_a,
            _tile_heads(dsa_q_norm[l], DSA_HEADS), _tile_heads(dsa_k_norm[l], DSA_HEADS), tm=tm)
        out_a = _dsa(q, iq, kw, k, vt, ikb, rel_bias, dsa_k_norm[l], ltri, batch=batch, seq=seq, kc=kc)
        out_b = _gdn(gx, ab, gdn_conv[l], ea, eb,
                     jnp.repeat(gdn_A_log[l].astype(F32), HEAD_DIM).reshape(1, GDN_W),
                     jnp.repeat(gdn_dt_bias[l].astype(F32), HEAD_DIM).reshape(1, GDN_W),
                     _tile_heads(gdn_out_norm[l], GDN_HEADS), grp_a, batch=batch, seq=seq, tb=tm)
        kt, vp = _memkv(mem, mem_norm[l].reshape(1, d), w_mem_kv[l].astype(BF16),
                        _tile_heads(mem_k_norm[l], MEM_HEADS), grp_m)
        out_c = _memattn(mq, _tile_heads(mem_q_norm[l], MEM_HEADS), grp_m, kt, vp,
                         batch=batch, seq=seq, tm=tm)
        wo = w_out[l].astype(BF16)
        xf = _outproj(xf, out_a, out_b, out_c, wo[:DSA_W], wo[DSA_W:DSA_W + GDN_W],
                      wo[DSA_W + GDN_W:], tm=tm)
        xf = _ffn(xf, ffn2_norm[l], ffn2_w_gate[l].astype(BF16), ffn2_w_up[l].astype(BF16),
                  ffn2_w_down[l].astype(BF16), tm=tm, tf=tf)
    return xf.reshape(batch, seq, d)
```

```python
import functools
import math

import jax
import jax.numpy as jnp
import numpy as np
from jax import lax
from jax.experimental import pallas as pl
from jax.experimental.pallas import tpu as pltpu

F32 = jnp.float32
BF16 = jnp.bfloat16

HEAD_DIM = 64
DSA_HEADS = 6
GDN_HEADS = 6
MEM_HEADS = 4
DSA_W = DSA_HEADS * HEAD_DIM
GDN_W = GDN_HEADS * HEAD_DIM
MEM_W = MEM_HEADS * HEAD_DIM
IDX_HEADS = 8
IDX_DIM = 32
TOPK_MAX = 256
Q_BLOCK = 128
GDN_CHUNK = 64
CONV_WIDTH = 4
NUM_BUCKETS = 32
MAX_DISTANCE = 128
EPS = 1e-6

LANES = 128
VMEM_LIMIT = 52 * 1024 * 1024
NEG = -1e30
LOG2E = math.log2(math.e)
V_ROWS = HEAD_DIM + 16
MAX_LOGIT_SPREAD = 80.0
PROBES_PER_ROUND = 4
FAST_ROUNDS = 4

SEG_A = 0
SEG_IQ = SEG_A + 3 * DSA_W
SEG_KW = SEG_IQ + IDX_HEADS * IDX_DIM
SEG_G = SEG_KW + LANES
SEG_AB = SEG_G + 4 * GDN_W
SEG_MQ = SEG_AB + LANES
IN_PACKED = SEG_MQ + MEM_W


def _bdot(a, b):
    return jnp.dot(a.astype(BF16), b.astype(BF16), preferred_element_type=F32)


def _split(x, terms):
    out = []
    for _ in range(terms - 1):
        hi = x.astype(BF16)
        out.append(hi)
        x = x - hi.astype(F32)
    out.append(x.astype(BF16))
    return out


def _dot3(a, b):
    ah, al = _split(a, 2)
    bh, bl = _split(b, 2)
    mm = lambda x, y: jnp.dot(x, y, preferred_element_type=F32)
    return mm(ah, bh) + (mm(ah, bl) + mm(al, bh))


def _dot_sel(a, sel, terms):
    selb = sel.astype(BF16)
    acc = None
    for piece in _split(a, terms):
        d = jnp.dot(piece, selb, preferred_element_type=F32)
        acc = d if acc is None else acc + d
    return acc


def _sel_dot(sel, b, terms):
    selb = sel.astype(BF16)
    acc = None
    for piece in _split(b, terms):
        d = jnp.dot(selb, piece, preferred_element_type=F32)
        acc = d if acc is None else acc + d
    return acc


def _group_sum(x, grp):
    return _dot_sel(x, grp, 2)


def _sigmoid(x):
    return 1.0 / (1.0 + jnp.exp(-x))


def _params(*sem):
    return pltpu.CompilerParams(dimension_semantics=sem, vmem_limit_bytes=VMEM_LIMIT)


def _ffn_kernel(x_ref, g_ref, wg_ref, wu_ref, wd_ref, o_ref, h_scr, acc_scr):
    f = pl.program_id(1)

    @pl.when(f == 0)
    def _():
        x = x_ref[...]
        ms = jnp.mean(x * x, axis=-1, keepdims=True)
        h_scr[...] = (x * lax.rsqrt(ms + EPS) * g_ref[...]).astype(BF16)
        acc_scr[...] = jnp.zeros_like(acc_scr)

    h = h_scr[...]
    a = jnp.dot(h, wg_ref[...], preferred_element_type=F32)
    u = jnp.dot(h, wu_ref[...], preferred_element_type=F32)
    z = (a * _sigmoid(a)) * u
    acc_scr[...] += jnp.dot(z.astype(BF16), wd_ref[...], preferred_element_type=F32)

    @pl.when(f == pl.num_programs(1) - 1)
    def _():
        o_ref[...] = x_ref[...] + 0.5 * acc_scr[...]


def _ffn(x, gain, wg, wu, wd, *, tm, tf):
    m, d = x.shape
    dff = wg.shape[1]
    return pl.pallas_call(
        _ffn_kernel,
        grid=(m // tm, dff // tf),
        in_specs=[
            pl.BlockSpec((tm, d), lambda i, f: (i, 0)),
            pl.BlockSpec((1, d), lambda i, f: (0, 0)),
            pl.BlockSpec((d, tf), lambda i, f: (0, f)),
            pl.BlockSpec((d, tf), lambda i, f: (0, f)),
            pl.BlockSpec((tf, d), lambda i, f: (f, 0)),
        ],
        out_specs=pl.BlockSpec((tm, d), lambda i, f: (i, 0)),
        out_shape=jax.ShapeDtypeStruct((m, d), F32),
        scratch_shapes=[pltpu.VMEM((tm, d), BF16), pltpu.VMEM((tm, d), F32)],
        compiler_params=_params("parallel", "arbitrary"),
        name="ffn",
    )(x, gain.reshape(1, d), wg, wu, wd)


def _inproj_kernel(x_ref, g_ref, w_ref, grp_ref, qg_ref, kg_ref,
                   q_ref, k_ref, vt_ref, iq_ref, ikb_ref, kw_ref, gx_ref, ab_ref, mq_ref):
    x = x_ref[...]
    ms = jnp.mean(x * x, axis=-1, keepdims=True)
    h = (x * lax.rsqrt(ms + EPS) * g_ref[...]).astype(BF16)
    p = jnp.dot(h, w_ref[...], preferred_element_type=F32)

    grp = grp_ref[...]
    dq = p[:, SEG_A:SEG_A + DSA_W]
    dk = p[:, SEG_A + DSA_W:SEG_A + 2 * DSA_W]
    dv = p[:, SEG_A + 2 * DSA_W:SEG_A + 3 * DSA_W]
    qms = _group_sum(dq * dq, grp) * (1.0 / HEAD_DIM)
    kms = _group_sum(dk * dk, grp) * (1.0 / HEAD_DIM)
    q_ref[...] = (dq * lax.rsqrt(qms + EPS) * qg_ref[...] * (HEAD_DIM ** -0.5 * LOG2E)).astype(BF16)
    k_ref[...] = (dk * lax.rsqrt(kms + EPS) * kg_ref[...]).astype(BF16)
    dvt = dv.T
    ones = jnp.ones((V_ROWS - HEAD_DIM, dvt.shape[1]), F32)
    for hd in range(DSA_HEADS):
        vt_ref[hd] = jnp.concatenate([dvt[hd * HEAD_DIM:(hd + 1) * HEAD_DIM], ones], axis=0).astype(BF16)

    iq_ref[...] = (p[:, SEG_IQ:SEG_IQ + IDX_HEADS * IDX_DIM] * (IDX_DIM ** -0.5)).astype(BF16)
    kw = p[:, SEG_KW:SEG_KW + LANES]
    kw_ref[...] = kw
    ikb_ref[...] = kw.astype(BF16)
    gx_ref[...] = p[:, SEG_G:SEG_G + 4 * GDN_W]
    ab_ref[...] = p[:, SEG_AB:SEG_AB + LANES]
    mq_ref[...] = p[:, SEG_MQ:SEG_MQ + MEM_W]


def _inproj(x, gain, w_packed, grp, qg, kg, *, tm):
    m, d = x.shape
    row = lambda w: pl.BlockSpec((tm, w), lambda i: (i, 0))
    full = lambda a: pl.BlockSpec(a.shape, lambda i: (0,) * a.ndim)
    out_shape = (
        jax.ShapeDtypeStruct((m, DSA_W), BF16),
        jax.ShapeDtypeStruct((m, DSA_W), BF16),
        jax.ShapeDtypeStruct((DSA_HEADS, V_ROWS, m), BF16),
        jax.ShapeDtypeStruct((m, IDX_HEADS * IDX_DIM), BF16),
        jax.ShapeDtypeStruct((m, LANES), BF16),
        jax.ShapeDtypeStruct((m, LANES), F32),
        jax.ShapeDtypeStruct((m, 4 * GDN_W), F32),
        jax.ShapeDtypeStruct((m, LANES), F32),
        jax.ShapeDtypeStruct((m, MEM_W), F32),
    )
    out_specs = (row(DSA_W), row(DSA_W), pl.BlockSpec((DSA_HEADS, V_ROWS, tm), lambda i: (0, 0, i)),
                 row(IDX_HEADS * IDX_DIM), row(LANES), row(LANES), row(4 * GDN_W),
                 row(LANES), row(MEM_W))
    return pl.pallas_call(
        _inproj_kernel,
        grid=(m // tm,),
        in_specs=[row(d), full(gain), full(w_packed), full(grp), full(qg), full(kg)],
        out_specs=out_specs,
        out_shape=out_shape,
        compiler_params=_params("parallel"),
        name="inproj",
    )(x, gain, w_packed, grp, qg, kg)


_DENORMAL_TOP = 0x007FFFFF


def _float_to_key(f):
    bits = lax.bitcast_convert_type(f, jnp.int32)
    mag = jnp.maximum((bits & jnp.int32(0x7FFFFFFF)) - _DENORMAL_TOP, 0)
    return jnp.where(bits >= 0, mag, -mag)


def _key_to_float(key):
    mag = jnp.abs(key)
    bits = jnp.where(mag > 0, mag + _DENORMAL_TOP, 0)
    return lax.bitcast_convert_type(jnp.where(key < 0, bits | jnp.int32(-2 ** 31), bits), F32)


def _upper_normal_quantile(p):
    pp = jnp.clip(jnp.minimum(p, 1.0 - p), 1e-30, 0.5)
    t = jnp.sqrt(-2.0 * jnp.log(pp))
    z = t - (2.515517 + t * (0.802853 + t * 0.010328)) / (1.0 + t * (1.432788 + t * (0.189269 + t * 0.001308)))
    return jnp.where(p <= 0.5, z, -z)


def _tree(op, x, group=8):
    parts = x.reshape(x.shape[0] // group, group, LANES)
    k = parts.shape[0]
    while k > 1:
        k //= 2
        parts = op(parts[:k], parts[k:2 * k])
    return parts[0]


def _dsa_kernel(q_ref, iq_ref, kwq_ref, k_ref, vt_ref, ik_ref, tab_ref, ltri_ref, o_ref,
                s_scr, iqt_scr, qpad_scr, bias_scr, mx_scr, brange_scr, *, kc, top_k):
    j = pl.program_id(1)
    per = kc // Q_BLOCK
    pad = kc - Q_BLOCK
    nch = j // per + 1
    q_pos = j * Q_BLOCK + lax.broadcasted_iota(jnp.int32, (1, LANES), 1)
    row_iota = lax.broadcasted_iota(jnp.int32, (kc, LANES), 0)

    def rows(i):
        return pl.ds(pl.multiple_of(j * Q_BLOCK - i * kc, Q_BLOCK), kc)

    def chunk_pairs(start, body, init):
        odd = (nch - start) % 2
        state = lax.fori_loop(0, odd, lambda _, st: body(start, st), init)
        return lax.fori_loop(0, (nch - start) // 2,
                             lambda t, st: body(start + odd + 2 * t + 1, body(start + odd + 2 * t, st)), state)

    @pl.when((pl.program_id(0) == 0) & (j == 0))
    def _():
        r = lax.broadcasted_iota(jnp.int32, (2 * Q_BLOCK, LANES), 0)
        c = lax.broadcasted_iota(jnp.int32, (2 * Q_BLOCK, LANES), 1)
        dist = Q_BLOCK + c - r
        bucket = jnp.zeros_like(dist)
        for first in _BUCKET_STARTS:
            bucket = bucket + jnp.where(dist >= first, 1, 0)
        for h in range(DSA_HEADS):
            far = tab_ref[NUM_BUCKETS - 1, h]
            delta = jnp.zeros((2 * Q_BLOCK, LANES), F32)
            b_hi = jnp.float32(0.0)
            b_lo = jnp.float32(0.0)
            for b in range(NUM_BUCKETS - 1):
                delta = jnp.where(bucket == b, tab_ref[b, h] - far, delta)
                b_hi = jnp.maximum(b_hi, (tab_ref[b, h] - far) * LOG2E)
                b_lo = jnp.minimum(b_lo, (tab_ref[b, h] - far) * LOG2E)
            bias_scr[h] = jnp.where(dist >= 0, delta * LOG2E, 0.0)
            brange_scr[0, h] = b_hi
            brange_scr[1, h] = b_lo

    iqt = iq_ref[...].astype(F32).T
    zpad = jnp.zeros((LANES - IDX_DIM, LANES), F32)
    for h in range(IDX_HEADS):
        iqt_scr[:, h * LANES:(h + 1) * LANES] = jnp.concatenate(
            [iqt[h * IDX_DIM:(h + 1) * IDX_DIM], zpad], axis=0).astype(BF16)
    wt = kwq_ref[...].T[IDX_DIM:IDX_DIM + IDX_HEADS, :] * (IDX_HEADS ** -0.5)
    qt = q_ref[...].astype(F32).T
    half = lax.broadcasted_iota(jnp.int32, (LANES, LANES), 0) < HEAD_DIM
    for h in range(DSA_HEADS):
        pair = qt[(h // 2) * LANES:(h // 2 + 1) * LANES]
        keep = half if h % 2 == 0 else jnp.logical_not(half)
        qpad_scr[:, h * LANES:(h + 1) * LANES] = jnp.where(keep, pair, 0.0).astype(BF16)

    kmax = tab_ref[NUM_BUCKETS, 0]
    spread = jnp.zeros((1, LANES), F32)
    for h in range(DSA_HEADS):
        b_hi = brange_scr[0, h]
        b_lo = brange_scr[1, h]
        qh = qt[h * HEAD_DIM:(h + 1) * HEAD_DIM]
        bound = jnp.sqrt(jnp.sum(qh * qh, axis=0, keepdims=True)) * kmax
        mx_scr[h:h + 1, :] = bound + b_hi
        spread = jnp.maximum(spread, 2.0 * bound + (b_hi - b_lo))
    bound_ok = jnp.max(spread) <= MAX_LOGIT_SPREAD

    def score_chunk(i, carry):
        d = jnp.dot(ik_ref[0, rows(i), :], iqt_scr[...], preferred_element_type=F32)
        acc = jnp.maximum(d[:, :LANES], 0.0) * wt[0:1, :]
        for h in range(1, IDX_HEADS):
            acc = acc + jnp.maximum(d[:, h * LANES:(h + 1) * LANES], 0.0) * wt[h:h + 1, :]
        key = row_iota + (j * Q_BLOCK - i * kc - pad)
        adm = jnp.where(key >= 0, key, q_pos + 1) <= q_pos
        s_scr[rows(i), :] = jnp.where(adm, acc, -jnp.inf)
        fin = jnp.where(adm, acc, 0.0)
        tot, sq, top = carry
        return (tot + _tree(jnp.add, fin), sq + _tree(jnp.add, fin * fin),
                jnp.maximum(top, _tree(jnp.maximum, jnp.where(adm, acc, -jnp.inf))))

    zero8 = jnp.zeros((8, LANES), F32)
    tot, sq, top = chunk_pairs(0, score_chunk, (zero8, zero8, jnp.full((8, LANES), -jnp.inf, F32)))

    def count(pred):
        def body(i, cnt):
            return cnt + _tree(jnp.add, jnp.where(pred(s_scr[rows(i), :]), 1.0, 0.0))
        cnt = chunk_pairs(0, body, zero8)
        return jnp.sum(cnt, axis=0, keepdims=True)

    def max_below(t):
        def body(i, m):
            s = s_scr[rows(i), :]
            return jnp.maximum(m, _tree(jnp.maximum, jnp.where(s < t, s, -jnp.inf)))
        m = lax.fori_loop(0, nch, body, jnp.full((8, LANES), -jnp.inf, F32))
        return jnp.max(m, axis=0, keepdims=True)

    kf = float(top_k)
    n_adm = (q_pos + 1).astype(F32)
    mean = jnp.sum(tot, axis=0, keepdims=True) / n_adm
    std = jnp.sqrt(jnp.maximum(jnp.sum(sq, axis=0, keepdims=True) / n_adm - mean * mean, 0.0))
    first_guess = mean + _upper_normal_quantile(kf / n_adm) * std
    lowest = jnp.float32(-3.0e38)
    top1 = jnp.minimum(jnp.max(top, axis=0, keepdims=True), -lowest)

    def update(st, key):
        lo, hi, c_lo, c_hi, lo_set, hi_set, done, run = st
        key = jnp.clip(key, lo + 1, hi - 1)
        t = _key_to_float(key)
        c = count(lambda s: s >= t)
        active = done < 0.5
        up = active & (c >= kf)
        dn = active & (c < kf)
        lo, c_lo, lo_set = jnp.where(up, key, lo), jnp.where(up, c, c_lo), jnp.where(up, 1.0, lo_set)
        hi, c_hi, hi_set = jnp.where(dn, key, hi), jnp.where(dn, c, c_hi), jnp.where(dn, 1.0, hi_set)
        done = jnp.where((c_lo == kf) | (hi <= lo + 1), 1.0, done)
        return lo, hi, c_lo, c_hi, lo_set, hi_set, done, run

    def guess(st, midpoint):
        lo, hi, c_lo, c_hi, lo_set, hi_set, done, run = st
        v_lo, v_hi = _key_to_float(lo), _key_to_float(hi)
        both = (lo_set > 0.5) & (hi_set > 0.5)
        step = std * 0.25 * jnp.exp2(run)
        if midpoint:
            inner = 0.5 * v_lo + 0.5 * v_hi
        else:
            l_lo, l_hi = jnp.log(c_lo), jnp.log(jnp.maximum(c_hi, 0.5))
            frac = jnp.clip((l_lo - math.log(kf - 0.5)) / jnp.maximum(l_lo - l_hi, 1e-6), 0.0, 1.0)
            inner = v_lo + (v_hi - v_lo) * frac
        t = jnp.where(both, inner,
                      jnp.where(hi_set > 0.5, v_hi - step, jnp.where(lo_set > 0.5, v_lo + step, first_guess)))
        t = jnp.where(t != t, 0.0, t)
        run = jnp.where(both, 0.0, run + 1.0)
        return _float_to_key(t), (lo, hi, c_lo, c_hi, lo_set, hi_set, done, run)

    def pending(st):
        return 1.0 - jnp.min(st[6])

    flag0 = jnp.zeros((1, LANES), F32)
    st = (jnp.full((1, LANES), _float_to_key(lowest), jnp.int32), _float_to_key(top1) + 1,
          n_adm, flag0, flag0, flag0, jnp.where(n_adm <= kf, 1.0, 0.0), flag0)

    def fast_round(carry):
        p, _, st = carry
        for r in range(PROBES_PER_ROUND):
            key, st = guess(st, midpoint=(r == PROBES_PER_ROUND - 1))
            st = update(st, key)
        return p + 1, pending(st), st

    _, _, st = lax.while_loop(lambda c: (c[0] < FAST_ROUNDS) & (c[1] > 0.5), fast_round,
                              (jnp.int32(0), pending(st), st))

    def safe_round(carry):
        p, _, st = carry
        st = update(st, st[0] + lax.shift_right_logical(st[1] - st[0], 1))
        lo, hi, done = st[0], st[1], st[6]
        below = _float_to_key(max_below(_key_to_float(hi)))
        hi = jnp.where(done < 0.5, jnp.clip(below + 1, lo + 1, hi), hi)
        done = jnp.where(hi <= lo + 1, 1.0, done)
        st = update((lo, hi) + st[2:6] + (done, st[7]), hi - 1)
        return p + 1, pending(st), st

    _, _, st = lax.while_loop(lambda c: (c[1] > 0.5) & (c[0] < 34), safe_round,
                              (jnp.int32(0), pending(st), st))

    thr = _key_to_float(st[0])
    has_ties = jnp.max(st[2]) > kf

    @pl.when(jnp.logical_not(has_ties))
    def _():
        def mask_chunk(i, carry):
            s_scr[rows(i), :] = jnp.where(s_scr[rows(i), :] >= thr, 0.0, NEG)
            return carry
        lax.fori_loop(0, nch, mask_chunk, 0)

    @pl.when(has_ties)
    def _():
        need = kf - count(lambda s: s > thr)

        def mask_chunk(t, run):
            i = nch - 1 - t
            s = s_scr[rows(i), :]
            tie = jnp.where(s == thr, 1.0, 0.0)
            pref = jnp.dot(ltri_ref[...], tie.astype(BF16), preferred_element_type=F32) + run
            tie_sel = jnp.where(pref <= need, tie, 0.0)
            sel = jnp.where(s > thr, 1.0, tie_sel)
            s_scr[rows(i), :] = jnp.where(sel > 0.5, 0.0, NEG)
            return run + jnp.sum(tie, axis=0, keepdims=True)
        lax.fori_loop(0, nch, mask_chunk, jnp.zeros((1, LANES), F32))

    def logits(i, with_bias):
        msk = s_scr[rows(i), :]
        out = []
        for pr in range(DSA_HEADS // 2):
            l2 = jnp.dot(k_ref[0, rows(i), pr * LANES:(pr + 1) * LANES],
                         qpad_scr[:, 2 * pr * LANES:(2 * pr + 2) * LANES],
                         preferred_element_type=F32)
            for e in range(2):
                l = l2[:, e * LANES:(e + 1) * LANES] + msk
                if with_bias:
                    far_rows = kc - 2 * Q_BLOCK
                    l = jnp.concatenate([l[:far_rows], l[far_rows:] + bias_scr[2 * pr + e]], axis=0)
                out.append(l)
        return out

    def max_step(i, ms, with_bias):
        return tuple(jnp.maximum(m, _tree(jnp.maximum, l)) for m, l in zip(ms, logits(i, with_bias)))

    @pl.when(jnp.logical_not(bound_ok))
    def _():
        ms = max_step(0, tuple(jnp.full((8, LANES), NEG, F32) for _ in range(DSA_HEADS)), True)
        ms = lax.fori_loop(1, nch, lambda i, m: max_step(i, m, False), ms)
        for h in range(DSA_HEADS):
            mx_scr[h:h + 1, :] = jnp.max(ms[h], axis=0, keepdims=True)

    mx = [mx_scr[h:h + 1, :] for h in range(DSA_HEADS)]

    def pv_step(i, accs, with_bias):
        return tuple(
            acc + jnp.dot(vt_ref[h, :, rows(i)], jnp.exp2(l - mx[h]).astype(BF16),
                          preferred_element_type=F32)
            for h, (acc, l) in enumerate(zip(accs, logits(i, with_bias))))

    accs = pv_step(0, tuple(jnp.zeros((V_ROWS, LANES), F32) for _ in range(DSA_HEADS)), True)
    accs = chunk_pairs(1, lambda i, a: pv_step(i, a, False), accs)
    outs = [acc[:HEAD_DIM] / acc[HEAD_DIM:HEAD_DIM + 1] for acc in accs]
    o_ref[...] = jnp.concatenate(outs, axis=0).T


def _dsa(q, iq, kw, k, vt, ikb, rel_bias, k_gain, ltri, *, batch, seq, kc):
    nb = seq // Q_BLOCK
    top_k = min(TOPK_MAX, seq // 4)
    pad = kc - Q_BLOCK
    seqp = seq + pad
    kp = jnp.pad(k.reshape(batch, seq, DSA_W), ((0, 0), (pad, 0), (0, 0)))
    ikp = jnp.pad(ikb.reshape(batch, seq, LANES), ((0, 0), (pad, 0), (0, 0)))
    vtp = jnp.pad(vt.reshape(DSA_HEADS, V_ROWS, batch, seq),
                  ((0, 0), (0, 0), (0, 0), (pad, 0))).reshape(DSA_HEADS, V_ROWS, batch * seqp)
    kmax = (HEAD_DIM ** 0.5 * 1.01) * jnp.max(jnp.abs(k_gain.astype(F32)))
    table = jnp.concatenate([rel_bias.astype(F32), jnp.full((1, DSA_HEADS), kmax, F32)], axis=0)
    qrow = lambda w: pl.BlockSpec((Q_BLOCK, w), lambda b, j: (b * nb + j, 0))
    return pl.pallas_call(
        functools.partial(_dsa_kernel, kc=kc, top_k=top_k),
        grid=(batch, nb),
        in_specs=[
            qrow(DSA_W), qrow(IDX_HEADS * IDX_DIM), qrow(LANES),
            pl.BlockSpec((1, seqp, DSA_W), lambda b, j: (b, 0, 0)),
            pl.BlockSpec((DSA_HEADS, V_ROWS, seqp), lambda b, j: (0, 0, b)),
            pl.BlockSpec((1, seqp, LANES), lambda b, j: (b, 0, 0)),
            pl.BlockSpec(memory_space=pltpu.SMEM),
            pl.BlockSpec(ltri.shape, lambda b, j: (0, 0)),
        ],
        out_specs=qrow(DSA_W),
        out_shape=jax.ShapeDtypeStruct((batch * seq, DSA_W), F32),
        scratch_shapes=[
            pltpu.VMEM((seqp, LANES), F32),
            pltpu.VMEM((LANES, IDX_HEADS * LANES), BF16),
            pltpu.VMEM((LANES, DSA_HEADS * LANES), BF16),
            pltpu.VMEM((DSA_HEADS, 2 * Q_BLOCK, LANES), F32),
            pltpu.VMEM((8, LANES), F32),
            pltpu.SMEM((2, DSA_HEADS), F32),
        ],
        compiler_params=_params("arbitrary", "arbitrary"),
        name="dsa",
    )(q, iq, kw, kp, vtp, ikp, table, ltri)


GDN_GROUP = 2


def _gdn_kernel(x_ref, ab_ref, cw_ref, ea_ref, eb_ref, alog_ref, dtb_ref, gain_ref, grp_ref,
                o_ref, xpad_scr, qkv_scr, gb_scr, gc_scr, gr_scr, pre_scr, st_scr, o_scr, *, tb):
    t = pl.program_id(1)
    cin = 3 * GDN_W
    n_chunks = tb // GDN_CHUNK
    pairs = range(GDN_HEADS // 2)
    lanes = lambda p: slice(p * LANES, (p + 1) * LANES)

    @pl.when(t == 0)
    def _():
        xpad_scr[0:8, :] = jnp.zeros((8, cin), F32)
        st_scr[...] = jnp.zeros_like(st_scr)

    x = x_ref[...]
    xpad_scr[8:8 + tb, :] = x[:, :cin]
    conv = jnp.zeros((tb, cin), F32)
    for jj in range(CONV_WIDTH):
        conv = conv + cw_ref[jj:jj + 1, :] * xpad_scr[pl.ds(8 - (CONV_WIDTH - 1) + jj, tb), :]
    xpad_scr[0:8, :] = x[tb - 8:tb, :cin]
    qkv = conv * _sigmoid(conv)
    grp = grp_ref[...]
    q = qkv[:, :GDN_W]
    k = qkv[:, GDN_W:2 * GDN_W]
    qkv_scr[:, :GDN_W] = q * lax.rsqrt(_group_sum(q * q, grp) + EPS) * (HEAD_DIM ** -0.5)
    qkv_scr[:, GDN_W:2 * GDN_W] = k * lax.rsqrt(_group_sum(k * k, grp) + EPS)
    qkv_scr[:, 2 * GDN_W:] = qkv[:, 2 * GDN_W:]

    ab = ab_ref[...]
    a_e = _dot_sel(ab, ea_ref[...], 3) + dtb_ref[...]
    b_e = _dot_sel(ab, eb_ref[...], 3)
    softplus = jnp.maximum(a_e, 0.0) + jnp.log(1.0 + jnp.exp(-jnp.abs(a_e)))
    gb_scr[:, :GDN_W] = -jnp.exp(alog_ref[...]) * softplus
    gb_scr[:, GDN_W:] = _sigmoid(b_e)

    r64 = lax.broadcasted_iota(jnp.int32, (GDN_CHUNK, LANES), 0)
    c64 = lax.broadcasted_iota(jnp.int32, (GDN_CHUNK, LANES), 1) % GDN_CHUNK
    causal = c64 <= r64
    strict = c64 < r64
    eye = jnp.where(c64 == r64, 1.0, 0.0)
    r128 = lax.broadcasted_iota(jnp.int32, (LANES, LANES), 0)
    c128 = lax.broadcasted_iota(jnp.int32, (LANES, LANES), 1)
    bdmask = (r128 // HEAD_DIM) == (c128 // HEAD_DIM)

    def bd(m):
        return jnp.where(bdmask, jnp.concatenate([m, m], axis=0), 0.0)

    lt_r = lax.broadcasted_iota(jnp.int32, (GDN_CHUNK, GDN_CHUNK), 0)
    lt_c = lax.broadcasted_iota(jnp.int32, (GDN_CHUNK, GDN_CHUNK), 1)
    ltri = jnp.where(lt_c <= lt_r, 1.0, 0.0)
    ones8 = jnp.ones((8, GDN_CHUNK), F32)
    up_r = lax.broadcasted_iota(jnp.int32, (GDN_CHUNK, GDN_W), 0)
    up_c = lax.broadcasted_iota(jnp.int32, (GDN_CHUNK, GDN_W), 1) % GDN_CHUNK
    upper = jnp.where(up_r <= up_c, 1.0, 0.0)
    for ci in range(n_chunks):
        g = gb_scr[ci * GDN_CHUNK:(ci + 1) * GDN_CHUNK, :GDN_W]
        gc_scr[ci * GDN_CHUNK:(ci + 1) * GDN_CHUNK, :] = _sel_dot(ltri, g, 3)
        gr_scr[ci * 8:(ci + 1) * 8, :] = _sel_dot(ones8, g * upper, 3)

    def par_body(gi, carry):
        inst = [(u, p) for u in range(GDN_GROUP) for p in pairs]
        rows = [pl.ds(pl.multiple_of((gi * GDN_GROUP + u) * GDN_CHUNK, GDN_CHUNK), GDN_CHUNK)
                for u in range(GDN_GROUP)]
        grow = [pl.ds(pl.multiple_of((gi * GDN_GROUP + u) * 8, 8), 8) for u in range(GDN_GROUP)]
        off = lambda s, p: slice(s * GDN_W + p * LANES, s * GDN_W + (p + 1) * LANES)
        qp = [qkv_scr[rows[u], off(0, p)] for u, p in inst]
        kp = [qkv_scr[rows[u], off(1, p)] for u, p in inst]
        vp = [qkv_scr[rows[u], off(2, p)] for u, p in inst]
        beta = [gb_scr[rows[u], off(1, p)] for u, p in inst]
        gc = [gc_scr[rows[u], lanes(p)] for u, p in inst]
        gr = [gr_scr[grow[u], lanes(p)][0:1, :] for u, p in inst]
        dmat = [jnp.exp(jnp.where(causal, a - b, NEG)) for a, b in zip(gc, gr)]
        kbd = [jnp.where(bdmask, jnp.concatenate([a, a], axis=0).T, 0.0) for a in kp]
        kb = [a * b for a, b in zip(kp, beta)]
        kk = [_dot3(a, b) for a, b in zip(kb, kbd)]
        qk = [_dot3(a, b) for a, b in zip(qp, kbd)]
        lm = [jnp.where(strict, a * d, 0.0) for a, d in zip(kk, dmat)]
        aintra = [a * d for a, d in zip(qk, dmat)]
        tinv = [eye - a for a in lm]
        lpow = lm
        for _ in range(5):
            lpow = [_dot3(a, bd(a)) for a in lpow]
            tinv = [a + _dot3(a, bd(b)) for a, b in zip(tinv, lpow)]
        egc = [jnp.exp(a) for a in gc]
        un = [_dot3(a, bd(v * b)) for a, v, b in zip(tinv, vp, beta)]
        wn = [_dot3(a, bd(b * e)) for a, b, e in zip(tinv, kb, egc)]
        for n, (u, p) in enumerate(inst):
            pre_scr[0, rows[u], lanes(p)] = un[n]
            pre_scr[1, rows[u], lanes(p)] = wn[n]
            pre_scr[2, rows[u], lanes(p)] = aintra[n]
            pre_scr[3, rows[u], lanes(p)] = qp[n] * egc[n]
            pre_scr[4, rows[u], lanes(p)] = kp[n] * jnp.exp(gc[n][GDN_CHUNK - 1:GDN_CHUNK, :] - gc[n])
        return carry

    lax.fori_loop(0, n_chunks // GDN_GROUP, par_body, 0)

    def rec_body(ci, carry):
        rows = pl.ds(pl.multiple_of(ci * GDN_CHUNK, GDN_CHUNK), GDN_CHUNK)
        tail = gc_scr[pl.ds(pl.multiple_of(ci * GDN_CHUNK + GDN_CHUNK - 8, 8), 8), :]
        egl = jnp.exp(tail[7:8, :])
        sbd = [st_scr[p] for p in pairs]
        un, wn, aintra, qe, kd = [[pre_scr[s, rows, lanes(p)] for p in pairs] for s in range(5)]
        ws = [_dot3(a, s) for a, s in zip(wn, sbd)]
        qs = [_dot3(a, s) for a, s in zip(qe, sbd)]
        v_new = [a - b for a, b in zip(un, ws)]
        o = [a + _dot3(b, bd(v)) for a, b, v in zip(qs, aintra, v_new)]
        upd = [_dot3(a.T, v) for a, v in zip(kd, v_new)]
        for p in pairs:
            st_scr[p] = sbd[p] * egl[:, lanes(p)] + jnp.where(bdmask, upd[p], 0.0)
            o_scr[rows, lanes(p)] = o[p]
        return carry

    lax.fori_loop(0, n_chunks, rec_body, 0)

    o = o_scr[...]
    oms = _group_sum(o * o, grp) * (1.0 / HEAD_DIM)
    z = x[:, cin:]
    o_ref[...] = o * lax.rsqrt(oms + EPS) * gain_ref[...] * (z * _sigmoid(z))


def _gdn(gx, ab, conv_w, ea, eb, alog, dtb, gain, grp, *, batch, seq, tb):
    nt = seq // tb
    cin = 3 * GDN_W
    row = lambda w: pl.BlockSpec((tb, w), lambda b, t: (b * nt + t, 0))
    full = lambda a: pl.BlockSpec(a.shape, lambda b, t: (0,) * a.ndim)
    return pl.pallas_call(
        functools.partial(_gdn_kernel, tb=tb),
        grid=(batch, nt),
        in_specs=[row(4 * GDN_W), row(LANES), full(conv_w), full(ea), full(eb), full(alog),
                  full(dtb), full(gain), full(grp)],
        out_specs=row(GDN_W),
        out_shape=jax.ShapeDtypeStruct((batch * seq, GDN_W), F32),
        scratch_shapes=[
            pltpu.VMEM((tb + 8, cin), F32),
            pltpu.VMEM((tb, cin), F32),
            pltpu.VMEM((tb, 2 * GDN_W), F32),
            pltpu.VMEM((tb, GDN_W), F32),
            pltpu.VMEM((tb // 8, GDN_W), F32),
            pltpu.VMEM((5, tb, GDN_W), F32),
            pltpu.VMEM((GDN_HEADS // 2, LANES, LANES), F32),
            pltpu.VMEM((tb, GDN_W), F32),
        ],
        compiler_params=_params("arbitrary", "arbitrary"),
        name="gdn",
    )(gx, ab, conv_w, ea, eb, alog, dtb, gain, grp)


def _memkv_kernel(mem_ref, g_ref, w_ref, kg_ref, grp_ref, kt_ref, vp_ref):
    x = mem_ref[0]
    ms = jnp.mean(x * x, axis=-1, keepdims=True)
    h = (x * lax.rsqrt(ms + EPS) * g_ref[...]).astype(BF16)
    kv = jnp.dot(h, w_ref[...], preferred_element_type=F32)
    km = kv[:, :MEM_W]
    vm = kv[:, MEM_W:]
    kms = _group_sum(km * km, grp_ref[...]) * (1.0 / HEAD_DIM)
    kt = (km * lax.rsqrt(kms + EPS) * kg_ref[...]).T
    n_mem = x.shape[0]
    top = lax.broadcasted_iota(jnp.int32, (LANES, n_mem), 0) < HEAD_DIM
    left = lax.broadcasted_iota(jnp.int32, (n_mem, LANES), 1) < HEAD_DIM
    for hh in range(MEM_HEADS):
        pr = slice((hh // 2) * LANES, (hh // 2 + 1) * LANES)
        keep_r = top if hh % 2 == 0 else jnp.logical_not(top)
        keep_c = left if hh % 2 == 0 else jnp.logical_not(left)
        kt_ref[0, hh] = jnp.where(keep_r, kt[pr, :], 0.0).astype(BF16)
        vp_ref[0, hh] = jnp.where(keep_c, vm[:, pr], 0.0).astype(BF16)


def _memkv(mem, gain, w, kg, grp):
    b, n_mem, d = mem.shape
    full = lambda a: pl.BlockSpec(a.shape, lambda i: (0,) * a.ndim)
    return pl.pallas_call(
        _memkv_kernel,
        grid=(b,),
        in_specs=[pl.BlockSpec((1, n_mem, d), lambda i: (i, 0, 0)), full(gain), full(w), full(kg),
                  full(grp)],
        out_specs=(pl.BlockSpec((1, MEM_HEADS, LANES, n_mem), lambda i: (i, 0, 0, 0)),
                   pl.BlockSpec((1, MEM_HEADS, n_mem, LANES), lambda i: (i, 0, 0, 0))),
        out_shape=(jax.ShapeDtypeStruct((b, MEM_HEADS, LANES, n_mem), BF16),
                   jax.ShapeDtypeStruct((b, MEM_HEADS, n_mem, LANES), BF16)),
        compiler_params=_params("parallel"),
        name="memkv",
    )(mem, gain, w, kg, grp)


def _memattn_kernel(q_ref, qg_ref, grp_ref, kt_ref, vp_ref, o_ref):
    q = q_ref[...]
    qms = _group_sum(q * q, grp_ref[...]) * (1.0 / HEAD_DIM)
    qn = (q * lax.rsqrt(qms + EPS) * qg_ref[...] * (HEAD_DIM ** -0.5)).astype(BF16)
    outs = []
    for pr in range(MEM_HEADS // 2):
        qp = qn[:, pr * LANES:(pr + 1) * LANES]
        acc = None
        for e in range(2):
            hh = 2 * pr + e
            l = jnp.dot(qp, kt_ref[0, hh], preferred_element_type=F32)
            l = l - jnp.max(l, axis=-1, keepdims=True)
            p = jnp.exp(l)
            p = p / jnp.sum(p, axis=-1, keepdims=True)
            o = jnp.dot(p.astype(BF16), vp_ref[0, hh], preferred_element_type=F32)
            acc = o if acc is None else acc + o
        outs.append(acc)
    o_ref[...] = jnp.concatenate(outs, axis=-1)


def _memattn(mq, qg, grp, kt, vp, *, batch, seq, tm):
    nt = seq // tm
    n_mem = kt.shape[-1]
    full = lambda a: pl.BlockSpec(a.shape, lambda b, t: (0,) * a.ndim)
    return pl.pallas_call(
        _memattn_kernel,
        grid=(batch, nt),
        in_specs=[
            pl.BlockSpec((tm, MEM_W), lambda b, t: (b * nt + t, 0)), full(qg), full(grp),
            pl.BlockSpec((1, MEM_HEADS, LANES, n_mem), lambda b, t: (b, 0, 0, 0)),
            pl.BlockSpec((1, MEM_HEADS, n_mem, LANES), lambda b, t: (b, 0, 0, 0)),
        ],
        out_specs=pl.BlockSpec((tm, MEM_W), lambda b, t: (b * nt + t, 0)),
        out_shape=jax.ShapeDtypeStruct((batch * seq, MEM_W), F32),
        compiler_params=_params("parallel", "parallel"),
        name="memattn",
    )(mq, qg, grp, kt, vp)


def _outproj_kernel(x_ref, a_ref, b_ref, c_ref, wa_ref, wb_ref, wc_ref, o_ref):
    y = _bdot(a_ref[...], wa_ref[...]) + _bdot(b_ref[...], wb_ref[...]) + _bdot(c_ref[...], wc_ref[...])
    o_ref[...] = x_ref[...] + y


def _outproj(x, a, b, c, wa, wb, wc, *, tm):
    m, d = x.shape
    row = lambda w: pl.BlockSpec((tm, w), lambda i: (i, 0))
    full = lambda arr: pl.BlockSpec(arr.shape, lambda i: (0,) * arr.ndim)
    return pl.pallas_call(
        _outproj_kernel,
        grid=(m // tm,),
        in_specs=[row(d), row(DSA_W), row(GDN_W), row(MEM_W), full(wa), full(wb), full(wc)],
        out_specs=row(d),
        out_shape=jax.ShapeDtypeStruct((m, d), F32),
        compiler_params=_params("parallel"),
        name="outproj",
    )(x, a, b, c, wa, wb, wc)


def _group_ones(width):
    idx = np.arange(width) // HEAD_DIM
    return jnp.asarray((idx[:, None] == idx[None, :]).astype(np.float32))


def _t5_bucket_np(n):
    max_exact = NUM_BUCKETS // 2
    nf = np.maximum(n, 1).astype(np.float32)
    large = max_exact + (np.log(nf / np.float32(max_exact)) / np.float32(math.log(MAX_DISTANCE / max_exact))
                         * (NUM_BUCKETS - max_exact)).astype(np.int32)
    large = np.minimum(large, NUM_BUCKETS - 1)
    return np.where(n < max_exact, n, large)


def _bucket_starts():
    dist = np.arange(2 * Q_BLOCK)
    bucket = _t5_bucket_np(dist)
    assert (np.diff(bucket) >= 0).all() and bucket[-1] == NUM_BUCKETS - 1
    return tuple(int(dist[bucket >= b].min()) for b in range(1, NUM_BUCKETS))


_BUCKET_STARTS = _bucket_starts()


def _cast_kernel(w_ref, o_ref):
    o_ref[...] = w_ref[...].astype(BF16)


def _to_bf16(w, *, rows):
    nl, nr, nc = w.shape
    spec = pl.BlockSpec((1, rows, nc), lambda l, r: (l, r, 0))
    return pl.pallas_call(
        _cast_kernel,
        grid=(nl, nr // rows),
        in_specs=[spec],
        out_specs=spec,
        out_shape=jax.ShapeDtypeStruct(w.shape, BF16),
        compiler_params=_params("parallel", "parallel"),
        name="cast",
    )(w)


def _pack_w_in(w):
    o = 3 * DSA_W
    iq = w[:, o:o + IDX_HEADS * IDX_DIM]
    o += IDX_HEADS * IDX_DIM
    kw = w[:, o:o + IDX_DIM + IDX_HEADS]
    o += IDX_DIM + IDX_HEADS
    g = w[:, o:o + 4 * GDN_W]
    o += 4 * GDN_W
    ab = w[:, o:o + 2 * GDN_HEADS]
    o += 2 * GDN_HEADS
    mq = w[:, o:o + MEM_W]
    pad = lambda a: jnp.pad(a, ((0, 0), (0, LANES - a.shape[1])))
    return jnp.concatenate([w[:, :3 * DSA_W], iq, pad(kw), g, pad(ab), mq], axis=1)


def _tile_heads(v, heads):
    return jnp.tile(v.astype(F32), heads).reshape(1, heads * HEAD_DIM)


def kernel(x, mem, ffn1_norm, ffn1_w_gate, ffn1_w_up, ffn1_w_down, mix_norm, w_in, dsa_q_norm, dsa_k_norm, rel_bias, gdn_conv, gdn_A_log, gdn_dt_bias, gdn_out_norm, mem_norm, w_mem_kv, mem_q_norm, mem_k_norm, w_out, ffn2_norm, ffn2_w_gate, ffn2_w_up, ffn2_w_down):
    batch, seq, d = x.shape
    depth = w_in.shape[0]
    m = batch * seq
    tm = min(512, seq)
    kc = min(512, seq)
    dff = ffn1_w_gate.shape[-1]
    tf = dff // 2 if (dff // 2) % LANES == 0 else dff

    grp_a = _group_ones(DSA_W)
    grp_m = _group_ones(MEM_W)
    ltri = jnp.asarray(np.tril(np.ones((kc, kc), np.float32))).astype(BF16)
    heads_of = np.arange(GDN_W) // HEAD_DIM
    ea = jnp.asarray((np.arange(LANES)[:, None] == heads_of[None, :]).astype(np.float32))
    eb = jnp.asarray((np.arange(LANES)[:, None] == heads_of[None, :] + GDN_HEADS).astype(np.float32))

    w_in_b = _to_bf16(w_in, rows=min(256, d))
    xf = x.reshape(m, d)
    for l in range(depth):
        xf = _ffn(xf, ffn1_norm[l], ffn1_w_gate[l].astype(BF16), ffn1_w_up[l].astype(BF16),
                  ffn1_w_down[l].astype(BF16), tm=tm, tf=tf)
        (q, k, vt, iq, ikb, kw, gx, ab, mq) = _inproj(
            xf, mix_norm[l].reshape(1, d), _pack_w_in(w_in_b[l]), grp_a,
            _tile_heads(dsa_q_norm[l], DSA_HEADS), _tile_heads(dsa_k_norm[l], DSA_HEADS), tm=tm)
        out_a = _dsa(q, iq, kw, k, vt, ikb, rel_bias, dsa_k_norm[l], ltri, batch=batch, seq=seq, kc=kc)
        out_b = _gdn(gx, ab, gdn_conv[l], ea, eb,
                     jnp.repeat(gdn_A_log[l].astype(F32), HEAD_DIM).reshape(1, GDN_W),
                     jnp.repeat(gdn_dt_bias[l].astype(F32), HEAD_DIM).reshape(1, GDN_W),
                     _tile_heads(gdn_out_norm[l], GDN_HEADS), grp_a, batch=batch, seq=seq, tb=tm)
        kt, vp = _memkv(mem, mem_norm[l].reshape(1, d), w_mem_kv[l].astype(BF16),
                        _tile_heads(mem_k_norm[l], MEM_HEADS), grp_m)
        out_c = _memattn(mq, _tile_heads(mem_q_norm[l], MEM_HEADS), grp_m, kt, vp,
                         batch=batch, seq=seq, tm=tm)
        wo = w_out[l].astype(BF16)
        xf = _outproj(xf, out_a, out_b, out_c, wo[:DSA_W], wo[DSA_W:DSA_W + GDN_W],
                      wo[DSA_W + GDN_W:], tm=tm)
        xf = _ffn(xf, ffn2_norm[l], ffn2_w_gate[l].astype(BF16), ffn2_w_up[l].astype(BF16),
                  ffn2_w_down[l].astype(BF16), tm=tm, tf=tf)
    return xf.reshape(batch, seq, d)
```

```python
import functools
import math

import jax
import jax.numpy as jnp
import numpy as np
from jax import lax
from jax.experimental import pallas as pl
from jax.experimental.pallas import tpu as pltpu

F32 = jnp.float32
BF16 = jnp.bfloat16

HEAD_DIM = 64
DSA_HEADS = 6
GDN_HEADS = 6
MEM_HEADS = 4
DSA_W = DSA_HEADS * HEAD_DIM
GDN_W = GDN_HEADS * HEAD_DIM
MEM_W = MEM_HEADS * HEAD_DIM
IDX_HEADS = 8
IDX_DIM = 32
TOPK_MAX = 256
Q_BLOCK = 128
GDN_CHUNK = 64
CONV_WIDTH = 4
NUM_BUCKETS = 32
MAX_DISTANCE = 128
EPS = 1e-6

LANES = 128
VMEM_LIMIT = 52 * 1024 * 1024
NEG = -1e30
LOG2E = math.log2(math.e)
V_ROWS = HEAD_DIM + 16
MAX_LOGIT_SPREAD = 80.0
PROBES_PER_ROUND = 4
FAST_ROUNDS = 4

SEG_A = 0
SEG_IQ = SEG_A + 3 * DSA_W
SEG_KW = SEG_IQ + IDX_HEADS * IDX_DIM
SEG_G = SEG_KW + LANES
SEG_AB = SEG_G + 4 * GDN_W
SEG_MQ = SEG_AB + LANES
IN_PACKED = SEG_MQ + MEM_W


def _bdot(a, b):
    return jnp.dot(a.astype(BF16), b.astype(BF16), preferred_element_type=F32)


def _split(x, terms):
    out = []
    for _ in range(terms - 1):
        hi = x.astype(BF16)
        out.append(hi)
        x = x - hi.astype(F32)
    out.append(x.astype(BF16))
    return out


def _dot3(a, b):
    ah, al = _split(a, 2)
    bh, bl = _split(b, 2)
    mm = lambda x, y: jnp.dot(x, y, preferred_element_type=F32)
    return mm(ah, bh) + (mm(ah, bl) + mm(al, bh))


def _dot_sel(a, sel, terms):
    selb = sel.astype(BF16)
    acc = None
    for piece in _split(a, terms):
        d = jnp.dot(piece, selb, preferred_element_type=F32)
        acc = d if acc is None else acc + d
    return acc


def _sel_dot(sel, b, terms):
    selb = sel.astype(BF16)
    acc = None
    for piece in _split(b, terms):
        d = jnp.dot(selb, piece, preferred_element_type=F32)
        acc = d if acc is None else acc + d
    return acc


def _group_sum(x, grp):
    return _dot_sel(x, grp, 2)


def _sigmoid(x):
    return 1.0 / (1.0 + jnp.exp(-x))


def _params(*sem):
    return pltpu.CompilerParams(dimension_semantics=sem, vmem_limit_bytes=VMEM_LIMIT)


def _ffn_kernel(x_ref, g_ref, wg_ref, wu_ref, wd_ref, o_ref, h_scr, acc_scr):
    f = pl.program_id(1)

    @pl.when(f == 0)
    def _():
        x = x_ref[...]
        ms = jnp.mean(x * x, axis=-1, keepdims=True)
        h_scr[...] = (x * lax.rsqrt(ms + EPS) * g_ref[...]).astype(BF16)
        acc_scr[...] = jnp.zeros_like(acc_scr)

    h = h_scr[...]
    a = jnp.dot(h, wg_ref[...], preferred_element_type=F32)
    u = jnp.dot(h, wu_ref[...], preferred_element_type=F32)
    z = (a * _sigmoid(a)) * u
    acc_scr[...] += jnp.dot(z.astype(BF16), wd_ref[...], preferred_element_type=F32)

    @pl.when(f == pl.num_programs(1) - 1)
    def _():
        o_ref[...] = x_ref[...] + 0.5 * acc_scr[...]


def _ffn(x, gain, wg, wu, wd, *, tm, tf):
    m, d = x.shape
    dff = wg.shape[1]
    return pl.pallas_call(
        _ffn_kernel,
        grid=(m // tm, dff // tf),
        in_specs=[
            pl.BlockSpec((tm, d), lambda i, f: (i, 0)),
            pl.BlockSpec((1, d), lambda i, f: (0, 0)),
            pl.BlockSpec((d, tf), lambda i, f: (0, f)),
            pl.BlockSpec((d, tf), lambda i, f: (0, f)),
            pl.BlockSpec((tf, d), lambda i, f: (f, 0)),
        ],
        out_specs=pl.BlockSpec((tm, d), lambda i, f: (i, 0)),
        out_shape=jax.ShapeDtypeStruct((m, d), F32),
        scratch_shapes=[pltpu.VMEM((tm, d), BF16), pltpu.VMEM((tm, d), F32)],
        compiler_params=_params("parallel", "arbitrary"),
        name="ffn",
    )(x, gain.reshape(1, d), wg, wu, wd)


def _inproj_kernel(x_ref, g_ref, w_ref, grp_ref, qg_ref, kg_ref,
                   q_ref, k_ref, vt_ref, iq_ref, ikb_ref, kw_ref, gx_ref, ab_ref, mq_ref):
    x = x_ref[...]
    ms = jnp.mean(x * x, axis=-1, keepdims=True)
    h = (x * lax.rsqrt(ms + EPS) * g_ref[...]).astype(BF16)
    p = jnp.dot(h, w_ref[...], preferred_element_type=F32)

    grp = grp_ref[...]
    dq = p[:, SEG_A:SEG_A + DSA_W]
    dk = p[:, SEG_A + DSA_W:SEG_A + 2 * DSA_W]
    dv = p[:, SEG_A + 2 * DSA_W:SEG_A + 3 * DSA_W]
    qms = _group_sum(dq * dq, grp) * (1.0 / HEAD_DIM)
    kms = _group_sum(dk * dk, grp) * (1.0 / HEAD_DIM)
    q_ref[...] = (dq * lax.rsqrt(qms + EPS) * qg_ref[...] * (HEAD_DIM ** -0.5 * LOG2E)).astype(BF16)
    k_ref[...] = (dk * lax.rsqrt(kms + EPS) * kg_ref[...]).astype(BF16)
    dvt = dv.T
    ones = jnp.ones((V_ROWS - HEAD_DIM, dvt.shape[1]), F32)
    for hd in range(DSA_HEADS):
        vt_ref[hd] = jnp.concatenate([dvt[hd * HEAD_DIM:(hd + 1) * HEAD_DIM], ones], axis=0).astype(BF16)

    iq_ref[...] = (p[:, SEG_IQ:SEG_IQ + IDX_HEADS * IDX_DIM] * (IDX_DIM ** -0.5)).astype(BF16)
    kw = p[:, SEG_KW:SEG_KW + LANES]
    kw_ref[...] = kw
    ikb_ref[...] = kw.astype(BF16)
    gx_ref[...] = p[:, SEG_G:SEG_G + 4 * GDN_W]
    ab_ref[...] = p[:, SEG_AB:SEG_AB + LANES]
    mq_ref[...] = p[:, SEG_MQ:SEG_MQ + MEM_W]


def _inproj(x, gain, w_packed, grp, qg, kg, *, tm):
    m, d = x.shape
    row = lambda w: pl.BlockSpec((tm, w), lambda i: (i, 0))
    full = lambda a: pl.BlockSpec(a.shape, lambda i: (0,) * a.ndim)
    out_shape = (
        jax.ShapeDtypeStruct((m, DSA_W), BF16),
        jax.ShapeDtypeStruct((m, DSA_W), BF16),
        jax.ShapeDtypeStruct((DSA_HEADS, V_ROWS, m), BF16),
        jax.ShapeDtypeStruct((m, IDX_HEADS * IDX_DIM), BF16),
        jax.ShapeDtypeStruct((m, LANES), BF16),
        jax.ShapeDtypeStruct((m, LANES), F32),
        jax.ShapeDtypeStruct((m, 4 * GDN_W), F32),
        jax.ShapeDtypeStruct((m, LANES), F32),
        jax.ShapeDtypeStruct((m, MEM_W), F32),
    )
    out_specs = (row(DSA_W), row(DSA_W), pl.BlockSpec((DSA_HEADS, V_ROWS, tm), lambda i: (0, 0, i)),
                 row(IDX_HEADS * IDX_DIM), row(LANES), row(LANES), row(4 * GDN_W),
                 row(LANES), row(MEM_W))
    return pl.pallas_call(
        _inproj_kernel,
        grid=(m // tm,),
        in_specs=[row(d), full(gain), full(w_packed), full(grp), full(qg), full(kg)],
        out_specs=out_specs,
        out_shape=out_shape,
        compiler_params=_params("parallel"),
        name="inproj",
    )(x, gain, w_packed, grp, qg, kg)


_DENORMAL_TOP = 0x007FFFFF


def _float_to_key(f):
    bits = lax.bitcast_convert_type(f, jnp.int32)
    mag = jnp.maximum((bits & jnp.int32(0x7FFFFFFF)) - _DENORMAL_TOP, 0)
    return jnp.where(bits >= 0, mag, -mag)


def _key_to_float(key):
    mag = jnp.abs(key)
    bits = jnp.where(mag > 0, mag + _DENORMAL_TOP, 0)
    return lax.bitcast_convert_type(jnp.where(key < 0, bits | jnp.int32(-2 ** 31), bits), F32)


def _upper_normal_quantile(p):
    pp = jnp.clip(jnp.minimum(p, 1.0 - p), 1e-30, 0.5)
    t = jnp.sqrt(-2.0 * jnp.log(pp))
    z = t - (2.515517 + t * (0.802853 + t * 0.010328)) / (1.0 + t * (1.432788 + t * (0.189269 + t * 0.001308)))
    return jnp.where(p <= 0.5, z, -z)


def _tree(op, x, group=8):
    parts = x.reshape(x.shape[0] // group, group, LANES)
    k = parts.shape[0]
    while k > 1:
        k //= 2
        parts = op(parts[:k], parts[k:2 * k])
    return parts[0]


def _dsa_kernel(q_ref, iq_ref, kwq_ref, k_ref, vt_ref, ik_ref, tab_ref, ltri_ref, o_ref,
                s_scr, iqt_scr, qpad_scr, bias_scr, mx_scr, brange_scr, *, kc, top_k):
    j = pl.program_id(1)
    per = kc // Q_BLOCK
    pad = kc - Q_BLOCK
    nch = j // per + 1
    q_pos = j * Q_BLOCK + lax.broadcasted_iota(jnp.int32, (1, LANES), 1)
    row_iota = lax.broadcasted_iota(jnp.int32, (kc, LANES), 0)

    def rows(i):
        return pl.ds(pl.multiple_of(j * Q_BLOCK - i * kc, Q_BLOCK), kc)

    def chunk_pairs(start, body, init, width=2, stop=None):
        count = jnp.maximum((nch if stop is None else stop) - start, 0)
        rest = count % width
        state = lax.fori_loop(0, rest, lambda r, st: body(start + r, st), init)

        def group(t, st):
            for u in range(width):
                st = body(start + rest + width * t + u, st)
            return st
        return lax.fori_loop(0, count // width, group, state)

    @pl.when((pl.program_id(0) == 0) & (j == 0))
    def _():
        r = lax.broadcasted_iota(jnp.int32, (2 * Q_BLOCK, LANES), 0)
        c = lax.broadcasted_iota(jnp.int32, (2 * Q_BLOCK, LANES), 1)
        dist = Q_BLOCK + c - r
        bucket = jnp.zeros_like(dist)
        for first in _BUCKET_STARTS:
            bucket = bucket + jnp.where(dist >= first, 1, 0)
        for h in range(DSA_HEADS):
            far = tab_ref[NUM_BUCKETS - 1, h]
            delta = jnp.zeros((2 * Q_BLOCK, LANES), F32)
            b_hi = jnp.float32(0.0)
            b_lo = jnp.float32(0.0)
            for b in range(NUM_BUCKETS - 1):
                delta = jnp.where(bucket == b, tab_ref[b, h] - far, delta)
                b_hi = jnp.maximum(b_hi, (tab_ref[b, h] - far) * LOG2E)
                b_lo = jnp.minimum(b_lo, (tab_ref[b, h] - far) * LOG2E)
            bias_scr[h] = jnp.where(dist >= 0, delta * LOG2E, 0.0)
            brange_scr[0, h] = b_hi
            brange_scr[1, h] = b_lo

    iqt = iq_ref[...].astype(F32).T
    zpad = jnp.zeros((LANES - IDX_DIM, LANES), F32)
    for h in range(IDX_HEADS):
        iqt_scr[:, h * LANES:(h + 1) * LANES] = jnp.concatenate(
            [iqt[h * IDX_DIM:(h + 1) * IDX_DIM], zpad], axis=0).astype(BF16)
    wt = kwq_ref[...].T[IDX_DIM:IDX_DIM + IDX_HEADS, :] * (IDX_HEADS ** -0.5)
    qt = q_ref[...].astype(F32).T
    half = lax.broadcasted_iota(jnp.int32, (LANES, LANES), 0) < HEAD_DIM
    for h in range(DSA_HEADS):
        pair = qt[(h // 2) * LANES:(h // 2 + 1) * LANES]
        keep = half if h % 2 == 0 else jnp.logical_not(half)
        qpad_scr[:, h * LANES:(h + 1) * LANES] = jnp.where(keep, pair, 0.0).astype(BF16)

    kmax = tab_ref[NUM_BUCKETS, 0]
    spread = jnp.zeros((1, LANES), F32)
    for h in range(DSA_HEADS):
        b_hi = brange_scr[0, h]
        b_lo = brange_scr[1, h]
        qh = qt[h * HEAD_DIM:(h + 1) * HEAD_DIM]
        bound = jnp.sqrt(jnp.sum(qh * qh, axis=0, keepdims=True)) * kmax
        mx_scr[h:h + 1, :] = bound + b_hi
        spread = jnp.maximum(spread, 2.0 * bound + (b_hi - b_lo))
    bound_ok = jnp.max(spread) <= MAX_LOGIT_SPREAD

    def score_chunk(i, carry, edge):
        d = jnp.dot(ik_ref[0, rows(i), :], iqt_scr[...], preferred_element_type=F32)
        acc = jnp.maximum(d[:, :LANES], 0.0) * wt[0:1, :]
        for h in range(1, IDX_HEADS):
            acc = acc + jnp.maximum(d[:, h * LANES:(h + 1) * LANES], 0.0) * wt[h:h + 1, :]
        sc = fin = acc
        if edge:
            key = row_iota + (j * Q_BLOCK - i * kc - pad)
            adm = jnp.where(key >= 0, key, q_pos + 1) <= q_pos
            sc = jnp.where(adm, acc, -jnp.inf)
            fin = jnp.where(adm, acc, 0.0)
        s_scr[rows(i), :] = sc
        tot, sq, top = carry
        return (tot + _tree(jnp.add, fin), sq + _tree(jnp.add, fin * fin),
                jnp.maximum(top, _tree(jnp.maximum, sc)))

    zero8 = jnp.zeros((8, LANES), F32)
    stats = score_chunk(0, (zero8, zero8, jnp.full((8, LANES), -jnp.inf, F32)), True)
    stats = lax.fori_loop(0, jnp.minimum(nch - 1, 1), lambda _, st: score_chunk(nch - 1, st, True), stats)
    tot, sq, top = chunk_pairs(1, lambda i, st: score_chunk(i, st, False), stats, width=4, stop=nch - 1)

    def count(pred):
        def body(i, cnt):
            return cnt + _tree(jnp.add, jnp.where(pred(s_scr[rows(i), :]), 1.0, 0.0))
        cnt = chunk_pairs(0, body, zero8)
        return jnp.sum(cnt, axis=0, keepdims=True)

    def max_below(t):
        def body(i, m):
            s = s_scr[rows(i), :]
            return jnp.maximum(m, _tree(jnp.maximum, jnp.where(s < t, s, -jnp.inf)))
        m = lax.fori_loop(0, nch, body, jnp.full((8, LANES), -jnp.inf, F32))
        return jnp.max(m, axis=0, keepdims=True)

    kf = float(top_k)
    n_adm = (q_pos + 1).astype(F32)
    mean = jnp.sum(tot, axis=0, keepdims=True) / n_adm
    std = jnp.sqrt(jnp.maximum(jnp.sum(sq, axis=0, keepdims=True) / n_adm - mean * mean, 0.0))
    first_guess = mean + _upper_normal_quantile(kf / n_adm) * std
    lowest = jnp.float32(-3.0e38)
    top1 = jnp.minimum(jnp.max(top, axis=0, keepdims=True), -lowest)

    def update(st, key):
        lo, hi, c_lo, c_hi, lo_set, hi_set, done, run = st
        key = jnp.clip(key, lo + 1, hi - 1)
        t = _key_to_float(key)
        c = count(lambda s: s >= t)
        active = done < 0.5
        up = active & (c >= kf)
        dn = active & (c < kf)
        lo, c_lo, lo_set = jnp.where(up, key, lo), jnp.where(up, c, c_lo), jnp.where(up, 1.0, lo_set)
        hi, c_hi, hi_set = jnp.where(dn, key, hi), jnp.where(dn, c, c_hi), jnp.where(dn, 1.0, hi_set)
        done = jnp.where((c_lo == kf) | (hi <= lo + 1), 1.0, done)
        return lo, hi, c_lo, c_hi, lo_set, hi_set, done, run

    def guess(st, midpoint):
        lo, hi, c_lo, c_hi, lo_set, hi_set, done, run = st
        v_lo, v_hi = _key_to_float(lo), _key_to_float(hi)
        both = (lo_set > 0.5) & (hi_set > 0.5)
        step = std * 0.25 * jnp.exp2(run)
        if midpoint:
            inner = 0.5 * v_lo + 0.5 * v_hi
        else:
            l_lo, l_hi = jnp.log(c_lo), jnp.log(jnp.maximum(c_hi, 0.5))
            frac = jnp.clip((l_lo - math.log(kf - 0.5)) / jnp.maximum(l_lo - l_hi, 1e-6), 0.0, 1.0)
            inner = v_lo + (v_hi - v_lo) * frac
        t = jnp.where(both, inner,
                      jnp.where(hi_set > 0.5, v_hi - step, jnp.where(lo_set > 0.5, v_lo + step, first_guess)))
        t = jnp.where(t != t, 0.0, t)
        run = jnp.where(both, 0.0, run + 1.0)
        return _float_to_key(t), (lo, hi, c_lo, c_hi, lo_set, hi_set, done, run)

    def pending(st):
        return 1.0 - jnp.min(st[6])

    flag0 = jnp.zeros((1, LANES), F32)
    st = (jnp.full((1, LANES), _float_to_key(lowest), jnp.int32), _float_to_key(top1) + 1,
          n_adm, flag0, flag0, flag0, jnp.where(n_adm <= kf, 1.0, 0.0), flag0)

    def fast_round(carry):
        p, _, st = carry
        for r in range(PROBES_PER_ROUND):
            key, st = guess(st, midpoint=(r == PROBES_PER_ROUND - 1))
            st = update(st, key)
        return p + 1, pending(st), st

    _, _, st = lax.while_loop(lambda c: (c[0] < FAST_ROUNDS) & (c[1] > 0.5), fast_round,
                              (jnp.int32(0), pending(st), st))

    def safe_round(carry):
        p, _, st = carry
        st = update(st, st[0] + lax.shift_right_logical(st[1] - st[0], 1))
        lo, hi, done = st[0], st[1], st[6]
        below = _float_to_key(max_below(_key_to_float(hi)))
        hi = jnp.where(done < 0.5, jnp.clip(below + 1, lo + 1, hi), hi)
        done = jnp.where(hi <= lo + 1, 1.0, done)
        st = update((lo, hi) + st[2:6] + (done, st[7]), hi - 1)
        return p + 1, pending(st), st

    _, _, st = lax.while_loop(lambda c: (c[1] > 0.5) & (c[0] < 34), safe_round,
                              (jnp.int32(0), pending(st), st))

    thr = _key_to_float(st[0])
    has_ties = jnp.max(st[2]) > kf

    @pl.when(jnp.logical_not(has_ties))
    def _():
        def mask_chunk(i, carry):
            s_scr[rows(i), :] = jnp.where(s_scr[rows(i), :] >= thr, 0.0, NEG)
            return carry
        lax.fori_loop(0, nch, mask_chunk, 0)

    @pl.when(has_ties)
    def _():
        need = kf - count(lambda s: s > thr)

        def mask_chunk(t, run):
            i = nch - 1 - t
            s = s_scr[rows(i), :]
            tie = jnp.where(s == thr, 1.0, 0.0)
            pref = jnp.dot(ltri_ref[...], tie.astype(BF16), preferred_element_type=F32) + run
            tie_sel = jnp.where(pref <= need, tie, 0.0)
            sel = jnp.where(s > thr, 1.0, tie_sel)
            s_scr[rows(i), :] = jnp.where(sel > 0.5, 0.0, NEG)
            return run + jnp.sum(tie, axis=0, keepdims=True)
        lax.fori_loop(0, nch, mask_chunk, jnp.zeros((1, LANES), F32))

    def logits(i, with_bias):
        msk = s_scr[rows(i), :]
        out = []
        for pr in range(DSA_HEADS // 2):
            l2 = jnp.dot(k_ref[0, rows(i), pr * LANES:(pr + 1) * LANES],
                         qpad_scr[:, 2 * pr * LANES:(2 * pr + 2) * LANES],
                         preferred_element_type=F32)
            for e in range(2):
                l = l2[:, e * LANES:(e + 1) * LANES] + msk
                if with_bias:
                    far_rows = kc - 2 * Q_BLOCK
                    l = jnp.concatenate([l[:far_rows], l[far_rows:] + bias_scr[2 * pr + e]], axis=0)
                out.append(l)
        return out

    def max_step(i, ms, with_bias):
        return tuple(jnp.maximum(m, _tree(jnp.maximum, l)) for m, l in zip(ms, logits(i, with_bias)))

    @pl.when(jnp.logical_not(bound_ok))
    def _():
        ms = max_step(0, tuple(jnp.full((8, LANES), NEG, F32) for _ in range(DSA_HEADS)), True)
        ms = lax.fori_loop(1, nch, lambda i, m: max_step(i, m, False), ms)
        for h in range(DSA_HEADS):
            mx_scr[h:h + 1, :] = jnp.max(ms[h], axis=0, keepdims=True)

    mx = [mx_scr[h:h + 1, :] for h in range(DSA_HEADS)]

    def pv_step(i, accs, with_bias):
        return tuple(
            acc + jnp.dot(vt_ref[h, :, rows(i)], jnp.exp2(l - mx[h]).astype(BF16),
                          preferred_element_type=F32)
            for h, (acc, l) in enumerate(zip(accs, logits(i, with_bias))))

    accs = pv_step(0, tuple(jnp.zeros((V_ROWS, LANES), F32) for _ in range(DSA_HEADS)), True)
    accs = chunk_pairs(1, lambda i, a: pv_step(i, a, False), accs, width=4)
    outs = [acc[:HEAD_DIM] / acc[HEAD_DIM:HEAD_DIM + 1] for acc in accs]
    o_ref[...] = jnp.concatenate(outs, axis=0).T


def _dsa(q, iq, kw, k, vt, ikb, rel_bias, k_gain, ltri, *, batch, seq, kc):
    nb = seq // Q_BLOCK
    top_k = min(TOPK_MAX, seq // 4)
    pad = kc - Q_BLOCK
    seqp = seq + pad
    kp = jnp.pad(k.reshape(batch, seq, DSA_W), ((0, 0), (pad, 0), (0, 0)))
    ikp = jnp.pad(ikb.reshape(batch, seq, LANES), ((0, 0), (pad, 0), (0, 0)))
    vtp = jnp.pad(vt.reshape(DSA_HEADS, V_ROWS, batch, seq),
                  ((0, 0), (0, 0), (0, 0), (pad, 0))).reshape(DSA_HEADS, V_ROWS, batch * seqp)
    kmax = (HEAD_DIM ** 0.5 * 1.01) * jnp.max(jnp.abs(k_gain.astype(F32)))
    table = jnp.concatenate([rel_bias.astype(F32), jnp.full((1, DSA_HEADS), kmax, F32)], axis=0)
    qrow = lambda w: pl.BlockSpec((Q_BLOCK, w), lambda b, j: (b * nb + j, 0))
    return pl.pallas_call(
        functools.partial(_dsa_kernel, kc=kc, top_k=top_k),
        grid=(batch, nb),
        in_specs=[
            qrow(DSA_W), qrow(IDX_HEADS * IDX_DIM), qrow(LANES),
            pl.BlockSpec((1, seqp, DSA_W), lambda b, j: (b, 0, 0)),
            pl.BlockSpec((DSA_HEADS, V_ROWS, seqp), lambda b, j: (0, 0, b)),
            pl.BlockSpec((1, seqp, LANES), lambda b, j: (b, 0, 0)),
            pl.BlockSpec(memory_space=pltpu.SMEM),
            pl.BlockSpec(ltri.shape, lambda b, j: (0, 0)),
        ],
        out_specs=qrow(DSA_W),
        out_shape=jax.ShapeDtypeStruct((batch * seq, DSA_W), F32),
        scratch_shapes=[
            pltpu.VMEM((seqp, LANES), F32),
            pltpu.VMEM((LANES, IDX_HEADS * LANES), BF16),
            pltpu.VMEM((LANES, DSA_HEADS * LANES), BF16),
            pltpu.VMEM((DSA_HEADS, 2 * Q_BLOCK, LANES), F32),
            pltpu.VMEM((8, LANES), F32),
            pltpu.SMEM((2, DSA_HEADS), F32),
        ],
        compiler_params=_params("arbitrary", "arbitrary"),
        name="dsa",
    )(q, iq, kw, kp, vtp, ikp, table, ltri)


GDN_GROUP = 2


def _gdn_kernel(x_ref, ab_ref, cw_ref, ea_ref, eb_ref, alog_ref, dtb_ref, gain_ref, grp_ref,
                o_ref, xpad_scr, qkv_scr, gb_scr, gc_scr, gr_scr, pre_scr, st_scr, o_scr, *, tb):
    t = pl.program_id(1)
    cin = 3 * GDN_W
    n_chunks = tb // GDN_CHUNK
    pairs = range(GDN_HEADS // 2)
    lanes = lambda p: slice(p * LANES, (p + 1) * LANES)

    @pl.when(t == 0)
    def _():
        xpad_scr[0:8, :] = jnp.zeros((8, cin), F32)
        st_scr[...] = jnp.zeros_like(st_scr)

    x = x_ref[...]
    xpad_scr[8:8 + tb, :] = x[:, :cin]
    conv = jnp.zeros((tb, cin), F32)
    for jj in range(CONV_WIDTH):
        conv = conv + cw_ref[jj:jj + 1, :] * xpad_scr[pl.ds(8 - (CONV_WIDTH - 1) + jj, tb), :]
    xpad_scr[0:8, :] = x[tb - 8:tb, :cin]
    qkv = conv * _sigmoid(conv)
    grp = grp_ref[...]
    q = qkv[:, :GDN_W]
    k = qkv[:, GDN_W:2 * GDN_W]
    qkv_scr[:, :GDN_W] = q * lax.rsqrt(_group_sum(q * q, grp) + EPS) * (HEAD_DIM ** -0.5)
    qkv_scr[:, GDN_W:2 * GDN_W] = k * lax.rsqrt(_group_sum(k * k, grp) + EPS)
    qkv_scr[:, 2 * GDN_W:] = qkv[:, 2 * GDN_W:]

    ab = ab_ref[...]
    a_e = _dot_sel(ab, ea_ref[...], 3) + dtb_ref[...]
    b_e = _dot_sel(ab, eb_ref[...], 3)
    softplus = jnp.maximum(a_e, 0.0) + jnp.log(1.0 + jnp.exp(-jnp.abs(a_e)))
    gb_scr[:, :GDN_W] = -jnp.exp(alog_ref[...]) * softplus
    gb_scr[:, GDN_W:] = _sigmoid(b_e)

    r64 = lax.broadcasted_iota(jnp.int32, (GDN_CHUNK, LANES), 0)
    c64 = lax.broadcasted_iota(jnp.int32, (GDN_CHUNK, LANES), 1) % GDN_CHUNK
    causal = c64 <= r64
    strict = c64 < r64
    eye = jnp.where(c64 == r64, 1.0, 0.0)
    r128 = lax.broadcasted_iota(jnp.int32, (LANES, LANES), 0)
    c128 = lax.broadcasted_iota(jnp.int32, (LANES, LANES), 1)
    bdmask = (r128 // HEAD_DIM) == (c128 // HEAD_DIM)

    def bd(m):
        return jnp.where(bdmask, jnp.concatenate([m, m], axis=0), 0.0)

    lt_r = lax.broadcasted_iota(jnp.int32, (GDN_CHUNK, GDN_CHUNK), 0)
    lt_c = lax.broadcasted_iota(jnp.int32, (GDN_CHUNK, GDN_CHUNK), 1)
    ltri = jnp.where(lt_c <= lt_r, 1.0, 0.0)
    ones8 = jnp.ones((8, GDN_CHUNK), F32)
    up_r = lax.broadcasted_iota(jnp.int32, (GDN_CHUNK, GDN_W), 0)
    up_c = lax.broadcasted_iota(jnp.int32, (GDN_CHUNK, GDN_W), 1) % GDN_CHUNK
    upper = jnp.where(up_r <= up_c, 1.0, 0.0)
    for ci in range(n_chunks):
        g = gb_scr[ci * GDN_CHUNK:(ci + 1) * GDN_CHUNK, :GDN_W]
        gc_scr[ci * GDN_CHUNK:(ci + 1) * GDN_CHUNK, :] = _sel_dot(ltri, g, 3)
        gr_scr[ci * 8:(ci + 1) * 8, :] = _sel_dot(ones8, g * upper, 3)

    def par_body(gi, carry):
        inst = [(u, p) for u in range(GDN_GROUP) for p in pairs]
        rows = [pl.ds(pl.multiple_of((gi * GDN_GROUP + u) * GDN_CHUNK, GDN_CHUNK), GDN_CHUNK)
                for u in range(GDN_GROUP)]
        grow = [pl.ds(pl.multiple_of((gi * GDN_GROUP + u) * 8, 8), 8) for u in range(GDN_GROUP)]
        off = lambda s, p: slice(s * GDN_W + p * LANES, s * GDN_W + (p + 1) * LANES)
        qp = [qkv_scr[rows[u], off(0, p)] for u, p in inst]
        kp = [qkv_scr[rows[u], off(1, p)] for u, p in inst]
        vp = [qkv_scr[rows[u], off(2, p)] for u, p in inst]
        beta = [gb_scr[rows[u], off(1, p)] for u, p in inst]
        gc = [gc_scr[rows[u], lanes(p)] for u, p in inst]
        gr = [gr_scr[grow[u], lanes(p)][0:1, :] for u, p in inst]
        dmat = [jnp.exp(jnp.where(causal, a - b, NEG)) for a, b in zip(gc, gr)]
        kbd = [jnp.where(bdmask, jnp.concatenate([a, a], axis=0).T, 0.0) for a in kp]
        kb = [a * b for a, b in zip(kp, beta)]
        kk = [_dot3(a, b) for a, b in zip(kb, kbd)]
        qk = [_dot3(a, b) for a, b in zip(qp, kbd)]
        lm = [jnp.where(strict, a * d, 0.0) for a, d in zip(kk, dmat)]
        aintra = [a * d for a, d in zip(qk, dmat)]
        tinv = [eye - a for a in lm]
        lpow = lm
        for _ in range(5):
            lpow = [_dot3(a, bd(a)) for a in lpow]
            tinv = [a + _dot3(a, bd(b)) for a, b in zip(tinv, lpow)]
        egc = [jnp.exp(a) for a in gc]
        un = [_dot3(a, bd(v * b)) for a, v, b in zip(tinv, vp, beta)]
        wn = [_dot3(a, bd(b * e)) for a, b, e in zip(tinv, kb, egc)]
        for n, (u, p) in enumerate(inst):
            pre_scr[0, rows[u], lanes(p)] = un[n]
            pre_scr[1, rows[u], lanes(p)] = wn[n]
            pre_scr[2, rows[u], lanes(p)] = aintra[n]
            pre_scr[3, rows[u], lanes(p)] = qp[n] * egc[n]
            pre_scr[4, rows[u], lanes(p)] = kp[n] * jnp.exp(gc[n][GDN_CHUNK - 1:GDN_CHUNK, :] - gc[n])
        return carry

    lax.fori_loop(0, n_chunks // GDN_GROUP, par_body, 0)

    def rec_body(ci, carry):
        rows = pl.ds(pl.multiple_of(ci * GDN_CHUNK, GDN_CHUNK), GDN_CHUNK)
        tail = gc_scr[pl.ds(pl.multiple_of(ci * GDN_CHUNK + GDN_CHUNK - 8, 8), 8), :]
        egl = jnp.exp(tail[7:8, :])
        sbd = [st_scr[p] for p in pairs]
        un, wn, aintra, qe, kd = [[pre_scr[s, rows, lanes(p)] for p in pairs] for s in range(5)]
        ws = [_dot3(a, s) for a, s in zip(wn, sbd)]
        qs = [_dot3(a, s) for a, s in zip(qe, sbd)]
        v_new = [a - b for a, b in zip(un, ws)]
        o = [a + _dot3(b, bd(v)) for a, b, v in zip(qs, aintra, v_new)]
        upd = [_dot3(a.T, v) for a, v in zip(kd, v_new)]
        for p in pairs:
            st_scr[p] = sbd[p] * egl[:, lanes(p)] + jnp.where(bdmask, upd[p], 0.0)
            o_scr[rows, lanes(p)] = o[p]
        return carry

    lax.fori_loop(0, n_chunks, rec_body, 0)

    o = o_scr[...]
    oms = _group_sum(o * o, grp) * (1.0 / HEAD_DIM)
    z = x[:, cin:]
    o_ref[...] = o * lax.rsqrt(oms + EPS) * gain_ref[...] * (z * _sigmoid(z))


def _gdn(gx, ab, conv_w, ea, eb, alog, dtb, gain, grp, *, batch, seq, tb):
    nt = seq // tb
    cin = 3 * GDN_W
    row = lambda w: pl.BlockSpec((tb, w), lambda b, t: (b * nt + t, 0))
    full = lambda a: pl.BlockSpec(a.shape, lambda b, t: (0,) * a.ndim)
    return pl.pallas_call(
        functools.partial(_gdn_kernel, tb=tb),
        grid=(batch, nt),
        in_specs=[row(4 * GDN_W), row(LANES), full(conv_w), full(ea), full(eb), full(alog),
                  full(dtb), full(gain), full(grp)],
        out_specs=row(GDN_W),
        out_shape=jax.ShapeDtypeStruct((batch * seq, GDN_W), F32),
        scratch_shapes=[
            pltpu.VMEM((tb + 8, cin), F32),
            pltpu.VMEM((tb, cin), F32),
            pltpu.VMEM((tb, 2 * GDN_W), F32),
            pltpu.VMEM((tb, GDN_W), F32),
            pltpu.VMEM((tb // 8, GDN_W), F32),
            pltpu.VMEM((5, tb, GDN_W), F32),
            pltpu.VMEM((GDN_HEADS // 2, LANES, LANES), F32),
            pltpu.VMEM((tb, GDN_W), F32),
        ],
        compiler_params=_params("arbitrary", "arbitrary"),
        name="gdn",
    )(gx, ab, conv_w, ea, eb, alog, dtb, gain, grp)


def _memkv_kernel(mem_ref, g_ref, w_ref, kg_ref, grp_ref, kt_ref, vp_ref):
    x = mem_ref[0]
    ms = jnp.mean(x * x, axis=-1, keepdims=True)
    h = (x * lax.rsqrt(ms + EPS) * g_ref[...]).astype(BF16)
    kv = jnp.dot(h, w_ref[...], preferred_element_type=F32)
    km = kv[:, :MEM_W]
    vm = kv[:, MEM_W:]
    kms = _group_sum(km * km, grp_ref[...]) * (1.0 / HEAD_DIM)
    kt = (km * lax.rsqrt(kms + EPS) * kg_ref[...]).T
    n_mem = x.shape[0]
    top = lax.broadcasted_iota(jnp.int32, (LANES, n_mem), 0) < HEAD_DIM
    left = lax.broadcasted_iota(jnp.int32, (n_mem, LANES), 1) < HEAD_DIM
    for hh in range(MEM_HEADS):
        pr = slice((hh // 2) * LANES, (hh // 2 + 1) * LANES)
        keep_r = top if hh % 2 == 0 else jnp.logical_not(top)
        keep_c = left if hh % 2 == 0 else jnp.logical_not(left)
        kt_ref[0, hh] = jnp.where(keep_r, kt[pr, :], 0.0).astype(BF16)
        vp_ref[0, hh] = jnp.where(keep_c, vm[:, pr], 0.0).astype(BF16)


def _memkv(mem, gain, w, kg, grp):
    b, n_mem, d = mem.shape
    full = lambda a: pl.BlockSpec(a.shape, lambda i: (0,) * a.ndim)
    return pl.pallas_call(
        _memkv_kernel,
        grid=(b,),
        in_specs=[pl.BlockSpec((1, n_mem, d), lambda i: (i, 0, 0)), full(gain), full(w), full(kg),
                  full(grp)],
        out_specs=(pl.BlockSpec((1, MEM_HEADS, LANES, n_mem), lambda i: (i, 0, 0, 0)),
                   pl.BlockSpec((1, MEM_HEADS, n_mem, LANES), lambda i: (i, 0, 0, 0))),
        out_shape=(jax.ShapeDtypeStruct((b, MEM_HEADS, LANES, n_mem), BF16),
                   jax.ShapeDtypeStruct((b, MEM_HEADS, n_mem, LANES), BF16)),
        compiler_params=_params("parallel"),
        name="memkv",
    )(mem, gain, w, kg, grp)


def _memattn_kernel(q_ref, qg_ref, grp_ref, kt_ref, vp_ref, o_ref):
    q = q_ref[...]
    qms = _group_sum(q * q, grp_ref[...]) * (1.0 / HEAD_DIM)
    qn = (q * lax.rsqrt(qms + EPS) * qg_ref[...] * (HEAD_DIM ** -0.5)).astype(BF16)
    outs = []
    for pr in range(MEM_HEADS // 2):
        qp = qn[:, pr * LANES:(pr + 1) * LANES]
        acc = None
        for e in range(2):
            hh = 2 * pr + e
            l = jnp.dot(qp, kt_ref[0, hh], preferred_element_type=F32)
            l = l - jnp.max(l, axis=-1, keepdims=True)
            p = jnp.exp(l)
            p = p / jnp.sum(p, axis=-1, keepdims=True)
            o = jnp.dot(p.astype(BF16), vp_ref[0, hh], preferred_element_type=F32)
            acc = o if acc is None else acc + o
        outs.append(acc)
    o_ref[...] = jnp.concatenate(outs, axis=-1)


def _memattn(mq, qg, grp, kt, vp, *, batch, seq, tm):
    nt = seq // tm
    n_mem = kt.shape[-1]
    full = lambda a: pl.BlockSpec(a.shape, lambda b, t: (0,) * a.ndim)
    return pl.pallas_call(
        _memattn_kernel,
        grid=(batch, nt),
        in_specs=[
            pl.BlockSpec((tm, MEM_W), lambda b, t: (b * nt + t, 0)), full(qg), full(grp),
            pl.BlockSpec((1, MEM_HEADS, LANES, n_mem), lambda b, t: (b, 0, 0, 0)),
            pl.BlockSpec((1, MEM_HEADS, n_mem, LANES), lambda b, t: (b, 0, 0, 0)),
        ],
        out_specs=pl.BlockSpec((tm, MEM_W), lambda b, t: (b * nt + t, 0)),
        out_shape=jax.ShapeDtypeStruct((batch * seq, MEM_W), F32),
        compiler_params=_params("parallel", "parallel"),
        name="memattn",
    )(mq, qg, grp, kt, vp)


def _outproj_kernel(x_ref, a_ref, b_ref, c_ref, wa_ref, wb_ref, wc_ref, o_ref):
    y = _bdot(a_ref[...], wa_ref[...]) + _bdot(b_ref[...], wb_ref[...]) + _bdot(c_ref[...], wc_ref[...])
    o_ref[...] = x_ref[...] + y


def _outproj(x, a, b, c, wa, wb, wc, *, tm):
    m, d = x.shape
    row = lambda w: pl.BlockSpec((tm, w), lambda i: (i, 0))
    full = lambda arr: pl.BlockSpec(arr.shape, lambda i: (0,) * arr.ndim)
    return pl.pallas_call(
        _outproj_kernel,
        grid=(m // tm,),
        in_specs=[row(d), row(DSA_W), row(GDN_W), row(MEM_W), full(wa), full(wb), full(wc)],
        out_specs=row(d),
        out_shape=jax.ShapeDtypeStruct((m, d), F32),
        compiler_params=_params("parallel"),
        name="outproj",
    )(x, a, b, c, wa, wb, wc)


def _group_ones(width):
    idx = np.arange(width) // HEAD_DIM
    return jnp.asarray((idx[:, None] == idx[None, :]).astype(np.float32))


def _t5_bucket_np(n):
    max_exact = NUM_BUCKETS // 2
    nf = np.maximum(n, 1).astype(np.float32)
    large = max_exact + (np.log(nf / np.float32(max_exact)) / np.float32(math.log(MAX_DISTANCE / max_exact))
                         * (NUM_BUCKETS - max_exact)).astype(np.int32)
    large = np.minimum(large, NUM_BUCKETS - 1)
    return np.where(n < max_exact, n, large)


def _bucket_starts():
    dist = np.arange(2 * Q_BLOCK)
    bucket = _t5_bucket_np(dist)
    assert (np.diff(bucket) >= 0).all() and bucket[-1] == NUM_BUCKETS - 1
    return tuple(int(dist[bucket >= b].min()) for b in range(1, NUM_BUCKETS))


_BUCKET_STARTS = _bucket_starts()


def _cast_kernel(w_ref, o_ref):
    o_ref[...] = w_ref[...].astype(BF16)


def _to_bf16(w, *, rows):
    nl, nr, nc = w.shape
    spec = pl.BlockSpec((1, rows, nc), lambda l, r: (l, r, 0))
    return pl.pallas_call(
        _cast_kernel,
        grid=(nl, nr // rows),
        in_specs=[spec],
        out_specs=spec,
        out_shape=jax.ShapeDtypeStruct(w.shape, BF16),
        compiler_params=_params("parallel", "parallel"),
        name="cast",
    )(w)


def _pack_w_in(w):
    o = 3 * DSA_W
    iq = w[:, o:o + IDX_HEADS * IDX_DIM]
    o += IDX_HEADS * IDX_DIM
    kw = w[:, o:o + IDX_DIM + IDX_HEADS]
    o += IDX_DIM + IDX_HEADS
    g = w[:, o:o + 4 * GDN_W]
    o += 4 * GDN_W
    ab = w[:, o:o + 2 * GDN_HEADS]
    o += 2 * GDN_HEADS
    mq = w[:, o:o + MEM_W]
    pad = lambda a: jnp.pad(a, ((0, 0), (0, LANES - a.shape[1])))
    return jnp.concatenate([w[:, :3 * DSA_W], iq, pad(kw), g, pad(ab), mq], axis=1)


def _tile_heads(v, heads):
    return jnp.tile(v.astype(F32), heads).reshape(1, heads * HEAD_DIM)


def kernel(x, mem, ffn1_norm, ffn1_w_gate, ffn1_w_up, ffn1_w_down, mix_norm, w_in, dsa_q_norm, dsa_k_norm, rel_bias, gdn_conv, gdn_A_log, gdn_dt_bias, gdn_out_norm, mem_norm, w_mem_kv, mem_q_norm, mem_k_norm, w_out, ffn2_norm, ffn2_w_gate, ffn2_w_up, ffn2_w_down):
    batch, seq, d = x.shape
    depth = w_in.shape[0]
    m = batch * seq
    tm = min(512, seq)
    kc = min(512, seq)
    dff = ffn1_w_gate.shape[-1]
    tf = dff // 2 if (dff // 2) % LANES == 0 else dff

    grp_a = _group_ones(DSA_W)
    grp_m = _group_ones(MEM_W)
    ltri = jnp.asarray(np.tril(np.ones((kc, kc), np.float32))).astype(BF16)
    heads_of = np.arange(GDN_W) // HEAD_DIM
    ea = jnp.asarray((np.arange(LANES)[:, None] == heads_of[None, :]).astype(np.float32))
    eb = jnp.asarray((np.arange(LANES)[:, None] == heads_of[None, :] + GDN_HEADS).astype(np.float32))

    w_in_b = _to_bf16(w_in, rows=min(256, d))
    xf = x.reshape(m, d)
    for l in range(depth):
        xf = _ffn(xf, ffn1_norm[l], ffn1_w_gate[l].astype(BF16), ffn1_w_up[l].astype(BF16),
                  ffn1_w_down[l].astype(BF16), tm=tm, tf=tf)
        (q, k, vt, iq, ikb, kw, gx, ab, mq) = _inproj(
            xf, mix_norm[l].reshape(1, d), _pack_w_in(w_in_b[l]), grp_a,
            _tile_heads(dsa_q_norm[l], DSA_HEADS), _tile_heads(dsa_k_norm[l], DSA_HEADS), tm=tm)
        out_a = _dsa(q, iq, kw, k, vt, ikb, rel_bias, dsa_k_norm[l], ltri, batch=batch, seq=seq, kc=kc)
        out_b = _gdn(gx, ab, gdn_conv[l], ea, eb,
                     jnp.repeat(gdn_A_log[l].astype(F32), HEAD_DIM).reshape(1, GDN_W),
                     jnp.repeat(gdn_dt_bias[l].astype(F32), HEAD_DIM).reshape(1, GDN_W),
                     _tile_heads(gdn_out_norm[l], GDN_HEADS), grp_a, batch=batch, seq=seq, tb=tm)
        kt, vp = _memkv(mem, mem_norm[l].reshape(1, d), w_mem_kv[l].astype(BF16),
                        _tile_heads(mem_k_norm[l], MEM_HEADS), grp_m)
        out_c = _memattn(mq, _tile_heads(mem_q_norm[l], MEM_HEADS), grp_m, kt, vp,
                         batch=batch, seq=seq, tm=tm)
        wo = w_out[l].astype(BF16)
        xf = _outproj(xf, out_a, out_b, out_c, wo[:DSA_W], wo[DSA_W:DSA_W + GDN_W],
                      wo[DSA_W + GDN_W:], tm=tm)
        xf = _ffn(xf, ffn2_norm[l], ffn2_w_gate[l].astype(BF16), ffn2_w_up[l].astype(BF16),
                  ffn2_w_down[l].astype(BF16), tm=tm, tf=tf)
    return xf.reshape(batch, seq, d)
```

```python
import functools
import math

import jax
import jax.numpy as jnp
import numpy as np
from jax import lax
from jax.experimental import pallas as pl
from jax.experimental.pallas import tpu as pltpu

F32 = jnp.float32
BF16 = jnp.bfloat16

HEAD_DIM = 64
DSA_HEADS = 6
GDN_HEADS = 6
MEM_HEADS = 4
DSA_W = DSA_HEADS * HEAD_DIM
GDN_W = GDN_HEADS * HEAD_DIM
MEM_W = MEM_HEADS * HEAD_DIM
IDX_HEADS = 8
IDX_DIM = 32
TOPK_MAX = 256
Q_BLOCK = 128
GDN_CHUNK = 64
CONV_WIDTH = 4
NUM_BUCKETS = 32
MAX_DISTANCE = 128
EPS = 1e-6

LANES = 128
VMEM_LIMIT = 52 * 1024 * 1024
NEG = -1e30
LOG2E = math.log2(math.e)
V_ROWS = HEAD_DIM + 16
MAX_LOGIT_SPREAD = 80.0
PROBES_PER_ROUND = 4
MIDPOINT_EVERY = 4
FAST_ROUNDS = 4

SEG_A = 0
SEG_IQ = SEG_A + 3 * DSA_W
SEG_KW = SEG_IQ + IDX_HEADS * IDX_DIM
SEG_G = SEG_KW + LANES
SEG_AB = SEG_G + 4 * GDN_W
SEG_MQ = SEG_AB + LANES
IN_PACKED = SEG_MQ + MEM_W


def _bdot(a, b):
    return jnp.dot(a.astype(BF16), b.astype(BF16), preferred_element_type=F32)


def _split(x, terms):
    out = []
    for _ in range(terms - 1):
        hi = x.astype(BF16)
        out.append(hi)
        x = x - hi.astype(F32)
    out.append(x.astype(BF16))
    return out


def _dot3(a, b):
    ah, al = _split(a, 2)
    bh, bl = _split(b, 2)
    mm = lambda x, y: jnp.dot(x, y, preferred_element_type=F32)
    return mm(ah, bh) + (mm(ah, bl) + mm(al, bh))


def _dot_sel(a, sel, terms):
    selb = sel.astype(BF16)
    acc = None
    for piece in _split(a, terms):
        d = jnp.dot(piece, selb, preferred_element_type=F32)
        acc = d if acc is None else acc + d
    return acc


def _sel_dot(sel, b, terms):
    selb = sel.astype(BF16)
    acc = None
    for piece in _split(b, terms):
        d = jnp.dot(selb, piece, preferred_element_type=F32)
        acc = d if acc is None else acc + d
    return acc


def _group_sum(x, grp):
    return _dot_sel(x, grp, 2)


def _sigmoid(x):
    return 1.0 / (1.0 + jnp.exp(-x))


def _params(*sem):
    return pltpu.CompilerParams(dimension_semantics=sem, vmem_limit_bytes=VMEM_LIMIT)


def _ffn_kernel(*refs, n_mix):
    x_ref, g_ref, wg_ref, wu_ref, wd_ref = refs[:5]
    mix_refs = refs[5:5 + 2 * n_mix]
    o_ref, x_scr, h_scr, acc_scr = refs[5 + 2 * n_mix:]
    f = pl.program_id(1)

    @pl.when(f == 0)
    def _():
        x = x_ref[...]
        for k in range(n_mix):
            x = x + _bdot(mix_refs[k][...], mix_refs[n_mix + k][...])
        x_scr[...] = x
        ms = jnp.mean(x * x, axis=-1, keepdims=True)
        h_scr[...] = (x * lax.rsqrt(ms + EPS) * g_ref[...]).astype(BF16)
        acc_scr[...] = jnp.zeros_like(acc_scr)

    h = h_scr[...]
    a = jnp.dot(h, wg_ref[...], preferred_element_type=F32)
    u = jnp.dot(h, wu_ref[...], preferred_element_type=F32)
    z = (a * _sigmoid(a)) * u
    acc_scr[...] += jnp.dot(z.astype(BF16), wd_ref[...], preferred_element_type=F32)

    @pl.when(f == pl.num_programs(1) - 1)
    def _():
        o_ref[...] = x_scr[...] + 0.5 * acc_scr[...]


def _ffn(x, gain, wg, wu, wd, mix=(), mix_w=(), *, tm, tf):
    m, d = x.shape
    dff = wg.shape[1]
    row = lambda w: pl.BlockSpec((tm, w), lambda i, f: (i, 0))
    full = lambda a: pl.BlockSpec(a.shape, lambda i, f: (0,) * a.ndim)
    return pl.pallas_call(
        functools.partial(_ffn_kernel, n_mix=len(mix)),
        grid=(m // tm, dff // tf),
        in_specs=[
            row(d),
            pl.BlockSpec((1, d), lambda i, f: (0, 0)),
            pl.BlockSpec((d, tf), lambda i, f: (0, f)),
            pl.BlockSpec((d, tf), lambda i, f: (0, f)),
            pl.BlockSpec((tf, d), lambda i, f: (f, 0)),
        ] + [row(a.shape[1]) for a in mix] + [full(w) for w in mix_w],
        out_specs=row(d),
        out_shape=jax.ShapeDtypeStruct((m, d), F32),
        scratch_shapes=[pltpu.VMEM((tm, d), F32), pltpu.VMEM((tm, d), BF16), pltpu.VMEM((tm, d), F32)],
        compiler_params=_params("parallel", "arbitrary"),
        name="ffn",
    )(x, gain.reshape(1, d), wg, wu, wd, *mix, *mix_w)


def _inproj_kernel(x_ref, g_ref, w_ref, grp_ref, qg_ref, kg_ref,
                   q_ref, k_ref, vt_ref, iq_ref, ikb_ref, kw_ref, gx_ref, ab_ref, mq_ref):
    x = x_ref[...]
    ms = jnp.mean(x * x, axis=-1, keepdims=True)
    h = (x * lax.rsqrt(ms + EPS) * g_ref[...]).astype(BF16)
    p = jnp.dot(h, w_ref[...], preferred_element_type=F32)

    grp = grp_ref[...]
    dq = p[:, SEG_A:SEG_A + DSA_W]
    dk = p[:, SEG_A + DSA_W:SEG_A + 2 * DSA_W]
    dv = p[:, SEG_A + 2 * DSA_W:SEG_A + 3 * DSA_W]
    qms = _group_sum(dq * dq, grp) * (1.0 / HEAD_DIM)
    kms = _group_sum(dk * dk, grp) * (1.0 / HEAD_DIM)
    q_ref[...] = (dq * lax.rsqrt(qms + EPS) * qg_ref[...] * (HEAD_DIM ** -0.5 * LOG2E)).astype(BF16)
    k_ref[...] = (dk * lax.rsqrt(kms + EPS) * kg_ref[...]).astype(BF16)
    dvt = dv.T
    ones = jnp.ones((V_ROWS - HEAD_DIM, dvt.shape[1]), F32)
    for hd in range(DSA_HEADS):
        vt_ref[hd] = jnp.concatenate([dvt[hd * HEAD_DIM:(hd + 1) * HEAD_DIM], ones], axis=0).astype(BF16)

    iq_ref[...] = (p[:, SEG_IQ:SEG_IQ + IDX_HEADS * IDX_DIM] * (IDX_DIM ** -0.5)).astype(BF16)
    kw = p[:, SEG_KW:SEG_KW + LANES]
    kw_ref[...] = kw
    ikb_ref[...] = kw.astype(BF16)
    gx_ref[...] = p[:, SEG_G:SEG_G + 4 * GDN_W]
    ab_ref[...] = p[:, SEG_AB:SEG_AB + LANES]
    mq_ref[...] = p[:, SEG_MQ:SEG_MQ + MEM_W]


def _inproj(x, gain, w_packed, grp, qg, kg, *, tm):
    m, d = x.shape
    row = lambda w: pl.BlockSpec((tm, w), lambda i: (i, 0))
    full = lambda a: pl.BlockSpec(a.shape, lambda i: (0,) * a.ndim)
    out_shape = (
        jax.ShapeDtypeStruct((m, DSA_W), BF16),
        jax.ShapeDtypeStruct((m, DSA_W), BF16),
        jax.ShapeDtypeStruct((DSA_HEADS, V_ROWS, m), BF16),
        jax.ShapeDtypeStruct((m, IDX_HEADS * IDX_DIM), BF16),
        jax.ShapeDtypeStruct((m, LANES), BF16),
        jax.ShapeDtypeStruct((m, LANES), F32),
        jax.ShapeDtypeStruct((m, 4 * GDN_W), F32),
        jax.ShapeDtypeStruct((m, LANES), F32),
        jax.ShapeDtypeStruct((m, MEM_W), F32),
    )
    out_specs = (row(DSA_W), row(DSA_W), pl.BlockSpec((DSA_HEADS, V_ROWS, tm), lambda i: (0, 0, i)),
                 row(IDX_HEADS * IDX_DIM), row(LANES), row(LANES), row(4 * GDN_W),
                 row(LANES), row(MEM_W))
    return pl.pallas_call(
        _inproj_kernel,
        grid=(m // tm,),
        in_specs=[row(d), full(gain), full(w_packed), full(grp), full(qg), full(kg)],
        out_specs=out_specs,
        out_shape=out_shape,
        compiler_params=_params("parallel"),
        name="inproj",
    )(x, gain, w_packed, grp, qg, kg)


_DENORMAL_TOP = 0x007FFFFF


def _float_to_key(f):
    bits = lax.bitcast_convert_type(f, jnp.int32)
    mag = jnp.maximum((bits & jnp.int32(0x7FFFFFFF)) - _DENORMAL_TOP, 0)
    return jnp.where(bits >= 0, mag, -mag)


def _key_to_float(key):
    mag = jnp.abs(key)
    bits = jnp.where(mag > 0, mag + _DENORMAL_TOP, 0)
    return lax.bitcast_convert_type(jnp.where(key < 0, bits | jnp.int32(-2 ** 31), bits), F32)


def _upper_normal_quantile(p):
    pp = jnp.clip(jnp.minimum(p, 1.0 - p), 1e-30, 0.5)
    t = jnp.sqrt(-2.0 * jnp.log(pp))
    z = t - (2.515517 + t * (0.802853 + t * 0.010328)) / (1.0 + t * (1.432788 + t * (0.189269 + t * 0.001308)))
    return jnp.where(p <= 0.5, z, -z)


def _tree(op, x, group=8):
    parts = x.reshape(x.shape[0] // group, group, LANES)
    k = parts.shape[0]
    while k > 1:
        k //= 2
        parts = op(parts[:k], parts[k:2 * k])
    return parts[0]


def _dsa_kernel(q_ref, iq_ref, kwq_ref, k_ref, vt_ref, ik_ref, tab_ref, ltri_ref, o_ref,
                s_scr, iqt_scr, qpad_scr, bias_scr, mx_scr, brange_scr, *, kc, top_k):
    j = pl.program_id(1)
    per = kc // Q_BLOCK
    pad = kc - Q_BLOCK
    nch = j // per + 1
    q_pos = j * Q_BLOCK + lax.broadcasted_iota(jnp.int32, (1, LANES), 1)
    row_iota = lax.broadcasted_iota(jnp.int32, (kc, LANES), 0)

    def rows(i):
        return pl.ds(pl.multiple_of(j * Q_BLOCK - i * kc, Q_BLOCK), kc)

    def chunk_pairs(start, body, init, width=2, stop=None):
        count = jnp.maximum((nch if stop is None else stop) - start, 0)
        rest = count % width
        state = lax.fori_loop(0, rest, lambda r, st: body(start + r, st), init)

        def group(t, st):
            for u in range(width):
                st = body(start + rest + width * t + u, st)
            return st
        return lax.fori_loop(0, count // width, group, state)

    @pl.when((pl.program_id(0) == 0) & (j == 0))
    def _():
        r = lax.broadcasted_iota(jnp.int32, (2 * Q_BLOCK, LANES), 0)
        c = lax.broadcasted_iota(jnp.int32, (2 * Q_BLOCK, LANES), 1)
        dist = Q_BLOCK + c - r
        bucket = jnp.zeros_like(dist)
        for first in _BUCKET_STARTS:
            bucket = bucket + jnp.where(dist >= first, 1, 0)
        for h in range(DSA_HEADS):
            far = tab_ref[NUM_BUCKETS - 1, h]
            delta = jnp.zeros((2 * Q_BLOCK, LANES), F32)
            b_hi = jnp.float32(0.0)
            b_lo = jnp.float32(0.0)
            for b in range(NUM_BUCKETS - 1):
                delta = jnp.where(bucket == b, tab_ref[b, h] - far, delta)
                b_hi = jnp.maximum(b_hi, (tab_ref[b, h] - far) * LOG2E)
                b_lo = jnp.minimum(b_lo, (tab_ref[b, h] - far) * LOG2E)
            bias_scr[h] = jnp.where(dist >= 0, delta * LOG2E, 0.0)
            brange_scr[0, h] = b_hi
            brange_scr[1, h] = b_lo

    iqt = iq_ref[...].astype(F32).T
    zpad = jnp.zeros((LANES - IDX_DIM, LANES), F32)
    for h in range(IDX_HEADS):
        iqt_scr[:, h * LANES:(h + 1) * LANES] = jnp.concatenate(
            [iqt[h * IDX_DIM:(h + 1) * IDX_DIM], zpad], axis=0).astype(BF16)
    wt = kwq_ref[...].T[IDX_DIM:IDX_DIM + IDX_HEADS, :] * (IDX_HEADS ** -0.5)
    qt = q_ref[...].astype(F32).T
    half = lax.broadcasted_iota(jnp.int32, (LANES, LANES), 0) < HEAD_DIM
    for h in range(DSA_HEADS):
        pair = qt[(h // 2) * LANES:(h // 2 + 1) * LANES]
        keep = half if h % 2 == 0 else jnp.logical_not(half)
        qpad_scr[:, h * LANES:(h + 1) * LANES] = jnp.where(keep, pair, 0.0).astype(BF16)

    kmax = tab_ref[NUM_BUCKETS, 0]
    spread = jnp.zeros((1, LANES), F32)
    for h in range(DSA_HEADS):
        b_hi = brange_scr[0, h]
        b_lo = brange_scr[1, h]
        qh = qt[h * HEAD_DIM:(h + 1) * HEAD_DIM]
        bound = jnp.sqrt(jnp.sum(qh * qh, axis=0, keepdims=True)) * kmax
        mx_scr[h:h + 1, :] = bound + b_hi
        spread = jnp.maximum(spread, 2.0 * bound + (b_hi - b_lo))
    bound_ok = jnp.max(spread) <= MAX_LOGIT_SPREAD

    def score_chunk(i, carry, edge):
        d = jnp.dot(ik_ref[0, rows(i), :], iqt_scr[...], preferred_element_type=F32)
        acc = jnp.maximum(d[:, :LANES], 0.0) * wt[0:1, :]
        for h in range(1, IDX_HEADS):
            acc = acc + jnp.maximum(d[:, h * LANES:(h + 1) * LANES], 0.0) * wt[h:h + 1, :]
        sc = fin = acc
        if edge:
            key = row_iota + (j * Q_BLOCK - i * kc - pad)
            adm = jnp.where(key >= 0, key, q_pos + 1) <= q_pos
            sc = jnp.where(adm, acc, -jnp.inf)
            fin = jnp.where(adm, acc, 0.0)
        s_scr[rows(i), :] = sc
        tot, sq, top = carry
        return (tot + _tree(jnp.add, fin), sq + _tree(jnp.add, fin * fin),
                jnp.maximum(top, _tree(jnp.maximum, sc)))

    zero8 = jnp.zeros((8, LANES), F32)
    stats = score_chunk(0, (zero8, zero8, jnp.full((8, LANES), -jnp.inf, F32)), True)
    stats = lax.fori_loop(0, jnp.minimum(nch - 1, 1), lambda _, st: score_chunk(nch - 1, st, True), stats)
    tot, sq, top = chunk_pairs(1, lambda i, st: score_chunk(i, st, False), stats, width=4, stop=nch - 1)

    def count(pred):
        def body(i, cnt):
            return cnt + _tree(jnp.add, jnp.where(pred(s_scr[rows(i), :]), 1.0, 0.0))
        cnt = chunk_pairs(0, body, zero8)
        return jnp.sum(cnt, axis=0, keepdims=True)

    def max_below(t):
        def body(i, m):
            s = s_scr[rows(i), :]
            return jnp.maximum(m, _tree(jnp.maximum, jnp.where(s < t, s, -jnp.inf)))
        m = lax.fori_loop(0, nch, body, jnp.full((8, LANES), -jnp.inf, F32))
        return jnp.max(m, axis=0, keepdims=True)

    kf = float(top_k)
    n_adm = (q_pos + 1).astype(F32)
    mean = jnp.sum(tot, axis=0, keepdims=True) / n_adm
    std = jnp.sqrt(jnp.maximum(jnp.sum(sq, axis=0, keepdims=True) / n_adm - mean * mean, 0.0))
    first_guess = mean + _upper_normal_quantile(kf / n_adm) * std
    lowest = jnp.float32(-3.0e38)
    top1 = jnp.minimum(jnp.max(top, axis=0, keepdims=True), -lowest)

    def update(st, key):
        lo, hi, c_lo, c_hi, lo_set, hi_set, done, run = st
        key = jnp.clip(key, lo + 1, hi - 1)
        t = _key_to_float(key)
        c = count(lambda s: s >= t)
        active = done < 0.5
        up = active & (c >= kf)
        dn = active & (c < kf)
        lo, c_lo, lo_set = jnp.where(up, key, lo), jnp.where(up, c, c_lo), jnp.where(up, 1.0, lo_set)
        hi, c_hi, hi_set = jnp.where(dn, key, hi), jnp.where(dn, c, c_hi), jnp.where(dn, 1.0, hi_set)
        done = jnp.where((c_lo == kf) | (hi <= lo + 1), 1.0, done)
        return lo, hi, c_lo, c_hi, lo_set, hi_set, done, run

    def guess(st, midpoint):
        lo, hi, c_lo, c_hi, lo_set, hi_set, done, run = st
        v_lo, v_hi = _key_to_float(lo), _key_to_float(hi)
        both = (lo_set > 0.5) & (hi_set > 0.5)
        step = std * 0.25 * jnp.exp2(run)
        if midpoint:
            inner = 0.5 * v_lo + 0.5 * v_hi
        else:
            l_lo, l_hi = jnp.log(c_lo), jnp.log(jnp.maximum(c_hi, 0.5))
            frac = jnp.clip((l_lo - math.log(kf - 0.5)) / jnp.maximum(l_lo - l_hi, 1e-6), 0.0, 1.0)
            inner = v_lo + (v_hi - v_lo) * frac
        t = jnp.where(both, inner,
                      jnp.where(hi_set > 0.5, v_hi - step, jnp.where(lo_set > 0.5, v_lo + step, first_guess)))
        t = jnp.where(t != t, 0.0, t)
        run = jnp.where(both, 0.0, run + 1.0)
        return _float_to_key(t), (lo, hi, c_lo, c_hi, lo_set, hi_set, done, run)

    def pending(st):
        return 1.0 - jnp.min(st[6])

    flag0 = jnp.zeros((1, LANES), F32)
    st = (jnp.full((1, LANES), _float_to_key(lowest), jnp.int32), _float_to_key(top1) + 1,
          n_adm, flag0, flag0, flag0, jnp.where(n_adm <= kf, 1.0, 0.0), flag0)

    def probes(st):
        for r in range(PROBES_PER_ROUND):
            key, st = guess(st, midpoint=((r + 1) % MIDPOINT_EVERY == 0))
            st = update(st, key)
        return st

    st = probes(probes(st))

    def fast_round(carry):
        st = probes(carry[2])
        return carry[0] + 1, pending(st), st

    _, _, st = lax.while_loop(lambda c: (c[0] < FAST_ROUNDS) & (c[1] > 0.5), fast_round,
                              (jnp.int32(2), pending(st), st))

    def safe_round(carry):
        p, _, st = carry
        st = update(st, st[0] + lax.shift_right_logical(st[1] - st[0], 1))
        lo, hi, done = st[0], st[1], st[6]
        below = _float_to_key(max_below(_key_to_float(hi)))
        hi = jnp.where(done < 0.5, jnp.clip(below + 1, lo + 1, hi), hi)
        done = jnp.where(hi <= lo + 1, 1.0, done)
        st = update((lo, hi) + st[2:6] + (done, st[7]), hi - 1)
        return p + 1, pending(st), st

    _, _, st = lax.while_loop(lambda c: (c[1] > 0.5) & (c[0] < 34), safe_round,
                              (jnp.int32(0), pending(st), st))

    thr = _key_to_float(st[0])
    has_ties = jnp.max(st[2]) > kf

    @pl.when(jnp.logical_not(has_ties))
    def _():
        def mask_chunk(i, carry):
            s_scr[rows(i), :] = jnp.where(s_scr[rows(i), :] >= thr, 0.0, NEG)
            return carry
        lax.fori_loop(0, nch, mask_chunk, 0)

    @pl.when(has_ties)
    def _():
        need = kf - count(lambda s: s > thr)

        def mask_chunk(t, run):
            i = nch - 1 - t
            s = s_scr[rows(i), :]
            tie = jnp.where(s == thr, 1.0, 0.0)
            pref = jnp.dot(ltri_ref[...], tie.astype(BF16), preferred_element_type=F32) + run
            tie_sel = jnp.where(pref <= need, tie, 0.0)
            sel = jnp.where(s > thr, 1.0, tie_sel)
            s_scr[rows(i), :] = jnp.where(sel > 0.5, 0.0, NEG)
            return run + jnp.sum(tie, axis=0, keepdims=True)
        lax.fori_loop(0, nch, mask_chunk, jnp.zeros((1, LANES), F32))

    def logits(i, with_bias):
        msk = s_scr[rows(i), :]
        out = []
        for pr in range(DSA_HEADS // 2):
            l2 = jnp.dot(k_ref[0, rows(i), pr * LANES:(pr + 1) * LANES],
                         qpad_scr[:, 2 * pr * LANES:(2 * pr + 2) * LANES],
                         preferred_element_type=F32)
            for e in range(2):
                l = l2[:, e * LANES:(e + 1) * LANES] + msk
                if with_bias:
                    far_rows = kc - 2 * Q_BLOCK
                    l = jnp.concatenate([l[:far_rows], l[far_rows:] + bias_scr[2 * pr + e]], axis=0)
                out.append(l)
        return out

    def max_step(i, ms, with_bias):
        return tuple(jnp.maximum(m, _tree(jnp.maximum, l)) for m, l in zip(ms, logits(i, with_bias)))

    @pl.when(jnp.logical_not(bound_ok))
    def _():
        ms = max_step(0, tuple(jnp.full((8, LANES), NEG, F32) for _ in range(DSA_HEADS)), True)
        ms = lax.fori_loop(1, nch, lambda i, m: max_step(i, m, False), ms)
        for h in range(DSA_HEADS):
            mx_scr[h:h + 1, :] = jnp.max(ms[h], axis=0, keepdims=True)

    mx = [mx_scr[h:h + 1, :] for h in range(DSA_HEADS)]

    def pv_step(i, accs, with_bias):
        return tuple(
            acc + jnp.dot(vt_ref[h, :, rows(i)], jnp.exp2(l - mx[h]).astype(BF16),
                          preferred_element_type=F32)
            for h, (acc, l) in enumerate(zip(accs, logits(i, with_bias))))

    accs = pv_step(0, tuple(jnp.zeros((V_ROWS, LANES), F32) for _ in range(DSA_HEADS)), True)
    accs = chunk_pairs(1, lambda i, a: pv_step(i, a, False), accs, width=4)
    outs = [acc[:HEAD_DIM] / acc[HEAD_DIM:HEAD_DIM + 1] for acc in accs]
    o_ref[...] = jnp.concatenate(outs, axis=0).T


def _dsa(q, iq, kw, k, vt, ikb, rel_bias, k_gain, ltri, *, batch, seq, kc):
    nb = seq // Q_BLOCK
    top_k = min(TOPK_MAX, seq // 4)
    pad = kc - Q_BLOCK
    seqp = seq + pad
    kp = jnp.pad(k.reshape(batch, seq, DSA_W), ((0, 0), (pad, 0), (0, 0)))
    ikp = jnp.pad(ikb.reshape(batch, seq, LANES), ((0, 0), (pad, 0), (0, 0)))
    vtp = jnp.pad(vt.reshape(DSA_HEADS, V_ROWS, batch, seq),
                  ((0, 0), (0, 0), (0, 0), (pad, 0))).reshape(DSA_HEADS, V_ROWS, batch * seqp)
    kmax = (HEAD_DIM ** 0.5 * 1.01) * jnp.max(jnp.abs(k_gain.astype(F32)))
    table = jnp.concatenate([rel_bias.astype(F32), jnp.full((1, DSA_HEADS), kmax, F32)], axis=0)
    qrow = lambda w: pl.BlockSpec((Q_BLOCK, w), lambda b, j: (b * nb + j, 0))
    return pl.pallas_call(
        functools.partial(_dsa_kernel, kc=kc, top_k=top_k),
        grid=(batch, nb),
        in_specs=[
            qrow(DSA_W), qrow(IDX_HEADS * IDX_DIM), qrow(LANES),
            pl.BlockSpec((1, seqp, DSA_W), lambda b, j: (b, 0, 0)),
            pl.BlockSpec((DSA_HEADS, V_ROWS, seqp), lambda b, j: (0, 0, b)),
            pl.BlockSpec((1, seqp, LANES), lambda b, j: (b, 0, 0)),
            pl.BlockSpec(memory_space=pltpu.SMEM),
            pl.BlockSpec(ltri.shape, lambda b, j: (0, 0)),
        ],
        out_specs=qrow(DSA_W),
        out_shape=jax.ShapeDtypeStruct((batch * seq, DSA_W), F32),
        scratch_shapes=[
            pltpu.VMEM((seqp, LANES), F32),
            pltpu.VMEM((LANES, IDX_HEADS * LANES), BF16),
            pltpu.VMEM((LANES, DSA_HEADS * LANES), BF16),
            pltpu.VMEM((DSA_HEADS, 2 * Q_BLOCK, LANES), F32),
            pltpu.VMEM((8, LANES), F32),
            pltpu.SMEM((2, DSA_HEADS), F32),
        ],
        compiler_params=_params("arbitrary", "arbitrary"),
        name="dsa",
    )(q, iq, kw, kp, vtp, ikp, table, ltri)


GDN_GROUP = 2


def _gdn_kernel(x_ref, ab_ref, cw_ref, ea_ref, eb_ref, alog_ref, dtb_ref, gain_ref, grp_ref,
                o_ref, xpad_scr, qkv_scr, gb_scr, gc_scr, gr_scr, pre_scr, st_scr, o_scr, *, tb):
    t = pl.program_id(1)
    cin = 3 * GDN_W
    n_chunks = tb // GDN_CHUNK
    pairs = range(GDN_HEADS // 2)
    lanes = lambda p: slice(p * LANES, (p + 1) * LANES)

    @pl.when(t == 0)
    def _():
        xpad_scr[0:8, :] = jnp.zeros((8, cin), F32)
        st_scr[...] = jnp.zeros_like(st_scr)

    x = x_ref[...]
    xpad_scr[8:8 + tb, :] = x[:, :cin]
    conv = jnp.zeros((tb, cin), F32)
    for jj in range(CONV_WIDTH):
        conv = conv + cw_ref[jj:jj + 1, :] * xpad_scr[pl.ds(8 - (CONV_WIDTH - 1) + jj, tb), :]
    xpad_scr[0:8, :] = x[tb - 8:tb, :cin]
    qkv = conv * _sigmoid(conv)
    grp = grp_ref[...]
    q = qkv[:, :GDN_W]
    k = qkv[:, GDN_W:2 * GDN_W]
    qkv_scr[:, :GDN_W] = q * lax.rsqrt(_group_sum(q * q, grp) + EPS) * (HEAD_DIM ** -0.5)
    qkv_scr[:, GDN_W:2 * GDN_W] = k * lax.rsqrt(_group_sum(k * k, grp) + EPS)
    qkv_scr[:, 2 * GDN_W:] = qkv[:, 2 * GDN_W:]

    ab = ab_ref[...]
    a_e = _dot_sel(ab, ea_ref[...], 3) + dtb_ref[...]
    b_e = _dot_sel(ab, eb_ref[...], 3)
    softplus = jnp.maximum(a_e, 0.0) + jnp.log(1.0 + jnp.exp(-jnp.abs(a_e)))
    gb_scr[:, :GDN_W] = -jnp.exp(alog_ref[...]) * softplus
    gb_scr[:, GDN_W:] = _sigmoid(b_e)

    r64 = lax.broadcasted_iota(jnp.int32, (GDN_CHUNK, LANES), 0)
    c64 = lax.broadcasted_iota(jnp.int32, (GDN_CHUNK, LANES), 1) % GDN_CHUNK
    causal = c64 <= r64
    strict = c64 < r64
    eye = jnp.where(c64 == r64, 1.0, 0.0)
    r128 = lax.broadcasted_iota(jnp.int32, (LANES, LANES), 0)
    c128 = lax.broadcasted_iota(jnp.int32, (LANES, LANES), 1)
    bdmask = (r128 // HEAD_DIM) == (c128 // HEAD_DIM)

    def bd(m):
        return jnp.where(bdmask, jnp.concatenate([m, m], axis=0), 0.0)

    lt_r = lax.broadcasted_iota(jnp.int32, (GDN_CHUNK, GDN_CHUNK), 0)
    lt_c = lax.broadcasted_iota(jnp.int32, (GDN_CHUNK, GDN_CHUNK), 1)
    ltri = jnp.where(lt_c <= lt_r, 1.0, 0.0)
    ones8 = jnp.ones((8, GDN_CHUNK), F32)
    up_r = lax.broadcasted_iota(jnp.int32, (GDN_CHUNK, GDN_W), 0)
    up_c = lax.broadcasted_iota(jnp.int32, (GDN_CHUNK, GDN_W), 1) % GDN_CHUNK
    upper = jnp.where(up_r <= up_c, 1.0, 0.0)
    for ci in range(n_chunks):
        g = gb_scr[ci * GDN_CHUNK:(ci + 1) * GDN_CHUNK, :GDN_W]
        gc_scr[ci * GDN_CHUNK:(ci + 1) * GDN_CHUNK, :] = _sel_dot(ltri, g, 3)
        gr_scr[ci * 8:(ci + 1) * 8, :] = _sel_dot(ones8, g * upper, 3)

    def par_body(gi, carry):
        inst = [(u, p) for u in range(GDN_GROUP) for p in pairs]
        rows = [pl.ds(pl.multiple_of((gi * GDN_GROUP + u) * GDN_CHUNK, GDN_CHUNK), GDN_CHUNK)
                for u in range(GDN_GROUP)]
        grow = [pl.ds(pl.multiple_of((gi * GDN_GROUP + u) * 8, 8), 8) for u in range(GDN_GROUP)]
        off = lambda s, p: slice(s * GDN_W + p * LANES, s * GDN_W + (p + 1) * LANES)
        qp = [qkv_scr[rows[u], off(0, p)] for u, p in inst]
        kp = [qkv_scr[rows[u], off(1, p)] for u, p in inst]
        vp = [qkv_scr[rows[u], off(2, p)] for u, p in inst]
        beta = [gb_scr[rows[u], off(1, p)] for u, p in inst]
        gc = [gc_scr[rows[u], lanes(p)] for u, p in inst]
        gr = [gr_scr[grow[u], lanes(p)][0:1, :] for u, p in inst]
        dmat = [jnp.exp(jnp.where(causal, a - b, NEG)) for a, b in zip(gc, gr)]
        kbd = [jnp.where(bdmask, jnp.concatenate([a, a], axis=0).T, 0.0) for a in kp]
        kb = [a * b for a, b in zip(kp, beta)]
        kk = [_dot3(a, b) for a, b in zip(kb, kbd)]
        qk = [_dot3(a, b) for a, b in zip(qp, kbd)]
        lm = [jnp.where(strict, a * d, 0.0) for a, d in zip(kk, dmat)]
        aintra = [a * d for a, d in zip(qk, dmat)]
        tinv = [eye - a for a in lm]
        lpow = lm
        for _ in range(5):
            lpow = [_dot3(a, bd(a)) for a in lpow]
            tinv = [a + _dot3(a, bd(b)) for a, b in zip(tinv, lpow)]
        egc = [jnp.exp(a) for a in gc]
        un = [_dot3(a, bd(v * b)) for a, v, b in zip(tinv, vp, beta)]
        wn = [_dot3(a, bd(b * e)) for a, b, e in zip(tinv, kb, egc)]
        for n, (u, p) in enumerate(inst):
            pre_scr[0, rows[u], lanes(p)] = un[n]
            pre_scr[1, rows[u], lanes(p)] = wn[n]
            pre_scr[2, rows[u], lanes(p)] = aintra[n]
            pre_scr[3, rows[u], lanes(p)] = qp[n] * egc[n]
            pre_scr[4, rows[u], lanes(p)] = kp[n] * jnp.exp(gc[n][GDN_CHUNK - 1:GDN_CHUNK, :] - gc[n])
        return carry

    lax.fori_loop(0, n_chunks // GDN_GROUP, par_body, 0)

    def rec_body(ci, carry):
        rows = pl.ds(pl.multiple_of(ci * GDN_CHUNK, GDN_CHUNK), GDN_CHUNK)
        tail = gc_scr[pl.ds(pl.multiple_of(ci * GDN_CHUNK + GDN_CHUNK - 8, 8), 8), :]
        egl = jnp.exp(tail[7:8, :])
        sbd = [st_scr[p] for p in pairs]
        un, wn, aintra, qe, kd = [[pre_scr[s, rows, lanes(p)] for p in pairs] for s in range(5)]
        ws = [_dot3(a, s) for a, s in zip(wn, sbd)]
        qs = [_dot3(a, s) for a, s in zip(qe, sbd)]
        v_new = [a - b for a, b in zip(un, ws)]
        o = [a + _dot3(b, bd(v)) for a, b, v in zip(qs, aintra, v_new)]
        upd = [_dot3(a.T, v) for a, v in zip(kd, v_new)]
        for p in pairs:
            st_scr[p] = sbd[p] * egl[:, lanes(p)] + jnp.where(bdmask, upd[p], 0.0)
            o_scr[rows, lanes(p)] = o[p]
        return carry

    lax.fori_loop(0, n_chunks, rec_body, 0)

    o = o_scr[...]
    oms = _group_sum(o * o, grp) * (1.0 / HEAD_DIM)
    z = x[:, cin:]
    o_ref[...] = o * lax.rsqrt(oms + EPS) * gain_ref[...] * (z * _sigmoid(z))


def _gdn(gx, ab, conv_w, ea, eb, alog, dtb, gain, grp, *, batch, seq, tb):
    nt = seq // tb
    cin = 3 * GDN_W
    row = lambda w: pl.BlockSpec((tb, w), lambda b, t: (b * nt + t, 0))
    full = lambda a: pl.BlockSpec(a.shape, lambda b, t: (0,) * a.ndim)
    return pl.pallas_call(
        functools.partial(_gdn_kernel, tb=tb),
        grid=(batch, nt),
        in_specs=[row(4 * GDN_W), row(LANES), full(conv_w), full(ea), full(eb), full(alog),
                  full(dtb), full(gain), full(grp)],
        out_specs=row(GDN_W),
        out_shape=jax.ShapeDtypeStruct((batch * seq, GDN_W), F32),
        scratch_shapes=[
            pltpu.VMEM((tb + 8, cin), F32),
            pltpu.VMEM((tb, cin), F32),
            pltpu.VMEM((tb, 2 * GDN_W), F32),
            pltpu.VMEM((tb, GDN_W), F32),
            pltpu.VMEM((tb // 8, GDN_W), F32),
            pltpu.VMEM((5, tb, GDN_W), F32),
            pltpu.VMEM((GDN_HEADS // 2, LANES, LANES), F32),
            pltpu.VMEM((tb, GDN_W), F32),
        ],
        compiler_params=_params("arbitrary", "arbitrary"),
        name="gdn",
    )(gx, ab, conv_w, ea, eb, alog, dtb, gain, grp)


def _memkv_kernel(mem_ref, g_ref, w_ref, kg_ref, grp_ref, kt_ref, vp_ref):
    x = mem_ref[0]
    ms = jnp.mean(x * x, axis=-1, keepdims=True)
    h = (x * lax.rsqrt(ms + EPS) * g_ref[...]).astype(BF16)
    kv = jnp.dot(h, w_ref[...], preferred_element_type=F32)
    km = kv[:, :MEM_W]
    vm = kv[:, MEM_W:]
    kms = _group_sum(km * km, grp_ref[...]) * (1.0 / HEAD_DIM)
    kt = (km * lax.rsqrt(kms + EPS) * kg_ref[...]).T
    n_mem = x.shape[0]
    top = lax.broadcasted_iota(jnp.int32, (LANES, n_mem), 0) < HEAD_DIM
    left = lax.broadcasted_iota(jnp.int32, (n_mem, LANES), 1) < HEAD_DIM
    for hh in range(MEM_HEADS):
        pr = slice((hh // 2) * LANES, (hh // 2 + 1) * LANES)
        keep_r = top if hh % 2 == 0 else jnp.logical_not(top)
        keep_c = left if hh % 2 == 0 else jnp.logical_not(left)
        kt_ref[0, hh] = jnp.where(keep_r, kt[pr, :], 0.0).astype(BF16)
        vp_ref[0, hh] = jnp.where(keep_c, vm[:, pr], 0.0).astype(BF16)


def _memkv(mem, gain, w, kg, grp):
    b, n_mem, d = mem.shape
    full = lambda a: pl.BlockSpec(a.shape, lambda i: (0,) * a.ndim)
    return pl.pallas_call(
        _memkv_kernel,
        grid=(b,),
        in_specs=[pl.BlockSpec((1, n_mem, d), lambda i: (i, 0, 0)), full(gain), full(w), full(kg),
                  full(grp)],
        out_specs=(pl.BlockSpec((1, MEM_HEADS, LANES, n_mem), lambda i: (i, 0, 0, 0)),
                   pl.BlockSpec((1, MEM_HEADS, n_mem, LANES), lambda i: (i, 0, 0, 0))),
        out_shape=(jax.ShapeDtypeStruct((b, MEM_HEADS, LANES, n_mem), BF16),
                   jax.ShapeDtypeStruct((b, MEM_HEADS, n_mem, LANES), BF16)),
        compiler_params=_params("parallel"),
        name="memkv",
    )(mem, gain, w, kg, grp)


def _memattn_kernel(q_ref, qg_ref, grp_ref, kt_ref, vp_ref, o_ref):
    q = q_ref[...]
    qms = _group_sum(q * q, grp_ref[...]) * (1.0 / HEAD_DIM)
    qn = (q * lax.rsqrt(qms + EPS) * qg_ref[...] * (HEAD_DIM ** -0.5)).astype(BF16)
    outs = []
    for pr in range(MEM_HEADS // 2):
        qp = qn[:, pr * LANES:(pr + 1) * LANES]
        acc = None
        for e in range(2):
            hh = 2 * pr + e
            l = jnp.dot(qp, kt_ref[0, hh], preferred_element_type=F32)
            l = l - jnp.max(l, axis=-1, keepdims=True)
            p = jnp.exp(l)
            p = p / jnp.sum(p, axis=-1, keepdims=True)
            o = jnp.dot(p.astype(BF16), vp_ref[0, hh], preferred_element_type=F32)
            acc = o if acc is None else acc + o
        outs.append(acc)
    o_ref[...] = jnp.concatenate(outs, axis=-1)


def _memattn(mq, qg, grp, kt, vp, *, batch, seq, tm):
    nt = seq // tm
    n_mem = kt.shape[-1]
    full = lambda a: pl.BlockSpec(a.shape, lambda b, t: (0,) * a.ndim)
    return pl.pallas_call(
        _memattn_kernel,
        grid=(batch, nt),
        in_specs=[
            pl.BlockSpec((tm, MEM_W), lambda b, t: (b * nt + t, 0)), full(qg), full(grp),
            pl.BlockSpec((1, MEM_HEADS, LANES, n_mem), lambda b, t: (b, 0, 0, 0)),
            pl.BlockSpec((1, MEM_HEADS, n_mem, LANES), lambda b, t: (b, 0, 0, 0)),
        ],
        out_specs=pl.BlockSpec((tm, MEM_W), lambda b, t: (b * nt + t, 0)),
        out_shape=jax.ShapeDtypeStruct((batch * seq, MEM_W), F32),
        compiler_params=_params("parallel", "parallel"),
        name="memattn",
    )(mq, qg, grp, kt, vp)


def _group_ones(width):
    idx = np.arange(width) // HEAD_DIM
    return jnp.asarray((idx[:, None] == idx[None, :]).astype(np.float32))


def _t5_bucket_np(n):
    max_exact = NUM_BUCKETS // 2
    nf = np.maximum(n, 1).astype(np.float32)
    large = max_exact + (np.log(nf / np.float32(max_exact)) / np.float32(math.log(MAX_DISTANCE / max_exact))
                         * (NUM_BUCKETS - max_exact)).astype(np.int32)
    large = np.minimum(large, NUM_BUCKETS - 1)
    return np.where(n < max_exact, n, large)


def _bucket_starts():
    dist = np.arange(2 * Q_BLOCK)
    bucket = _t5_bucket_np(dist)
    assert (np.diff(bucket) >= 0).all() and bucket[-1] == NUM_BUCKETS - 1
    return tuple(int(dist[bucket >= b].min()) for b in range(1, NUM_BUCKETS))


_BUCKET_STARTS = _bucket_starts()


def _cast_kernel(w_ref, o_ref):
    o_ref[...] = w_ref[...].astype(BF16)


def _to_bf16(w, *, rows):
    nl, nr, nc = w.shape
    spec = pl.BlockSpec((1, rows, nc), lambda l, r: (l, r, 0))
    return pl.pallas_call(
        _cast_kernel,
        grid=(nl, nr // rows),
        in_specs=[spec],
        out_specs=spec,
        out_shape=jax.ShapeDtypeStruct(w.shape, BF16),
        compiler_params=_params("parallel", "parallel"),
        name="cast",
    )(w)


def _pack_w_in(w):
    o = 3 * DSA_W
    iq = w[:, o:o + IDX_HEADS * IDX_DIM]
    o += IDX_HEADS * IDX_DIM
    kw = w[:, o:o + IDX_DIM + IDX_HEADS]
    o += IDX_DIM + IDX_HEADS
    g = w[:, o:o + 4 * GDN_W]
    o += 4 * GDN_W
    ab = w[:, o:o + 2 * GDN_HEADS]
    o += 2 * GDN_HEADS
    mq = w[:, o:o + MEM_W]
    pad = lambda a: jnp.pad(a, ((0, 0), (0, LANES - a.shape[1])))
    return jnp.concatenate([w[:, :3 * DSA_W], iq, pad(kw), g, pad(ab), mq], axis=1)


def _tile_heads(v, heads):
    return jnp.tile(v.astype(F32), heads).reshape(1, heads * HEAD_DIM)


def kernel(x, mem, ffn1_norm, ffn1_w_gate, ffn1_w_up, ffn1_w_down, mix_norm, w_in, dsa_q_norm, dsa_k_norm, rel_bias, gdn_conv, gdn_A_log, gdn_dt_bias, gdn_out_norm, mem_norm, w_mem_kv, mem_q_norm, mem_k_norm, w_out, ffn2_norm, ffn2_w_gate, ffn2_w_up, ffn2_w_down):
    batch, seq, d = x.shape
    depth = w_in.shape[0]
    m = batch * seq
    tm = min(512, seq)
    kc = min(512, seq)
    dff = ffn1_w_gate.shape[-1]
    tf = dff // 2 if (dff // 2) % LANES == 0 else dff

    grp_a = _group_ones(DSA_W)
    grp_m = _group_ones(MEM_W)
    ltri = jnp.asarray(np.tril(np.ones((kc, kc), np.float32))).astype(BF16)
    heads_of = np.arange(GDN_W) // HEAD_DIM
    ea = jnp.asarray((np.arange(LANES)[:, None] == heads_of[None, :]).astype(np.float32))
    eb = jnp.asarray((np.arange(LANES)[:, None] == heads_of[None, :] + GDN_HEADS).astype(np.float32))

    w_in_b = _to_bf16(w_in, rows=min(256, d))
    xf = x.reshape(m, d)
    for l in range(depth):
        xf = _ffn(xf, ffn1_norm[l], ffn1_w_gate[l].astype(BF16), ffn1_w_up[l].astype(BF16),
                  ffn1_w_down[l].astype(BF16), tm=tm, tf=tf)
        (q, k, vt, iq, ikb, kw, gx, ab, mq) = _inproj(
            xf, mix_norm[l].reshape(1, d), _pack_w_in(w_in_b[l]), grp_a,
            _tile_heads(dsa_q_norm[l], DSA_HEADS), _tile_heads(dsa_k_norm[l], DSA_HEADS), tm=tm)
        out_a = _dsa(q, iq, kw, k, vt, ikb, rel_bias, dsa_k_norm[l], ltri, batch=batch, seq=seq, kc=kc)
        out_b = _gdn(gx, ab, gdn_conv[l], ea, eb,
                     jnp.repeat(gdn_A_log[l].astype(F32), HEAD_DIM).reshape(1, GDN_W),
                     jnp.repeat(gdn_dt_bias[l].astype(F32), HEAD_DIM).reshape(1, GDN_W),
                     _tile_heads(gdn_out_norm[l], GDN_HEADS), grp_a, batch=batch, seq=seq, tb=tm)
        kt, vp = _memkv(mem, mem_norm[l].reshape(1, d), w_mem_kv[l].astype(BF16),
                        _tile_heads(mem_k_norm[l], MEM_HEADS), grp_m)
        out_c = _memattn(mq, _tile_heads(mem_q_norm[l], MEM_HEADS), grp_m, kt, vp,
                         batch=batch, seq=seq, tm=tm)
        wo = w_out[l].astype(BF16)
        xf = _ffn(xf, ffn2_norm[l], ffn2_w_gate[l].astype(BF16), ffn2_w_up[l].astype(BF16),
                  ffn2_w_down[l].astype(BF16), mix=(out_a, out_b, out_c),
                  mix_w=(wo[:DSA_W], wo[DSA_W:DSA_W + GDN_W], wo[DSA_W + GDN_W:]), tm=tm, tf=tf)
    return xf.reshape(batch, seq, d)
```

```python
import functools
import math

import jax
import jax.numpy as jnp
import numpy as np
from jax import lax
from jax.experimental import pallas as pl
from jax.experimental.pallas import tpu as pltpu

F32 = jnp.float32
BF16 = jnp.bfloat16

HEAD_DIM = 64
DSA_HEADS = 6
GDN_HEADS = 6
MEM_HEADS = 4
DSA_W = DSA_HEADS * HEAD_DIM
GDN_W = GDN_HEADS * HEAD_DIM
MEM_W = MEM_HEADS * HEAD_DIM
IDX_HEADS = 8
IDX_DIM = 32
TOPK_MAX = 256
Q_BLOCK = 128
GDN_CHUNK = 64
CONV_WIDTH = 4
NUM_BUCKETS = 32
MAX_DISTANCE = 128
EPS = 1e-6

LANES = 128
VMEM_LIMIT = 52 * 1024 * 1024
NEG = -1e30
LOG2E = math.log2(math.e)
V_ROWS = HEAD_DIM + 16
MAX_LOGIT_SPREAD = 80.0
PROBES_PER_ROUND = 4
MIDPOINT_EVERY = 4
FAST_ROUNDS = 4

SEG_A = 0
SEG_IQ = SEG_A + 3 * DSA_W
SEG_KW = SEG_IQ + IDX_HEADS * IDX_DIM
SEG_G = SEG_KW + LANES
SEG_AB = SEG_G + 4 * GDN_W
SEG_MQ = SEG_AB + LANES
IN_PACKED = SEG_MQ + MEM_W


def _bdot(a, b):
    return jnp.dot(a.astype(BF16), b.astype(BF16), preferred_element_type=F32)


def _split(x, terms):
    out = []
    for _ in range(terms - 1):
        hi = x.astype(BF16)
        out.append(hi)
        x = x - hi.astype(F32)
    out.append(x.astype(BF16))
    return out


def _dot3(a, b):
    ah, al = _split(a, 2)
    bh, bl = _split(b, 2)
    mm = lambda x, y: jnp.dot(x, y, preferred_element_type=F32)
    return mm(ah, bh) + (mm(ah, bl) + mm(al, bh))


def _dot_sel(a, sel, terms):
    selb = sel.astype(BF16)
    acc = None
    for piece in _split(a, terms):
        d = jnp.dot(piece, selb, preferred_element_type=F32)
        acc = d if acc is None else acc + d
    return acc


def _sel_dot(sel, b, terms):
    selb = sel.astype(BF16)
    acc = None
    for piece in _split(b, terms):
        d = jnp.dot(selb, piece, preferred_element_type=F32)
        acc = d if acc is None else acc + d
    return acc


def _group_sum(x, grp):
    return _dot_sel(x, grp, 2)


def _sigmoid(x):
    return 1.0 / (1.0 + jnp.exp(-x))


def _params(*sem):
    return pltpu.CompilerParams(dimension_semantics=sem, vmem_limit_bytes=VMEM_LIMIT)


def _ffn_kernel(*refs, n_mix):
    x_ref, g_ref, wg_ref, wu_ref, wd_ref = refs[:5]
    mix_refs = refs[5:5 + 2 * n_mix]
    o_ref, x_scr, h_scr, acc_scr = refs[5 + 2 * n_mix:]
    f = pl.program_id(1)

    @pl.when(f == 0)
    def _():
        x = x_ref[...]
        for k in range(n_mix):
            x = x + _bdot(mix_refs[k][...], mix_refs[n_mix + k][...])
        x_scr[...] = x
        ms = jnp.mean(x * x, axis=-1, keepdims=True)
        h_scr[...] = (x * lax.rsqrt(ms + EPS) * g_ref[...]).astype(BF16)
        acc_scr[...] = jnp.zeros_like(acc_scr)

    h = h_scr[...]
    a = jnp.dot(h, wg_ref[...], preferred_element_type=F32)
    u = jnp.dot(h, wu_ref[...], preferred_element_type=F32)
    z = (a * _sigmoid(a)) * u
    acc_scr[...] += jnp.dot(z.astype(BF16), wd_ref[...], preferred_element_type=F32)

    @pl.when(f == pl.num_programs(1) - 1)
    def _():
        o_ref[...] = x_scr[...] + 0.5 * acc_scr[...]


def _ffn(x, gain, wg, wu, wd, mix=(), mix_w=(), *, tm, tf):
    m, d = x.shape
    dff = wg.shape[1]
    row = lambda w: pl.BlockSpec((tm, w), lambda i, f: (i, 0))
    full = lambda a: pl.BlockSpec(a.shape, lambda i, f: (0,) * a.ndim)
    return pl.pallas_call(
        functools.partial(_ffn_kernel, n_mix=len(mix)),
        grid=(m // tm, dff // tf),
        in_specs=[
            row(d),
            pl.BlockSpec((1, d), lambda i, f: (0, 0)),
            pl.BlockSpec((d, tf), lambda i, f: (0, f)),
            pl.BlockSpec((d, tf), lambda i, f: (0, f)),
            pl.BlockSpec((tf, d), lambda i, f: (f, 0)),
        ] + [row(a.shape[1]) for a in mix] + [full(w) for w in mix_w],
        out_specs=row(d),
        out_shape=jax.ShapeDtypeStruct((m, d), F32),
        scratch_shapes=[pltpu.VMEM((tm, d), F32), pltpu.VMEM((tm, d), BF16), pltpu.VMEM((tm, d), F32)],
        compiler_params=_params("parallel", "arbitrary"),
        name="ffn",
    )(x, gain.reshape(1, d), wg, wu, wd, *mix, *mix_w)


def _inproj_kernel(x_ref, g_ref, w_ref, grp_ref, qg_ref, kg_ref,
                   q_ref, k_ref, vt_ref, iq_ref, ikb_ref, kw_ref, gx_ref, ab_ref, mq_ref):
    x = x_ref[...]
    ms = jnp.mean(x * x, axis=-1, keepdims=True)
    h = (x * lax.rsqrt(ms + EPS) * g_ref[...]).astype(BF16)
    p = jnp.dot(h, w_ref[...], preferred_element_type=F32)

    grp = grp_ref[...]
    dq = p[:, SEG_A:SEG_A + DSA_W]
    dk = p[:, SEG_A + DSA_W:SEG_A + 2 * DSA_W]
    dv = p[:, SEG_A + 2 * DSA_W:SEG_A + 3 * DSA_W]
    qms = _group_sum(dq * dq, grp) * (1.0 / HEAD_DIM)
    kms = _group_sum(dk * dk, grp) * (1.0 / HEAD_DIM)
    q_ref[...] = (dq * lax.rsqrt(qms + EPS) * qg_ref[...] * (HEAD_DIM ** -0.5 * LOG2E)).astype(BF16)
    k_ref[...] = (dk * lax.rsqrt(kms + EPS) * kg_ref[...]).astype(BF16)
    dvt = dv.T
    ones = jnp.ones((V_ROWS - HEAD_DIM, dvt.shape[1]), F32)
    for hd in range(DSA_HEADS):
        vt_ref[hd] = jnp.concatenate([dvt[hd * HEAD_DIM:(hd + 1) * HEAD_DIM], ones], axis=0).astype(BF16)

    iq_ref[...] = (p[:, SEG_IQ:SEG_IQ + IDX_HEADS * IDX_DIM] * (IDX_DIM ** -0.5)).astype(BF16)
    kw = p[:, SEG_KW:SEG_KW + LANES]
    kw_ref[...] = kw
    ikb_ref[...] = kw.astype(BF16)
    gx_ref[...] = p[:, SEG_G:SEG_G + 4 * GDN_W]
    ab_ref[...] = p[:, SEG_AB:SEG_AB + LANES]
    mq_ref[...] = p[:, SEG_MQ:SEG_MQ + MEM_W]


def _inproj(x, gain, w_packed, grp, qg, kg, *, tm):
    m, d = x.shape
    row = lambda w: pl.BlockSpec((tm, w), lambda i: (i, 0))
    full = lambda a: pl.BlockSpec(a.shape, lambda i: (0,) * a.ndim)
    out_shape = (
        jax.ShapeDtypeStruct((m, DSA_W), BF16),
        jax.ShapeDtypeStruct((m, DSA_W), BF16),
        jax.ShapeDtypeStruct((DSA_HEADS, V_ROWS, m), BF16),
        jax.ShapeDtypeStruct((m, IDX_HEADS * IDX_DIM), BF16),
        jax.ShapeDtypeStruct((m, LANES), BF16),
        jax.ShapeDtypeStruct((m, LANES), F32),
        jax.ShapeDtypeStruct((m, 4 * GDN_W), F32),
        jax.ShapeDtypeStruct((m, LANES), F32),
        jax.ShapeDtypeStruct((m, MEM_W), F32),
    )
    out_specs = (row(DSA_W), row(DSA_W), pl.BlockSpec((DSA_HEADS, V_ROWS, tm), lambda i: (0, 0, i)),
                 row(IDX_HEADS * IDX_DIM), row(LANES), row(LANES), row(4 * GDN_W),
                 row(LANES), row(MEM_W))
    return pl.pallas_call(
        _inproj_kernel,
        grid=(m // tm,),
        in_specs=[row(d), full(gain), full(w_packed), full(grp), full(qg), full(kg)],
        out_specs=out_specs,
        out_shape=out_shape,
        compiler_params=_params("parallel"),
        name="inproj",
    )(x, gain, w_packed, grp, qg, kg)


_DENORMAL_TOP = 0x007FFFFF


def _float_to_key(f):
    bits = lax.bitcast_convert_type(f, jnp.int32)
    mag = jnp.maximum((bits & jnp.int32(0x7FFFFFFF)) - _DENORMAL_TOP, 0)
    return jnp.where(bits >= 0, mag, -mag)


def _key_to_float(key):
    mag = jnp.abs(key)
    bits = jnp.where(mag > 0, mag + _DENORMAL_TOP, 0)
    return lax.bitcast_convert_type(jnp.where(key < 0, bits | jnp.int32(-2 ** 31), bits), F32)


def _upper_normal_quantile(p):
    pp = jnp.clip(jnp.minimum(p, 1.0 - p), 1e-30, 0.5)
    t = jnp.sqrt(-2.0 * jnp.log(pp))
    z = t - (2.515517 + t * (0.802853 + t * 0.010328)) / (1.0 + t * (1.432788 + t * (0.189269 + t * 0.001308)))
    return jnp.where(p <= 0.5, z, -z)


def _tree(op, x, group=8):
    parts = x.reshape(x.shape[0] // group, group, LANES)
    k = parts.shape[0]
    while k > 1:
        k //= 2
        parts = op(parts[:k], parts[k:2 * k])
    return parts[0]


def _dsa_kernel(q_ref, iq_ref, kwq_ref, k_ref, vt_ref, ik_ref, tab_ref, ltri_ref, o_ref,
                s_scr, iqt_scr, qpad_scr, bias_scr, mx_scr, brange_scr, *, kc, top_k):
    j = pl.program_id(1)
    per = kc // Q_BLOCK
    pad = kc - Q_BLOCK
    nch = j // per + 1
    q_pos = j * Q_BLOCK + lax.broadcasted_iota(jnp.int32, (1, LANES), 1)
    row_iota = lax.broadcasted_iota(jnp.int32, (kc, LANES), 0)

    def rows(i):
        return pl.ds(pl.multiple_of(j * Q_BLOCK - i * kc, Q_BLOCK), kc)

    def chunk_pairs(start, body, init, width=2, stop=None):
        count = jnp.maximum((nch if stop is None else stop) - start, 0)
        rest = count % width
        state = lax.fori_loop(0, rest, lambda r, st: body(start + r, st), init)

        def group(t, st):
            for u in range(width):
                st = body(start + rest + width * t + u, st)
            return st
        return lax.fori_loop(0, count // width, group, state)

    @pl.when((pl.program_id(0) == 0) & (j == 0))
    def _():
        r = lax.broadcasted_iota(jnp.int32, (2 * Q_BLOCK, LANES), 0)
        c = lax.broadcasted_iota(jnp.int32, (2 * Q_BLOCK, LANES), 1)
        dist = Q_BLOCK + c - r
        bucket = jnp.zeros_like(dist)
        for first in _BUCKET_STARTS:
            bucket = bucket + jnp.where(dist >= first, 1, 0)
        for h in range(DSA_HEADS):
            far = tab_ref[NUM_BUCKETS - 1, h]
            delta = jnp.zeros((2 * Q_BLOCK, LANES), F32)
            b_hi = jnp.float32(0.0)
            b_lo = jnp.float32(0.0)
            for b in range(NUM_BUCKETS - 1):
                delta = jnp.where(bucket == b, tab_ref[b, h] - far, delta)
                b_hi = jnp.maximum(b_hi, (tab_ref[b, h] - far) * LOG2E)
                b_lo = jnp.minimum(b_lo, (tab_ref[b, h] - far) * LOG2E)
            bias_scr[h] = jnp.where(dist >= 0, delta * LOG2E, 0.0)
            brange_scr[0, h] = b_hi
            brange_scr[1, h] = b_lo

    iqt = iq_ref[...].astype(F32).T
    zpad = jnp.zeros((LANES - IDX_DIM, LANES), F32)
    for h in range(IDX_HEADS):
        iqt_scr[:, h * LANES:(h + 1) * LANES] = jnp.concatenate(
            [iqt[h * IDX_DIM:(h + 1) * IDX_DIM], zpad], axis=0).astype(BF16)
    wt = kwq_ref[...].T[IDX_DIM:IDX_DIM + IDX_HEADS, :] * (IDX_HEADS ** -0.5)
    qt = q_ref[...].astype(F32).T
    half = lax.broadcasted_iota(jnp.int32, (LANES, LANES), 0) < HEAD_DIM
    for h in range(DSA_HEADS):
        pair = qt[(h // 2) * LANES:(h // 2 + 1) * LANES]
        keep = half if h % 2 == 0 else jnp.logical_not(half)
        qpad_scr[:, h * LANES:(h + 1) * LANES] = jnp.where(keep, pair, 0.0).astype(BF16)

    kmax = tab_ref[NUM_BUCKETS, 0]
    spread = jnp.zeros((1, LANES), F32)
    for h in range(DSA_HEADS):
        b_hi = brange_scr[0, h]
        b_lo = brange_scr[1, h]
        qh = qt[h * HEAD_DIM:(h + 1) * HEAD_DIM]
        bound = jnp.sqrt(jnp.sum(qh * qh, axis=0, keepdims=True)) * kmax
        mx_scr[h:h + 1, :] = bound + b_hi
        spread = jnp.maximum(spread, 2.0 * bound + (b_hi - b_lo))
    bound_ok = jnp.max(spread) <= MAX_LOGIT_SPREAD

    def score_chunk(i, carry, edge):
        d = jnp.dot(ik_ref[0, rows(i), :], iqt_scr[...], preferred_element_type=F32)
        acc = jnp.maximum(d[:, :LANES], 0.0) * wt[0:1, :]
        for h in range(1, IDX_HEADS):
            acc = acc + jnp.maximum(d[:, h * LANES:(h + 1) * LANES], 0.0) * wt[h:h + 1, :]
        sc = fin = acc
        if edge:
            key = row_iota + (j * Q_BLOCK - i * kc - pad)
            adm = jnp.where(key >= 0, key, q_pos + 1) <= q_pos
            sc = jnp.where(adm, acc, -jnp.inf)
            fin = jnp.where(adm, acc, 0.0)
        s_scr[rows(i), :] = sc
        tot, sq, top = carry
        return (tot + _tree(jnp.add, fin), sq + _tree(jnp.add, fin * fin),
                jnp.maximum(top, _tree(jnp.maximum, sc)))

    zero8 = jnp.zeros((8, LANES), F32)
    stats = score_chunk(0, (zero8, zero8, jnp.full((8, LANES), -jnp.inf, F32)), True)
    stats = lax.fori_loop(0, jnp.minimum(nch - 1, 1), lambda _, st: score_chunk(nch - 1, st, True), stats)
    tot, sq, top = chunk_pairs(1, lambda i, st: score_chunk(i, st, False), stats, width=4, stop=nch - 1)

    def count(pred):
        def body(i, cnt):
            return cnt + _tree(jnp.add, jnp.where(pred(s_scr[rows(i), :]), 1.0, 0.0))
        cnt = chunk_pairs(0, body, zero8)
        return jnp.sum(cnt, axis=0, keepdims=True)

    def max_below(t):
        def body(i, m):
            s = s_scr[rows(i), :]
            return jnp.maximum(m, _tree(jnp.maximum, jnp.where(s < t, s, -jnp.inf)))
        m = lax.fori_loop(0, nch, body, jnp.full((8, LANES), -jnp.inf, F32))
        return jnp.max(m, axis=0, keepdims=True)

    kf = float(top_k)
    n_adm = (q_pos + 1).astype(F32)
    mean = jnp.sum(tot, axis=0, keepdims=True) / n_adm
    std = jnp.sqrt(jnp.maximum(jnp.sum(sq, axis=0, keepdims=True) / n_adm - mean * mean, 0.0))
    first_guess = mean + _upper_normal_quantile(kf / n_adm) * std
    lowest = jnp.float32(-3.0e38)
    top1 = jnp.minimum(jnp.max(top, axis=0, keepdims=True), -lowest)

    def update(st, key):
        lo, hi, c_lo, c_hi, lo_set, hi_set, done, run = st
        key = jnp.clip(key, lo + 1, hi - 1)
        t = _key_to_float(key)
        c = count(lambda s: s >= t)
        active = done < 0.5
        up = active & (c >= kf)
        dn = active & (c < kf)
        lo, c_lo, lo_set = jnp.where(up, key, lo), jnp.where(up, c, c_lo), jnp.where(up, 1.0, lo_set)
        hi, c_hi, hi_set = jnp.where(dn, key, hi), jnp.where(dn, c, c_hi), jnp.where(dn, 1.0, hi_set)
        done = jnp.where((c_lo == kf) | (hi <= lo + 1), 1.0, done)
        return lo, hi, c_lo, c_hi, lo_set, hi_set, done, run

    def guess(st, midpoint):
        lo, hi, c_lo, c_hi, lo_set, hi_set, done, run = st
        v_lo, v_hi = _key_to_float(lo), _key_to_float(hi)
        both = (lo_set > 0.5) & (hi_set > 0.5)
        step = std * 0.25 * jnp.exp2(run)
        if midpoint:
            inner = 0.5 * v_lo + 0.5 * v_hi
        else:
            l_lo, l_hi = jnp.log(c_lo), jnp.log(jnp.maximum(c_hi, 0.5))
            frac = jnp.clip((l_lo - math.log(kf - 0.5)) / jnp.maximum(l_lo - l_hi, 1e-6), 0.0, 1.0)
            inner = v_lo + (v_hi - v_lo) * frac
        t = jnp.where(both, inner,
                      jnp.where(hi_set > 0.5, v_hi - step, jnp.where(lo_set > 0.5, v_lo + step, first_guess)))
        t = jnp.where(t != t, 0.0, t)
        run = jnp.where(both, 0.0, run + 1.0)
        return _float_to_key(t), (lo, hi, c_lo, c_hi, lo_set, hi_set, done, run)

    def pending(st):
        return 1.0 - jnp.min(st[6])

    flag0 = jnp.zeros((1, LANES), F32)
    st = (jnp.full((1, LANES), _float_to_key(lowest), jnp.int32), _float_to_key(top1) + 1,
          n_adm, flag0, flag0, flag0, jnp.where(n_adm <= kf, 1.0, 0.0), flag0)

    def probes(st):
        for r in range(PROBES_PER_ROUND):
            key, st = guess(st, midpoint=((r + 1) % MIDPOINT_EVERY == 0))
            st = update(st, key)
        return st

    st = probes(probes(st))

    def fast_round(carry):
        st = probes(carry[2])
        return carry[0] + 1, pending(st), st

    _, _, st = lax.while_loop(lambda c: (c[0] < FAST_ROUNDS) & (c[1] > 0.5), fast_round,
                              (jnp.int32(2), pending(st), st))

    def safe_round(carry):
        p, _, st = carry
        st = update(st, st[0] + lax.shift_right_logical(st[1] - st[0], 1))
        lo, hi, done = st[0], st[1], st[6]
        below = _float_to_key(max_below(_key_to_float(hi)))
        hi = jnp.where(done < 0.5, jnp.clip(below + 1, lo + 1, hi), hi)
        done = jnp.where(hi <= lo + 1, 1.0, done)
        st = update((lo, hi) + st[2:6] + (done, st[7]), hi - 1)
        return p + 1, pending(st), st

    _, _, st = lax.while_loop(lambda c: (c[1] > 0.5) & (c[0] < 34), safe_round,
                              (jnp.int32(0), pending(st), st))

    thr = _key_to_float(st[0])
    has_ties = jnp.max(st[2]) > kf

    @pl.when(jnp.logical_not(has_ties))
    def _():
        def mask_chunk(i, carry):
            s_scr[rows(i), :] = jnp.where(s_scr[rows(i), :] >= thr, 0.0, NEG)
            return carry
        lax.fori_loop(0, nch, mask_chunk, 0)

    @pl.when(has_ties)
    def _():
        need = kf - count(lambda s: s > thr)

        def mask_chunk(t, run):
            i = nch - 1 - t
            s = s_scr[rows(i), :]
            tie = jnp.where(s == thr, 1.0, 0.0)
            pref = jnp.dot(ltri_ref[...], tie.astype(BF16), preferred_element_type=F32) + run
            tie_sel = jnp.where(pref <= need, tie, 0.0)
            sel = jnp.where(s > thr, 1.0, tie_sel)
            s_scr[rows(i), :] = jnp.where(sel > 0.5, 0.0, NEG)
            return run + jnp.sum(tie, axis=0, keepdims=True)
        lax.fori_loop(0, nch, mask_chunk, jnp.zeros((1, LANES), F32))

    def logits(i, with_bias):
        msk = s_scr[rows(i), :]
        out = []
        for pr in range(DSA_HEADS // 2):
            l2 = jnp.dot(k_ref[0, rows(i), pr * LANES:(pr + 1) * LANES],
                         qpad_scr[:, 2 * pr * LANES:(2 * pr + 2) * LANES],
                         preferred_element_type=F32)
            for e in range(2):
                l = l2[:, e * LANES:(e + 1) * LANES] + msk
                if with_bias:
                    far_rows = kc - 2 * Q_BLOCK
                    l = jnp.concatenate([l[:far_rows], l[far_rows:] + bias_scr[2 * pr + e]], axis=0)
                out.append(l)
        return out

    def max_step(i, ms, with_bias):
        return tuple(jnp.maximum(m, _tree(jnp.maximum, l)) for m, l in zip(ms, logits(i, with_bias)))

    @pl.when(jnp.logical_not(bound_ok))
    def _():
        ms = max_step(0, tuple(jnp.full((8, LANES), NEG, F32) for _ in range(DSA_HEADS)), True)
        ms = lax.fori_loop(1, nch, lambda i, m: max_step(i, m, False), ms)
        for h in range(DSA_HEADS):
            mx_scr[h:h + 1, :] = jnp.max(ms[h], axis=0, keepdims=True)

    mx = [mx_scr[h:h + 1, :] for h in range(DSA_HEADS)]

    def pv_step(i, accs, with_bias):
        return tuple(
            acc + jnp.dot(vt_ref[h, :, rows(i)], jnp.exp2(l - mx[h]).astype(BF16),
                          preferred_element_type=F32)
            for h, (acc, l) in enumerate(zip(accs, logits(i, with_bias))))

    accs = pv_step(0, tuple(jnp.zeros((V_ROWS, LANES), F32) for _ in range(DSA_HEADS)), True)
    accs = chunk_pairs(1, lambda i, a: pv_step(i, a, False), accs, width=4)
    outs = [acc[:HEAD_DIM] / acc[HEAD_DIM:HEAD_DIM + 1] for acc in accs]
    o_ref[...] = jnp.concatenate(outs, axis=0).T


def _dsa(q, iq, kw, k, vt, ikb, rel_bias, k_gain, ltri, *, batch, seq, kc):
    nb = seq // Q_BLOCK
    top_k = min(TOPK_MAX, seq // 4)
    pad = kc - Q_BLOCK
    seqp = seq + pad
    kp = jnp.pad(k.reshape(batch, seq, DSA_W), ((0, 0), (pad, 0), (0, 0)))
    ikp = jnp.pad(ikb.reshape(batch, seq, LANES), ((0, 0), (pad, 0), (0, 0)))
    vtp = jnp.pad(vt.reshape(DSA_HEADS, V_ROWS, batch, seq),
                  ((0, 0), (0, 0), (0, 0), (pad, 0))).reshape(DSA_HEADS, V_ROWS, batch * seqp)
    kmax = (HEAD_DIM ** 0.5 * 1.01) * jnp.max(jnp.abs(k_gain.astype(F32)))
    table = jnp.concatenate([rel_bias.astype(F32), jnp.full((1, DSA_HEADS), kmax, F32)], axis=0)
    qrow = lambda w: pl.BlockSpec((Q_BLOCK, w), lambda b, j: (b * nb + j, 0))
    return pl.pallas_call(
        functools.partial(_dsa_kernel, kc=kc, top_k=top_k),
        grid=(batch, nb),
        in_specs=[
            qrow(DSA_W), qrow(IDX_HEADS * IDX_DIM), qrow(LANES),
            pl.BlockSpec((1, seqp, DSA_W), lambda b, j: (b, 0, 0)),
            pl.BlockSpec((DSA_HEADS, V_ROWS, seqp), lambda b, j: (0, 0, b)),
            pl.BlockSpec((1, seqp, LANES), lambda b, j: (b, 0, 0)),
            pl.BlockSpec(memory_space=pltpu.SMEM),
            pl.BlockSpec(ltri.shape, lambda b, j: (0, 0)),
        ],
        out_specs=qrow(DSA_W),
        out_shape=jax.ShapeDtypeStruct((batch * seq, DSA_W), F32),
        scratch_shapes=[
            pltpu.VMEM((seqp, LANES), F32),
            pltpu.VMEM((LANES, IDX_HEADS * LANES), BF16),
            pltpu.VMEM((LANES, DSA_HEADS * LANES), BF16),
            pltpu.VMEM((DSA_HEADS, 2 * Q_BLOCK, LANES), F32),
            pltpu.VMEM((8, LANES), F32),
            pltpu.SMEM((2, DSA_HEADS), F32),
        ],
        compiler_params=_params("arbitrary", "arbitrary"),
        name="dsa",
    )(q, iq, kw, kp, vtp, ikp, table, ltri)


GDN_GROUP = 2


def _gdn_kernel(x_ref, ab_ref, cw_ref, ea_ref, eb_ref, alog_ref, dtb_ref, gain_ref, grp_ref,
                o_ref, xpad_scr, qkv_scr, gb_scr, gc_scr, gr_scr, pre_scr, st_scr, o_scr, *, tb):
    t = pl.program_id(1)
    cin = 3 * GDN_W
    n_chunks = tb // GDN_CHUNK
    pairs = range(GDN_HEADS // 2)
    lanes = lambda p: slice(p * LANES, (p + 1) * LANES)

    @pl.when(t == 0)
    def _():
        xpad_scr[0:8, :] = jnp.zeros((8, cin), F32)
        st_scr[...] = jnp.zeros_like(st_scr)

    x = x_ref[...]
    xpad_scr[8:8 + tb, :] = x[:, :cin]
    conv = jnp.zeros((tb, cin), F32)
    for jj in range(CONV_WIDTH):
        conv = conv + cw_ref[jj:jj + 1, :] * xpad_scr[pl.ds(8 - (CONV_WIDTH - 1) + jj, tb), :]
    xpad_scr[0:8, :] = x[tb - 8:tb, :cin]
    qkv = conv * _sigmoid(conv)
    grp = grp_ref[...]
    q = qkv[:, :GDN_W]
    k = qkv[:, GDN_W:2 * GDN_W]
    qkv_scr[:, :GDN_W] = q * lax.rsqrt(_group_sum(q * q, grp) + EPS) * (HEAD_DIM ** -0.5)
    qkv_scr[:, GDN_W:2 * GDN_W] = k * lax.rsqrt(_group_sum(k * k, grp) + EPS)
    qkv_scr[:, 2 * GDN_W:] = qkv[:, 2 * GDN_W:]

    ab = ab_ref[...]
    a_e = _dot_sel(ab, ea_ref[...], 3) + dtb_ref[...]
    b_e = _dot_sel(ab, eb_ref[...], 3)
    softplus = jnp.maximum(a_e, 0.0) + jnp.log(1.0 + jnp.exp(-jnp.abs(a_e)))
    gb_scr[:, :GDN_W] = -jnp.exp(alog_ref[...]) * softplus
    gb_scr[:, GDN_W:] = _sigmoid(b_e)

    r64 = lax.broadcasted_iota(jnp.int32, (GDN_CHUNK, LANES), 0)
    c64 = lax.broadcasted_iota(jnp.int32, (GDN_CHUNK, LANES), 1) % GDN_CHUNK
    causal = c64 <= r64
    strict = c64 < r64
    eye = jnp.where(c64 == r64, 1.0, 0.0)
    r128 = lax.broadcasted_iota(jnp.int32, (LANES, LANES), 0)
    c128 = lax.broadcasted_iota(jnp.int32, (LANES, LANES), 1)
    bdmask = (r128 // HEAD_DIM) == (c128 // HEAD_DIM)

    def bd(m):
        return jnp.where(bdmask, jnp.concatenate([m, m], axis=0), 0.0)

    lt_r = lax.broadcasted_iota(jnp.int32, (GDN_CHUNK, GDN_CHUNK), 0)
    lt_c = lax.broadcasted_iota(jnp.int32, (GDN_CHUNK, GDN_CHUNK), 1)
    ltri = jnp.where(lt_c <= lt_r, 1.0, 0.0)
    ones8 = jnp.ones((8, GDN_CHUNK), F32)
    up_r = lax.broadcasted_iota(jnp.int32, (GDN_CHUNK, GDN_W), 0)
    up_c = lax.broadcasted_iota(jnp.int32, (GDN_CHUNK, GDN_W), 1) % GDN_CHUNK
    upper = jnp.where(up_r <= up_c, 1.0, 0.0)
    for ci in range(n_chunks):
        g = gb_scr[ci * GDN_CHUNK:(ci + 1) * GDN_CHUNK, :GDN_W]
        gc_scr[ci * GDN_CHUNK:(ci + 1) * GDN_CHUNK, :] = _sel_dot(ltri, g, 3)
        gr_scr[ci * 8:(ci + 1) * 8, :] = _sel_dot(ones8, g * upper, 3)

    def par_body(gi, carry):
        inst = [(u, p) for u in range(GDN_GROUP) for p in pairs]
        rows = [pl.ds(pl.multiple_of((gi * GDN_GROUP + u) * GDN_CHUNK, GDN_CHUNK), GDN_CHUNK)
                for u in range(GDN_GROUP)]
        grow = [pl.ds(pl.multiple_of((gi * GDN_GROUP + u) * 8, 8), 8) for u in range(GDN_GROUP)]
        off = lambda s, p: slice(s * GDN_W + p * LANES, s * GDN_W + (p + 1) * LANES)
        qp = [qkv_scr[rows[u], off(0, p)] for u, p in inst]
        kp = [qkv_scr[rows[u], off(1, p)] for u, p in inst]
        vp = [qkv_scr[rows[u], off(2, p)] for u, p in inst]
        beta = [gb_scr[rows[u], off(1, p)] for u, p in inst]
        gc = [gc_scr[rows[u], lanes(p)] for u, p in inst]
        gr = [gr_scr[grow[u], lanes(p)][0:1, :] for u, p in inst]
        dmat = [jnp.exp(jnp.where(causal, a - b, NEG)) for a, b in zip(gc, gr)]
        kbd = [jnp.where(bdmask, jnp.concatenate([a, a], axis=0).T, 0.0) for a in kp]
        kb = [a * b for a, b in zip(kp, beta)]
        kk = [_bdot(a, b) for a, b in zip(kb, kbd)]
        qk = [_bdot(a, b) for a, b in zip(qp, kbd)]
        lm = [jnp.where(strict, a * d, 0.0) for a, d in zip(kk, dmat)]
        aintra = [a * d for a, d in zip(qk, dmat)]
        tinv = [eye - a for a in lm]
        lpow = lm
        for _ in range(5):
            lpow = [_dot3(a, bd(a)) for a in lpow]
            tinv = [a + _dot3(a, bd(b)) for a, b in zip(tinv, lpow)]
        egc = [jnp.exp(a) for a in gc]
        un = [_dot3(a, bd(v * b)) for a, v, b in zip(tinv, vp, beta)]
        wn = [_dot3(a, bd(b * e)) for a, b, e in zip(tinv, kb, egc)]
        for n, (u, p) in enumerate(inst):
            pre_scr[0, rows[u], lanes(p)] = un[n]
            pre_scr[1, rows[u], lanes(p)] = wn[n]
            pre_scr[2, rows[u], lanes(p)] = aintra[n]
            pre_scr[3, rows[u], lanes(p)] = qp[n] * egc[n]
            pre_scr[4, rows[u], lanes(p)] = kp[n] * jnp.exp(gc[n][GDN_CHUNK - 1:GDN_CHUNK, :] - gc[n])
        return carry

    lax.fori_loop(0, n_chunks // GDN_GROUP, par_body, 0)

    def rec_body(ci, carry):
        rows = pl.ds(pl.multiple_of(ci * GDN_CHUNK, GDN_CHUNK), GDN_CHUNK)
        tail = gc_scr[pl.ds(pl.multiple_of(ci * GDN_CHUNK + GDN_CHUNK - 8, 8), 8), :]
        egl = jnp.exp(tail[7:8, :])
        sbd = [st_scr[p] for p in pairs]
        un, wn, aintra, qe, kd = [[pre_scr[s, rows, lanes(p)] for p in pairs] for s in range(5)]
        ws = [_bdot(a, s) for a, s in zip(wn, sbd)]
        qs = [_bdot(a, s) for a, s in zip(qe, sbd)]
        v_new = [a - b for a, b in zip(un, ws)]
        o = [a + _bdot(b, bd(v)) for a, b, v in zip(qs, aintra, v_new)]
        upd = [_bdot(a.T, v) for a, v in zip(kd, v_new)]
        for p in pairs:
            st_scr[p] = sbd[p] * egl[:, lanes(p)] + jnp.where(bdmask, upd[p], 0.0)
            o_scr[rows, lanes(p)] = o[p]
        return carry

    lax.fori_loop(0, n_chunks, rec_body, 0)

    o = o_scr[...]
    oms = _group_sum(o * o, grp) * (1.0 / HEAD_DIM)
    z = x[:, cin:]
    o_ref[...] = o * lax.rsqrt(oms + EPS) * gain_ref[...] * (z * _sigmoid(z))


def _gdn(gx, ab, conv_w, ea, eb, alog, dtb, gain, grp, *, batch, seq, tb):
    nt = seq // tb
    cin = 3 * GDN_W
    row = lambda w: pl.BlockSpec((tb, w), lambda b, t: (b * nt + t, 0))
    full = lambda a: pl.BlockSpec(a.shape, lambda b, t: (0,) * a.ndim)
    return pl.pallas_call(
        functools.partial(_gdn_kernel, tb=tb),
        grid=(batch, nt),
        in_specs=[row(4 * GDN_W), row(LANES), full(conv_w), full(ea), full(eb), full(alog),
                  full(dtb), full(gain), full(grp)],
        out_specs=row(GDN_W),
        out_shape=jax.ShapeDtypeStruct((batch * seq, GDN_W), F32),
        scratch_shapes=[
            pltpu.VMEM((tb + 8, cin), F32),
            pltpu.VMEM((tb, cin), F32),
            pltpu.VMEM((tb, 2 * GDN_W), F32),
            pltpu.VMEM((tb, GDN_W), F32),
            pltpu.VMEM((tb // 8, GDN_W), F32),
            pltpu.VMEM((5, tb, GDN_W), F32),
            pltpu.VMEM((GDN_HEADS // 2, LANES, LANES), F32),
            pltpu.VMEM((tb, GDN_W), F32),
        ],
        compiler_params=_params("arbitrary", "arbitrary"),
        name="gdn",
    )(gx, ab, conv_w, ea, eb, alog, dtb, gain, grp)


def _memkv_kernel(mem_ref, g_ref, w_ref, kg_ref, grp_ref, kt_ref, vp_ref):
    x = mem_ref[0]
    ms = jnp.mean(x * x, axis=-1, keepdims=True)
    h = (x * lax.rsqrt(ms + EPS) * g_ref[...]).astype(BF16)
    kv = jnp.dot(h, w_ref[...], preferred_element_type=F32)
    km = kv[:, :MEM_W]
    vm = kv[:, MEM_W:]
    kms = _group_sum(km * km, grp_ref[...]) * (1.0 / HEAD_DIM)
    kt = (km * lax.rsqrt(kms + EPS) * kg_ref[...]).T
    n_mem = x.shape[0]
    top = lax.broadcasted_iota(jnp.int32, (LANES, n_mem), 0) < HEAD_DIM
    left = lax.broadcasted_iota(jnp.int32, (n_mem, LANES), 1) < HEAD_DIM
    for hh in range(MEM_HEADS):
        pr = slice((hh // 2) * LANES, (hh // 2 + 1) * LANES)
        keep_r = top if hh % 2 == 0 else jnp.logical_not(top)
        keep_c = left if hh % 2 == 0 else jnp.logical_not(left)
        kt_ref[0, hh] = jnp.where(keep_r, kt[pr, :], 0.0).astype(BF16)
        vp_ref[0, hh] = jnp.where(keep_c, vm[:, pr], 0.0).astype(BF16)


def _memkv(mem, gain, w, kg, grp):
    b, n_mem, d = mem.shape
    full = lambda a: pl.BlockSpec(a.shape, lambda i: (0,) * a.ndim)
    return pl.pallas_call(
        _memkv_kernel,
        grid=(b,),
        in_specs=[pl.BlockSpec((1, n_mem, d), lambda i: (i, 0, 0)), full(gain), full(w), full(kg),
                  full(grp)],
        out_specs=(pl.BlockSpec((1, MEM_HEADS, LANES, n_mem), lambda i: (i, 0, 0, 0)),
                   pl.BlockSpec((1, MEM_HEADS, n_mem, LANES), lambda i: (i, 0, 0, 0))),
        out_shape=(jax.ShapeDtypeStruct((b, MEM_HEADS, LANES, n_mem), BF16),
                   jax.ShapeDtypeStruct((b, MEM_HEADS, n_mem, LANES), BF16)),
        compiler_params=_params("parallel"),
        name="memkv",
    )(mem, gain, w, kg, grp)


def _memattn_kernel(q_ref, qg_ref, grp_ref, kt_ref, vp_ref, o_ref):
    q = q_ref[...]
    qms = _group_sum(q * q, grp_ref[...]) * (1.0 / HEAD_DIM)
    qn = (q * lax.rsqrt(qms + EPS) * qg_ref[...] * (HEAD_DIM ** -0.5)).astype(BF16)
    outs = []
    for pr in range(MEM_HEADS // 2):
        qp = qn[:, pr * LANES:(pr + 1) * LANES]
        acc = None
        for e in range(2):
            hh = 2 * pr + e
            l = jnp.dot(qp, kt_ref[0, hh], preferred_element_type=F32)
            l = l - jnp.max(l, axis=-1, keepdims=True)
            p = jnp.exp(l)
            p = p / jnp.sum(p, axis=-1, keepdims=True)
            o = jnp.dot(p.astype(BF16), vp_ref[0, hh], preferred_element_type=F32)
            acc = o if acc is None else acc + o
        outs.append(acc)
    o_ref[...] = jnp.concatenate(outs, axis=-1)


def _memattn(mq, qg, grp, kt, vp, *, batch, seq, tm):
    nt = seq // tm
    n_mem = kt.shape[-1]
    full = lambda a: pl.BlockSpec(a.shape, lambda b, t: (0,) * a.ndim)
    return pl.pallas_call(
        _memattn_kernel,
        grid=(batch, nt),
        in_specs=[
            pl.BlockSpec((tm, MEM_W), lambda b, t: (b * nt + t, 0)), full(qg), full(grp),
            pl.BlockSpec((1, MEM_HEADS, LANES, n_mem), lambda b, t: (b, 0, 0, 0)),
            pl.BlockSpec((1, MEM_HEADS, n_mem, LANES), lambda b, t: (b, 0, 0, 0)),
        ],
        out_specs=pl.BlockSpec((tm, MEM_W), lambda b, t: (b * nt + t, 0)),
        out_shape=jax.ShapeDtypeStruct((batch * seq, MEM_W), F32),
        compiler_params=_params("parallel", "parallel"),
        name="memattn",
    )(mq, qg, grp, kt, vp)


def _group_ones(width):
    idx = np.arange(width) // HEAD_DIM
    return jnp.asarray((idx[:, None] == idx[None, :]).astype(np.float32))


def _t5_bucket_np(n):
    max_exact = NUM_BUCKETS // 2
    nf = np.maximum(n, 1).astype(np.float32)
    large = max_exact + (np.log(nf / np.float32(max_exact)) / np.float32(math.log(MAX_DISTANCE / max_exact))
                         * (NUM_BUCKETS - max_exact)).astype(np.int32)
    large = np.minimum(large, NUM_BUCKETS - 1)
    return np.where(n < max_exact, n, large)


def _bucket_starts():
    dist = np.arange(2 * Q_BLOCK)
    bucket = _t5_bucket_np(dist)
    assert (np.diff(bucket) >= 0).all() and bucket[-1] == NUM_BUCKETS - 1
    return tuple(int(dist[bucket >= b].min()) for b in range(1, NUM_BUCKETS))


_BUCKET_STARTS = _bucket_starts()


def _cast_kernel(w_ref, o_ref):
    o_ref[...] = w_ref[...].astype(BF16)


def _to_bf16(w, *, rows):
    nl, nr, nc = w.shape
    spec = pl.BlockSpec((1, rows, nc), lambda l, r: (l, r, 0))
    return pl.pallas_call(
        _cast_kernel,
        grid=(nl, nr // rows),
        in_specs=[spec],
        out_specs=spec,
        out_shape=jax.ShapeDtypeStruct(w.shape, BF16),
        compiler_params=_params("parallel", "parallel"),
        name="cast",
    )(w)


def _pack_w_in(w):
    o = 3 * DSA_W
    iq = w[:, o:o + IDX_HEADS * IDX_DIM]
    o += IDX_HEADS * IDX_DIM
    kw = w[:, o:o + IDX_DIM + IDX_HEADS]
    o += IDX_DIM + IDX_HEADS
    g = w[:, o:o + 4 * GDN_W]
    o += 4 * GDN_W
    ab = w[:, o:o + 2 * GDN_HEADS]
    o += 2 * GDN_HEADS
    mq = w[:, o:o + MEM_W]
    pad = lambda a: jnp.pad(a, ((0, 0), (0, LANES - a.shape[1])))
    return jnp.concatenate([w[:, :3 * DSA_W], iq, pad(kw), g, pad(ab), mq], axis=1)


def _tile_heads(v, heads):
    return jnp.tile(v.astype(F32), heads).reshape(1, heads * HEAD_DIM)


def kernel(x, mem, ffn1_norm, ffn1_w_gate, ffn1_w_up, ffn1_w_down, mix_norm, w_in, dsa_q_norm, dsa_k_norm, rel_bias, gdn_conv, gdn_A_log, gdn_dt_bias, gdn_out_norm, mem_norm, w_mem_kv, mem_q_norm, mem_k_norm, w_out, ffn2_norm, ffn2_w_gate, ffn2_w_up, ffn2_w_down):
    batch, seq, d = x.shape
    depth = w_in.shape[0]
    m = batch * seq
    tm = min(512, seq)
    kc = min(512, seq)
    dff = ffn1_w_gate.shape[-1]
    tf = dff // 2 if (dff // 2) % LANES == 0 else dff

    grp_a = _group_ones(DSA_W)
    grp_m = _group_ones(MEM_W)
    ltri = jnp.asarray(np.tril(np.ones((kc, kc), np.float32))).astype(BF16)
    heads_of = np.arange(GDN_W) // HEAD_DIM
    ea = jnp.asarray((np.arange(LANES)[:, None] == heads_of[None, :]).astype(np.float32))
    eb = jnp.asarray((np.arange(LANES)[:, None] == heads_of[None, :] + GDN_HEADS).astype(np.float32))

    w_in_b = _to_bf16(w_in, rows=min(256, d))
    xf = x.reshape(m, d)
    for l in range(depth):
        xf = _ffn(xf, ffn1_norm[l], ffn1_w_gate[l].astype(BF16), ffn1_w_up[l].astype(BF16),
                  ffn1_w_down[l].astype(BF16), tm=tm, tf=tf)
        (q, k, vt, iq, ikb, kw, gx, ab, mq) = _inproj(
            xf, mix_norm[l].reshape(1, d), _pack_w_in(w_in_b[l]), grp_a,
            _tile_heads(dsa_q_norm[l], DSA_HEADS), _tile_heads(dsa_k_norm[l], DSA_HEADS), tm=tm)
        out_a = _dsa(q, iq, kw, k, vt, ikb, rel_bias, dsa_k_norm[l], ltri, batch=batch, seq=seq, kc=kc)
        out_b = _gdn(gx, ab, gdn_conv[l], ea, eb,
                     jnp.repeat(gdn_A_log[l].astype(F32), HEAD_DIM).reshape(1, GDN_W),
                     jnp.repeat(gdn_dt_bias[l].astype(F32), HEAD_DIM).reshape(1, GDN_W),
                     _tile_heads(gdn_out_norm[l], GDN_HEADS), grp_a, batch=batch, seq=seq, tb=tm)
        kt, vp = _memkv(mem, mem_norm[l].reshape(1, d), w_mem_kv[l].astype(BF16),
                        _tile_heads(mem_k_norm[l], MEM_HEADS), grp_m)
        out_c = _memattn(mq, _tile_heads(mem_q_norm[l], MEM_HEADS), grp_m, kt, vp,
                         batch=batch, seq=seq, tm=tm)
        wo = w_out[l].astype(BF16)
        xf = _ffn(xf, ffn2_norm[l], ffn2_w_gate[l].astype(BF16), ffn2_w_up[l].astype(BF16),
                  ffn2_w_down[l].astype(BF16), mix=(out_a, out_b, out_c),
                  mix_w=(wo[:DSA_W], wo[DSA_W:DSA_W + GDN_W], wo[DSA_W + GDN_W:]), tm=tm, tf=tf)
    return xf.reshape(batch, seq, d)
```

```python
import functools
import math

import jax
import jax.numpy as jnp
import numpy as np
from jax import lax
from jax.experimental import pallas as pl
from jax.experimental.pallas import tpu as pltpu

F32 = jnp.float32
BF16 = jnp.bfloat16

HEAD_DIM = 64
DSA_HEADS = 6
GDN_HEADS = 6
MEM_HEADS = 4
DSA_W = DSA_HEADS * HEAD_DIM
GDN_W = GDN_HEADS * HEAD_DIM
MEM_W = MEM_HEADS * HEAD_DIM
IDX_HEADS = 8
IDX_DIM = 32
TOPK_MAX = 256
Q_BLOCK = 128
GDN_CHUNK = 64
CONV_WIDTH = 4
NUM_BUCKETS = 32
MAX_DISTANCE = 128
EPS = 1e-6

LANES = 128
VMEM_LIMIT = 52 * 1024 * 1024
NEG = -1e30
LOG2E = math.log2(math.e)
V_ROWS = HEAD_DIM + 16
MAX_LOGIT_SPREAD = 80.0
PROBES_PER_ROUND = 4
MIDPOINT_EVERY = 4
FAST_ROUNDS = 4

SEG_A = 0
SEG_IQ = SEG_A + 3 * DSA_W
SEG_KW = SEG_IQ + IDX_HEADS * IDX_DIM
SEG_G = SEG_KW + LANES
SEG_AB = SEG_G + 4 * GDN_W
SEG_MQ = SEG_AB + LANES
IN_PACKED = SEG_MQ + MEM_W


def _bdot(a, b):
    return jnp.dot(a.astype(BF16), b.astype(BF16), preferred_element_type=F32)


def _split(x, terms):
    out = []
    for _ in range(terms - 1):
        hi = x.astype(BF16)
        out.append(hi)
        x = x - hi.astype(F32)
    out.append(x.astype(BF16))
    return out


def _dot3(a, b):
    ah, al = _split(a, 2)
    bh, bl = _split(b, 2)
    mm = lambda x, y: jnp.dot(x, y, preferred_element_type=F32)
    return mm(ah, bh) + (mm(ah, bl) + mm(al, bh))


def _dot_sel(a, sel, terms):
    selb = sel.astype(BF16)
    acc = None
    for piece in _split(a, terms):
        d = jnp.dot(piece, selb, preferred_element_type=F32)
        acc = d if acc is None else acc + d
    return acc


def _sel_dot(sel, b, terms):
    selb = sel.astype(BF16)
    acc = None
    for piece in _split(b, terms):
        d = jnp.dot(selb, piece, preferred_element_type=F32)
        acc = d if acc is None else acc + d
    return acc


def _group_sum(x, grp):
    return _dot_sel(x, grp, 2)


def _sigmoid(x):
    return 1.0 / (1.0 + jnp.exp(-x))


def _params(*sem):
    return pltpu.CompilerParams(dimension_semantics=sem, vmem_limit_bytes=VMEM_LIMIT)


def _ffn_kernel(*refs, n_mix):
    x_ref, g_ref, wg_ref, wu_ref, wd_ref = refs[:5]
    mix_refs = refs[5:5 + 2 * n_mix]
    o_ref, x_scr, h_scr, acc_scr = refs[5 + 2 * n_mix:]
    f = pl.program_id(1)

    @pl.when(f == 0)
    def _():
        x = x_ref[...]
        for k in range(n_mix):
            x = x + _bdot(mix_refs[k][...], mix_refs[n_mix + k][...])
        x_scr[...] = x
        ms = jnp.mean(x * x, axis=-1, keepdims=True)
        h_scr[...] = (x * lax.rsqrt(ms + EPS) * g_ref[...]).astype(BF16)
        acc_scr[...] = jnp.zeros_like(acc_scr)

    h = h_scr[...]
    a = jnp.dot(h, wg_ref[...], preferred_element_type=F32)
    u = jnp.dot(h, wu_ref[...], preferred_element_type=F32)
    z = (a * _sigmoid(a)) * u
    acc_scr[...] += jnp.dot(z.astype(BF16), wd_ref[...], preferred_element_type=F32)

    @pl.when(f == pl.num_programs(1) - 1)
    def _():
        o_ref[...] = x_scr[...] + 0.5 * acc_scr[...]


def _ffn(x, gain, wg, wu, wd, mix=(), mix_w=(), *, tm, tf):
    m, d = x.shape
    dff = wg.shape[1]
    row = lambda w: pl.BlockSpec((tm, w), lambda i, f: (i, 0))
    full = lambda a: pl.BlockSpec(a.shape, lambda i, f: (0,) * a.ndim)
    return pl.pallas_call(
        functools.partial(_ffn_kernel, n_mix=len(mix)),
        grid=(m // tm, dff // tf),
        in_specs=[
            row(d),
            pl.BlockSpec((1, d), lambda i, f: (0, 0)),
            pl.BlockSpec((d, tf), lambda i, f: (0, f)),
            pl.BlockSpec((d, tf), lambda i, f: (0, f)),
            pl.BlockSpec((tf, d), lambda i, f: (f, 0)),
        ] + [row(a.shape[1]) for a in mix] + [full(w) for w in mix_w],
        out_specs=row(d),
        out_shape=jax.ShapeDtypeStruct((m, d), F32),
        scratch_shapes=[pltpu.VMEM((tm, d), F32), pltpu.VMEM((tm, d), BF16), pltpu.VMEM((tm, d), F32)],
        compiler_params=_params("parallel", "arbitrary"),
        name="ffn",
    )(x, gain.reshape(1, d), wg, wu, wd, *mix, *mix_w)


def _inproj_kernel(x_ref, g_ref, w_ref, grp_ref, qg_ref, kg_ref,
                   q_ref, k_ref, vt_ref, iq_ref, ikb_ref, kw_ref, gx_ref, ab_ref, mq_ref):
    x = x_ref[...]
    ms = jnp.mean(x * x, axis=-1, keepdims=True)
    h = (x * lax.rsqrt(ms + EPS) * g_ref[...]).astype(BF16)
    p = jnp.dot(h, w_ref[...], preferred_element_type=F32)

    grp = grp_ref[...]
    dq = p[:, SEG_A:SEG_A + DSA_W]
    dk = p[:, SEG_A + DSA_W:SEG_A + 2 * DSA_W]
    dv = p[:, SEG_A + 2 * DSA_W:SEG_A + 3 * DSA_W]
    qms = _group_sum(dq * dq, grp) * (1.0 / HEAD_DIM)
    kms = _group_sum(dk * dk, grp) * (1.0 / HEAD_DIM)
    q_ref[...] = (dq * lax.rsqrt(qms + EPS) * qg_ref[...] * (HEAD_DIM ** -0.5 * LOG2E)).astype(BF16)
    k_ref[...] = (dk * lax.rsqrt(kms + EPS) * kg_ref[...]).astype(BF16)
    dvt = dv.T
    ones = jnp.ones((V_ROWS - HEAD_DIM, dvt.shape[1]), F32)
    for hd in range(DSA_HEADS):
        vt_ref[hd] = jnp.concatenate([dvt[hd * HEAD_DIM:(hd + 1) * HEAD_DIM], ones], axis=0).astype(BF16)

    iq_ref[...] = (p[:, SEG_IQ:SEG_IQ + IDX_HEADS * IDX_DIM] * (IDX_DIM ** -0.5)).astype(BF16)
    kw = p[:, SEG_KW:SEG_KW + LANES]
    kw_ref[...] = kw
    ikb_ref[...] = kw.astype(BF16)
    gx_ref[...] = p[:, SEG_G:SEG_G + 4 * GDN_W]
    ab_ref[...] = p[:, SEG_AB:SEG_AB + LANES]
    mq_ref[...] = p[:, SEG_MQ:SEG_MQ + MEM_W]


def _inproj(x, gain, w_packed, grp, qg, kg, *, tm):
    m, d = x.shape
    row = lambda w: pl.BlockSpec((tm, w), lambda i: (i, 0))
    full = lambda a: pl.BlockSpec(a.shape, lambda i: (0,) * a.ndim)
    out_shape = (
        jax.ShapeDtypeStruct((m, DSA_W), BF16),
        jax.ShapeDtypeStruct((m, DSA_W), BF16),
        jax.ShapeDtypeStruct((DSA_HEADS, V_ROWS, m), BF16),
        jax.ShapeDtypeStruct((m, IDX_HEADS * IDX_DIM), BF16),
        jax.ShapeDtypeStruct((m, LANES), BF16),
        jax.ShapeDtypeStruct((m, LANES), F32),
        jax.ShapeDtypeStruct((m, 4 * GDN_W), F32),
        jax.ShapeDtypeStruct((m, LANES), F32),
        jax.ShapeDtypeStruct((m, MEM_W), F32),
    )
    out_specs = (row(DSA_W), row(DSA_W), pl.BlockSpec((DSA_HEADS, V_ROWS, tm), lambda i: (0, 0, i)),
                 row(IDX_HEADS * IDX_DIM), row(LANES), row(LANES), row(4 * GDN_W),
                 row(LANES), row(MEM_W))
    return pl.pallas_call(
        _inproj_kernel,
        grid=(m // tm,),
        in_specs=[row(d), full(gain), full(w_packed), full(grp), full(qg), full(kg)],
        out_specs=out_specs,
        out_shape=out_shape,
        compiler_params=_params("parallel"),
        name="inproj",
    )(x, gain, w_packed, grp, qg, kg)


_DENORMAL_TOP = 0x007FFFFF


def _float_to_key(f):
    bits = lax.bitcast_convert_type(f, jnp.int32)
    mag = jnp.maximum((bits & jnp.int32(0x7FFFFFFF)) - _DENORMAL_TOP, 0)
    return jnp.where(bits >= 0, mag, -mag)


def _key_to_float(key):
    mag = jnp.abs(key)
    bits = jnp.where(mag > 0, mag + _DENORMAL_TOP, 0)
    return lax.bitcast_convert_type(jnp.where(key < 0, bits | jnp.int32(-2 ** 31), bits), F32)


def _upper_normal_quantile(p):
    pp = jnp.clip(jnp.minimum(p, 1.0 - p), 1e-30, 0.5)
    t = jnp.sqrt(-2.0 * jnp.log(pp))
    z = t - (2.515517 + t * (0.802853 + t * 0.010328)) / (1.0 + t * (1.432788 + t * (0.189269 + t * 0.001308)))
    return jnp.where(p <= 0.5, z, -z)


def _tree(op, x, group=8):
    parts = x.reshape(x.shape[0] // group, group, LANES)
    k = parts.shape[0]
    while k > 1:
        k //= 2
        parts = op(parts[:k], parts[k:2 * k])
    return parts[0]


def _dsa_kernel(q_ref, iq_ref, kwq_ref, k_ref, vt_ref, ik_ref, tab_ref, ltri_ref, o_ref,
                s_scr, iqt_scr, qpad_scr, bias_scr, mx_scr, brange_scr, *, kc, top_k):
    j = pl.program_id(1)
    per = kc // Q_BLOCK
    pad = kc - Q_BLOCK
    nch = j // per + 1
    q_pos = j * Q_BLOCK + lax.broadcasted_iota(jnp.int32, (1, LANES), 1)
    row_iota = lax.broadcasted_iota(jnp.int32, (kc, LANES), 0)

    def rows(i):
        return pl.ds(pl.multiple_of(j * Q_BLOCK - i * kc, Q_BLOCK), kc)

    def chunk_pairs(start, body, init, width=2, stop=None):
        count = jnp.maximum((nch if stop is None else stop) - start, 0)
        rest = count % width
        state = lax.fori_loop(0, rest, lambda r, st: body(start + r, st), init)

        def group(t, st):
            for u in range(width):
                st = body(start + rest + width * t + u, st)
            return st
        return lax.fori_loop(0, count // width, group, state)

    @pl.when((pl.program_id(0) == 0) & (j == 0))
    def _():
        r = lax.broadcasted_iota(jnp.int32, (2 * Q_BLOCK, LANES), 0)
        c = lax.broadcasted_iota(jnp.int32, (2 * Q_BLOCK, LANES), 1)
        dist = Q_BLOCK + c - r
        bucket = jnp.zeros_like(dist)
        for first in _BUCKET_STARTS:
            bucket = bucket + jnp.where(dist >= first, 1, 0)
        for h in range(DSA_HEADS):
            far = tab_ref[NUM_BUCKETS - 1, h]
            delta = jnp.zeros((2 * Q_BLOCK, LANES), F32)
            b_hi = jnp.float32(0.0)
            b_lo = jnp.float32(0.0)
            for b in range(NUM_BUCKETS - 1):
                delta = jnp.where(bucket == b, tab_ref[b, h] - far, delta)
                b_hi = jnp.maximum(b_hi, (tab_ref[b, h] - far) * LOG2E)
                b_lo = jnp.minimum(b_lo, (tab_ref[b, h] - far) * LOG2E)
            bias_scr[h] = jnp.where(dist >= 0, delta * LOG2E, 0.0)
            brange_scr[0, h] = b_hi
            brange_scr[1, h] = b_lo

    iqt = iq_ref[...].astype(F32).T
    zpad = jnp.zeros((LANES - IDX_DIM, LANES), F32)
    for h in range(IDX_HEADS):
        iqt_scr[:, h * LANES:(h + 1) * LANES] = jnp.concatenate(
            [iqt[h * IDX_DIM:(h + 1) * IDX_DIM], zpad], axis=0).astype(BF16)
    wt = kwq_ref[...].T[IDX_DIM:IDX_DIM + IDX_HEADS, :] * (IDX_HEADS ** -0.5)
    qt = q_ref[...].astype(F32).T
    half = lax.broadcasted_iota(jnp.int32, (LANES, LANES), 0) < HEAD_DIM
    for h in range(DSA_HEADS):
        pair = qt[(h // 2) * LANES:(h // 2 + 1) * LANES]
        keep = half if h % 2 == 0 else jnp.logical_not(half)
        qpad_scr[:, h * LANES:(h + 1) * LANES] = jnp.where(keep, pair, 0.0).astype(BF16)

    kmax = tab_ref[NUM_BUCKETS, 0]
    spread = jnp.zeros((1, LANES), F32)
    for h in range(DSA_HEADS):
        b_hi = brange_scr[0, h]
        b_lo = brange_scr[1, h]
        qh = qt[h * HEAD_DIM:(h + 1) * HEAD_DIM]
        bound = jnp.sqrt(jnp.sum(qh * qh, axis=0, keepdims=True)) * kmax
        mx_scr[h:h + 1, :] = bound + b_hi
        spread = jnp.maximum(spread, 2.0 * bound + (b_hi - b_lo))
    bound_ok = jnp.max(spread) <= MAX_LOGIT_SPREAD

    def score_chunk(i, carry, edge):
        d = jnp.dot(ik_ref[0, rows(i), :], iqt_scr[...], preferred_element_type=F32)
        acc = jnp.maximum(d[:, :LANES], 0.0) * wt[0:1, :]
        for h in range(1, IDX_HEADS):
            acc = acc + jnp.maximum(d[:, h * LANES:(h + 1) * LANES], 0.0) * wt[h:h + 1, :]
        sc = fin = acc
        if edge:
            key = row_iota + (j * Q_BLOCK - i * kc - pad)
            adm = jnp.where(key >= 0, key, q_pos + 1) <= q_pos
            sc = jnp.where(adm, acc, -jnp.inf)
            fin = jnp.where(adm, acc, 0.0)
        s_scr[rows(i), :] = sc
        tot, sq, top = carry
        return (tot + _tree(jnp.add, fin), sq + _tree(jnp.add, fin * fin),
                jnp.maximum(top, _tree(jnp.maximum, sc)))

    zero8 = jnp.zeros((8, LANES), F32)
    stats = score_chunk(0, (zero8, zero8, jnp.full((8, LANES), -jnp.inf, F32)), True)
    stats = lax.fori_loop(0, jnp.minimum(nch - 1, 1), lambda _, st: score_chunk(nch - 1, st, True), stats)
    tot, sq, top = chunk_pairs(1, lambda i, st: score_chunk(i, st, False), stats, width=4, stop=nch - 1)

    def count(pred):
        def body(i, cnt):
            return cnt + _tree(jnp.add, jnp.where(pred(s_scr[rows(i), :]), 1.0, 0.0))
        cnt = chunk_pairs(0, body, zero8, width=4)
        return jnp.sum(cnt, axis=0, keepdims=True)

    def max_below(t):
        def body(i, m):
            s = s_scr[rows(i), :]
            return jnp.maximum(m, _tree(jnp.maximum, jnp.where(s < t, s, -jnp.inf)))
        m = lax.fori_loop(0, nch, body, jnp.full((8, LANES), -jnp.inf, F32))
        return jnp.max(m, axis=0, keepdims=True)

    kf = float(top_k)
    n_adm = (q_pos + 1).astype(F32)
    mean = jnp.sum(tot, axis=0, keepdims=True) / n_adm
    std = jnp.sqrt(jnp.maximum(jnp.sum(sq, axis=0, keepdims=True) / n_adm - mean * mean, 0.0))
    first_guess = mean + _upper_normal_quantile(kf / n_adm) * std
    lowest = jnp.float32(-3.0e38)
    top1 = jnp.minimum(jnp.max(top, axis=0, keepdims=True), -lowest)

    def update(st, key):
        lo, hi, c_lo, c_hi, lo_set, hi_set, done, run = st
        key = jnp.clip(key, lo + 1, hi - 1)
        t = _key_to_float(key)
        c = count(lambda s: s >= t)
        active = done < 0.5
        up = active & (c >= kf)
        dn = active & (c < kf)
        lo, c_lo, lo_set = jnp.where(up, key, lo), jnp.where(up, c, c_lo), jnp.where(up, 1.0, lo_set)
        hi, c_hi, hi_set = jnp.where(dn, key, hi), jnp.where(dn, c, c_hi), jnp.where(dn, 1.0, hi_set)
        done = jnp.where((c_lo == kf) | (hi <= lo + 1), 1.0, done)
        return lo, hi, c_lo, c_hi, lo_set, hi_set, done, run

    def guess(st, midpoint):
        lo, hi, c_lo, c_hi, lo_set, hi_set, done, run = st
        v_lo, v_hi = _key_to_float(lo), _key_to_float(hi)
        both = (lo_set > 0.5) & (hi_set > 0.5)
        step = std * 0.25 * jnp.exp2(run)
        if midpoint:
            inner = 0.5 * v_lo + 0.5 * v_hi
        else:
            l_lo, l_hi = jnp.log(c_lo), jnp.log(jnp.maximum(c_hi, 0.5))
            frac = jnp.clip((l_lo - math.log(kf - 0.5)) / jnp.maximum(l_lo - l_hi, 1e-6), 0.0, 1.0)
            inner = v_lo + (v_hi - v_lo) * frac
        t = jnp.where(both, inner,
                      jnp.where(hi_set > 0.5, v_hi - step, jnp.where(lo_set > 0.5, v_lo + step, first_guess)))
        t = jnp.where(t != t, 0.0, t)
        run = jnp.where(both, 0.0, run + 1.0)
        return _float_to_key(t), (lo, hi, c_lo, c_hi, lo_set, hi_set, done, run)

    def pending(st):
        return 1.0 - jnp.min(st[6])

    flag0 = jnp.zeros((1, LANES), F32)
    st = (jnp.full((1, LANES), _float_to_key(lowest), jnp.int32), _float_to_key(top1) + 1,
          n_adm, flag0, flag0, flag0, jnp.where(n_adm <= kf, 1.0, 0.0), flag0)

    def probes(st):
        for r in range(PROBES_PER_ROUND):
            key, st = guess(st, midpoint=((r + 1) % MIDPOINT_EVERY == 0))
            st = update(st, key)
        return st

    st = probes(probes(st))

    def fast_round(carry):
        st = probes(carry[2])
        return carry[0] + 1, pending(st), st

    _, _, st = lax.while_loop(lambda c: (c[0] < FAST_ROUNDS) & (c[1] > 0.5), fast_round,
                              (jnp.int32(2), pending(st), st))

    def safe_round(carry):
        p, _, st = carry
        st = update(st, st[0] + lax.shift_right_logical(st[1] - st[0], 1))
        lo, hi, done = st[0], st[1], st[6]
        below = _float_to_key(max_below(_key_to_float(hi)))
        hi = jnp.where(done < 0.5, jnp.clip(below + 1, lo + 1, hi), hi)
        done = jnp.where(hi <= lo + 1, 1.0, done)
        st = update((lo, hi) + st[2:6] + (done, st[7]), hi - 1)
        return p + 1, pending(st), st

    _, _, st = lax.while_loop(lambda c: (c[1] > 0.5) & (c[0] < 34), safe_round,
                              (jnp.int32(0), pending(st), st))

    thr = _key_to_float(st[0])
    has_ties = jnp.max(st[2]) > kf

    @pl.when(jnp.logical_not(has_ties))
    def _():
        def mask_chunk(i, carry):
            s_scr[rows(i), :] = jnp.where(s_scr[rows(i), :] >= thr, 0.0, NEG)
            return carry
        lax.fori_loop(0, nch, mask_chunk, 0)

    @pl.when(has_ties)
    def _():
        need = kf - count(lambda s: s > thr)

        def mask_chunk(t, run):
            i = nch - 1 - t
            s = s_scr[rows(i), :]
            tie = jnp.where(s == thr, 1.0, 0.0)
            pref = jnp.dot(ltri_ref[...], tie.astype(BF16), preferred_element_type=F32) + run
            tie_sel = jnp.where(pref <= need, tie, 0.0)
            sel = jnp.where(s > thr, 1.0, tie_sel)
            s_scr[rows(i), :] = jnp.where(sel > 0.5, 0.0, NEG)
            return run + jnp.sum(tie, axis=0, keepdims=True)
        lax.fori_loop(0, nch, mask_chunk, jnp.zeros((1, LANES), F32))

    def logits(i, with_bias):
        msk = s_scr[rows(i), :]
        out = []
        for pr in range(DSA_HEADS // 2):
            l2 = jnp.dot(k_ref[0, rows(i), pr * LANES:(pr + 1) * LANES],
                         qpad_scr[:, 2 * pr * LANES:(2 * pr + 2) * LANES],
                         preferred_element_type=F32)
            for e in range(2):
                l = l2[:, e * LANES:(e + 1) * LANES] + msk
                if with_bias:
                    far_rows = kc - 2 * Q_BLOCK
                    l = jnp.concatenate([l[:far_rows], l[far_rows:] + bias_scr[2 * pr + e]], axis=0)
                out.append(l)
        return out

    def max_step(i, ms, with_bias):
        return tuple(jnp.maximum(m, _tree(jnp.maximum, l)) for m, l in zip(ms, logits(i, with_bias)))

    @pl.when(jnp.logical_not(bound_ok))
    def _():
        ms = max_step(0, tuple(jnp.full((8, LANES), NEG, F32) for _ in range(DSA_HEADS)), True)
        ms = lax.fori_loop(1, nch, lambda i, m: max_step(i, m, False), ms)
        for h in range(DSA_HEADS):
            mx_scr[h:h + 1, :] = jnp.max(ms[h], axis=0, keepdims=True)

    mx = [mx_scr[h:h + 1, :] for h in range(DSA_HEADS)]

    def pv_step(i, accs, with_bias):
        return tuple(
            acc + jnp.dot(vt_ref[h, :, rows(i)], jnp.exp2(l - mx[h]).astype(BF16),
                          preferred_element_type=F32)
            for h, (acc, l) in enumerate(zip(accs, logits(i, with_bias))))

    accs = pv_step(0, tuple(jnp.zeros((V_ROWS, LANES), F32) for _ in range(DSA_HEADS)), True)
    accs = chunk_pairs(1, lambda i, a: pv_step(i, a, False), accs, width=4)
    outs = [acc[:HEAD_DIM] / acc[HEAD_DIM:HEAD_DIM + 1] for acc in accs]
    o_ref[...] = jnp.concatenate(outs, axis=0).T


def _dsa(q, iq, kw, k, vt, ikb, rel_bias, k_gain, ltri, *, batch, seq, kc):
    nb = seq // Q_BLOCK
    top_k = min(TOPK_MAX, seq // 4)
    pad = kc - Q_BLOCK
    seqp = seq + pad
    kp = jnp.pad(k.reshape(batch, seq, DSA_W), ((0, 0), (pad, 0), (0, 0)))
    ikp = jnp.pad(ikb.reshape(batch, seq, LANES), ((0, 0), (pad, 0), (0, 0)))
    vtp = jnp.pad(vt.reshape(DSA_HEADS, V_ROWS, batch, seq),
                  ((0, 0), (0, 0), (0, 0), (pad, 0))).reshape(DSA_HEADS, V_ROWS, batch * seqp)
    kmax = (HEAD_DIM ** 0.5 * 1.01) * jnp.max(jnp.abs(k_gain.astype(F32)))
    table = jnp.concatenate([rel_bias.astype(F32), jnp.full((1, DSA_HEADS), kmax, F32)], axis=0)
    qrow = lambda w: pl.BlockSpec((Q_BLOCK, w), lambda b, j: (b * nb + j, 0))
    return pl.pallas_call(
        functools.partial(_dsa_kernel, kc=kc, top_k=top_k),
        grid=(batch, nb),
        in_specs=[
            qrow(DSA_W), qrow(IDX_HEADS * IDX_DIM), qrow(LANES),
            pl.BlockSpec((1, seqp, DSA_W), lambda b, j: (b, 0, 0)),
            pl.BlockSpec((DSA_HEADS, V_ROWS, seqp), lambda b, j: (0, 0, b)),
            pl.BlockSpec((1, seqp, LANES), lambda b, j: (b, 0, 0)),
            pl.BlockSpec(memory_space=pltpu.SMEM),
            pl.BlockSpec(ltri.shape, lambda b, j: (0, 0)),
        ],
        out_specs=qrow(DSA_W),
        out_shape=jax.ShapeDtypeStruct((batch * seq, DSA_W), F32),
        scratch_shapes=[
            pltpu.VMEM((seqp, LANES), F32),
            pltpu.VMEM((LANES, IDX_HEADS * LANES), BF16),
            pltpu.VMEM((LANES, DSA_HEADS * LANES), BF16),
            pltpu.VMEM((DSA_HEADS, 2 * Q_BLOCK, LANES), F32),
            pltpu.VMEM((8, LANES), F32),
            pltpu.SMEM((2, DSA_HEADS), F32),
        ],
        compiler_params=_params("arbitrary", "arbitrary"),
        name="dsa",
    )(q, iq, kw, kp, vtp, ikp, table, ltri)


GDN_GROUP = 2


def _gdn_kernel(x_ref, ab_ref, cw_ref, ea_ref, eb_ref, alog_ref, dtb_ref, gain_ref, grp_ref,
                o_ref, xpad_scr, qkv_scr, gb_scr, gc_scr, gr_scr, pre_scr, st_scr, o_scr, *, tb):
    t = pl.program_id(1)
    cin = 3 * GDN_W
    n_chunks = tb // GDN_CHUNK
    pairs = range(GDN_HEADS // 2)
    lanes = lambda p: slice(p * LANES, (p + 1) * LANES)

    @pl.when(t == 0)
    def _():
        xpad_scr[0:8, :] = jnp.zeros((8, cin), F32)
        st_scr[...] = jnp.zeros_like(st_scr)

    x = x_ref[...]
    xpad_scr[8:8 + tb, :] = x[:, :cin]
    conv = jnp.zeros((tb, cin), F32)
    for jj in range(CONV_WIDTH):
        conv = conv + cw_ref[jj:jj + 1, :] * xpad_scr[pl.ds(8 - (CONV_WIDTH - 1) + jj, tb), :]
    xpad_scr[0:8, :] = x[tb - 8:tb, :cin]
    qkv = conv * _sigmoid(conv)
    grp = grp_ref[...]
    q = qkv[:, :GDN_W]
    k = qkv[:, GDN_W:2 * GDN_W]
    qkv_scr[:, :GDN_W] = q * lax.rsqrt(_group_sum(q * q, grp) + EPS) * (HEAD_DIM ** -0.5)
    qkv_scr[:, GDN_W:2 * GDN_W] = k * lax.rsqrt(_group_sum(k * k, grp) + EPS)
    qkv_scr[:, 2 * GDN_W:] = qkv[:, 2 * GDN_W:]

    ab = ab_ref[...]
    a_e = _dot_sel(ab, ea_ref[...], 3) + dtb_ref[...]
    b_e = _dot_sel(ab, eb_ref[...], 3)
    softplus = jnp.maximum(a_e, 0.0) + jnp.log(1.0 + jnp.exp(-jnp.abs(a_e)))
    gb_scr[:, :GDN_W] = -jnp.exp(alog_ref[...]) * softplus
    gb_scr[:, GDN_W:] = _sigmoid(b_e)

    r64 = lax.broadcasted_iota(jnp.int32, (GDN_CHUNK, LANES), 0)
    c64 = lax.broadcasted_iota(jnp.int32, (GDN_CHUNK, LANES), 1) % GDN_CHUNK
    causal = c64 <= r64
    strict = c64 < r64
    eye = jnp.where(c64 == r64, 1.0, 0.0)
    r128 = lax.broadcasted_iota(jnp.int32, (LANES, LANES), 0)
    c128 = lax.broadcasted_iota(jnp.int32, (LANES, LANES), 1)
    bdmask = (r128 // HEAD_DIM) == (c128 // HEAD_DIM)

    def bd(m):
        return jnp.where(bdmask, jnp.concatenate([m, m], axis=0), 0.0)

    lt_r = lax.broadcasted_iota(jnp.int32, (GDN_CHUNK, GDN_CHUNK), 0)
    lt_c = lax.broadcasted_iota(jnp.int32, (GDN_CHUNK, GDN_CHUNK), 1)
    ltri = jnp.where(lt_c <= lt_r, 1.0, 0.0)
    ones8 = jnp.ones((8, GDN_CHUNK), F32)
    up_r = lax.broadcasted_iota(jnp.int32, (GDN_CHUNK, GDN_W), 0)
    up_c = lax.broadcasted_iota(jnp.int32, (GDN_CHUNK, GDN_W), 1) % GDN_CHUNK
    upper = jnp.where(up_r <= up_c, 1.0, 0.0)
    for ci in range(n_chunks):
        g = gb_scr[ci * GDN_CHUNK:(ci + 1) * GDN_CHUNK, :GDN_W]
        gc_scr[ci * GDN_CHUNK:(ci + 1) * GDN_CHUNK, :] = _sel_dot(ltri, g, 3)
        gr_scr[ci * 8:(ci + 1) * 8, :] = _sel_dot(ones8, g * upper, 3)

    def par_body(gi, carry):
        inst = [(u, p) for u in range(GDN_GROUP) for p in pairs]
        rows = [pl.ds(pl.multiple_of((gi * GDN_GROUP + u) * GDN_CHUNK, GDN_CHUNK), GDN_CHUNK)
                for u in range(GDN_GROUP)]
        grow = [pl.ds(pl.multiple_of((gi * GDN_GROUP + u) * 8, 8), 8) for u in range(GDN_GROUP)]
        off = lambda s, p: slice(s * GDN_W + p * LANES, s * GDN_W + (p + 1) * LANES)
        qp = [qkv_scr[rows[u], off(0, p)] for u, p in inst]
        kp = [qkv_scr[rows[u], off(1, p)] for u, p in inst]
        vp = [qkv_scr[rows[u], off(2, p)] for u, p in inst]
        beta = [gb_scr[rows[u], off(1, p)] for u, p in inst]
        gc = [gc_scr[rows[u], lanes(p)] for u, p in inst]
        gr = [gr_scr[grow[u], lanes(p)][0:1, :] for u, p in inst]
        dmat = [jnp.exp(jnp.where(causal, a - b, NEG)) for a, b in zip(gc, gr)]
        kbd = [jnp.where(bdmask, jnp.concatenate([a, a], axis=0).T, 0.0) for a in kp]
        kb = [a * b for a, b in zip(kp, beta)]
        kk = [_bdot(a, b) for a, b in zip(kb, kbd)]
        qk = [_bdot(a, b) for a, b in zip(qp, kbd)]
        lm = [jnp.where(strict, a * d, 0.0) for a, d in zip(kk, dmat)]
        aintra = [a * d for a, d in zip(qk, dmat)]
        tinv = [eye - a for a in lm]
        lpow = lm
        for _ in range(5):
            lpow = [_dot3(a, bd(a)) for a in lpow]
            tinv = [a + _dot3(a, bd(b)) for a, b in zip(tinv, lpow)]
        egc = [jnp.exp(a) for a in gc]
        un = [_dot3(a, bd(v * b)) for a, v, b in zip(tinv, vp, beta)]
        wn = [_dot3(a, bd(b * e)) for a, b, e in zip(tinv, kb, egc)]
        for n, (u, p) in enumerate(inst):
            pre_scr[0, rows[u], lanes(p)] = un[n]
            pre_scr[1, rows[u], lanes(p)] = wn[n]
            pre_scr[2, rows[u], lanes(p)] = aintra[n]
            pre_scr[3, rows[u], lanes(p)] = qp[n] * egc[n]
            pre_scr[4, rows[u], lanes(p)] = kp[n] * jnp.exp(gc[n][GDN_CHUNK - 1:GDN_CHUNK, :] - gc[n])
        return carry

    lax.fori_loop(0, n_chunks // GDN_GROUP, par_body, 0)

    def rec_body(ci, carry):
        rows = pl.ds(pl.multiple_of(ci * GDN_CHUNK, GDN_CHUNK), GDN_CHUNK)
        tail = gc_scr[pl.ds(pl.multiple_of(ci * GDN_CHUNK + GDN_CHUNK - 8, 8), 8), :]
        egl = jnp.exp(tail[7:8, :])
        sbd = [st_scr[p] for p in pairs]
        un, wn, aintra, qe, kd = [[pre_scr[s, rows, lanes(p)] for p in pairs] for s in range(5)]
        ws = [_bdot(a, s) for a, s in zip(wn, sbd)]
        qs = [_bdot(a, s) for a, s in zip(qe, sbd)]
        v_new = [a - b for a, b in zip(un, ws)]
        o = [a + _bdot(b, bd(v)) for a, b, v in zip(qs, aintra, v_new)]
        upd = [_bdot(a.T, v) for a, v in zip(kd, v_new)]
        for p in pairs:
            st_scr[p] = sbd[p] * egl[:, lanes(p)] + jnp.where(bdmask, upd[p], 0.0)
            o_scr[rows, lanes(p)] = o[p]
        return carry

    lax.fori_loop(0, n_chunks, rec_body, 0)

    o = o_scr[...]
    oms = _group_sum(o * o, grp) * (1.0 / HEAD_DIM)
    z = x[:, cin:]
    o_ref[...] = o * lax.rsqrt(oms + EPS) * gain_ref[...] * (z * _sigmoid(z))


def _gdn(gx, ab, conv_w, ea, eb, alog, dtb, gain, grp, *, batch, seq, tb):
    nt = seq // tb
    cin = 3 * GDN_W
    row = lambda w: pl.BlockSpec((tb, w), lambda b, t: (b * nt + t, 0))
    full = lambda a: pl.BlockSpec(a.shape, lambda b, t: (0,) * a.ndim)
    return pl.pallas_call(
        functools.partial(_gdn_kernel, tb=tb),
        grid=(batch, nt),
        in_specs=[row(4 * GDN_W), row(LANES), full(conv_w), full(ea), full(eb), full(alog),
                  full(dtb), full(gain), full(grp)],
        out_specs=row(GDN_W),
        out_shape=jax.ShapeDtypeStruct((batch * seq, GDN_W), F32),
        scratch_shapes=[
            pltpu.VMEM((tb + 8, cin), F32),
            pltpu.VMEM((tb, cin), F32),
            pltpu.VMEM((tb, 2 * GDN_W), F32),
            pltpu.VMEM((tb, GDN_W), F32),
            pltpu.VMEM((tb // 8, GDN_W), F32),
            pltpu.VMEM((5, tb, GDN_W), F32),
            pltpu.VMEM((GDN_HEADS // 2, LANES, LANES), F32),
            pltpu.VMEM((tb, GDN_W), F32),
        ],
        compiler_params=_params("arbitrary", "arbitrary"),
        name="gdn",
    )(gx, ab, conv_w, ea, eb, alog, dtb, gain, grp)


def _memkv_kernel(mem_ref, g_ref, w_ref, kg_ref, grp_ref, kt_ref, vp_ref):
    x = mem_ref[0]
    ms = jnp.mean(x * x, axis=-1, keepdims=True)
    h = (x * lax.rsqrt(ms + EPS) * g_ref[...]).astype(BF16)
    kv = jnp.dot(h, w_ref[...], preferred_element_type=F32)
    km = kv[:, :MEM_W]
    vm = kv[:, MEM_W:]
    kms = _group_sum(km * km, grp_ref[...]) * (1.0 / HEAD_DIM)
    kt = (km * lax.rsqrt(kms + EPS) * kg_ref[...]).T
    n_mem = x.shape[0]
    top = lax.broadcasted_iota(jnp.int32, (LANES, n_mem), 0) < HEAD_DIM
    left = lax.broadcasted_iota(jnp.int32, (n_mem, LANES), 1) < HEAD_DIM
    for hh in range(MEM_HEADS):
        pr = slice((hh // 2) * LANES, (hh // 2 + 1) * LANES)
        keep_r = top if hh % 2 == 0 else jnp.logical_not(top)
        keep_c = left if hh % 2 == 0 else jnp.logical_not(left)
        kt_ref[0, hh] = jnp.where(keep_r, kt[pr, :], 0.0).astype(BF16)
        vp_ref[0, hh] = jnp.where(keep_c, vm[:, pr], 0.0).astype(BF16)


def _memkv(mem, gain, w, kg, grp):
    b, n_mem, d = mem.shape
    full = lambda a: pl.BlockSpec(a.shape, lambda i: (0,) * a.ndim)
    return pl.pallas_call(
        _memkv_kernel,
        grid=(b,),
        in_specs=[pl.BlockSpec((1, n_mem, d), lambda i: (i, 0, 0)), full(gain), full(w), full(kg),
                  full(grp)],
        out_specs=(pl.BlockSpec((1, MEM_HEADS, LANES, n_mem), lambda i: (i, 0, 0, 0)),
                   pl.BlockSpec((1, MEM_HEADS, n_mem, LANES), lambda i: (i, 0, 0, 0))),
        out_shape=(jax.ShapeDtypeStruct((b, MEM_HEADS, LANES, n_mem), BF16),
                   jax.ShapeDtypeStruct((b, MEM_HEADS, n_mem, LANES), BF16)),
        compiler_params=_params("parallel"),
        name="memkv",
    )(mem, gain, w, kg, grp)


def _memattn_kernel(q_ref, qg_ref, grp_ref, kt_ref, vp_ref, o_ref):
    q = q_ref[...]
    qms = _group_sum(q * q, grp_ref[...]) * (1.0 / HEAD_DIM)
    qn = (q * lax.rsqrt(qms + EPS) * qg_ref[...] * (HEAD_DIM ** -0.5)).astype(BF16)
    outs = []
    for pr in range(MEM_HEADS // 2):
        qp = qn[:, pr * LANES:(pr + 1) * LANES]
        acc = None
        for e in range(2):
            hh = 2 * pr + e
            l = jnp.dot(qp, kt_ref[0, hh], preferred_element_type=F32)
            l = l - jnp.max(l, axis=-1, keepdims=True)
            p = jnp.exp(l)
            p = p / jnp.sum(p, axis=-1, keepdims=True)
            o = jnp.dot(p.astype(BF16), vp_ref[0, hh], preferred_element_type=F32)
            acc = o if acc is None else acc + o
        outs.append(acc)
    o_ref[...] = jnp.concatenate(outs, axis=-1)


def _memattn(mq, qg, grp, kt, vp, *, batch, seq, tm):
    nt = seq // tm
    n_mem = kt.shape[-1]
    full = lambda a: pl.BlockSpec(a.shape, lambda b, t: (0,) * a.ndim)
    return pl.pallas_call(
        _memattn_kernel,
        grid=(batch, nt),
        in_specs=[
            pl.BlockSpec((tm, MEM_W), lambda b, t: (b * nt + t, 0)), full(qg), full(grp),
            pl.BlockSpec((1, MEM_HEADS, LANES, n_mem), lambda b, t: (b, 0, 0, 0)),
            pl.BlockSpec((1, MEM_HEADS, n_mem, LANES), lambda b, t: (b, 0, 0, 0)),
        ],
        out_specs=pl.BlockSpec((tm, MEM_W), lambda b, t: (b * nt + t, 0)),
        out_shape=jax.ShapeDtypeStruct((batch * seq, MEM_W), F32),
        compiler_params=_params("parallel", "parallel"),
        name="memattn",
    )(mq, qg, grp, kt, vp)


def _group_ones(width):
    idx = np.arange(width) // HEAD_DIM
    return jnp.asarray((idx[:, None] == idx[None, :]).astype(np.float32))


def _t5_bucket_np(n):
    max_exact = NUM_BUCKETS // 2
    nf = np.maximum(n, 1).astype(np.float32)
    large = max_exact + (np.log(nf / np.float32(max_exact)) / np.float32(math.log(MAX_DISTANCE / max_exact))
                         * (NUM_BUCKETS - max_exact)).astype(np.int32)
    large = np.minimum(large, NUM_BUCKETS - 1)
    return np.where(n < max_exact, n, large)


def _bucket_starts():
    dist = np.arange(2 * Q_BLOCK)
    bucket = _t5_bucket_np(dist)
    assert (np.diff(bucket) >= 0).all() and bucket[-1] == NUM_BUCKETS - 1
    return tuple(int(dist[bucket >= b].min()) for b in range(1, NUM_BUCKETS))


_BUCKET_STARTS = _bucket_starts()


def _cast_kernel(w_ref, o_ref):
    o_ref[...] = w_ref[...].astype(BF16)


def _to_bf16(w, *, rows):
    nl, nr, nc = w.shape
    spec = pl.BlockSpec((1, rows, nc), lambda l, r: (l, r, 0))
    return pl.pallas_call(
        _cast_kernel,
        grid=(nl, nr // rows),
        in_specs=[spec],
        out_specs=spec,
        out_shape=jax.ShapeDtypeStruct(w.shape, BF16),
        compiler_params=_params("parallel", "parallel"),
        name="cast",
    )(w)


def _pack_w_in(w):
    o = 3 * DSA_W
    iq = w[:, o:o + IDX_HEADS * IDX_DIM]
    o += IDX_HEADS * IDX_DIM
    kw = w[:, o:o + IDX_DIM + IDX_HEADS]
    o += IDX_DIM + IDX_HEADS
    g = w[:, o:o + 4 * GDN_W]
    o += 4 * GDN_W
    ab = w[:, o:o + 2 * GDN_HEADS]
    o += 2 * GDN_HEADS
    mq = w[:, o:o + MEM_W]
    pad = lambda a: jnp.pad(a, ((0, 0), (0, LANES - a.shape[1])))
    return jnp.concatenate([w[:, :3 * DSA_W], iq, pad(kw), g, pad(ab), mq], axis=1)


def _tile_heads(v, heads):
    return jnp.tile(v.astype(F32), heads).reshape(1, heads * HEAD_DIM)


def kernel(x, mem, ffn1_norm, ffn1_w_gate, ffn1_w_up, ffn1_w_down, mix_norm, w_in, dsa_q_norm, dsa_k_norm, rel_bias, gdn_conv, gdn_A_log, gdn_dt_bias, gdn_out_norm, mem_norm, w_mem_kv, mem_q_norm, mem_k_norm, w_out, ffn2_norm, ffn2_w_gate, ffn2_w_up, ffn2_w_down):
    batch, seq, d = x.shape
    depth = w_in.shape[0]
    m = batch * seq
    tm = min(512, seq)
    kc = min(512, seq)
    dff = ffn1_w_gate.shape[-1]
    tf = dff // 2 if (dff // 2) % LANES == 0 else dff

    grp_a = _group_ones(DSA_W)
    grp_m = _group_ones(MEM_W)
    ltri = jnp.asarray(np.tril(np.ones((kc, kc), np.float32))).astype(BF16)
    heads_of = np.arange(GDN_W) // HEAD_DIM
    ea = jnp.asarray((np.arange(LANES)[:, None] == heads_of[None, :]).astype(np.float32))
    eb = jnp.asarray((np.arange(LANES)[:, None] == heads_of[None, :] + GDN_HEADS).astype(np.float32))

    w_in_b = _to_bf16(w_in, rows=min(256, d))
    xf = x.reshape(m, d)
    for l in range(depth):
        xf = _ffn(xf, ffn1_norm[l], ffn1_w_gate[l].astype(BF16), ffn1_w_up[l].astype(BF16),
                  ffn1_w_down[l].astype(BF16), tm=tm, tf=tf)
        (q, k, vt, iq, ikb, kw, gx, ab, mq) = _inproj(
            xf, mix_norm[l].reshape(1, d), _pack_w_in(w_in_b[l]), grp_a,
            _tile_heads(dsa_q_norm[l], DSA_HEADS), _tile_heads(dsa_k_norm[l], DSA_HEADS), tm=tm)
        out_a = _dsa(q, iq, kw, k, vt, ikb, rel_bias, dsa_k_norm[l], ltri, batch=batch, seq=seq, kc=kc)
        out_b = _gdn(gx, ab, gdn_conv[l], ea, eb,
                     jnp.repeat(gdn_A_log[l].astype(F32), HEAD_DIM).reshape(1, GDN_W),
                     jnp.repeat(gdn_dt_bias[l].astype(F32), HEAD_DIM).reshape(1, GDN_W),
                     _tile_heads(gdn_out_norm[l], GDN_HEADS), grp_a, batch=batch, seq=seq, tb=tm)
        kt, vp = _memkv(mem, mem_norm[l].reshape(1, d), w_mem_kv[l].astype(BF16),
                        _tile_heads(mem_k_norm[l], MEM_HEADS), grp_m)
        out_c = _memattn(mq, _tile_heads(mem_q_norm[l], MEM_HEADS), grp_m, kt, vp,
                         batch=batch, seq=seq, tm=tm)
        wo = w_out[l].astype(BF16)
        xf = _ffn(xf, ffn2_norm[l], ffn2_w_gate[l].astype(BF16), ffn2_w_up[l].astype(BF16),
                  ffn2_w_down[l].astype(BF16), mix=(out_a, out_b, out_c),
                  mix_w=(wo[:DSA_W], wo[DSA_W:DSA_W + GDN_W], wo[DSA_W + GDN_W:]), tm=tm, tf=tf)
    return xf.reshape(batch, seq, d)
```

```python
import functools
import math

import jax
import jax.numpy as jnp
import numpy as np
from jax import lax
from jax.experimental import pallas as pl
from jax.experimental.pallas import tpu as pltpu

F32 = jnp.float32
BF16 = jnp.bfloat16

HEAD_DIM = 64
DSA_HEADS = 6
GDN_HEADS = 6
MEM_HEADS = 4
DSA_W = DSA_HEADS * HEAD_DIM
GDN_W = GDN_HEADS * HEAD_DIM
MEM_W = MEM_HEADS * HEAD_DIM
IDX_HEADS = 8
IDX_DIM = 32
TOPK_MAX = 256
Q_BLOCK = 128
GDN_CHUNK = 64
CONV_WIDTH = 4
NUM_BUCKETS = 32
MAX_DISTANCE = 128
EPS = 1e-6

LANES = 128
VMEM_LIMIT = 52 * 1024 * 1024
NEG = -1e30
LOG2E = math.log2(math.e)
V_ROWS = HEAD_DIM + 16
MAX_LOGIT_SPREAD = 80.0
PROBES_PER_ROUND = 4
MIDPOINT_EVERY = 4
FAST_ROUNDS = 4

SEG_A = 0
SEG_IQ = SEG_A + 3 * DSA_W
SEG_KW = SEG_IQ + IDX_HEADS * IDX_DIM
SEG_G = SEG_KW + LANES
SEG_AB = SEG_G + 4 * GDN_W
SEG_MQ = SEG_AB + LANES
IN_PACKED = SEG_MQ + MEM_W


def _bdot(a, b):
    return jnp.dot(a.astype(BF16), b.astype(BF16), preferred_element_type=F32)


def _split(x, terms):
    out = []
    for _ in range(terms - 1):
        hi = x.astype(BF16)
        out.append(hi)
        x = x - hi.astype(F32)
    out.append(x.astype(BF16))
    return out


def _dot3(a, b):
    ah, al = _split(a, 2)
    bh, bl = _split(b, 2)
    mm = lambda x, y: jnp.dot(x, y, preferred_element_type=F32)
    return mm(ah, bh) + (mm(ah, bl) + mm(al, bh))


def _dot_sel(a, sel, terms):
    selb = sel.astype(BF16)
    acc = None
    for piece in _split(a, terms):
        d = jnp.dot(piece, selb, preferred_element_type=F32)
        acc = d if acc is None else acc + d
    return acc


def _sel_dot(sel, b, terms):
    selb = sel.astype(BF16)
    acc = None
    for piece in _split(b, terms):
        d = jnp.dot(selb, piece, preferred_element_type=F32)
        acc = d if acc is None else acc + d
    return acc


def _group_sum(x, grp):
    return _dot_sel(x, grp, 2)


def _sigmoid(x):
    return 1.0 / (1.0 + jnp.exp(-x))


def _params(*sem):
    return pltpu.CompilerParams(dimension_semantics=sem, vmem_limit_bytes=VMEM_LIMIT)


def _ffn_kernel(*refs, n_mix):
    x_ref, g_ref, wg_ref, wu_ref, wd_ref = refs[:5]
    mix_refs = refs[5:5 + 2 * n_mix]
    o_ref, x_scr, h_scr, acc_scr = refs[5 + 2 * n_mix:]
    f = pl.program_id(1)

    @pl.when(f == 0)
    def _():
        x = x_ref[...]
        for k in range(n_mix):
            x = x + _bdot(mix_refs[k][...], mix_refs[n_mix + k][...])
        x_scr[...] = x
        ms = jnp.mean(x * x, axis=-1, keepdims=True)
        h_scr[...] = (x * lax.rsqrt(ms + EPS) * g_ref[...]).astype(BF16)
        acc_scr[...] = jnp.zeros_like(acc_scr)

    h = h_scr[...]
    a = jnp.dot(h, wg_ref[...], preferred_element_type=F32)
    u = jnp.dot(h, wu_ref[...], preferred_element_type=F32)
    z = (a * _sigmoid(a)) * u
    acc_scr[...] += jnp.dot(z.astype(BF16), wd_ref[...], preferred_element_type=F32)

    @pl.when(f == pl.num_programs(1) - 1)
    def _():
        o_ref[...] = x_scr[...] + 0.5 * acc_scr[...]


def _ffn(x, gain, wg, wu, wd, mix=(), mix_w=(), *, tm, tf):
    m, d = x.shape
    dff = wg.shape[1]
    row = lambda w: pl.BlockSpec((tm, w), lambda i, f: (i, 0))
    full = lambda a: pl.BlockSpec(a.shape, lambda i, f: (0,) * a.ndim)
    return pl.pallas_call(
        functools.partial(_ffn_kernel, n_mix=len(mix)),
        grid=(m // tm, dff // tf),
        in_specs=[
            row(d),
            pl.BlockSpec((1, d), lambda i, f: (0, 0)),
            pl.BlockSpec((d, tf), lambda i, f: (0, f)),
            pl.BlockSpec((d, tf), lambda i, f: (0, f)),
            pl.BlockSpec((tf, d), lambda i, f: (f, 0)),
        ] + [row(a.shape[1]) for a in mix] + [full(w) for w in mix_w],
        out_specs=row(d),
        out_shape=jax.ShapeDtypeStruct((m, d), F32),
        scratch_shapes=[pltpu.VMEM((tm, d), F32), pltpu.VMEM((tm, d), BF16), pltpu.VMEM((tm, d), F32)],
        compiler_params=_params("parallel", "arbitrary"),
        name="ffn",
    )(x, gain.reshape(1, d), wg, wu, wd, *mix, *mix_w)


def _inproj_kernel(x_ref, g_ref, w_ref, grp_ref, qg_ref, kg_ref,
                   q_ref, k_ref, vt_ref, iq_ref, ikb_ref, kw_ref, gx_ref, ab_ref, mq_ref):
    x = x_ref[...]
    ms = jnp.mean(x * x, axis=-1, keepdims=True)
    h = (x * lax.rsqrt(ms + EPS) * g_ref[...]).astype(BF16)
    p = jnp.dot(h, w_ref[...], preferred_element_type=F32)

    grp = grp_ref[...]
    dq = p[:, SEG_A:SEG_A + DSA_W]
    dk = p[:, SEG_A + DSA_W:SEG_A + 2 * DSA_W]
    dv = p[:, SEG_A + 2 * DSA_W:SEG_A + 3 * DSA_W]
    qms = _group_sum(dq * dq, grp) * (1.0 / HEAD_DIM)
    kms = _group_sum(dk * dk, grp) * (1.0 / HEAD_DIM)
    q_ref[...] = (dq * lax.rsqrt(qms + EPS) * qg_ref[...] * (HEAD_DIM ** -0.5 * LOG2E)).astype(BF16)
    k_ref[...] = (dk * lax.rsqrt(kms + EPS) * kg_ref[...]).astype(BF16)
    dvt = dv.T
    ones = jnp.ones((V_ROWS - HEAD_DIM, dvt.shape[1]), F32)
    for hd in range(DSA_HEADS):
        vt_ref[hd] = jnp.concatenate([dvt[hd * HEAD_DIM:(hd + 1) * HEAD_DIM], ones], axis=0).astype(BF16)

    iq_ref[...] = (p[:, SEG_IQ:SEG_IQ + IDX_HEADS * IDX_DIM] * (IDX_DIM ** -0.5)).astype(BF16)
    kw = p[:, SEG_KW:SEG_KW + LANES]
    kw_ref[...] = kw
    ikb_ref[...] = kw.astype(BF16)
    gx_ref[...] = p[:, SEG_G:SEG_G + 4 * GDN_W]
    ab_ref[...] = p[:, SEG_AB:SEG_AB + LANES]
    mq_ref[...] = p[:, SEG_MQ:SEG_MQ + MEM_W]


def _inproj(x, gain, w_packed, grp, qg, kg, *, tm):
    m, d = x.shape
    row = lambda w: pl.BlockSpec((tm, w), lambda i: (i, 0))
    full = lambda a: pl.BlockSpec(a.shape, lambda i: (0,) * a.ndim)
    out_shape = (
        jax.ShapeDtypeStruct((m, DSA_W), BF16),
        jax.ShapeDtypeStruct((m, DSA_W), BF16),
        jax.ShapeDtypeStruct((DSA_HEADS, V_ROWS, m), BF16),
        jax.ShapeDtypeStruct((m, IDX_HEADS * IDX_DIM), BF16),
        jax.ShapeDtypeStruct((m, LANES), BF16),
        jax.ShapeDtypeStruct((m, LANES), F32),
        jax.ShapeDtypeStruct((m, 4 * GDN_W), F32),
        jax.ShapeDtypeStruct((m, LANES), F32),
        jax.ShapeDtypeStruct((m, MEM_W), F32),
    )
    out_specs = (row(DSA_W), row(DSA_W), pl.BlockSpec((DSA_HEADS, V_ROWS, tm), lambda i: (0, 0, i)),
                 row(IDX_HEADS * IDX_DIM), row(LANES), row(LANES), row(4 * GDN_W),
                 row(LANES), row(MEM_W))
    return pl.pallas_call(
        _inproj_kernel,
        grid=(m // tm,),
        in_specs=[row(d), full(gain), full(w_packed), full(grp), full(qg), full(kg)],
        out_specs=out_specs,
        out_shape=out_shape,
        compiler_params=_params("parallel"),
        name="inproj",
    )(x, gain, w_packed, grp, qg, kg)


_DENORMAL_TOP = 0x007FFFFF


def _float_to_key(f):
    bits = lax.bitcast_convert_type(f, jnp.int32)
    mag = jnp.maximum((bits & jnp.int32(0x7FFFFFFF)) - _DENORMAL_TOP, 0)
    return jnp.where(bits >= 0, mag, -mag)


def _key_to_float(key):
    mag = jnp.abs(key)
    bits = jnp.where(mag > 0, mag + _DENORMAL_TOP, 0)
    return lax.bitcast_convert_type(jnp.where(key < 0, bits | jnp.int32(-2 ** 31), bits), F32)


def _upper_normal_quantile(p):
    pp = jnp.clip(jnp.minimum(p, 1.0 - p), 1e-30, 0.5)
    t = jnp.sqrt(-2.0 * jnp.log(pp))
    z = t - (2.515517 + t * (0.802853 + t * 0.010328)) / (1.0 + t * (1.432788 + t * (0.189269 + t * 0.001308)))
    return jnp.where(p <= 0.5, z, -z)


def _tree(op, x, group=8):
    parts = x.reshape(x.shape[0] // group, group, LANES)
    k = parts.shape[0]
    while k > 1:
        k //= 2
        parts = op(parts[:k], parts[k:2 * k])
    return parts[0]


def _dsa_kernel(q_ref, iq_ref, kwq_ref, k_ref, vt_ref, ik_ref, tab_ref, ltri_ref, o_ref,
                s_scr, iqt_scr, qpad_scr, bias_scr, mx_scr, brange_scr, *, kc, top_k):
    j = pl.program_id(1)
    per = kc // Q_BLOCK
    pad = kc - Q_BLOCK
    nch = j // per + 1
    q_pos = j * Q_BLOCK + lax.broadcasted_iota(jnp.int32, (1, LANES), 1)
    row_iota = lax.broadcasted_iota(jnp.int32, (kc, LANES), 0)

    def rows(i):
        return pl.ds(pl.multiple_of(j * Q_BLOCK - i * kc, Q_BLOCK), kc)

    def chunk_pairs(start, body, init, width=2, stop=None):
        count = jnp.maximum((nch if stop is None else stop) - start, 0)
        rest = count % width
        state = lax.fori_loop(0, rest, lambda r, st: body(start + r, st), init)

        def group(t, st):
            for u in range(width):
                st = body(start + rest + width * t + u, st)
            return st
        return lax.fori_loop(0, count // width, group, state)

    @pl.when((pl.program_id(0) == 0) & (j == 0))
    def _():
        r = lax.broadcasted_iota(jnp.int32, (2 * Q_BLOCK, LANES), 0)
        c = lax.broadcasted_iota(jnp.int32, (2 * Q_BLOCK, LANES), 1)
        dist = Q_BLOCK + c - r
        bucket = jnp.zeros_like(dist)
        for first in _BUCKET_STARTS:
            bucket = bucket + jnp.where(dist >= first, 1, 0)
        for h in range(DSA_HEADS):
            far = tab_ref[NUM_BUCKETS - 1, h]
            delta = jnp.zeros((2 * Q_BLOCK, LANES), F32)
            b_hi = jnp.float32(0.0)
            b_lo = jnp.float32(0.0)
            for b in range(NUM_BUCKETS - 1):
                delta = jnp.where(bucket == b, tab_ref[b, h] - far, delta)
                b_hi = jnp.maximum(b_hi, (tab_ref[b, h] - far) * LOG2E)
                b_lo = jnp.minimum(b_lo, (tab_ref[b, h] - far) * LOG2E)
            bias_scr[h] = jnp.where(dist >= 0, delta * LOG2E, 0.0)
            brange_scr[0, h] = b_hi
            brange_scr[1, h] = b_lo

    iqt = iq_ref[...].astype(F32).T
    zpad = jnp.zeros((LANES - IDX_DIM, LANES), F32)
    for h in range(IDX_HEADS):
        iqt_scr[:, h * LANES:(h + 1) * LANES] = jnp.concatenate(
            [iqt[h * IDX_DIM:(h + 1) * IDX_DIM], zpad], axis=0).astype(BF16)
    wt = kwq_ref[...].T[IDX_DIM:IDX_DIM + IDX_HEADS, :] * (IDX_HEADS ** -0.5)
    qt = q_ref[...].astype(F32).T
    half = lax.broadcasted_iota(jnp.int32, (LANES, LANES), 0) < HEAD_DIM
    for h in range(DSA_HEADS):
        pair = qt[(h // 2) * LANES:(h // 2 + 1) * LANES]
        keep = half if h % 2 == 0 else jnp.logical_not(half)
        qpad_scr[:, h * LANES:(h + 1) * LANES] = jnp.where(keep, pair, 0.0).astype(BF16)

    bound = tab_ref[NUM_BUCKETS, 0] * tab_ref[NUM_BUCKETS, 1]
    spread = jnp.float32(0.0)
    for h in range(DSA_HEADS):
        b_hi = brange_scr[0, h]
        b_lo = brange_scr[1, h]
        mx_scr[h:h + 1, :] = jnp.full((1, LANES), bound + b_hi, F32)
        spread = jnp.maximum(spread, 2.0 * bound + (b_hi - b_lo))
    bound_ok = spread <= MAX_LOGIT_SPREAD

    def score_chunk(i, carry, edge):
        d = jnp.dot(ik_ref[0, rows(i), :], iqt_scr[...], preferred_element_type=F32)
        acc = jnp.maximum(d[:, :LANES], 0.0) * wt[0:1, :]
        for h in range(1, IDX_HEADS):
            acc = acc + jnp.maximum(d[:, h * LANES:(h + 1) * LANES], 0.0) * wt[h:h + 1, :]
        sc = fin = acc
        if edge:
            key = row_iota + (j * Q_BLOCK - i * kc - pad)
            adm = jnp.where(key >= 0, key, q_pos + 1) <= q_pos
            sc = jnp.where(adm, acc, -jnp.inf)
            fin = jnp.where(adm, acc, 0.0)
        s_scr[rows(i), :] = sc
        tot, sq, top = carry
        return (tot + _tree(jnp.add, fin), sq + _tree(jnp.add, fin * fin),
                jnp.maximum(top, _tree(jnp.maximum, sc)))

    zero8 = jnp.zeros((8, LANES), F32)
    stats = score_chunk(0, (zero8, zero8, jnp.full((8, LANES), -jnp.inf, F32)), True)
    stats = lax.fori_loop(0, jnp.minimum(nch - 1, 1), lambda _, st: score_chunk(nch - 1, st, True), stats)
    tot, sq, top = chunk_pairs(1, lambda i, st: score_chunk(i, st, False), stats, width=4, stop=nch - 1)

    def count(pred):
        def body(i, cnt):
            return cnt + _tree(jnp.add, jnp.where(pred(s_scr[rows(i), :]), 1.0, 0.0))
        cnt = chunk_pairs(0, body, zero8, width=4)
        return jnp.sum(cnt, axis=0, keepdims=True)

    def max_below(t):
        def body(i, m):
            s = s_scr[rows(i), :]
            return jnp.maximum(m, _tree(jnp.maximum, jnp.where(s < t, s, -jnp.inf)))
        m = lax.fori_loop(0, nch, body, jnp.full((8, LANES), -jnp.inf, F32))
        return jnp.max(m, axis=0, keepdims=True)

    kf = float(top_k)
    n_adm = (q_pos + 1).astype(F32)
    mean = jnp.sum(tot, axis=0, keepdims=True) / n_adm
    std = jnp.sqrt(jnp.maximum(jnp.sum(sq, axis=0, keepdims=True) / n_adm - mean * mean, 0.0))
    first_guess = mean + _upper_normal_quantile(kf / n_adm) * std
    lowest = jnp.float32(-3.0e38)
    top1 = jnp.minimum(jnp.max(top, axis=0, keepdims=True), -lowest)

    def update(st, key):
        lo, hi, c_lo, c_hi, lo_set, hi_set, done, run, v_lo, v_hi, l_lo, l_hi = st
        key = jnp.clip(key, lo + 1, hi - 1)
        t = _key_to_float(key)
        c = count(lambda s: s >= t)
        lc = jnp.log(jnp.maximum(c, 0.5))
        active = done < 0.5
        up = active & (c >= kf)
        dn = active & (c < kf)
        lo, c_lo, lo_set = jnp.where(up, key, lo), jnp.where(up, c, c_lo), jnp.where(up, 1.0, lo_set)
        hi, c_hi, hi_set = jnp.where(dn, key, hi), jnp.where(dn, c, c_hi), jnp.where(dn, 1.0, hi_set)
        v_lo, l_lo = jnp.where(up, t, v_lo), jnp.where(up, lc, l_lo)
        v_hi, l_hi = jnp.where(dn, t, v_hi), jnp.where(dn, lc, l_hi)
        done = jnp.where((c_lo == kf) | (hi <= lo + 1), 1.0, done)
        return lo, hi, c_lo, c_hi, lo_set, hi_set, done, run, v_lo, v_hi, l_lo, l_hi

    def guess(st, midpoint):
        lo, hi, c_lo, c_hi, lo_set, hi_set, done, run, v_lo, v_hi, l_lo, l_hi = st
        both = (lo_set > 0.5) & (hi_set > 0.5)
        step = std * 0.25 * jnp.exp2(run)
        if midpoint:
            inner = 0.5 * v_lo + 0.5 * v_hi
        else:
            frac = jnp.clip((l_lo - math.log(kf - 0.5)) / jnp.maximum(l_lo - l_hi, 1e-6), 0.0, 1.0)
            inner = v_lo + (v_hi - v_lo) * frac
        t = jnp.where(both, inner,
                      jnp.where(hi_set > 0.5, v_hi - step, jnp.where(lo_set > 0.5, v_lo + step, first_guess)))
        t = jnp.where(t != t, 0.0, t)
        run = jnp.where(both, 0.0, run + 1.0)
        return _float_to_key(t), (lo, hi, c_lo, c_hi, lo_set, hi_set, done, run, v_lo, v_hi, l_lo, l_hi)

    def pending(st):
        return 1.0 - jnp.min(st[6])

    flag0 = jnp.zeros((1, LANES), F32)
    hi0 = _float_to_key(top1) + 1
    st = (jnp.full((1, LANES), _float_to_key(lowest), jnp.int32), hi0,
          n_adm, flag0, flag0, flag0, jnp.where(n_adm <= kf, 1.0, 0.0), flag0,
          jnp.full((1, LANES), lowest, F32), _key_to_float(hi0), jnp.log(n_adm),
          jnp.full((1, LANES), math.log(0.5), F32))

    def probes(st):
        for r in range(PROBES_PER_ROUND):
            key, st = guess(st, midpoint=((r + 1) % MIDPOINT_EVERY == 0))
            st = update(st, key)
        return st

    st = probes(probes(st))

    def fast_round(carry):
        st = probes(carry[2])
        return carry[0] + 1, pending(st), st

    _, _, st = lax.while_loop(lambda c: (c[0] < FAST_ROUNDS) & (c[1] > 0.5), fast_round,
                              (jnp.int32(2), pending(st), st))

    def safe_round(carry):
        p, _, st = carry
        st = update(st, st[0] + lax.shift_right_logical(st[1] - st[0], 1))
        lo, hi, done = st[0], st[1], st[6]
        below = _float_to_key(max_below(st[9]))
        hi = jnp.where(done < 0.5, jnp.clip(below + 1, lo + 1, hi), hi)
        done = jnp.where(hi <= lo + 1, 1.0, done)
        st = update((lo, hi) + st[2:6] + (done, st[7], st[8], _key_to_float(hi)) + st[10:], hi - 1)
        return p + 1, pending(st), st

    _, _, st = lax.while_loop(lambda c: (c[1] > 0.5) & (c[0] < 34), safe_round,
                              (jnp.int32(0), pending(st), st))

    thr = _key_to_float(st[0])
    has_ties = jnp.max(st[2]) > kf

    @pl.when(jnp.logical_not(has_ties))
    def _():
        def mask_chunk(i, carry):
            s_scr[rows(i), :] = jnp.where(s_scr[rows(i), :] >= thr, 0.0, NEG)
            return carry
        lax.fori_loop(0, nch, mask_chunk, 0)

    @pl.when(has_ties)
    def _():
        need = kf - count(lambda s: s > thr)

        def mask_chunk(t, run):
            i = nch - 1 - t
            s = s_scr[rows(i), :]
            tie = jnp.where(s == thr, 1.0, 0.0)
            pref = jnp.dot(ltri_ref[...], tie.astype(BF16), preferred_element_type=F32) + run
            tie_sel = jnp.where(pref <= need, tie, 0.0)
            sel = jnp.where(s > thr, 1.0, tie_sel)
            s_scr[rows(i), :] = jnp.where(sel > 0.5, 0.0, NEG)
            return run + jnp.sum(tie, axis=0, keepdims=True)
        lax.fori_loop(0, nch, mask_chunk, jnp.zeros((1, LANES), F32))

    def logits(i, with_bias):
        msk = s_scr[rows(i), :]
        out = []
        for pr in range(DSA_HEADS // 2):
            l2 = jnp.dot(k_ref[0, rows(i), pr * LANES:(pr + 1) * LANES],
                         qpad_scr[:, 2 * pr * LANES:(2 * pr + 2) * LANES],
                         preferred_element_type=F32)
            for e in range(2):
                l = l2[:, e * LANES:(e + 1) * LANES] + msk
                if with_bias:
                    far_rows = kc - 2 * Q_BLOCK
                    l = jnp.concatenate([l[:far_rows], l[far_rows:] + bias_scr[2 * pr + e]], axis=0)
                out.append(l)
        return out

    def max_step(i, ms, with_bias):
        return tuple(jnp.maximum(m, _tree(jnp.maximum, l)) for m, l in zip(ms, logits(i, with_bias)))

    @pl.when(jnp.logical_not(bound_ok))
    def _():
        ms = max_step(0, tuple(jnp.full((8, LANES), NEG, F32) for _ in range(DSA_HEADS)), True)
        ms = lax.fori_loop(1, nch, lambda i, m: max_step(i, m, False), ms)
        for h in range(DSA_HEADS):
            mx_scr[h:h + 1, :] = jnp.max(ms[h], axis=0, keepdims=True)

    mx = [mx_scr[h:h + 1, :] for h in range(DSA_HEADS)]

    def pv_step(i, accs, with_bias):
        return tuple(
            acc + jnp.dot(vt_ref[h, :, rows(i)], jnp.exp2(l - mx[h]).astype(BF16),
                          preferred_element_type=F32)
            for h, (acc, l) in enumerate(zip(accs, logits(i, with_bias))))

    accs = pv_step(0, tuple(jnp.zeros((V_ROWS, LANES), F32) for _ in range(DSA_HEADS)), True)
    accs = chunk_pairs(1, lambda i, a: pv_step(i, a, False), accs, width=4)
    outs = [acc[:HEAD_DIM] / acc[HEAD_DIM:HEAD_DIM + 1] for acc in accs]
    o_ref[...] = jnp.concatenate(outs, axis=0).T


def _dsa(q, iq, kw, k, vt, ikb, rel_bias, q_gain, k_gain, ltri, *, batch, seq, kc):
    nb = seq // Q_BLOCK
    top_k = min(TOPK_MAX, seq // 4)
    pad = kc - Q_BLOCK
    seqp = seq + pad
    kp = jnp.pad(k.reshape(batch, seq, DSA_W), ((0, 0), (pad, 0), (0, 0)))
    ikp = jnp.pad(ikb.reshape(batch, seq, LANES), ((0, 0), (pad, 0), (0, 0)))
    vtp = jnp.pad(vt.reshape(DSA_HEADS, V_ROWS, batch, seq),
                  ((0, 0), (0, 0), (0, 0), (pad, 0))).reshape(DSA_HEADS, V_ROWS, batch * seqp)
    kmax = (HEAD_DIM ** 0.5 * 1.01) * jnp.max(jnp.abs(k_gain.astype(F32)))
    qmax = (LOG2E * 1.01) * jnp.max(jnp.abs(q_gain.astype(F32)))
    norms = jnp.stack([kmax, qmax] + [jnp.zeros((), F32)] * (DSA_HEADS - 2)).reshape(1, DSA_HEADS)
    table = jnp.concatenate([rel_bias.astype(F32), norms], axis=0)
    qrow = lambda w: pl.BlockSpec((Q_BLOCK, w), lambda b, j: (b * nb + j, 0))
    return pl.pallas_call(
        functools.partial(_dsa_kernel, kc=kc, top_k=top_k),
        grid=(batch, nb),
        in_specs=[
            qrow(DSA_W), qrow(IDX_HEADS * IDX_DIM), qrow(LANES),
            pl.BlockSpec((1, seqp, DSA_W), lambda b, j: (b, 0, 0)),
            pl.BlockSpec((DSA_HEADS, V_ROWS, seqp), lambda b, j: (0, 0, b)),
            pl.BlockSpec((1, seqp, LANES), lambda b, j: (b, 0, 0)),
            pl.BlockSpec(memory_space=pltpu.SMEM),
            pl.BlockSpec(ltri.shape, lambda b, j: (0, 0)),
        ],
        out_specs=qrow(DSA_W),
        out_shape=jax.ShapeDtypeStruct((batch * seq, DSA_W), F32),
        scratch_shapes=[
            pltpu.VMEM((seqp, LANES), F32),
            pltpu.VMEM((LANES, IDX_HEADS * LANES), BF16),
            pltpu.VMEM((LANES, DSA_HEADS * LANES), BF16),
            pltpu.VMEM((DSA_HEADS, 2 * Q_BLOCK, LANES), F32),
            pltpu.VMEM((8, LANES), F32),
            pltpu.SMEM((2, DSA_HEADS), F32),
        ],
        compiler_params=_params("arbitrary", "arbitrary"),
        name="dsa",
    )(q, iq, kw, kp, vtp, ikp, table, ltri)


GDN_GROUP = 2


def _gdn_kernel(x_ref, ab_ref, cw_ref, ea_ref, eb_ref, alog_ref, dtb_ref, gain_ref, grp_ref,
                o_ref, xpad_scr, qkv_scr, gb_scr, gc_scr, gr_scr, pre_scr, st_scr, o_scr, *, tb):
    t = pl.program_id(1)
    cin = 3 * GDN_W
    n_chunks = tb // GDN_CHUNK
    pairs = range(GDN_HEADS // 2)
    lanes = lambda p: slice(p * LANES, (p + 1) * LANES)

    @pl.when(t == 0)
    def _():
        xpad_scr[0:8, :] = jnp.zeros((8, cin), F32)
        st_scr[...] = jnp.zeros_like(st_scr)

    x = x_ref[...]
    xpad_scr[8:8 + tb, :] = x[:, :cin]
    conv = jnp.zeros((tb, cin), F32)
    for jj in range(CONV_WIDTH):
        conv = conv + cw_ref[jj:jj + 1, :] * xpad_scr[pl.ds(8 - (CONV_WIDTH - 1) + jj, tb), :]
    xpad_scr[0:8, :] = x[tb - 8:tb, :cin]
    qkv = conv * _sigmoid(conv)
    grp = grp_ref[...]
    q = qkv[:, :GDN_W]
    k = qkv[:, GDN_W:2 * GDN_W]
    qkv_scr[:, :GDN_W] = q * lax.rsqrt(_group_sum(q * q, grp) + EPS) * (HEAD_DIM ** -0.5)
    qkv_scr[:, GDN_W:2 * GDN_W] = k * lax.rsqrt(_group_sum(k * k, grp) + EPS)
    qkv_scr[:, 2 * GDN_W:] = qkv[:, 2 * GDN_W:]

    ab = ab_ref[...]
    a_e = _dot_sel(ab, ea_ref[...], 3) + dtb_ref[...]
    b_e = _dot_sel(ab, eb_ref[...], 3)
    softplus = jnp.maximum(a_e, 0.0) + jnp.log(1.0 + jnp.exp(-jnp.abs(a_e)))
    gb_scr[:, :GDN_W] = -jnp.exp(alog_ref[...]) * softplus
    gb_scr[:, GDN_W:] = _sigmoid(b_e)

    r64 = lax.broadcasted_iota(jnp.int32, (GDN_CHUNK, LANES), 0)
    c64 = lax.broadcasted_iota(jnp.int32, (GDN_CHUNK, LANES), 1) % GDN_CHUNK
    causal = c64 <= r64
    strict = c64 < r64
    eye = jnp.where(c64 == r64, 1.0, 0.0)
    r128 = lax.broadcasted_iota(jnp.int32, (LANES, LANES), 0)
    c128 = lax.broadcasted_iota(jnp.int32, (LANES, LANES), 1)
    bdmask = (r128 // HEAD_DIM) == (c128 // HEAD_DIM)

    def bd(m):
        return jnp.where(bdmask, jnp.concatenate([m, m], axis=0), 0.0)

    lt_r = lax.broadcasted_iota(jnp.int32, (GDN_CHUNK, GDN_CHUNK), 0)
    lt_c = lax.broadcasted_iota(jnp.int32, (GDN_CHUNK, GDN_CHUNK), 1)
    ltri = jnp.where(lt_c <= lt_r, 1.0, 0.0)
    ones8 = jnp.ones((8, GDN_CHUNK), F32)
    up_r = lax.broadcasted_iota(jnp.int32, (GDN_CHUNK, GDN_W), 0)
    up_c = lax.broadcasted_iota(jnp.int32, (GDN_CHUNK, GDN_W), 1) % GDN_CHUNK
    upper = jnp.where(up_r <= up_c, 1.0, 0.0)
    for ci in range(n_chunks):
        g = gb_scr[ci * GDN_CHUNK:(ci + 1) * GDN_CHUNK, :GDN_W]
        gc_scr[ci * GDN_CHUNK:(ci + 1) * GDN_CHUNK, :] = _sel_dot(ltri, g, 3)
        gr_scr[ci * 8:(ci + 1) * 8, :] = _sel_dot(ones8, g * upper, 3)

    def par_body(gi, carry):
        inst = [(u, p) for u in range(GDN_GROUP) for p in pairs]
        rows = [pl.ds(pl.multiple_of((gi * GDN_GROUP + u) * GDN_CHUNK, GDN_CHUNK), GDN_CHUNK)
                for u in range(GDN_GROUP)]
        grow = [pl.ds(pl.multiple_of((gi * GDN_GROUP + u) * 8, 8), 8) for u in range(GDN_GROUP)]
        off = lambda s, p: slice(s * GDN_W + p * LANES, s * GDN_W + (p + 1) * LANES)
        qp = [qkv_scr[rows[u], off(0, p)] for u, p in inst]
        kp = [qkv_scr[rows[u], off(1, p)] for u, p in inst]
        vp = [qkv_scr[rows[u], off(2, p)] for u, p in inst]
        beta = [gb_scr[rows[u], off(1, p)] for u, p in inst]
        gc = [gc_scr[rows[u], lanes(p)] for u, p in inst]
        gr = [gr_scr[grow[u], lanes(p)][0:1, :] for u, p in inst]
        dmat = [jnp.exp(jnp.where(causal, a - b, NEG)) for a, b in zip(gc, gr)]
        kbd = [jnp.where(bdmask, jnp.concatenate([a, a], axis=0).T, 0.0) for a in kp]
        kb = [a * b for a, b in zip(kp, beta)]
        kk = [_bdot(a, b) for a, b in zip(kb, kbd)]
        qk = [_bdot(a, b) for a, b in zip(qp, kbd)]
        lm = [jnp.where(strict, a * d, 0.0) for a, d in zip(kk, dmat)]
        aintra = [a * d for a, d in zip(qk, dmat)]
        tinv = [eye - a for a in lm]
        lpow = lm
        for _ in range(5):
            lpow = [_dot3(a, bd(a)) for a in lpow]
            tinv = [a + _dot3(a, bd(b)) for a, b in zip(tinv, lpow)]
        egc = [jnp.exp(a) for a in gc]
        un = [_dot3(a, bd(v * b)) for a, v, b in zip(tinv, vp, beta)]
        wn = [_dot3(a, bd(b * e)) for a, b, e in zip(tinv, kb, egc)]
        for n, (u, p) in enumerate(inst):
            pre_scr[0, rows[u], lanes(p)] = un[n]
            pre_scr[1, rows[u], lanes(p)] = wn[n]
            pre_scr[2, rows[u], lanes(p)] = aintra[n]
            pre_scr[3, rows[u], lanes(p)] = qp[n] * egc[n]
            pre_scr[4, rows[u], lanes(p)] = kp[n] * jnp.exp(gc[n][GDN_CHUNK - 1:GDN_CHUNK, :] - gc[n])
        return carry

    lax.fori_loop(0, n_chunks // GDN_GROUP, par_body, 0)

    def rec_body(ci, carry):
        rows = pl.ds(pl.multiple_of(ci * GDN_CHUNK, GDN_CHUNK), GDN_CHUNK)
        tail = gc_scr[pl.ds(pl.multiple_of(ci * GDN_CHUNK + GDN_CHUNK - 8, 8), 8), :]
        egl = jnp.exp(tail[7:8, :])
        sbd = [st_scr[p] for p in pairs]
        un, wn, aintra, qe, kd = [[pre_scr[s, rows, lanes(p)] for p in pairs] for s in range(5)]
        ws = [_bdot(a, s) for a, s in zip(wn, sbd)]
        qs = [_bdot(a, s) for a, s in zip(qe, sbd)]
        v_new = [a - b for a, b in zip(un, ws)]
        o = [a + _bdot(b, bd(v)) for a, b, v in zip(qs, aintra, v_new)]
        upd = [_bdot(a.T, v) for a, v in zip(kd, v_new)]
        for p in pairs:
            st_scr[p] = sbd[p] * egl[:, lanes(p)] + jnp.where(bdmask, upd[p], 0.0)
            o_scr[rows, lanes(p)] = o[p]
        return carry

    lax.fori_loop(0, n_chunks, rec_body, 0)

    o = o_scr[...]
    oms = _group_sum(o * o, grp) * (1.0 / HEAD_DIM)
    z = x[:, cin:]
    o_ref[...] = o * lax.rsqrt(oms + EPS) * gain_ref[...] * (z * _sigmoid(z))


def _gdn(gx, ab, conv_w, ea, eb, alog, dtb, gain, grp, *, batch, seq, tb):
    nt = seq // tb
    cin = 3 * GDN_W
    row = lambda w: pl.BlockSpec((tb, w), lambda b, t: (b * nt + t, 0))
    full = lambda a: pl.BlockSpec(a.shape, lambda b, t: (0,) * a.ndim)
    return pl.pallas_call(
        functools.partial(_gdn_kernel, tb=tb),
        grid=(batch, nt),
        in_specs=[row(4 * GDN_W), row(LANES), full(conv_w), full(ea), full(eb), full(alog),
                  full(dtb), full(gain), full(grp)],
        out_specs=row(GDN_W),
        out_shape=jax.ShapeDtypeStruct((batch * seq, GDN_W), F32),
        scratch_shapes=[
            pltpu.VMEM((tb + 8, cin), F32),
            pltpu.VMEM((tb, cin), F32),
            pltpu.VMEM((tb, 2 * GDN_W), F32),
            pltpu.VMEM((tb, GDN_W), F32),
            pltpu.VMEM((tb // 8, GDN_W), F32),
            pltpu.VMEM((5, tb, GDN_W), F32),
            pltpu.VMEM((GDN_HEADS // 2, LANES, LANES), F32),
            pltpu.VMEM((tb, GDN_W), F32),
        ],
        compiler_params=_params("arbitrary", "arbitrary"),
        name="gdn",
    )(gx, ab, conv_w, ea, eb, alog, dtb, gain, grp)


def _memkv_kernel(mem_ref, g_ref, w_ref, kg_ref, grp_ref, kt_ref, vp_ref):
    x = mem_ref[0]
    ms = jnp.mean(x * x, axis=-1, keepdims=True)
    h = (x * lax.rsqrt(ms + EPS) * g_ref[...]).astype(BF16)
    kv = jnp.dot(h, w_ref[...], preferred_element_type=F32)
    km = kv[:, :MEM_W]
    vm = kv[:, MEM_W:]
    kms = _group_sum(km * km, grp_ref[...]) * (1.0 / HEAD_DIM)
    kt = (km * lax.rsqrt(kms + EPS) * kg_ref[...]).T
    n_mem = x.shape[0]
    top = lax.broadcasted_iota(jnp.int32, (LANES, n_mem), 0) < HEAD_DIM
    left = lax.broadcasted_iota(jnp.int32, (n_mem, LANES), 1) < HEAD_DIM
    for hh in range(MEM_HEADS):
        pr = slice((hh // 2) * LANES, (hh // 2 + 1) * LANES)
        keep_r = top if hh % 2 == 0 else jnp.logical_not(top)
        keep_c = left if hh % 2 == 0 else jnp.logical_not(left)
        kt_ref[0, hh] = jnp.where(keep_r, kt[pr, :], 0.0).astype(BF16)
        vp_ref[0, hh] = jnp.where(keep_c, vm[:, pr], 0.0).astype(BF16)


def _memkv(mem, gain, w, kg, grp):
    b, n_mem, d = mem.shape
    full = lambda a: pl.BlockSpec(a.shape, lambda i: (0,) * a.ndim)
    return pl.pallas_call(
        _memkv_kernel,
        grid=(b,),
        in_specs=[pl.BlockSpec((1, n_mem, d), lambda i: (i, 0, 0)), full(gain), full(w), full(kg),
                  full(grp)],
        out_specs=(pl.BlockSpec((1, MEM_HEADS, LANES, n_mem), lambda i: (i, 0, 0, 0)),
                   pl.BlockSpec((1, MEM_HEADS, n_mem, LANES), lambda i: (i, 0, 0, 0))),
        out_shape=(jax.ShapeDtypeStruct((b, MEM_HEADS, LANES, n_mem), BF16),
                   jax.ShapeDtypeStruct((b, MEM_HEADS, n_mem, LANES), BF16)),
        compiler_params=_params("parallel"),
        name="memkv",
    )(mem, gain, w, kg, grp)


def _memattn_kernel(q_ref, qg_ref, grp_ref, kt_ref, vp_ref, o_ref):
    q = q_ref[...]
    qms = _group_sum(q * q, grp_ref[...]) * (1.0 / HEAD_DIM)
    qn = (q * lax.rsqrt(qms + EPS) * qg_ref[...] * (HEAD_DIM ** -0.5)).astype(BF16)
    outs = []
    for pr in range(MEM_HEADS // 2):
        qp = qn[:, pr * LANES:(pr + 1) * LANES]
        acc = None
        for e in range(2):
            hh = 2 * pr + e
            l = jnp.dot(qp, kt_ref[0, hh], preferred_element_type=F32)
            l = l - jnp.max(l, axis=-1, keepdims=True)
            p = jnp.exp(l)
            p = p / jnp.sum(p, axis=-1, keepdims=True)
            o = jnp.dot(p.astype(BF16), vp_ref[0, hh], preferred_element_type=F32)
            acc = o if acc is None else acc + o
        outs.append(acc)
    o_ref[...] = jnp.concatenate(outs, axis=-1)


def _memattn(mq, qg, grp, kt, vp, *, batch, seq, tm):
    nt = seq // tm
    n_mem = kt.shape[-1]
    full = lambda a: pl.BlockSpec(a.shape, lambda b, t: (0,) * a.ndim)
    return pl.pallas_call(
        _memattn_kernel,
        grid=(batch, nt),
        in_specs=[
            pl.BlockSpec((tm, MEM_W), lambda b, t: (b * nt + t, 0)), full(qg), full(grp),
            pl.BlockSpec((1, MEM_HEADS, LANES, n_mem), lambda b, t: (b, 0, 0, 0)),
            pl.BlockSpec((1, MEM_HEADS, n_mem, LANES), lambda b, t: (b, 0, 0, 0)),
        ],
        out_specs=pl.BlockSpec((tm, MEM_W), lambda b, t: (b * nt + t, 0)),
        out_shape=jax.ShapeDtypeStruct((batch * seq, MEM_W), F32),
        compiler_params=_params("parallel", "parallel"),
        name="memattn",
    )(mq, qg, grp, kt, vp)


def _group_ones(width):
    idx = np.arange(width) // HEAD_DIM
    return jnp.asarray((idx[:, None] == idx[None, :]).astype(np.float32))


def _t5_bucket_np(n):
    max_exact = NUM_BUCKETS // 2
    nf = np.maximum(n, 1).astype(np.float32)
    large = max_exact + (np.log(nf / np.float32(max_exact)) / np.float32(math.log(MAX_DISTANCE / max_exact))
                         * (NUM_BUCKETS - max_exact)).astype(np.int32)
    large = np.minimum(large, NUM_BUCKETS - 1)
    return np.where(n < max_exact, n, large)


def _bucket_starts():
    dist = np.arange(2 * Q_BLOCK)
    bucket = _t5_bucket_np(dist)
    assert (np.diff(bucket) >= 0).all() and bucket[-1] == NUM_BUCKETS - 1
    return tuple(int(dist[bucket >= b].min()) for b in range(1, NUM_BUCKETS))


_BUCKET_STARTS = _bucket_starts()


def _cast_kernel(w_ref, o_ref):
    o_ref[...] = w_ref[...].astype(BF16)


def _to_bf16(w, *, rows):
    nl, nr, nc = w.shape
    spec = pl.BlockSpec((1, rows, nc), lambda l, r: (l, r, 0))
    return pl.pallas_call(
        _cast_kernel,
        grid=(nl, nr // rows),
        in_specs=[spec],
        out_specs=spec,
        out_shape=jax.ShapeDtypeStruct(w.shape, BF16),
        compiler_params=_params("parallel", "parallel"),
        name="cast",
    )(w)


def _pack_w_in(w):
    o = 3 * DSA_W
    iq = w[:, o:o + IDX_HEADS * IDX_DIM]
    o += IDX_HEADS * IDX_DIM
    kw = w[:, o:o + IDX_DIM + IDX_HEADS]
    o += IDX_DIM + IDX_HEADS
    g = w[:, o:o + 4 * GDN_W]
    o += 4 * GDN_W
    ab = w[:, o:o + 2 * GDN_HEADS]
    o += 2 * GDN_HEADS
    mq = w[:, o:o + MEM_W]
    pad = lambda a: jnp.pad(a, ((0, 0), (0, LANES - a.shape[1])))
    return jnp.concatenate([w[:, :3 * DSA_W], iq, pad(kw), g, pad(ab), mq], axis=1)


def _tile_heads(v, heads):
    return jnp.tile(v.astype(F32), heads).reshape(1, heads * HEAD_DIM)


def kernel(x, mem, ffn1_norm, ffn1_w_gate, ffn1_w_up, ffn1_w_down, mix_norm, w_in, dsa_q_norm, dsa_k_norm, rel_bias, gdn_conv, gdn_A_log, gdn_dt_bias, gdn_out_norm, mem_norm, w_mem_kv, mem_q_norm, mem_k_norm, w_out, ffn2_norm, ffn2_w_gate, ffn2_w_up, ffn2_w_down):
    batch, seq, d = x.shape
    depth = w_in.shape[0]
    m = batch * seq
    tm = min(512, seq)
    kc = min(512, seq)
    dff = ffn1_w_gate.shape[-1]
    tf = dff // 2 if (dff // 2) % LANES == 0 else dff

    grp_a = _group_ones(DSA_W)
    grp_m = _group_ones(MEM_W)
    ltri = jnp.asarray(np.tril(np.ones((kc, kc), np.float32))).astype(BF16)
    heads_of = np.arange(GDN_W) // HEAD_DIM
    ea = jnp.asarray((np.arange(LANES)[:, None] == heads_of[None, :]).astype(np.float32))
    eb = jnp.asarray((np.arange(LANES)[:, None] == heads_of[None, :] + GDN_HEADS).astype(np.float32))

    w_in_b = _to_bf16(w_in, rows=min(256, d))
    xf = x.reshape(m, d)
    for l in range(depth):
        xf = _ffn(xf, ffn1_norm[l], ffn1_w_gate[l].astype(BF16), ffn1_w_up[l].astype(BF16),
                  ffn1_w_down[l].astype(BF16), tm=tm, tf=tf)
        (q, k, vt, iq, ikb, kw, gx, ab, mq) = _inproj(
            xf, mix_norm[l].reshape(1, d), _pack_w_in(w_in_b[l]), grp_a,
            _tile_heads(dsa_q_norm[l], DSA_HEADS), _tile_heads(dsa_k_norm[l], DSA_HEADS), tm=tm)
        out_a = _dsa(q, iq, kw, k, vt, ikb, rel_bias, dsa_q_norm[l], dsa_k_norm[l], ltri,
                     batch=batch, seq=seq, kc=kc)
        out_b = _gdn(gx, ab, gdn_conv[l], ea, eb,
                     jnp.repeat(gdn_A_log[l].astype(F32), HEAD_DIM).reshape(1, GDN_W),
                     jnp.repeat(gdn_dt_bias[l].astype(F32), HEAD_DIM).reshape(1, GDN_W),
                     _tile_heads(gdn_out_norm[l], GDN_HEADS), grp_a, batch=batch, seq=seq, tb=tm)
        kt, vp = _memkv(mem, mem_norm[l].reshape(1, d), w_mem_kv[l].astype(BF16),
                        _tile_heads(mem_k_norm[l], MEM_HEADS), grp_m)
        out_c = _memattn(mq, _tile_heads(mem_q_norm[l], MEM_HEADS), grp_m, kt, vp,
                         batch=batch, seq=seq, tm=tm)
        wo = w_out[l].astype(BF16)
        xf = _ffn(xf, ffn2_norm[l], ffn2_w_gate[l].astype(BF16), ffn2_w_up[l].astype(BF16),
                  ffn2_w_down[l].astype(BF16), mix=(out_a, out_b, out_c),
                  mix_w=(wo[:DSA_W], wo[DSA_W:DSA_W + GDN_W], wo[DSA_W + GDN_W:]), tm=tm, tf=tf)
    return xf.reshape(batch, seq, d)
```

```python
import functools
import math

import jax
import jax.numpy as jnp
import numpy as np
from jax import lax
from jax.experimental import pallas as pl
from jax.experimental.pallas import tpu as pltpu

F32 = jnp.float32
BF16 = jnp.bfloat16

HEAD_DIM = 64
DSA_HEADS = 6
GDN_HEADS = 6
MEM_HEADS = 4
DSA_W = DSA_HEADS * HEAD_DIM
GDN_W = GDN_HEADS * HEAD_DIM
MEM_W = MEM_HEADS * HEAD_DIM
IDX_HEADS = 8
IDX_DIM = 32
TOPK_MAX = 256
Q_BLOCK = 128
GDN_CHUNK = 64
CONV_WIDTH = 4
NUM_BUCKETS = 32
MAX_DISTANCE = 128
EPS = 1e-6

LANES = 128
VMEM_LIMIT = 52 * 1024 * 1024
NEG = -1e30
LOG2E = math.log2(math.e)
V_ROWS = HEAD_DIM + 16
MAX_LOGIT_SPREAD = 80.0
PROBES_PER_ROUND = 4
MIDPOINT_EVERY = 4
FAST_ROUNDS = 4
STILL_SEARCHING = 1e9

SEG_A = 0
SEG_IQ = SEG_A + 3 * DSA_W
SEG_KW = SEG_IQ + IDX_HEADS * IDX_DIM
SEG_G = SEG_KW + LANES
SEG_AB = SEG_G + 4 * GDN_W
SEG_MQ = SEG_AB + LANES
IN_PACKED = SEG_MQ + MEM_W


def _bdot(a, b):
    return jnp.dot(a.astype(BF16), b.astype(BF16), preferred_element_type=F32)


def _split(x, terms):
    out = []
    for _ in range(terms - 1):
        hi = x.astype(BF16)
        out.append(hi)
        x = x - hi.astype(F32)
    out.append(x.astype(BF16))
    return out


def _dot3(a, b):
    ah, al = _split(a, 2)
    bh, bl = _split(b, 2)
    mm = lambda x, y: jnp.dot(x, y, preferred_element_type=F32)
    return mm(ah, bh) + (mm(ah, bl) + mm(al, bh))


def _dot_sel(a, sel, terms):
    selb = sel.astype(BF16)
    acc = None
    for piece in _split(a, terms):
        d = jnp.dot(piece, selb, preferred_element_type=F32)
        acc = d if acc is None else acc + d
    return acc


def _sel_dot(sel, b, terms):
    selb = sel.astype(BF16)
    acc = None
    for piece in _split(b, terms):
        d = jnp.dot(selb, piece, preferred_element_type=F32)
        acc = d if acc is None else acc + d
    return acc


def _group_sum(x, grp):
    return _dot_sel(x, grp, 2)


def _sigmoid(x):
    return 1.0 / (1.0 + jnp.exp(-x))


def _params(*sem):
    return pltpu.CompilerParams(dimension_semantics=sem, vmem_limit_bytes=VMEM_LIMIT)


def _ffn_kernel(*refs, n_mix):
    x_ref, g_ref, wg_ref, wu_ref, wd_ref = refs[:5]
    mix_refs = refs[5:5 + 2 * n_mix]
    o_ref, x_scr, h_scr, acc_scr = refs[5 + 2 * n_mix:]
    f = pl.program_id(1)

    @pl.when(f == 0)
    def _():
        x = x_ref[...]
        for k in range(n_mix):
            x = x + _bdot(mix_refs[k][...], mix_refs[n_mix + k][...])
        x_scr[...] = x
        ms = jnp.mean(x * x, axis=-1, keepdims=True)
        h_scr[...] = (x * lax.rsqrt(ms + EPS) * g_ref[...]).astype(BF16)
        acc_scr[...] = jnp.zeros_like(acc_scr)

    h = h_scr[...]
    a = jnp.dot(h, wg_ref[...], preferred_element_type=F32)
    u = jnp.dot(h, wu_ref[...], preferred_element_type=F32)
    z = (a * _sigmoid(a)) * u
    acc_scr[...] += jnp.dot(z.astype(BF16), wd_ref[...], preferred_element_type=F32)

    @pl.when(f == pl.num_programs(1) - 1)
    def _():
        o_ref[...] = x_scr[...] + 0.5 * acc_scr[...]


def _ffn(x, gain, wg, wu, wd, mix=(), mix_w=(), *, tm, tf):
    m, d = x.shape
    dff = wg.shape[1]
    row = lambda w: pl.BlockSpec((tm, w), lambda i, f: (i, 0))
    full = lambda a: pl.BlockSpec(a.shape, lambda i, f: (0,) * a.ndim)
    return pl.pallas_call(
        functools.partial(_ffn_kernel, n_mix=len(mix)),
        grid=(m // tm, dff // tf),
        in_specs=[
            row(d),
            pl.BlockSpec((1, d), lambda i, f: (0, 0)),
            pl.BlockSpec((d, tf), lambda i, f: (0, f)),
            pl.BlockSpec((d, tf), lambda i, f: (0, f)),
            pl.BlockSpec((tf, d), lambda i, f: (f, 0)),
        ] + [row(a.shape[1]) for a in mix] + [full(w) for w in mix_w],
        out_specs=row(d),
        out_shape=jax.ShapeDtypeStruct((m, d), F32),
        scratch_shapes=[pltpu.VMEM((tm, d), F32), pltpu.VMEM((tm, d), BF16), pltpu.VMEM((tm, d), F32)],
        compiler_params=_params("parallel", "arbitrary"),
        name="ffn",
    )(x, gain.reshape(1, d), wg, wu, wd, *mix, *mix_w)


def _inproj_kernel(x_ref, g_ref, w_ref, grp_ref, qg_ref, kg_ref,
                   q_ref, k_ref, vt_ref, iq_ref, ikb_ref, kw_ref, gx_ref, ab_ref, mq_ref):
    x = x_ref[...]
    ms = jnp.mean(x * x, axis=-1, keepdims=True)
    h = (x * lax.rsqrt(ms + EPS) * g_ref[...]).astype(BF16)
    p = jnp.dot(h, w_ref[...], preferred_element_type=F32)

    grp = grp_ref[...]
    dq = p[:, SEG_A:SEG_A + DSA_W]
    dk = p[:, SEG_A + DSA_W:SEG_A + 2 * DSA_W]
    dv = p[:, SEG_A + 2 * DSA_W:SEG_A + 3 * DSA_W]
    qms = _group_sum(dq * dq, grp) * (1.0 / HEAD_DIM)
    kms = _group_sum(dk * dk, grp) * (1.0 / HEAD_DIM)
    q_ref[...] = (dq * lax.rsqrt(qms + EPS) * qg_ref[...] * (HEAD_DIM ** -0.5 * LOG2E)).astype(BF16)
    k_ref[...] = (dk * lax.rsqrt(kms + EPS) * kg_ref[...]).astype(BF16)
    dvt = dv.T
    ones = jnp.ones((V_ROWS - HEAD_DIM, dvt.shape[1]), F32)
    for hd in range(DSA_HEADS):
        vt_ref[hd] = jnp.concatenate([dvt[hd * HEAD_DIM:(hd + 1) * HEAD_DIM], ones], axis=0).astype(BF16)

    iq_ref[...] = (p[:, SEG_IQ:SEG_IQ + IDX_HEADS * IDX_DIM] * (IDX_DIM ** -0.5)).astype(BF16)
    kw = p[:, SEG_KW:SEG_KW + LANES]
    kw_ref[...] = kw
    ikb_ref[...] = kw.astype(BF16)
    gx_ref[...] = p[:, SEG_G:SEG_G + 4 * GDN_W]
    ab_ref[...] = p[:, SEG_AB:SEG_AB + LANES]
    mq_ref[...] = p[:, SEG_MQ:SEG_MQ + MEM_W]


def _inproj(x, gain, w_packed, grp, qg, kg, *, tm):
    m, d = x.shape
    row = lambda w: pl.BlockSpec((tm, w), lambda i: (i, 0))
    full = lambda a: pl.BlockSpec(a.shape, lambda i: (0,) * a.ndim)
    out_shape = (
        jax.ShapeDtypeStruct((m, DSA_W), BF16),
        jax.ShapeDtypeStruct((m, DSA_W), BF16),
        jax.ShapeDtypeStruct((DSA_HEADS, V_ROWS, m), BF16),
        jax.ShapeDtypeStruct((m, IDX_HEADS * IDX_DIM), BF16),
        jax.ShapeDtypeStruct((m, LANES), BF16),
        jax.ShapeDtypeStruct((m, LANES), F32),
        jax.ShapeDtypeStruct((m, 4 * GDN_W), F32),
        jax.ShapeDtypeStruct((m, LANES), F32),
        jax.ShapeDtypeStruct((m, MEM_W), F32),
    )
    out_specs = (row(DSA_W), row(DSA_W), pl.BlockSpec((DSA_HEADS, V_ROWS, tm), lambda i: (0, 0, i)),
                 row(IDX_HEADS * IDX_DIM), row(LANES), row(LANES), row(4 * GDN_W),
                 row(LANES), row(MEM_W))
    return pl.pallas_call(
        _inproj_kernel,
        grid=(m // tm,),
        in_specs=[row(d), full(gain), full(w_packed), full(grp), full(qg), full(kg)],
        out_specs=out_specs,
        out_shape=out_shape,
        compiler_params=_params("parallel"),
        name="inproj",
    )(x, gain, w_packed, grp, qg, kg)


_DENORMAL_TOP = 0x007FFFFF


def _float_to_key(f):
    bits = lax.bitcast_convert_type(f, jnp.int32)
    mag = jnp.maximum((bits & jnp.int32(0x7FFFFFFF)) - _DENORMAL_TOP, 0)
    return jnp.where(bits >= 0, mag, -mag)


def _key_to_float(key):
    mag = jnp.abs(key)
    bits = jnp.where(mag > 0, mag + _DENORMAL_TOP, 0)
    return lax.bitcast_convert_type(jnp.where(key < 0, bits | jnp.int32(-2 ** 31), bits), F32)


def _upper_normal_quantile(p):
    pp = jnp.clip(jnp.minimum(p, 1.0 - p), 1e-30, 0.5)
    t = jnp.sqrt(-2.0 * jnp.log(pp))
    z = t - (2.515517 + t * (0.802853 + t * 0.010328)) / (1.0 + t * (1.432788 + t * (0.189269 + t * 0.001308)))
    return jnp.where(p <= 0.5, z, -z)


def _tree(op, x, group=8):
    parts = x.reshape(x.shape[0] // group, group, LANES)
    k = parts.shape[0]
    while k > 1:
        k //= 2
        parts = op(parts[:k], parts[k:2 * k])
    return parts[0]


def _dsa_kernel(q_ref, iq_ref, kwq_ref, k_ref, vt_ref, ik_ref, tab_ref, ltri_ref, o_ref,
                s_scr, iqt_scr, qpad_scr, bias_scr, mx_scr, brange_scr, *, kc, top_k):
    j = pl.program_id(1)
    per = kc // Q_BLOCK
    pad = kc - Q_BLOCK
    nch = j // per + 1
    q_pos = j * Q_BLOCK + lax.broadcasted_iota(jnp.int32, (1, LANES), 1)
    row_iota = lax.broadcasted_iota(jnp.int32, (kc, LANES), 0)

    def rows(i):
        return pl.ds(pl.multiple_of(j * Q_BLOCK - i * kc, Q_BLOCK), kc)

    def chunk_pairs(start, body, init, width=2, stop=None):
        count = jnp.maximum((nch if stop is None else stop) - start, 0)
        rest = count % width
        state = lax.fori_loop(0, rest, lambda r, st: body(start + r, st), init)

        def group(t, st):
            for u in range(width):
                st = body(start + rest + width * t + u, st)
            return st
        return lax.fori_loop(0, count // width, group, state)

    @pl.when((pl.program_id(0) == 0) & (j == 0))
    def _():
        r = lax.broadcasted_iota(jnp.int32, (2 * Q_BLOCK, LANES), 0)
        c = lax.broadcasted_iota(jnp.int32, (2 * Q_BLOCK, LANES), 1)
        dist = Q_BLOCK + c - r
        bucket = jnp.zeros_like(dist)
        for first in _BUCKET_STARTS:
            bucket = bucket + jnp.where(dist >= first, 1, 0)
        for h in range(DSA_HEADS):
            far = tab_ref[NUM_BUCKETS - 1, h]
            delta = jnp.zeros((2 * Q_BLOCK, LANES), F32)
            b_hi = jnp.float32(0.0)
            b_lo = jnp.float32(0.0)
            for b in range(NUM_BUCKETS - 1):
                delta = jnp.where(bucket == b, tab_ref[b, h] - far, delta)
                b_hi = jnp.maximum(b_hi, (tab_ref[b, h] - far) * LOG2E)
                b_lo = jnp.minimum(b_lo, (tab_ref[b, h] - far) * LOG2E)
            bias_scr[h] = jnp.where(dist >= 0, delta * LOG2E, 0.0)
            brange_scr[0, h] = b_hi
            brange_scr[1, h] = b_lo

    iqt = iq_ref[...].astype(F32).T
    zpad = jnp.zeros((LANES - IDX_DIM, LANES), F32)
    for h in range(IDX_HEADS):
        iqt_scr[:, h * LANES:(h + 1) * LANES] = jnp.concatenate(
            [iqt[h * IDX_DIM:(h + 1) * IDX_DIM], zpad], axis=0).astype(BF16)
    wt = kwq_ref[...].T[IDX_DIM:IDX_DIM + IDX_HEADS, :] * (IDX_HEADS ** -0.5)
    qt = q_ref[...].astype(F32).T
    half = lax.broadcasted_iota(jnp.int32, (LANES, LANES), 0) < HEAD_DIM
    for h in range(DSA_HEADS):
        pair = qt[(h // 2) * LANES:(h // 2 + 1) * LANES]
        keep = half if h % 2 == 0 else jnp.logical_not(half)
        qpad_scr[:, h * LANES:(h + 1) * LANES] = jnp.where(keep, pair, 0.0).astype(BF16)

    bound = tab_ref[NUM_BUCKETS, 0] * tab_ref[NUM_BUCKETS, 1]
    spread = jnp.float32(0.0)
    for h in range(DSA_HEADS):
        b_hi = brange_scr[0, h]
        b_lo = brange_scr[1, h]
        mx_scr[h:h + 1, :] = jnp.full((1, LANES), bound + b_hi, F32)
        spread = jnp.maximum(spread, 2.0 * bound + (b_hi - b_lo))
    bound_ok = spread <= MAX_LOGIT_SPREAD

    def score_chunk(i, carry, edge):
        d = jnp.dot(ik_ref[0, rows(i), :], iqt_scr[...], preferred_element_type=F32)
        acc = jnp.maximum(d[:, :LANES], 0.0) * wt[0:1, :]
        for h in range(1, IDX_HEADS):
            acc = acc + jnp.maximum(d[:, h * LANES:(h + 1) * LANES], 0.0) * wt[h:h + 1, :]
        sc = fin = acc
        if edge:
            key = row_iota + (j * Q_BLOCK - i * kc - pad)
            adm = jnp.where(key >= 0, key, q_pos + 1) <= q_pos
            sc = jnp.where(adm, acc, -jnp.inf)
            fin = jnp.where(adm, acc, 0.0)
        s_scr[rows(i), :] = sc
        tot, sq, top = carry
        return (tot + _tree(jnp.add, fin), sq + _tree(jnp.add, fin * fin),
                jnp.maximum(top, _tree(jnp.maximum, sc)))

    zero8 = jnp.zeros((8, LANES), F32)
    stats = score_chunk(0, (zero8, zero8, jnp.full((8, LANES), -jnp.inf, F32)), True)
    stats = lax.fori_loop(0, jnp.minimum(nch - 1, 1), lambda _, st: score_chunk(nch - 1, st, True), stats)
    tot, sq, top = chunk_pairs(1, lambda i, st: score_chunk(i, st, False), stats, width=4, stop=nch - 1)

    def count(pred):
        def body(i, cnt):
            return cnt + _tree(jnp.add, jnp.where(pred(s_scr[rows(i), :]), 1.0, 0.0))
        cnt = chunk_pairs(0, body, zero8, width=4)
        return jnp.sum(cnt, axis=0, keepdims=True)

    def max_below(t):
        def body(i, m):
            s = s_scr[rows(i), :]
            return jnp.maximum(m, _tree(jnp.maximum, jnp.where(s < t, s, -jnp.inf)))
        m = lax.fori_loop(0, nch, body, jnp.full((8, LANES), -jnp.inf, F32))
        return jnp.max(m, axis=0, keepdims=True)

    kf = float(top_k)
    n_adm = (q_pos + 1).astype(F32)
    mean = jnp.sum(tot, axis=0, keepdims=True) / n_adm
    std = jnp.sqrt(jnp.maximum(jnp.sum(sq, axis=0, keepdims=True) / n_adm - mean * mean, 0.0))
    first_guess = mean + _upper_normal_quantile(kf / n_adm) * std
    lowest = jnp.float32(-3.0e38)
    top1 = jnp.minimum(jnp.max(top, axis=0, keepdims=True), -lowest)

    def update(st, key):
        lo, hi, c_lo, c_hi, lo_set, hi_set, done, run, v_lo, v_hi, l_lo, l_hi = st
        key = jnp.clip(key, lo + 1, hi - 1)
        t = _key_to_float(key)
        c = count(lambda s: s >= t)
        lc = jnp.log(jnp.maximum(c, 0.5))
        active = done < 0.5
        up = active & (c >= kf)
        dn = active & (c < kf)
        lo, c_lo, lo_set = jnp.where(up, key, lo), jnp.where(up, c, c_lo), jnp.where(up, 1.0, lo_set)
        hi, c_hi, hi_set = jnp.where(dn, key, hi), jnp.where(dn, c, c_hi), jnp.where(dn, 1.0, hi_set)
        v_lo, l_lo = jnp.where(up, t, v_lo), jnp.where(up, lc, l_lo)
        v_hi, l_hi = jnp.where(dn, t, v_hi), jnp.where(dn, lc, l_hi)
        done = jnp.where((c_lo == kf) | (hi <= lo + 1), 1.0, done)
        return lo, hi, c_lo, c_hi, lo_set, hi_set, done, run, v_lo, v_hi, l_lo, l_hi

    def guess(st, midpoint):
        lo, hi, c_lo, c_hi, lo_set, hi_set, done, run, v_lo, v_hi, l_lo, l_hi = st
        both = (lo_set > 0.5) & (hi_set > 0.5)
        step = std * 0.25 * jnp.exp2(run)
        if midpoint:
            inner = 0.5 * v_lo + 0.5 * v_hi
        else:
            frac = jnp.clip((l_lo - math.log(kf - 0.5)) / jnp.maximum(l_lo - l_hi, 1e-6), 0.0, 1.0)
            inner = v_lo + (v_hi - v_lo) * frac
        t = jnp.where(both, inner,
                      jnp.where(hi_set > 0.5, v_hi - step, jnp.where(lo_set > 0.5, v_lo + step, first_guess)))
        t = jnp.where(t != t, 0.0, t)
        run = jnp.where(both, 0.0, run + 1.0)
        return _float_to_key(t), (lo, hi, c_lo, c_hi, lo_set, hi_set, done, run, v_lo, v_hi, l_lo, l_hi)

    def status(st):
        return jnp.max(jnp.where(st[6] < 0.5, STILL_SEARCHING, st[2]))

    flag0 = jnp.zeros((1, LANES), F32)
    hi0 = _float_to_key(top1) + 1
    st = (jnp.full((1, LANES), _float_to_key(lowest), jnp.int32), hi0,
          n_adm, flag0, flag0, flag0, jnp.where(n_adm <= kf, 1.0, 0.0), flag0,
          jnp.full((1, LANES), lowest, F32), _key_to_float(hi0), jnp.log(n_adm),
          jnp.full((1, LANES), math.log(0.5), F32))

    def probes(st):
        for r in range(PROBES_PER_ROUND):
            key, st = guess(st, midpoint=((r + 1) % MIDPOINT_EVERY == 0))
            st = update(st, key)
        return st

    st = lax.fori_loop(0, FAST_ROUNDS, lambda _, s: probes(s), st)

    def safe_round(carry):
        p, _, st = carry
        st = update(st, st[0] + lax.shift_right_logical(st[1] - st[0], 1))
        lo, hi, done = st[0], st[1], st[6]
        below = _float_to_key(max_below(st[9]))
        hi = jnp.where(done < 0.5, jnp.clip(below + 1, lo + 1, hi), hi)
        done = jnp.where(hi <= lo + 1, 1.0, done)
        st = update((lo, hi) + st[2:6] + (done, st[7], st[8], _key_to_float(hi)) + st[10:], hi - 1)
        return p + 1, status(st), st

    _, most, st = lax.while_loop(lambda c: (c[1] >= STILL_SEARCHING) & (c[0] < 34), safe_round,
                                 (jnp.int32(0), status(st), st))

    thr = _key_to_float(st[0])
    has_ties = most > kf

    @pl.when(jnp.logical_not(has_ties))
    def _():
        def mask_chunk(i, carry):
            s_scr[rows(i), :] = jnp.where(s_scr[rows(i), :] >= thr, 0.0, NEG)
            return carry
        lax.fori_loop(0, nch, mask_chunk, 0)

    @pl.when(has_ties)
    def _():
        need = kf - count(lambda s: s > thr)

        def mask_chunk(t, run):
            i = nch - 1 - t
            s = s_scr[rows(i), :]
            tie = jnp.where(s == thr, 1.0, 0.0)
            pref = jnp.dot(ltri_ref[...], tie.astype(BF16), preferred_element_type=F32) + run
            tie_sel = jnp.where(pref <= need, tie, 0.0)
            sel = jnp.where(s > thr, 1.0, tie_sel)
            s_scr[rows(i), :] = jnp.where(sel > 0.5, 0.0, NEG)
            return run + jnp.sum(tie, axis=0, keepdims=True)
        lax.fori_loop(0, nch, mask_chunk, jnp.zeros((1, LANES), F32))

    def logits(i, with_bias):
        msk = s_scr[rows(i), :]
        out = []
        for pr in range(DSA_HEADS // 2):
            l2 = jnp.dot(k_ref[0, rows(i), pr * LANES:(pr + 1) * LANES],
                         qpad_scr[:, 2 * pr * LANES:(2 * pr + 2) * LANES],
                         preferred_element_type=F32)
            for e in range(2):
                l = l2[:, e * LANES:(e + 1) * LANES] + msk
                if with_bias:
                    far_rows = kc - 2 * Q_BLOCK
                    l = jnp.concatenate([l[:far_rows], l[far_rows:] + bias_scr[2 * pr + e]], axis=0)
                out.append(l)
        return out

    def max_step(i, ms, with_bias):
        return tuple(jnp.maximum(m, _tree(jnp.maximum, l)) for m, l in zip(ms, logits(i, with_bias)))

    @pl.when(jnp.logical_not(bound_ok))
    def _():
        ms = max_step(0, tuple(jnp.full((8, LANES), NEG, F32) for _ in range(DSA_HEADS)), True)
        ms = lax.fori_loop(1, nch, lambda i, m: max_step(i, m, False), ms)
        for h in range(DSA_HEADS):
            mx_scr[h:h + 1, :] = jnp.max(ms[h], axis=0, keepdims=True)

    mx = [mx_scr[h:h + 1, :] for h in range(DSA_HEADS)]

    def pv_step(i, accs, with_bias):
        return tuple(
            acc + jnp.dot(vt_ref[h, :, rows(i)], jnp.exp2(l - mx[h]).astype(BF16),
                          preferred_element_type=F32)
            for h, (acc, l) in enumerate(zip(accs, logits(i, with_bias))))

    accs = pv_step(0, tuple(jnp.zeros((V_ROWS, LANES), F32) for _ in range(DSA_HEADS)), True)
    accs = chunk_pairs(1, lambda i, a: pv_step(i, a, False), accs, width=4)
    outs = [acc[:HEAD_DIM] / acc[HEAD_DIM:HEAD_DIM + 1] for acc in accs]
    o_ref[...] = jnp.concatenate(outs, axis=0).T


def _dsa(q, iq, kw, k, vt, ikb, rel_bias, q_gain, k_gain, ltri, *, batch, seq, kc):
    nb = seq // Q_BLOCK
    top_k = min(TOPK_MAX, seq // 4)
    pad = kc - Q_BLOCK
    seqp = seq + pad
    kp = jnp.pad(k.reshape(batch, seq, DSA_W), ((0, 0), (pad, 0), (0, 0)))
    ikp = jnp.pad(ikb.reshape(batch, seq, LANES), ((0, 0), (pad, 0), (0, 0)))
    vtp = jnp.pad(vt.reshape(DSA_HEADS, V_ROWS, batch, seq),
                  ((0, 0), (0, 0), (0, 0), (pad, 0))).reshape(DSA_HEADS, V_ROWS, batch * seqp)
    kmax = (HEAD_DIM ** 0.5 * 1.01) * jnp.max(jnp.abs(k_gain.astype(F32)))
    qmax = (LOG2E * 1.01) * jnp.max(jnp.abs(q_gain.astype(F32)))
    norms = jnp.stack([kmax, qmax] + [jnp.zeros((), F32)] * (DSA_HEADS - 2)).reshape(1, DSA_HEADS)
    table = jnp.concatenate([rel_bias.astype(F32), norms], axis=0)
    qrow = lambda w: pl.BlockSpec((Q_BLOCK, w), lambda b, j: (b * nb + j, 0))
    return pl.pallas_call(
        functools.partial(_dsa_kernel, kc=kc, top_k=top_k),
        grid=(batch, nb),
        in_specs=[
            qrow(DSA_W), qrow(IDX_HEADS * IDX_DIM), qrow(LANES),
            pl.BlockSpec((1, seqp, DSA_W), lambda b, j: (b, 0, 0)),
            pl.BlockSpec((DSA_HEADS, V_ROWS, seqp), lambda b, j: (0, 0, b)),
            pl.BlockSpec((1, seqp, LANES), lambda b, j: (b, 0, 0)),
            pl.BlockSpec(memory_space=pltpu.SMEM),
            pl.BlockSpec(ltri.shape, lambda b, j: (0, 0)),
        ],
        out_specs=qrow(DSA_W),
        out_shape=jax.ShapeDtypeStruct((batch * seq, DSA_W), F32),
        scratch_shapes=[
            pltpu.VMEM((seqp, LANES), F32),
            pltpu.VMEM((LANES, IDX_HEADS * LANES), BF16),
            pltpu.VMEM((LANES, DSA_HEADS * LANES), BF16),
            pltpu.VMEM((DSA_HEADS, 2 * Q_BLOCK, LANES), F32),
            pltpu.VMEM((8, LANES), F32),
            pltpu.SMEM((2, DSA_HEADS), F32),
        ],
        compiler_params=_params("arbitrary", "arbitrary"),
        name="dsa",
    )(q, iq, kw, kp, vtp, ikp, table, ltri)


GDN_GROUP = 2


def _gdn_kernel(x_ref, ab_ref, cw_ref, ea_ref, eb_ref, alog_ref, dtb_ref, gain_ref, grp_ref,
                o_ref, xpad_scr, qkv_scr, gb_scr, gc_scr, gr_scr, pre_scr, st_scr, o_scr, *, tb):
    t = pl.program_id(1)
    cin = 3 * GDN_W
    n_chunks = tb // GDN_CHUNK
    pairs = range(GDN_HEADS // 2)
    lanes = lambda p: slice(p * LANES, (p + 1) * LANES)

    @pl.when(t == 0)
    def _():
        xpad_scr[0:8, :] = jnp.zeros((8, cin), F32)
        st_scr[...] = jnp.zeros_like(st_scr)

    x = x_ref[...]
    xpad_scr[8:8 + tb, :] = x[:, :cin]
    conv = jnp.zeros((tb, cin), F32)
    for jj in range(CONV_WIDTH):
        conv = conv + cw_ref[jj:jj + 1, :] * xpad_scr[pl.ds(8 - (CONV_WIDTH - 1) + jj, tb), :]
    xpad_scr[0:8, :] = x[tb - 8:tb, :cin]
    qkv = conv * _sigmoid(conv)
    grp = grp_ref[...]
    q = qkv[:, :GDN_W]
    k = qkv[:, GDN_W:2 * GDN_W]
    qkv_scr[:, :GDN_W] = q * lax.rsqrt(_group_sum(q * q, grp) + EPS) * (HEAD_DIM ** -0.5)
    qkv_scr[:, GDN_W:2 * GDN_W] = k * lax.rsqrt(_group_sum(k * k, grp) + EPS)
    qkv_scr[:, 2 * GDN_W:] = qkv[:, 2 * GDN_W:]

    ab = ab_ref[...]
    a_e = _dot_sel(ab, ea_ref[...], 3) + dtb_ref[...]
    b_e = _dot_sel(ab, eb_ref[...], 3)
    softplus = jnp.maximum(a_e, 0.0) + jnp.log(1.0 + jnp.exp(-jnp.abs(a_e)))
    gb_scr[:, :GDN_W] = -jnp.exp(alog_ref[...]) * softplus
    gb_scr[:, GDN_W:] = _sigmoid(b_e)

    r64 = lax.broadcasted_iota(jnp.int32, (GDN_CHUNK, LANES), 0)
    c64 = lax.broadcasted_iota(jnp.int32, (GDN_CHUNK, LANES), 1) % GDN_CHUNK
    causal = c64 <= r64
    strict = c64 < r64
    eye = jnp.where(c64 == r64, 1.0, 0.0)
    r128 = lax.broadcasted_iota(jnp.int32, (LANES, LANES), 0)
    c128 = lax.broadcasted_iota(jnp.int32, (LANES, LANES), 1)
    bdmask = (r128 // HEAD_DIM) == (c128 // HEAD_DIM)

    def bd(m):
        return jnp.where(bdmask, jnp.concatenate([m, m], axis=0), 0.0)

    lt_r = lax.broadcasted_iota(jnp.int32, (GDN_CHUNK, GDN_CHUNK), 0)
    lt_c = lax.broadcasted_iota(jnp.int32, (GDN_CHUNK, GDN_CHUNK), 1)
    ltri = jnp.where(lt_c <= lt_r, 1.0, 0.0)
    ones8 = jnp.ones((8, GDN_CHUNK), F32)
    up_r = lax.broadcasted_iota(jnp.int32, (GDN_CHUNK, GDN_W), 0)
    up_c = lax.broadcasted_iota(jnp.int32, (GDN_CHUNK, GDN_W), 1) % GDN_CHUNK
    upper = jnp.where(up_r <= up_c, 1.0, 0.0)
    for ci in range(n_chunks):
        g = gb_scr[ci * GDN_CHUNK:(ci + 1) * GDN_CHUNK, :GDN_W]
        gc_scr[ci * GDN_CHUNK:(ci + 1) * GDN_CHUNK, :] = _sel_dot(ltri, g, 3)
        gr_scr[ci * 8:(ci + 1) * 8, :] = _sel_dot(ones8, g * upper, 3)

    def par_body(gi, carry):
        inst = [(u, p) for u in range(GDN_GROUP) for p in pairs]
        rows = [pl.ds(pl.multiple_of((gi * GDN_GROUP + u) * GDN_CHUNK, GDN_CHUNK), GDN_CHUNK)
                for u in range(GDN_GROUP)]
        grow = [pl.ds(pl.multiple_of((gi * GDN_GROUP + u) * 8, 8), 8) for u in range(GDN_GROUP)]
        off = lambda s, p: slice(s * GDN_W + p * LANES, s * GDN_W + (p + 1) * LANES)
        qp = [qkv_scr[rows[u], off(0, p)] for u, p in inst]
        kp = [qkv_scr[rows[u], off(1, p)] for u, p in inst]
        vp = [qkv_scr[rows[u], off(2, p)] for u, p in inst]
        beta = [gb_scr[rows[u], off(1, p)] for u, p in inst]
        gc = [gc_scr[rows[u], lanes(p)] for u, p in inst]
        gr = [gr_scr[grow[u], lanes(p)][0:1, :] for u, p in inst]
        dmat = [jnp.exp(jnp.where(causal, a - b, NEG)) for a, b in zip(gc, gr)]
        kbd = [jnp.where(bdmask, jnp.concatenate([a, a], axis=0).T, 0.0) for a in kp]
        kb = [a * b for a, b in zip(kp, beta)]
        kk = [_bdot(a, b) for a, b in zip(kb, kbd)]
        qk = [_bdot(a, b) for a, b in zip(qp, kbd)]
        lm = [jnp.where(strict, a * d, 0.0) for a, d in zip(kk, dmat)]
        aintra = [a * d for a, d in zip(qk, dmat)]
        tinv = [eye - a for a in lm]
        lpow = lm
        for _ in range(5):
            lpow = [_dot3(a, bd(a)) for a in lpow]
            tinv = [a + _dot3(a, bd(b)) for a, b in zip(tinv, lpow)]
        egc = [jnp.exp(a) for a in gc]
        un = [_dot3(a, bd(v * b)) for a, v, b in zip(tinv, vp, beta)]
        wn = [_dot3(a, bd(b * e)) for a, b, e in zip(tinv, kb, egc)]
        for n, (u, p) in enumerate(inst):
            pre_scr[0, rows[u], lanes(p)] = un[n]
            pre_scr[1, rows[u], lanes(p)] = wn[n]
            pre_scr[2, rows[u], lanes(p)] = aintra[n]
            pre_scr[3, rows[u], lanes(p)] = qp[n] * egc[n]
            pre_scr[4, rows[u], lanes(p)] = kp[n] * jnp.exp(gc[n][GDN_CHUNK - 1:GDN_CHUNK, :] - gc[n])
        return carry

    lax.fori_loop(0, n_chunks // GDN_GROUP, par_body, 0)

    def rec_body(ci, carry):
        rows = pl.ds(pl.multiple_of(ci * GDN_CHUNK, GDN_CHUNK), GDN_CHUNK)
        tail = gc_scr[pl.ds(pl.multiple_of(ci * GDN_CHUNK + GDN_CHUNK - 8, 8), 8), :]
        egl = jnp.exp(tail[7:8, :])
        sbd = [st_scr[p] for p in pairs]
        un, wn, aintra, qe, kd = [[pre_scr[s, rows, lanes(p)] for p in pairs] for s in range(5)]
        ws = [_bdot(a, s) for a, s in zip(wn, sbd)]
        qs = [_bdot(a, s) for a, s in zip(qe, sbd)]
        v_new = [a - b for a, b in zip(un, ws)]
        o = [a + _bdot(b, bd(v)) for a, b, v in zip(qs, aintra, v_new)]
        upd = [_bdot(a.T, v) for a, v in zip(kd, v_new)]
        for p in pairs:
            st_scr[p] = sbd[p] * egl[:, lanes(p)] + jnp.where(bdmask, upd[p], 0.0)
            o_scr[rows, lanes(p)] = o[p]
        return carry

    lax.fori_loop(0, n_chunks, rec_body, 0)

    o = o_scr[...]
    oms = _group_sum(o * o, grp) * (1.0 / HEAD_DIM)
    z = x[:, cin:]
    o_ref[...] = o * lax.rsqrt(oms + EPS) * gain_ref[...] * (z * _sigmoid(z))


def _gdn(gx, ab, conv_w, ea, eb, alog, dtb, gain, grp, *, batch, seq, tb):
    nt = seq // tb
    cin = 3 * GDN_W
    row = lambda w: pl.BlockSpec((tb, w), lambda b, t: (b * nt + t, 0))
    full = lambda a: pl.BlockSpec(a.shape, lambda b, t: (0,) * a.ndim)
    return pl.pallas_call(
        functools.partial(_gdn_kernel, tb=tb),
        grid=(batch, nt),
        in_specs=[row(4 * GDN_W), row(LANES), full(conv_w), full(ea), full(eb), full(alog),
                  full(dtb), full(gain), full(grp)],
        out_specs=row(GDN_W),
        out_shape=jax.ShapeDtypeStruct((batch * seq, GDN_W), F32),
        scratch_shapes=[
            pltpu.VMEM((tb + 8, cin), F32),
            pltpu.VMEM((tb, cin), F32),
            pltpu.VMEM((tb, 2 * GDN_W), F32),
            pltpu.VMEM((tb, GDN_W), F32),
            pltpu.VMEM((tb // 8, GDN_W), F32),
            pltpu.VMEM((5, tb, GDN_W), F32),
            pltpu.VMEM((GDN_HEADS // 2, LANES, LANES), F32),
            pltpu.VMEM((tb, GDN_W), F32),
        ],
        compiler_params=_params("arbitrary", "arbitrary"),
        name="gdn",
    )(gx, ab, conv_w, ea, eb, alog, dtb, gain, grp)


def _memkv_kernel(mem_ref, g_ref, w_ref, kg_ref, grp_ref, kt_ref, vp_ref):
    x = mem_ref[0]
    ms = jnp.mean(x * x, axis=-1, keepdims=True)
    h = (x * lax.rsqrt(ms + EPS) * g_ref[...]).astype(BF16)
    kv = jnp.dot(h, w_ref[...], preferred_element_type=F32)
    km = kv[:, :MEM_W]
    vm = kv[:, MEM_W:]
    kms = _group_sum(km * km, grp_ref[...]) * (1.0 / HEAD_DIM)
    kt = (km * lax.rsqrt(kms + EPS) * kg_ref[...]).T
    n_mem = x.shape[0]
    top = lax.broadcasted_iota(jnp.int32, (LANES, n_mem), 0) < HEAD_DIM
    left = lax.broadcasted_iota(jnp.int32, (n_mem, LANES), 1) < HEAD_DIM
    for hh in range(MEM_HEADS):
        pr = slice((hh // 2) * LANES, (hh // 2 + 1) * LANES)
        keep_r = top if hh % 2 == 0 else jnp.logical_not(top)
        keep_c = left if hh % 2 == 0 else jnp.logical_not(left)
        kt_ref[0, hh] = jnp.where(keep_r, kt[pr, :], 0.0).astype(BF16)
        vp_ref[0, hh] = jnp.where(keep_c, vm[:, pr], 0.0).astype(BF16)


def _memkv(mem, gain, w, kg, grp):
    b, n_mem, d = mem.shape
    full = lambda a: pl.BlockSpec(a.shape, lambda i: (0,) * a.ndim)
    return pl.pallas_call(
        _memkv_kernel,
        grid=(b,),
        in_specs=[pl.BlockSpec((1, n_mem, d), lambda i: (i, 0, 0)), full(gain), full(w), full(kg),
                  full(grp)],
        out_specs=(pl.BlockSpec((1, MEM_HEADS, LANES, n_mem), lambda i: (i, 0, 0, 0)),
                   pl.BlockSpec((1, MEM_HEADS, n_mem, LANES), lambda i: (i, 0, 0, 0))),
        out_shape=(jax.ShapeDtypeStruct((b, MEM_HEADS, LANES, n_mem), BF16),
                   jax.ShapeDtypeStruct((b, MEM_HEADS, n_mem, LANES), BF16)),
        compiler_params=_params("parallel"),
        name="memkv",
    )(mem, gain, w, kg, grp)


def _memattn_kernel(q_ref, qg_ref, grp_ref, kt_ref, vp_ref, o_ref):
    q = q_ref[...]
    qms = _group_sum(q * q, grp_ref[...]) * (1.0 / HEAD_DIM)
    qn = (q * lax.rsqrt(qms + EPS) * qg_ref[...] * (HEAD_DIM ** -0.5)).astype(BF16)
    outs = []
    for pr in range(MEM_HEADS // 2):
        qp = qn[:, pr * LANES:(pr + 1) * LANES]
        acc = None
        for e in range(2):
            hh = 2 * pr + e
            l = jnp.dot(qp, kt_ref[0, hh], preferred_element_type=F32)
            l = l - jnp.max(l, axis=-1, keepdims=True)
            p = jnp.exp(l)
            p = p / jnp.sum(p, axis=-1, keepdims=True)
            o = jnp.dot(p.astype(BF16), vp_ref[0, hh], preferred_element_type=F32)
            acc = o if acc is None else acc + o
        outs.append(acc)
    o_ref[...] = jnp.concatenate(outs, axis=-1)


def _memattn(mq, qg, grp, kt, vp, *, batch, seq, tm):
    nt = seq // tm
    n_mem = kt.shape[-1]
    full = lambda a: pl.BlockSpec(a.shape, lambda b, t: (0,) * a.ndim)
    return pl.pallas_call(
        _memattn_kernel,
        grid=(batch, nt),
        in_specs=[
            pl.BlockSpec((tm, MEM_W), lambda b, t: (b * nt + t, 0)), full(qg), full(grp),
            pl.BlockSpec((1, MEM_HEADS, LANES, n_mem), lambda b, t: (b, 0, 0, 0)),
            pl.BlockSpec((1, MEM_HEADS, n_mem, LANES), lambda b, t: (b, 0, 0, 0)),
        ],
        out_specs=pl.BlockSpec((tm, MEM_W), lambda b, t: (b * nt + t, 0)),
        out_shape=jax.ShapeDtypeStruct((batch * seq, MEM_W), F32),
        compiler_params=_params("parallel", "parallel"),
        name="memattn",
    )(mq, qg, grp, kt, vp)


def _group_ones(width):
    idx = np.arange(width) // HEAD_DIM
    return jnp.asarray((idx[:, None] == idx[None, :]).astype(np.float32))


def _t5_bucket_np(n):
    max_exact = NUM_BUCKETS // 2
    nf = np.maximum(n, 1).astype(np.float32)
    large = max_exact + (np.log(nf / np.float32(max_exact)) / np.float32(math.log(MAX_DISTANCE / max_exact))
                         * (NUM_BUCKETS - max_exact)).astype(np.int32)
    large = np.minimum(large, NUM_BUCKETS - 1)
    return np.where(n < max_exact, n, large)


def _bucket_starts():
    dist = np.arange(2 * Q_BLOCK)
    bucket = _t5_bucket_np(dist)
    assert (np.diff(bucket) >= 0).all() and bucket[-1] == NUM_BUCKETS - 1
    return tuple(int(dist[bucket >= b].min()) for b in range(1, NUM_BUCKETS))


_BUCKET_STARTS = _bucket_starts()


def _cast_kernel(w_ref, o_ref):
    o_ref[...] = w_ref[...].astype(BF16)


def _to_bf16(w, *, rows):
    nl, nr, nc = w.shape
    spec = pl.BlockSpec((1, rows, nc), lambda l, r: (l, r, 0))
    return pl.pallas_call(
        _cast_kernel,
        grid=(nl, nr // rows),
        in_specs=[spec],
        out_specs=spec,
        out_shape=jax.ShapeDtypeStruct(w.shape, BF16),
        compiler_params=_params("parallel", "parallel"),
        name="cast",
    )(w)


def _pack_w_in(w):
    o = 3 * DSA_W
    iq = w[:, o:o + IDX_HEADS * IDX_DIM]
    o += IDX_HEADS * IDX_DIM
    kw = w[:, o:o + IDX_DIM + IDX_HEADS]
    o += IDX_DIM + IDX_HEADS
    g = w[:, o:o + 4 * GDN_W]
    o += 4 * GDN_W
    ab = w[:, o:o + 2 * GDN_HEADS]
    o += 2 * GDN_HEADS
    mq = w[:, o:o + MEM_W]
    pad = lambda a: jnp.pad(a, ((0, 0), (0, LANES - a.shape[1])))
    return jnp.concatenate([w[:, :3 * DSA_W], iq, pad(kw), g, pad(ab), mq], axis=1)


def _tile_heads(v, heads):
    return jnp.tile(v.astype(F32), heads).reshape(1, heads * HEAD_DIM)


def kernel(x, mem, ffn1_norm, ffn1_w_gate, ffn1_w_up, ffn1_w_down, mix_norm, w_in, dsa_q_norm, dsa_k_norm, rel_bias, gdn_conv, gdn_A_log, gdn_dt_bias, gdn_out_norm, mem_norm, w_mem_kv, mem_q_norm, mem_k_norm, w_out, ffn2_norm, ffn2_w_gate, ffn2_w_up, ffn2_w_down):
    batch, seq, d = x.shape
    depth = w_in.shape[0]
    m = batch * seq
    tm = min(512, seq)
    kc = min(512, seq)
    dff = ffn1_w_gate.shape[-1]
    tf = dff // 2 if (dff // 2) % LANES == 0 else dff

    grp_a = _group_ones(DSA_W)
    grp_m = _group_ones(MEM_W)
    ltri = jnp.asarray(np.tril(np.ones((kc, kc), np.float32))).astype(BF16)
    heads_of = np.arange(GDN_W) // HEAD_DIM
    ea = jnp.asarray((np.arange(LANES)[:, None] == heads_of[None, :]).astype(np.float32))
    eb = jnp.asarray((np.arange(LANES)[:, None] == heads_of[None, :] + GDN_HEADS).astype(np.float32))

    w_in_b = _to_bf16(w_in, rows=min(256, d))
    xf = x.reshape(m, d)
    for l in range(depth):
        xf = _ffn(xf, ffn1_norm[l], ffn1_w_gate[l].astype(BF16), ffn1_w_up[l].astype(BF16),
                  ffn1_w_down[l].astype(BF16), tm=tm, tf=tf)
        (q, k, vt, iq, ikb, kw, gx, ab, mq) = _inproj(
            xf, mix_norm[l].reshape(1, d), _pack_w_in(w_in_b[l]), grp_a,
            _tile_heads(dsa_q_norm[l], DSA_HEADS), _tile_heads(dsa_k_norm[l], DSA_HEADS), tm=tm)
        out_a = _dsa(q, iq, kw, k, vt, ikb, rel_bias, dsa_q_norm[l], dsa_k_norm[l], ltri,
                     batch=batch, seq=seq, kc=kc)
        out_b = _gdn(gx, ab, gdn_conv[l], ea, eb,
                     jnp.repeat(gdn_A_log[l].astype(F32), HEAD_DIM).reshape(1, GDN_W),
                     jnp.repeat(gdn_dt_bias[l].astype(F32), HEAD_DIM).reshape(1, GDN_W),
                     _tile_heads(gdn_out_norm[l], GDN_HEADS), grp_a, batch=batch, seq=seq, tb=tm)
        kt, vp = _memkv(mem, mem_norm[l].reshape(1, d), w_mem_kv[l].astype(BF16),
                        _tile_heads(mem_k_norm[l], MEM_HEADS), grp_m)
        out_c = _memattn(mq, _tile_heads(mem_q_norm[l], MEM_HEADS), grp_m, kt, vp,
                         batch=batch, seq=seq, tm=tm)
        wo = w_out[l].astype(BF16)
        xf = _ffn(xf, ffn2_norm[l], ffn2_w_gate[l].astype(BF16), ffn2_w_up[l].astype(BF16),
                  ffn2_w_down[l].astype(BF16), mix=(out_a, out_b, out_c),
                  mix_w=(wo[:DSA_W], wo[DSA_W:DSA_W + GDN_W], wo[DSA_W + GDN_W:]), tm=tm, tf=tf)
    return xf.reshape(batch, seq, d)
```

```python
import functools
import math

import jax
import jax.numpy as jnp
import numpy as np
from jax import lax
from jax.experimental import pallas as pl
from jax.experimental.pallas import tpu as pltpu

F32 = jnp.float32
BF16 = jnp.bfloat16

HEAD_DIM = 64
DSA_HEADS = 6
GDN_HEADS = 6
MEM_HEADS = 4
DSA_W = DSA_HEADS * HEAD_DIM
GDN_W = GDN_HEADS * HEAD_DIM
MEM_W = MEM_HEADS * HEAD_DIM
IDX_HEADS = 8
IDX_DIM = 32
TOPK_MAX = 256
Q_BLOCK = 128
GDN_CHUNK = 64
CONV_WIDTH = 4
NUM_BUCKETS = 32
MAX_DISTANCE = 128
EPS = 1e-6

LANES = 128
VMEM_LIMIT = 52 * 1024 * 1024
NEG = -1e30
LOG2E = math.log2(math.e)
V_ROWS = HEAD_DIM + 16
MAX_LOGIT_SPREAD = 80.0
PROBES_PER_ROUND = 4
MIDPOINT_EVERY = 4
FAST_ROUNDS = 3
STILL_SEARCHING = 1e9

SEG_A = 0
SEG_IQ = SEG_A + 3 * DSA_W
SEG_KW = SEG_IQ + IDX_HEADS * IDX_DIM
SEG_G = SEG_KW + LANES
SEG_AB = SEG_G + 4 * GDN_W
SEG_MQ = SEG_AB + LANES
IN_PACKED = SEG_MQ + MEM_W


def _bdot(a, b):
    return jnp.dot(a.astype(BF16), b.astype(BF16), preferred_element_type=F32)


def _split(x, terms):
    out = []
    for _ in range(terms - 1):
        hi = x.astype(BF16)
        out.append(hi)
        x = x - hi.astype(F32)
    out.append(x.astype(BF16))
    return out


def _dot3(a, b):
    ah, al = _split(a, 2)
    bh, bl = _split(b, 2)
    mm = lambda x, y: jnp.dot(x, y, preferred_element_type=F32)
    return mm(ah, bh) + (mm(ah, bl) + mm(al, bh))


def _dot_sel(a, sel, terms):
    selb = sel.astype(BF16)
    acc = None
    for piece in _split(a, terms):
        d = jnp.dot(piece, selb, preferred_element_type=F32)
        acc = d if acc is None else acc + d
    return acc


def _sel_dot(sel, b, terms):
    selb = sel.astype(BF16)
    acc = None
    for piece in _split(b, terms):
        d = jnp.dot(selb, piece, preferred_element_type=F32)
        acc = d if acc is None else acc + d
    return acc


def _group_sum(x, grp):
    return _dot_sel(x, grp, 2)


def _sigmoid(x):
    return 1.0 / (1.0 + jnp.exp(-x))


def _params(*sem):
    return pltpu.CompilerParams(dimension_semantics=sem, vmem_limit_bytes=VMEM_LIMIT)


def _ffn_kernel(*refs, n_mix):
    x_ref, g_ref, wg_ref, wu_ref, wd_ref = refs[:5]
    mix_refs = refs[5:5 + 2 * n_mix]
    o_ref, x_scr, h_scr, acc_scr = refs[5 + 2 * n_mix:]
    f = pl.program_id(1)

    @pl.when(f == 0)
    def _():
        x = x_ref[...]
        for k in range(n_mix):
            x = x + _bdot(mix_refs[k][...], mix_refs[n_mix + k][...])
        x_scr[...] = x
        ms = jnp.mean(x * x, axis=-1, keepdims=True)
        h_scr[...] = (x * lax.rsqrt(ms + EPS) * g_ref[...]).astype(BF16)
        acc_scr[...] = jnp.zeros_like(acc_scr)

    h = h_scr[...]
    a = jnp.dot(h, wg_ref[...], preferred_element_type=F32)
    u = jnp.dot(h, wu_ref[...], preferred_element_type=F32)
    z = (a * _sigmoid(a)) * u
    acc_scr[...] += jnp.dot(z.astype(BF16), wd_ref[...], preferred_element_type=F32)

    @pl.when(f == pl.num_programs(1) - 1)
    def _():
        o_ref[...] = x_scr[...] + 0.5 * acc_scr[...]


def _ffn(x, gain, wg, wu, wd, mix=(), mix_w=(), *, tm, tf):
    m, d = x.shape
    dff = wg.shape[1]
    row = lambda w: pl.BlockSpec((tm, w), lambda i, f: (i, 0))
    full = lambda a: pl.BlockSpec(a.shape, lambda i, f: (0,) * a.ndim)
    return pl.pallas_call(
        functools.partial(_ffn_kernel, n_mix=len(mix)),
        grid=(m // tm, dff // tf),
        in_specs=[
            row(d),
            pl.BlockSpec((1, d), lambda i, f: (0, 0)),
            pl.BlockSpec((d, tf), lambda i, f: (0, f)),
            pl.BlockSpec((d, tf), lambda i, f: (0, f)),
            pl.BlockSpec((tf, d), lambda i, f: (f, 0)),
        ] + [row(a.shape[1]) for a in mix] + [full(w) for w in mix_w],
        out_specs=row(d),
        out_shape=jax.ShapeDtypeStruct((m, d), F32),
        scratch_shapes=[pltpu.VMEM((tm, d), F32), pltpu.VMEM((tm, d), BF16), pltpu.VMEM((tm, d), F32)],
        compiler_params=_params("parallel", "arbitrary"),
        name="ffn",
    )(x, gain.reshape(1, d), wg, wu, wd, *mix, *mix_w)


def _inproj_kernel(x_ref, g_ref, w_ref, grp_ref, qg_ref, kg_ref,
                   q_ref, k_ref, vt_ref, iq_ref, ikb_ref, kw_ref, gx_ref, ab_ref, mq_ref):
    x = x_ref[...]
    ms = jnp.mean(x * x, axis=-1, keepdims=True)
    h = (x * lax.rsqrt(ms + EPS) * g_ref[...]).astype(BF16)
    p = jnp.dot(h, w_ref[...], preferred_element_type=F32)

    grp = grp_ref[...]
    dq = p[:, SEG_A:SEG_A + DSA_W]
    dk = p[:, SEG_A + DSA_W:SEG_A + 2 * DSA_W]
    dv = p[:, SEG_A + 2 * DSA_W:SEG_A + 3 * DSA_W]
    qms = _group_sum(dq * dq, grp) * (1.0 / HEAD_DIM)
    kms = _group_sum(dk * dk, grp) * (1.0 / HEAD_DIM)
    q_ref[...] = (dq * lax.rsqrt(qms + EPS) * qg_ref[...] * (HEAD_DIM ** -0.5 * LOG2E)).astype(BF16)
    k_ref[...] = (dk * lax.rsqrt(kms + EPS) * kg_ref[...]).astype(BF16)
    dvt = dv.T
    ones = jnp.ones((V_ROWS - HEAD_DIM, dvt.shape[1]), F32)
    for hd in range(DSA_HEADS):
        vt_ref[hd] = jnp.concatenate([dvt[hd * HEAD_DIM:(hd + 1) * HEAD_DIM], ones], axis=0).astype(BF16)

    iq_ref[...] = (p[:, SEG_IQ:SEG_IQ + IDX_HEADS * IDX_DIM] * (IDX_DIM ** -0.5)).astype(BF16)
    kw = p[:, SEG_KW:SEG_KW + LANES]
    kw_ref[...] = kw
    ikb_ref[...] = kw.astype(BF16)
    gx_ref[...] = p[:, SEG_G:SEG_G + 4 * GDN_W]
    ab_ref[...] = p[:, SEG_AB:SEG_AB + LANES]
    mq_ref[...] = p[:, SEG_MQ:SEG_MQ + MEM_W]


def _inproj(x, gain, w_packed, grp, qg, kg, *, tm):
    m, d = x.shape
    row = lambda w: pl.BlockSpec((tm, w), lambda i: (i, 0))
    full = lambda a: pl.BlockSpec(a.shape, lambda i: (0,) * a.ndim)
    out_shape = (
        jax.ShapeDtypeStruct((m, DSA_W), BF16),
        jax.ShapeDtypeStruct((m, DSA_W), BF16),
        jax.ShapeDtypeStruct((DSA_HEADS, V_ROWS, m), BF16),
        jax.ShapeDtypeStruct((m, IDX_HEADS * IDX_DIM), BF16),
        jax.ShapeDtypeStruct((m, LANES), BF16),
        jax.ShapeDtypeStruct((m, LANES), F32),
        jax.ShapeDtypeStruct((m, 4 * GDN_W), F32),
        jax.ShapeDtypeStruct((m, LANES), F32),
        jax.ShapeDtypeStruct((m, MEM_W), F32),
    )
    out_specs = (row(DSA_W), row(DSA_W), pl.BlockSpec((DSA_HEADS, V_ROWS, tm), lambda i: (0, 0, i)),
                 row(IDX_HEADS * IDX_DIM), row(LANES), row(LANES), row(4 * GDN_W),
                 row(LANES), row(MEM_W))
    return pl.pallas_call(
        _inproj_kernel,
        grid=(m // tm,),
        in_specs=[row(d), full(gain), full(w_packed), full(grp), full(qg), full(kg)],
        out_specs=out_specs,
        out_shape=out_shape,
        compiler_params=_params("parallel"),
        name="inproj",
    )(x, gain, w_packed, grp, qg, kg)


_DENORMAL_TOP = 0x007FFFFF


def _float_to_key(f):
    bits = lax.bitcast_convert_type(f, jnp.int32)
    mag = jnp.maximum((bits & jnp.int32(0x7FFFFFFF)) - _DENORMAL_TOP, 0)
    return jnp.where(bits >= 0, mag, -mag)


def _key_to_float(key):
    mag = jnp.abs(key)
    bits = jnp.where(mag > 0, mag + _DENORMAL_TOP, 0)
    return lax.bitcast_convert_type(jnp.where(key < 0, bits | jnp.int32(-2 ** 31), bits), F32)


def _upper_normal_quantile(p):
    pp = jnp.clip(jnp.minimum(p, 1.0 - p), 1e-30, 0.5)
    t = jnp.sqrt(-2.0 * jnp.log(pp))
    z = t - (2.515517 + t * (0.802853 + t * 0.010328)) / (1.0 + t * (1.432788 + t * (0.189269 + t * 0.001308)))
    return jnp.where(p <= 0.5, z, -z)


def _tree(op, x, group=8):
    parts = x.reshape(x.shape[0] // group, group, LANES)
    k = parts.shape[0]
    while k > 1:
        k //= 2
        parts = op(parts[:k], parts[k:2 * k])
    return parts[0]


def _dsa_kernel(q_ref, iq_ref, kwq_ref, k_ref, vt_ref, ik_ref, tab_ref, ltri_ref, o_ref,
                s_scr, iqt_scr, qpad_scr, bias_scr, mx_scr, brange_scr, *, kc, top_k):
    j = pl.program_id(1)
    per = kc // Q_BLOCK
    pad = kc - Q_BLOCK
    nch = j // per + 1
    q_pos = j * Q_BLOCK + lax.broadcasted_iota(jnp.int32, (1, LANES), 1)
    row_iota = lax.broadcasted_iota(jnp.int32, (kc, LANES), 0)

    def rows(i):
        return pl.ds(pl.multiple_of(j * Q_BLOCK - i * kc, Q_BLOCK), kc)

    def chunk_pairs(start, body, init, width=2, stop=None):
        count = jnp.maximum((nch if stop is None else stop) - start, 0)
        rest = count % width
        state = lax.fori_loop(0, rest, lambda r, st: body(start + r, st), init)

        def group(t, st):
            for u in range(width):
                st = body(start + rest + width * t + u, st)
            return st
        return lax.fori_loop(0, count // width, group, state)

    @pl.when((pl.program_id(0) == 0) & (j == 0))
    def _():
        r = lax.broadcasted_iota(jnp.int32, (2 * Q_BLOCK, LANES), 0)
        c = lax.broadcasted_iota(jnp.int32, (2 * Q_BLOCK, LANES), 1)
        dist = Q_BLOCK + c - r
        bucket = jnp.zeros_like(dist)
        for first in _BUCKET_STARTS:
            bucket = bucket + jnp.where(dist >= first, 1, 0)
        for h in range(DSA_HEADS):
            far = tab_ref[NUM_BUCKETS - 1, h]
            delta = jnp.zeros((2 * Q_BLOCK, LANES), F32)
            b_hi = jnp.float32(0.0)
            b_lo = jnp.float32(0.0)
            for b in range(NUM_BUCKETS - 1):
                delta = jnp.where(bucket == b, tab_ref[b, h] - far, delta)
                b_hi = jnp.maximum(b_hi, (tab_ref[b, h] - far) * LOG2E)
                b_lo = jnp.minimum(b_lo, (tab_ref[b, h] - far) * LOG2E)
            bias_scr[h] = jnp.where(dist >= 0, delta * LOG2E, 0.0)
            brange_scr[0, h] = b_hi
            brange_scr[1, h] = b_lo

    iqt = iq_ref[...].astype(F32).T
    zpad = jnp.zeros((LANES - IDX_DIM, LANES), F32)
    for h in range(IDX_HEADS):
        iqt_scr[:, h * LANES:(h + 1) * LANES] = jnp.concatenate(
            [iqt[h * IDX_DIM:(h + 1) * IDX_DIM], zpad], axis=0).astype(BF16)
    wt = kwq_ref[...].T[IDX_DIM:IDX_DIM + IDX_HEADS, :] * (IDX_HEADS ** -0.5)
    qt = q_ref[...].astype(F32).T
    half = lax.broadcasted_iota(jnp.int32, (LANES, LANES), 0) < HEAD_DIM
    for h in range(DSA_HEADS):
        pair = qt[(h // 2) * LANES:(h // 2 + 1) * LANES]
        keep = half if h % 2 == 0 else jnp.logical_not(half)
        qpad_scr[:, h * LANES:(h + 1) * LANES] = jnp.where(keep, pair, 0.0).astype(BF16)

    bound = tab_ref[NUM_BUCKETS, 0] * tab_ref[NUM_BUCKETS, 1]
    spread = jnp.float32(0.0)
    for h in range(DSA_HEADS):
        b_hi = brange_scr[0, h]
        b_lo = brange_scr[1, h]
        mx_scr[h:h + 1, :] = jnp.full((1, LANES), bound + b_hi, F32)
        spread = jnp.maximum(spread, 2.0 * bound + (b_hi - b_lo))
    bound_ok = spread <= MAX_LOGIT_SPREAD

    def score_chunk(i, carry, edge):
        d = jnp.dot(ik_ref[0, rows(i), :], iqt_scr[...], preferred_element_type=F32)
        acc = jnp.maximum(d[:, :LANES], 0.0) * wt[0:1, :]
        for h in range(1, IDX_HEADS):
            acc = acc + jnp.maximum(d[:, h * LANES:(h + 1) * LANES], 0.0) * wt[h:h + 1, :]
        sc = fin = acc
        if edge:
            key = row_iota + (j * Q_BLOCK - i * kc - pad)
            adm = jnp.where(key >= 0, key, q_pos + 1) <= q_pos
            sc = jnp.where(adm, acc, -jnp.inf)
            fin = jnp.where(adm, acc, 0.0)
        s_scr[rows(i), :] = sc
        tot, sq, top = carry
        return (tot + _tree(jnp.add, fin), sq + _tree(jnp.add, fin * fin),
                jnp.maximum(top, _tree(jnp.maximum, sc)))

    zero8 = jnp.zeros((8, LANES), F32)
    stats = score_chunk(0, (zero8, zero8, jnp.full((8, LANES), -jnp.inf, F32)), True)
    stats = lax.fori_loop(0, jnp.minimum(nch - 1, 1), lambda _, st: score_chunk(nch - 1, st, True), stats)
    tot, sq, top = chunk_pairs(1, lambda i, st: score_chunk(i, st, False), stats, width=4, stop=nch - 1)

    def count(pred):
        def body(i, cnt):
            return cnt + _tree(jnp.add, jnp.where(pred(s_scr[rows(i), :]), 1.0, 0.0))
        cnt = chunk_pairs(0, body, zero8, width=4)
        return jnp.sum(cnt, axis=0, keepdims=True)

    def max_below(t):
        def body(i, m):
            s = s_scr[rows(i), :]
            return jnp.maximum(m, _tree(jnp.maximum, jnp.where(s < t, s, -jnp.inf)))
        m = lax.fori_loop(0, nch, body, jnp.full((8, LANES), -jnp.inf, F32))
        return jnp.max(m, axis=0, keepdims=True)

    kf = float(top_k)
    n_adm = (q_pos + 1).astype(F32)
    mean = jnp.sum(tot, axis=0, keepdims=True) / n_adm
    std = jnp.sqrt(jnp.maximum(jnp.sum(sq, axis=0, keepdims=True) / n_adm - mean * mean, 0.0))
    first_guess = mean + _upper_normal_quantile(kf / n_adm) * std
    lowest = jnp.float32(-3.0e38)
    top1 = jnp.minimum(jnp.max(top, axis=0, keepdims=True), -lowest)

    def update(st, key):
        lo, hi, c_lo, c_hi, lo_set, hi_set, done, run, v_lo, v_hi, l_lo, l_hi = st
        key = jnp.clip(key, lo + 1, hi - 1)
        t = _key_to_float(key)
        c = count(lambda s: s >= t)
        lc = jnp.log(jnp.maximum(c, 0.5))
        active = done < 0.5
        up = active & (c >= kf)
        dn = active & (c < kf)
        lo, c_lo, lo_set = jnp.where(up, key, lo), jnp.where(up, c, c_lo), jnp.where(up, 1.0, lo_set)
        hi, c_hi, hi_set = jnp.where(dn, key, hi), jnp.where(dn, c, c_hi), jnp.where(dn, 1.0, hi_set)
        v_lo, l_lo = jnp.where(up, t, v_lo), jnp.where(up, lc, l_lo)
        v_hi, l_hi = jnp.where(dn, t, v_hi), jnp.where(dn, lc, l_hi)
        done = jnp.where((c_lo == kf) | (hi <= lo + 1), 1.0, done)
        return lo, hi, c_lo, c_hi, lo_set, hi_set, done, run, v_lo, v_hi, l_lo, l_hi

    def guess(st, midpoint):
        lo, hi, c_lo, c_hi, lo_set, hi_set, done, run, v_lo, v_hi, l_lo, l_hi = st
        both = (lo_set > 0.5) & (hi_set > 0.5)
        step = std * 0.25 * jnp.exp2(run)
        if midpoint:
            inner = 0.5 * v_lo + 0.5 * v_hi
        else:
            frac = jnp.clip((l_lo - math.log(kf - 0.5)) / jnp.maximum(l_lo - l_hi, 1e-6), 0.0, 1.0)
            inner = v_lo + (v_hi - v_lo) * frac
        t = jnp.where(both, inner,
                      jnp.where(hi_set > 0.5, v_hi - step, jnp.where(lo_set > 0.5, v_lo + step, first_guess)))
        t = jnp.where(t != t, 0.0, t)
        run = jnp.where(both, 0.0, run + 1.0)
        return _float_to_key(t), (lo, hi, c_lo, c_hi, lo_set, hi_set, done, run, v_lo, v_hi, l_lo, l_hi)

    def status(st):
        return jnp.max(jnp.where(st[6] < 0.5, STILL_SEARCHING, st[2]))

    flag0 = jnp.zeros((1, LANES), F32)
    hi0 = _float_to_key(top1) + 1
    st = (jnp.full((1, LANES), _float_to_key(lowest), jnp.int32), hi0,
          n_adm, flag0, flag0, flag0, jnp.where(n_adm <= kf, 1.0, 0.0), flag0,
          jnp.full((1, LANES), lowest, F32), _key_to_float(hi0), jnp.log(n_adm),
          jnp.full((1, LANES), math.log(0.5), F32))

    def probes(st):
        for r in range(PROBES_PER_ROUND):
            key, st = guess(st, midpoint=((r + 1) % MIDPOINT_EVERY == 0))
            st = update(st, key)
        return st

    st = lax.fori_loop(0, FAST_ROUNDS, lambda _, s: probes(s), st)

    def safe_round(carry):
        p, _, st = carry
        st = update(st, st[0] + lax.shift_right_logical(st[1] - st[0], 1))
        lo, hi, done = st[0], st[1], st[6]
        below = _float_to_key(max_below(st[9]))
        hi = jnp.where(done < 0.5, jnp.clip(below + 1, lo + 1, hi), hi)
        done = jnp.where(hi <= lo + 1, 1.0, done)
        st = update((lo, hi) + st[2:6] + (done, st[7], st[8], _key_to_float(hi)) + st[10:], hi - 1)
        return p + 1, status(st), st

    _, most, st = lax.while_loop(lambda c: (c[1] >= STILL_SEARCHING) & (c[0] < 34), safe_round,
                                 (jnp.int32(0), status(st), st))

    thr = _key_to_float(st[0])
    has_ties = most > kf

    @pl.when(jnp.logical_not(has_ties))
    def _():
        def mask_chunk(i, carry):
            s_scr[rows(i), :] = jnp.where(s_scr[rows(i), :] >= thr, 0.0, NEG)
            return carry
        lax.fori_loop(0, nch, mask_chunk, 0)

    @pl.when(has_ties)
    def _():
        need = kf - count(lambda s: s > thr)

        def mask_chunk(t, run):
            i = nch - 1 - t
            s = s_scr[rows(i), :]
            tie = jnp.where(s == thr, 1.0, 0.0)
            pref = jnp.dot(ltri_ref[...], tie.astype(BF16), preferred_element_type=F32) + run
            tie_sel = jnp.where(pref <= need, tie, 0.0)
            sel = jnp.where(s > thr, 1.0, tie_sel)
            s_scr[rows(i), :] = jnp.where(sel > 0.5, 0.0, NEG)
            return run + jnp.sum(tie, axis=0, keepdims=True)
        lax.fori_loop(0, nch, mask_chunk, jnp.zeros((1, LANES), F32))

    def logits(i, with_bias):
        msk = s_scr[rows(i), :]
        out = []
        for pr in range(DSA_HEADS // 2):
            l2 = jnp.dot(k_ref[0, rows(i), pr * LANES:(pr + 1) * LANES],
                         qpad_scr[:, 2 * pr * LANES:(2 * pr + 2) * LANES],
                         preferred_element_type=F32)
            for e in range(2):
                l = l2[:, e * LANES:(e + 1) * LANES] + msk
                if with_bias:
                    far_rows = kc - 2 * Q_BLOCK
                    l = jnp.concatenate([l[:far_rows], l[far_rows:] + bias_scr[2 * pr + e]], axis=0)
                out.append(l)
        return out

    def max_step(i, ms, with_bias):
        return tuple(jnp.maximum(m, _tree(jnp.maximum, l)) for m, l in zip(ms, logits(i, with_bias)))

    @pl.when(jnp.logical_not(bound_ok))
    def _():
        ms = max_step(0, tuple(jnp.full((8, LANES), NEG, F32) for _ in range(DSA_HEADS)), True)
        ms = lax.fori_loop(1, nch, lambda i, m: max_step(i, m, False), ms)
        for h in range(DSA_HEADS):
            mx_scr[h:h + 1, :] = jnp.max(ms[h], axis=0, keepdims=True)

    mx = [mx_scr[h:h + 1, :] for h in range(DSA_HEADS)]

    def pv_step(i, accs, with_bias):
        return tuple(
            acc + jnp.dot(vt_ref[h, :, rows(i)], jnp.exp2(l - mx[h]).astype(BF16),
                          preferred_element_type=F32)
            for h, (acc, l) in enumerate(zip(accs, logits(i, with_bias))))

    accs = pv_step(0, tuple(jnp.zeros((V_ROWS, LANES), F32) for _ in range(DSA_HEADS)), True)
    accs = chunk_pairs(1, lambda i, a: pv_step(i, a, False), accs, width=4)
    outs = [acc[:HEAD_DIM] / acc[HEAD_DIM:HEAD_DIM + 1] for acc in accs]
    o_ref[...] = jnp.concatenate(outs, axis=0).T


def _dsa(q, iq, kw, k, vt, ikb, rel_bias, q_gain, k_gain, ltri, *, batch, seq, kc):
    nb = seq // Q_BLOCK
    top_k = min(TOPK_MAX, seq // 4)
    pad = kc - Q_BLOCK
    seqp = seq + pad
    kp = jnp.pad(k.reshape(batch, seq, DSA_W), ((0, 0), (pad, 0), (0, 0)))
    ikp = jnp.pad(ikb.reshape(batch, seq, LANES), ((0, 0), (pad, 0), (0, 0)))
    vtp = jnp.pad(vt.reshape(DSA_HEADS, V_ROWS, batch, seq),
                  ((0, 0), (0, 0), (0, 0), (pad, 0))).reshape(DSA_HEADS, V_ROWS, batch * seqp)
    kmax = (HEAD_DIM ** 0.5 * 1.01) * jnp.max(jnp.abs(k_gain.astype(F32)))
    qmax = (LOG2E * 1.01) * jnp.max(jnp.abs(q_gain.astype(F32)))
    norms = jnp.stack([kmax, qmax] + [jnp.zeros((), F32)] * (DSA_HEADS - 2)).reshape(1, DSA_HEADS)
    table = jnp.concatenate([rel_bias.astype(F32), norms], axis=0)
    qrow = lambda w: pl.BlockSpec((Q_BLOCK, w), lambda b, j: (b * nb + j, 0))
    return pl.pallas_call(
        functools.partial(_dsa_kernel, kc=kc, top_k=top_k),
        grid=(batch, nb),
        in_specs=[
            qrow(DSA_W), qrow(IDX_HEADS * IDX_DIM), qrow(LANES),
            pl.BlockSpec((1, seqp, DSA_W), lambda b, j: (b, 0, 0)),
            pl.BlockSpec((DSA_HEADS, V_ROWS, seqp), lambda b, j: (0, 0, b)),
            pl.BlockSpec((1, seqp, LANES), lambda b, j: (b, 0, 0)),
            pl.BlockSpec(memory_space=pltpu.SMEM),
            pl.BlockSpec(ltri.shape, lambda b, j: (0, 0)),
        ],
        out_specs=qrow(DSA_W),
        out_shape=jax.ShapeDtypeStruct((batch * seq, DSA_W), F32),
        scratch_shapes=[
            pltpu.VMEM((seqp, LANES), F32),
            pltpu.VMEM((LANES, IDX_HEADS * LANES), BF16),
            pltpu.VMEM((LANES, DSA_HEADS * LANES), BF16),
            pltpu.VMEM((DSA_HEADS, 2 * Q_BLOCK, LANES), F32),
            pltpu.VMEM((8, LANES), F32),
            pltpu.SMEM((2, DSA_HEADS), F32),
        ],
        compiler_params=_params("arbitrary", "arbitrary"),
        name="dsa",
    )(q, iq, kw, kp, vtp, ikp, table, ltri)


GDN_GROUP = 2


def _gdn_kernel(x_ref, ab_ref, cw_ref, ea_ref, eb_ref, alog_ref, dtb_ref, gain_ref, grp_ref,
                o_ref, xpad_scr, qkv_scr, gb_scr, gc_scr, gr_scr, pre_scr, st_scr, o_scr, *, tb):
    t = pl.program_id(1)
    cin = 3 * GDN_W
    n_chunks = tb // GDN_CHUNK
    pairs = range(GDN_HEADS // 2)
    lanes = lambda p: slice(p * LANES, (p + 1) * LANES)

    @pl.when(t == 0)
    def _():
        xpad_scr[0:8, :] = jnp.zeros((8, cin), F32)
        st_scr[...] = jnp.zeros_like(st_scr)

    x = x_ref[...]
    xpad_scr[8:8 + tb, :] = x[:, :cin]
    conv = jnp.zeros((tb, cin), F32)
    for jj in range(CONV_WIDTH):
        conv = conv + cw_ref[jj:jj + 1, :] * xpad_scr[pl.ds(8 - (CONV_WIDTH - 1) + jj, tb), :]
    xpad_scr[0:8, :] = x[tb - 8:tb, :cin]
    qkv = conv * _sigmoid(conv)
    grp = grp_ref[...]
    q = qkv[:, :GDN_W]
    k = qkv[:, GDN_W:2 * GDN_W]
    qkv_scr[:, :GDN_W] = q * lax.rsqrt(_group_sum(q * q, grp) + EPS) * (HEAD_DIM ** -0.5)
    qkv_scr[:, GDN_W:2 * GDN_W] = k * lax.rsqrt(_group_sum(k * k, grp) + EPS)
    qkv_scr[:, 2 * GDN_W:] = qkv[:, 2 * GDN_W:]

    ab = ab_ref[...]
    a_e = _dot_sel(ab, ea_ref[...], 3) + dtb_ref[...]
    b_e = _dot_sel(ab, eb_ref[...], 3)
    softplus = jnp.maximum(a_e, 0.0) + jnp.log(1.0 + jnp.exp(-jnp.abs(a_e)))
    gb_scr[:, :GDN_W] = -jnp.exp(alog_ref[...]) * softplus
    gb_scr[:, GDN_W:] = _sigmoid(b_e)

    r64 = lax.broadcasted_iota(jnp.int32, (GDN_CHUNK, LANES), 0)
    c64 = lax.broadcasted_iota(jnp.int32, (GDN_CHUNK, LANES), 1) % GDN_CHUNK
    causal = c64 <= r64
    strict = c64 < r64
    eye = jnp.where(c64 == r64, 1.0, 0.0)
    r128 = lax.broadcasted_iota(jnp.int32, (LANES, LANES), 0)
    c128 = lax.broadcasted_iota(jnp.int32, (LANES, LANES), 1)
    bdmask = (r128 // HEAD_DIM) == (c128 // HEAD_DIM)

    def bd(m):
        return jnp.where(bdmask, jnp.concatenate([m, m], axis=0), 0.0)

    lt_r = lax.broadcasted_iota(jnp.int32, (GDN_CHUNK, GDN_CHUNK), 0)
    lt_c = lax.broadcasted_iota(jnp.int32, (GDN_CHUNK, GDN_CHUNK), 1)
    ltri = jnp.where(lt_c <= lt_r, 1.0, 0.0)
    ones8 = jnp.ones((8, GDN_CHUNK), F32)
    up_r = lax.broadcasted_iota(jnp.int32, (GDN_CHUNK, GDN_W), 0)
    up_c = lax.broadcasted_iota(jnp.int32, (GDN_CHUNK, GDN_W), 1) % GDN_CHUNK
    upper = jnp.where(up_r <= up_c, 1.0, 0.0)
    for ci in range(n_chunks):
        g = gb_scr[ci * GDN_CHUNK:(ci + 1) * GDN_CHUNK, :GDN_W]
        gc_scr[ci * GDN_CHUNK:(ci + 1) * GDN_CHUNK, :] = _sel_dot(ltri, g, 3)
        gr_scr[ci * 8:(ci + 1) * 8, :] = _sel_dot(ones8, g * upper, 3)

    def par_body(gi, carry):
        inst = [(u, p) for u in range(GDN_GROUP) for p in pairs]
        rows = [pl.ds(pl.multiple_of((gi * GDN_GROUP + u) * GDN_CHUNK, GDN_CHUNK), GDN_CHUNK)
                for u in range(GDN_GROUP)]
        grow = [pl.ds(pl.multiple_of((gi * GDN_GROUP + u) * 8, 8), 8) for u in range(GDN_GROUP)]
        off = lambda s, p: slice(s * GDN_W + p * LANES, s * GDN_W + (p + 1) * LANES)
        qp = [qkv_scr[rows[u], off(0, p)] for u, p in inst]
        kp = [qkv_scr[rows[u], off(1, p)] for u, p in inst]
        vp = [qkv_scr[rows[u], off(2, p)] for u, p in inst]
        beta = [gb_scr[rows[u], off(1, p)] for u, p in inst]
        gc = [gc_scr[rows[u], lanes(p)] for u, p in inst]
        gr = [gr_scr[grow[u], lanes(p)][0:1, :] for u, p in inst]
        dmat = [jnp.exp(jnp.where(causal, a - b, NEG)) for a, b in zip(gc, gr)]
        kbd = [jnp.where(bdmask, jnp.concatenate([a, a], axis=0).T, 0.0) for a in kp]
        kb = [a * b for a, b in zip(kp, beta)]
        kk = [_bdot(a, b) for a, b in zip(kb, kbd)]
        qk = [_bdot(a, b) for a, b in zip(qp, kbd)]
        lm = [jnp.where(strict, a * d, 0.0) for a, d in zip(kk, dmat)]
        aintra = [a * d for a, d in zip(qk, dmat)]
        tinv = [eye - a for a in lm]
        lpow = lm
        for _ in range(5):
            lpow = [_dot3(a, bd(a)) for a in lpow]
            tinv = [a + _dot3(a, bd(b)) for a, b in zip(tinv, lpow)]
        egc = [jnp.exp(a) for a in gc]
        un = [_dot3(a, bd(v * b)) for a, v, b in zip(tinv, vp, beta)]
        wn = [_dot3(a, bd(b * e)) for a, b, e in zip(tinv, kb, egc)]
        for n, (u, p) in enumerate(inst):
            pre_scr[0, rows[u], lanes(p)] = un[n]
            pre_scr[1, rows[u], lanes(p)] = wn[n]
            pre_scr[2, rows[u], lanes(p)] = aintra[n]
            pre_scr[3, rows[u], lanes(p)] = qp[n] * egc[n]
            pre_scr[4, rows[u], lanes(p)] = kp[n] * jnp.exp(gc[n][GDN_CHUNK - 1:GDN_CHUNK, :] - gc[n])
        return carry

    lax.fori_loop(0, n_chunks // GDN_GROUP, par_body, 0)

    def rec_body(ci, carry):
        rows = pl.ds(pl.multiple_of(ci * GDN_CHUNK, GDN_CHUNK), GDN_CHUNK)
        tail = gc_scr[pl.ds(pl.multiple_of(ci * GDN_CHUNK + GDN_CHUNK - 8, 8), 8), :]
        egl = jnp.exp(tail[7:8, :])
        sbd = [st_scr[p] for p in pairs]
        un, wn, aintra, qe, kd = [[pre_scr[s, rows, lanes(p)] for p in pairs] for s in range(5)]
        ws = [_bdot(a, s) for a, s in zip(wn, sbd)]
        qs = [_bdot(a, s) for a, s in zip(qe, sbd)]
        v_new = [a - b for a, b in zip(un, ws)]
        o = [a + _bdot(b, bd(v)) for a, b, v in zip(qs, aintra, v_new)]
        upd = [_bdot(a.T, v) for a, v in zip(kd, v_new)]
        for p in pairs:
            st_scr[p] = sbd[p] * egl[:, lanes(p)] + jnp.where(bdmask, upd[p], 0.0)
            o_scr[rows, lanes(p)] = o[p]
        return carry

    lax.fori_loop(0, n_chunks, rec_body, 0)

    o = o_scr[...]
    oms = _group_sum(o * o, grp) * (1.0 / HEAD_DIM)
    z = x[:, cin:]
    o_ref[...] = o * lax.rsqrt(oms + EPS) * gain_ref[...] * (z * _sigmoid(z))


def _gdn(gx, ab, conv_w, ea, eb, alog, dtb, gain, grp, *, batch, seq, tb):
    nt = seq // tb
    cin = 3 * GDN_W
    row = lambda w: pl.BlockSpec((tb, w), lambda b, t: (b * nt + t, 0))
    full = lambda a: pl.BlockSpec(a.shape, lambda b, t: (0,) * a.ndim)
    return pl.pallas_call(
        functools.partial(_gdn_kernel, tb=tb),
        grid=(batch, nt),
        in_specs=[row(4 * GDN_W), row(LANES), full(conv_w), full(ea), full(eb), full(alog),
                  full(dtb), full(gain), full(grp)],
        out_specs=row(GDN_W),
        out_shape=jax.ShapeDtypeStruct((batch * seq, GDN_W), F32),
        scratch_shapes=[
            pltpu.VMEM((tb + 8, cin), F32),
            pltpu.VMEM((tb, cin), F32),
            pltpu.VMEM((tb, 2 * GDN_W), F32),
            pltpu.VMEM((tb, GDN_W), F32),
            pltpu.VMEM((tb // 8, GDN_W), F32),
            pltpu.VMEM((5, tb, GDN_W), F32),
            pltpu.VMEM((GDN_HEADS // 2, LANES, LANES), F32),
            pltpu.VMEM((tb, GDN_W), F32),
        ],
        compiler_params=_params("arbitrary", "arbitrary"),
        name="gdn",
    )(gx, ab, conv_w, ea, eb, alog, dtb, gain, grp)


def _memkv_kernel(mem_ref, g_ref, w_ref, kg_ref, grp_ref, kt_ref, vp_ref):
    x = mem_ref[0]
    ms = jnp.mean(x * x, axis=-1, keepdims=True)
    h = (x * lax.rsqrt(ms + EPS) * g_ref[...]).astype(BF16)
    kv = jnp.dot(h, w_ref[...], preferred_element_type=F32)
    km = kv[:, :MEM_W]
    vm = kv[:, MEM_W:]
    kms = _group_sum(km * km, grp_ref[...]) * (1.0 / HEAD_DIM)
    kt = (km * lax.rsqrt(kms + EPS) * kg_ref[...]).T
    n_mem = x.shape[0]
    top = lax.broadcasted_iota(jnp.int32, (LANES, n_mem), 0) < HEAD_DIM
    left = lax.broadcasted_iota(jnp.int32, (n_mem, LANES), 1) < HEAD_DIM
    for hh in range(MEM_HEADS):
        pr = slice((hh // 2) * LANES, (hh // 2 + 1) * LANES)
        keep_r = top if hh % 2 == 0 else jnp.logical_not(top)
        keep_c = left if hh % 2 == 0 else jnp.logical_not(left)
        kt_ref[0, hh] = jnp.where(keep_r, kt[pr, :], 0.0).astype(BF16)
        vp_ref[0, hh] = jnp.where(keep_c, vm[:, pr], 0.0).astype(BF16)


def _memkv(mem, gain, w, kg, grp):
    b, n_mem, d = mem.shape
    full = lambda a: pl.BlockSpec(a.shape, lambda i: (0,) * a.ndim)
    return pl.pallas_call(
        _memkv_kernel,
        grid=(b,),
        in_specs=[pl.BlockSpec((1, n_mem, d), lambda i: (i, 0, 0)), full(gain), full(w), full(kg),
                  full(grp)],
        out_specs=(pl.BlockSpec((1, MEM_HEADS, LANES, n_mem), lambda i: (i, 0, 0, 0)),
                   pl.BlockSpec((1, MEM_HEADS, n_mem, LANES), lambda i: (i, 0, 0, 0))),
        out_shape=(jax.ShapeDtypeStruct((b, MEM_HEADS, LANES, n_mem), BF16),
                   jax.ShapeDtypeStruct((b, MEM_HEADS, n_mem, LANES), BF16)),
        compiler_params=_params("parallel"),
        name="memkv",
    )(mem, gain, w, kg, grp)


def _memattn_kernel(q_ref, qg_ref, grp_ref, kt_ref, vp_ref, o_ref):
    q = q_ref[...]
    qms = _group_sum(q * q, grp_ref[...]) * (1.0 / HEAD_DIM)
    qn = (q * lax.rsqrt(qms + EPS) * qg_ref[...] * (HEAD_DIM ** -0.5)).astype(BF16)
    outs = []
    for pr in range(MEM_HEADS // 2):
        qp = qn[:, pr * LANES:(pr + 1) * LANES]
        acc = None
        for e in range(2):
            hh = 2 * pr + e
            l = jnp.dot(qp, kt_ref[0, hh], preferred_element_type=F32)
            l = l - jnp.max(l, axis=-1, keepdims=True)
            p = jnp.exp(l)
            p = p / jnp.sum(p, axis=-1, keepdims=True)
            o = jnp.dot(p.astype(BF16), vp_ref[0, hh], preferred_element_type=F32)
            acc = o if acc is None else acc + o
        outs.append(acc)
    o_ref[...] = jnp.concatenate(outs, axis=-1)


def _memattn(mq, qg, grp, kt, vp, *, batch, seq, tm):
    nt = seq // tm
    n_mem = kt.shape[-1]
    full = lambda a: pl.BlockSpec(a.shape, lambda b, t: (0,) * a.ndim)
    return pl.pallas_call(
        _memattn_kernel,
        grid=(batch, nt),
        in_specs=[
            pl.BlockSpec((tm, MEM_W), lambda b, t: (b * nt + t, 0)), full(qg), full(grp),
            pl.BlockSpec((1, MEM_HEADS, LANES, n_mem), lambda b, t: (b, 0, 0, 0)),
            pl.BlockSpec((1, MEM_HEADS, n_mem, LANES), lambda b, t: (b, 0, 0, 0)),
        ],
        out_specs=pl.BlockSpec((tm, MEM_W), lambda b, t: (b * nt + t, 0)),
        out_shape=jax.ShapeDtypeStruct((batch * seq, MEM_W), F32),
        compiler_params=_params("parallel", "parallel"),
        name="memattn",
    )(mq, qg, grp, kt, vp)


def _group_ones(width):
    idx = np.arange(width) // HEAD_DIM
    return jnp.asarray((idx[:, None] == idx[None, :]).astype(np.float32))


def _t5_bucket_np(n):
    max_exact = NUM_BUCKETS // 2
    nf = np.maximum(n, 1).astype(np.float32)
    large = max_exact + (np.log(nf / np.float32(max_exact)) / np.float32(math.log(MAX_DISTANCE / max_exact))
                         * (NUM_BUCKETS - max_exact)).astype(np.int32)
    large = np.minimum(large, NUM_BUCKETS - 1)
    return np.where(n < max_exact, n, large)


def _bucket_starts():
    dist = np.arange(2 * Q_BLOCK)
    bucket = _t5_bucket_np(dist)
    assert (np.diff(bucket) >= 0).all() and bucket[-1] == NUM_BUCKETS - 1
    return tuple(int(dist[bucket >= b].min()) for b in range(1, NUM_BUCKETS))


_BUCKET_STARTS = _bucket_starts()


def _cast_kernel(w_ref, o_ref):
    o_ref[...] = w_ref[...].astype(BF16)


def _to_bf16(w, *, rows):
    nl, nr, nc = w.shape
    spec = pl.BlockSpec((1, rows, nc), lambda l, r: (l, r, 0))
    return pl.pallas_call(
        _cast_kernel,
        grid=(nl, nr // rows),
        in_specs=[spec],
        out_specs=spec,
        out_shape=jax.ShapeDtypeStruct(w.shape, BF16),
        compiler_params=_params("parallel", "parallel"),
        name="cast",
    )(w)


def _pack_w_in(w):
    o = 3 * DSA_W
    iq = w[:, o:o + IDX_HEADS * IDX_DIM]
    o += IDX_HEADS * IDX_DIM
    kw = w[:, o:o + IDX_DIM + IDX_HEADS]
    o += IDX_DIM + IDX_HEADS
    g = w[:, o:o + 4 * GDN_W]
    o += 4 * GDN_W
    ab = w[:, o:o + 2 * GDN_HEADS]
    o += 2 * GDN_HEADS
    mq = w[:, o:o + MEM_W]
    pad = lambda a: jnp.pad(a, ((0, 0), (0, LANES - a.shape[1])))
    return jnp.concatenate([w[:, :3 * DSA_W], iq, pad(kw), g, pad(ab), mq], axis=1)


def _tile_heads(v, heads):
    return jnp.tile(v.astype(F32), heads).reshape(1, heads * HEAD_DIM)


def kernel(x, mem, ffn1_norm, ffn1_w_gate, ffn1_w_up, ffn1_w_down, mix_norm, w_in, dsa_q_norm, dsa_k_norm, rel_bias, gdn_conv, gdn_A_log, gdn_dt_bias, gdn_out_norm, mem_norm, w_mem_kv, mem_q_norm, mem_k_norm, w_out, ffn2_norm, ffn2_w_gate, ffn2_w_up, ffn2_w_down):
    batch, seq, d = x.shape
    depth = w_in.shape[0]
    m = batch * seq
    tm = min(512, seq)
    kc = min(512, seq)
    dff = ffn1_w_gate.shape[-1]
    tf = dff // 2 if (dff // 2) % LANES == 0 else dff

    grp_a = _group_ones(DSA_W)
    grp_m = _group_ones(MEM_W)
    ltri = jnp.asarray(np.tril(np.ones((kc, kc), np.float32))).astype(BF16)
    heads_of = np.arange(GDN_W) // HEAD_DIM
    ea = jnp.asarray((np.arange(LANES)[:, None] == heads_of[None, :]).astype(np.float32))
    eb = jnp.asarray((np.arange(LANES)[:, None] == heads_of[None, :] + GDN_HEADS).astype(np.float32))

    w_in_b = _to_bf16(w_in, rows=min(256, d))
    xf = x.reshape(m, d)
    for l in range(depth):
        xf = _ffn(xf, ffn1_norm[l], ffn1_w_gate[l].astype(BF16), ffn1_w_up[l].astype(BF16),
                  ffn1_w_down[l].astype(BF16), tm=tm, tf=tf)
        (q, k, vt, iq, ikb, kw, gx, ab, mq) = _inproj(
            xf, mix_norm[l].reshape(1, d), _pack_w_in(w_in_b[l]), grp_a,
            _tile_heads(dsa_q_norm[l], DSA_HEADS), _tile_heads(dsa_k_norm[l], DSA_HEADS), tm=tm)
        out_a = _dsa(q, iq, kw, k, vt, ikb, rel_bias, dsa_q_norm[l], dsa_k_norm[l], ltri,
                     batch=batch, seq=seq, kc=kc)
        out_b = _gdn(gx, ab, gdn_conv[l], ea, eb,
                     jnp.repeat(gdn_A_log[l].astype(F32), HEAD_DIM).reshape(1, GDN_W),
                     jnp.repeat(gdn_dt_bias[l].astype(F32), HEAD_DIM).reshape(1, GDN_W),
                     _tile_heads(gdn_out_norm[l], GDN_HEADS), grp_a, batch=batch, seq=seq, tb=tm)
        kt, vp = _memkv(mem, mem_norm[l].reshape(1, d), w_mem_kv[l].astype(BF16),
                        _tile_heads(mem_k_norm[l], MEM_HEADS), grp_m)
        out_c = _memattn(mq, _tile_heads(mem_q_norm[l], MEM_HEADS), grp_m, kt, vp,
                         batch=batch, seq=seq, tm=tm)
        wo = w_out[l].astype(BF16)
        xf = _ffn(xf, ffn2_norm[l], ffn2_w_gate[l].astype(BF16), ffn2_w_up[l].astype(BF16),
                  ffn2_w_down[l].astype(BF16), mix=(out_a, out_b, out_c),
                  mix_w=(wo[:DSA_W], wo[DSA_W:DSA_W + GDN_W], wo[DSA_W + GDN_W:]), tm=tm, tf=tf)
    return xf.reshape(batch, seq, d)
```

```python
import functools
import math

import jax
import jax.numpy as jnp
import numpy as np
from jax import lax
from jax.experimental import pallas as pl
from jax.experimental.pallas import tpu as pltpu

F32 = jnp.float32
BF16 = jnp.bfloat16

HEAD_DIM = 64
DSA_HEADS = 6
GDN_HEADS = 6
MEM_HEADS = 4
DSA_W = DSA_HEADS * HEAD_DIM
GDN_W = GDN_HEADS * HEAD_DIM
MEM_W = MEM_HEADS * HEAD_DIM
IDX_HEADS = 8
IDX_DIM = 32
TOPK_MAX = 256
Q_BLOCK = 128
GDN_CHUNK = 64
CONV_WIDTH = 4
NUM_BUCKETS = 32
MAX_DISTANCE = 128
EPS = 1e-6

LANES = 128
VMEM_LIMIT = 52 * 1024 * 1024
NEG = -1e30
LOG2E = math.log2(math.e)
V_ROWS = HEAD_DIM + 16
MAX_LOGIT_SPREAD = 80.0
PROBES_PER_ROUND = 4
MIDPOINT_EVERY = 4
FAST_ROUNDS = 3
STILL_SEARCHING = 1e9

SEG_A = 0
SEG_IQ = SEG_A + 3 * DSA_W
SEG_KW = SEG_IQ + IDX_HEADS * IDX_DIM
SEG_G = SEG_KW + LANES
SEG_AB = SEG_G + 4 * GDN_W
SEG_MQ = SEG_AB + LANES
IN_PACKED = SEG_MQ + MEM_W


def _bdot(a, b):
    return jnp.dot(a.astype(BF16), b.astype(BF16), preferred_element_type=F32)


def _split(x, terms):
    out = []
    for _ in range(terms - 1):
        hi = x.astype(BF16)
        out.append(hi)
        x = x - hi.astype(F32)
    out.append(x.astype(BF16))
    return out


def _dot3(a, b):
    ah, al = _split(a, 2)
    bh, bl = _split(b, 2)
    mm = lambda x, y: jnp.dot(x, y, preferred_element_type=F32)
    return mm(ah, bh) + (mm(ah, bl) + mm(al, bh))


def _dot_sel(a, sel, terms):
    selb = sel.astype(BF16)
    acc = None
    for piece in _split(a, terms):
        d = jnp.dot(piece, selb, preferred_element_type=F32)
        acc = d if acc is None else acc + d
    return acc


def _sel_dot(sel, b, terms):
    selb = sel.astype(BF16)
    acc = None
    for piece in _split(b, terms):
        d = jnp.dot(selb, piece, preferred_element_type=F32)
        acc = d if acc is None else acc + d
    return acc


def _group_sum(x, grp):
    return _dot_sel(x, grp, 2)


def _sigmoid(x):
    return 1.0 / (1.0 + jnp.exp(-x))


def _params(*sem):
    return pltpu.CompilerParams(dimension_semantics=sem, vmem_limit_bytes=VMEM_LIMIT)


def _ffn_kernel(*refs, n_mix):
    x_ref, g_ref, wg_ref, wu_ref, wd_ref = refs[:5]
    mix_refs = refs[5:5 + 2 * n_mix]
    o_ref, x_scr, h_scr, acc_scr = refs[5 + 2 * n_mix:]
    f = pl.program_id(1)

    @pl.when(f == 0)
    def _():
        x = x_ref[...]
        for k in range(n_mix):
            x = x + _bdot(mix_refs[k][...], mix_refs[n_mix + k][...])
        x_scr[...] = x
        ms = jnp.mean(x * x, axis=-1, keepdims=True)
        h_scr[...] = (x * lax.rsqrt(ms + EPS) * g_ref[...]).astype(BF16)
        acc_scr[...] = jnp.zeros_like(acc_scr)

    h = h_scr[...]
    a = jnp.dot(h, wg_ref[...], preferred_element_type=F32)
    u = jnp.dot(h, wu_ref[...], preferred_element_type=F32)
    z = (a * _sigmoid(a)) * u
    acc_scr[...] += jnp.dot(z.astype(BF16), wd_ref[...], preferred_element_type=F32)

    @pl.when(f == pl.num_programs(1) - 1)
    def _():
        o_ref[...] = x_scr[...] + 0.5 * acc_scr[...]


def _ffn(x, gain, wg, wu, wd, mix=(), mix_w=(), *, tm, tf):
    m, d = x.shape
    dff = wg.shape[1]
    row = lambda w: pl.BlockSpec((tm, w), lambda i, f: (i, 0))
    full = lambda a: pl.BlockSpec(a.shape, lambda i, f: (0,) * a.ndim)
    return pl.pallas_call(
        functools.partial(_ffn_kernel, n_mix=len(mix)),
        grid=(m // tm, dff // tf),
        in_specs=[
            row(d),
            pl.BlockSpec((1, d), lambda i, f: (0, 0)),
            pl.BlockSpec((d, tf), lambda i, f: (0, f)),
            pl.BlockSpec((d, tf), lambda i, f: (0, f)),
            pl.BlockSpec((tf, d), lambda i, f: (f, 0)),
        ] + [row(a.shape[1]) for a in mix] + [full(w) for w in mix_w],
        out_specs=row(d),
        out_shape=jax.ShapeDtypeStruct((m, d), F32),
        scratch_shapes=[pltpu.VMEM((tm, d), F32), pltpu.VMEM((tm, d), BF16), pltpu.VMEM((tm, d), F32)],
        compiler_params=_params("parallel", "arbitrary"),
        name="ffn",
    )(x, gain.reshape(1, d), wg, wu, wd, *mix, *mix_w)


def _inproj_kernel(x_ref, g_ref, w_ref, grp_ref, qg_ref, kg_ref,
                   q_ref, k_ref, vt_ref, iq_ref, ikb_ref, kw_ref, gx_ref, ab_ref, mq_ref):
    x = x_ref[...]
    ms = jnp.mean(x * x, axis=-1, keepdims=True)
    h = (x * lax.rsqrt(ms + EPS) * g_ref[...]).astype(BF16)
    p = jnp.dot(h, w_ref[...], preferred_element_type=F32)

    grp = grp_ref[...]
    dq = p[:, SEG_A:SEG_A + DSA_W]
    dk = p[:, SEG_A + DSA_W:SEG_A + 2 * DSA_W]
    dv = p[:, SEG_A + 2 * DSA_W:SEG_A + 3 * DSA_W]
    qms = _group_sum(dq * dq, grp) * (1.0 / HEAD_DIM)
    kms = _group_sum(dk * dk, grp) * (1.0 / HEAD_DIM)
    q_ref[...] = (dq * lax.rsqrt(qms + EPS) * qg_ref[...] * (HEAD_DIM ** -0.5 * LOG2E)).astype(BF16)
    k_ref[...] = (dk * lax.rsqrt(kms + EPS) * kg_ref[...]).astype(BF16)
    dvt = dv.T
    ones = jnp.ones((V_ROWS - HEAD_DIM, dvt.shape[1]), F32)
    for hd in range(DSA_HEADS):
        vt_ref[hd] = jnp.concatenate([dvt[hd * HEAD_DIM:(hd + 1) * HEAD_DIM], ones], axis=0).astype(BF16)

    iq_ref[...] = (p[:, SEG_IQ:SEG_IQ + IDX_HEADS * IDX_DIM] * (IDX_DIM ** -0.5)).astype(BF16)
    kw = p[:, SEG_KW:SEG_KW + LANES]
    kw_ref[...] = kw
    ikb_ref[...] = kw.astype(BF16)
    gx_ref[...] = p[:, SEG_G:SEG_G + 4 * GDN_W]
    ab_ref[...] = p[:, SEG_AB:SEG_AB + LANES]
    mq_ref[...] = p[:, SEG_MQ:SEG_MQ + MEM_W]


def _inproj(x, gain, w_packed, grp, qg, kg, *, tm):
    m, d = x.shape
    row = lambda w: pl.BlockSpec((tm, w), lambda i: (i, 0))
    full = lambda a: pl.BlockSpec(a.shape, lambda i: (0,) * a.ndim)
    out_shape = (
        jax.ShapeDtypeStruct((m, DSA_W), BF16),
        jax.ShapeDtypeStruct((m, DSA_W), BF16),
        jax.ShapeDtypeStruct((DSA_HEADS, V_ROWS, m), BF16),
        jax.ShapeDtypeStruct((m, IDX_HEADS * IDX_DIM), BF16),
        jax.ShapeDtypeStruct((m, LANES), BF16),
        jax.ShapeDtypeStruct((m, LANES), F32),
        jax.ShapeDtypeStruct((m, 4 * GDN_W), F32),
        jax.ShapeDtypeStruct((m, LANES), F32),
        jax.ShapeDtypeStruct((m, MEM_W), F32),
    )
    out_specs = (row(DSA_W), row(DSA_W), pl.BlockSpec((DSA_HEADS, V_ROWS, tm), lambda i: (0, 0, i)),
                 row(IDX_HEADS * IDX_DIM), row(LANES), row(LANES), row(4 * GDN_W),
                 row(LANES), row(MEM_W))
    return pl.pallas_call(
        _inproj_kernel,
        grid=(m // tm,),
        in_specs=[row(d), full(gain), full(w_packed), full(grp), full(qg), full(kg)],
        out_specs=out_specs,
        out_shape=out_shape,
        compiler_params=_params("parallel"),
        name="inproj",
    )(x, gain, w_packed, grp, qg, kg)


_DENORMAL_TOP = 0x007FFFFF


def _float_to_key(f):
    bits = lax.bitcast_convert_type(f, jnp.int32)
    mag = jnp.maximum((bits & jnp.int32(0x7FFFFFFF)) - _DENORMAL_TOP, 0)
    return jnp.where(bits >= 0, mag, -mag)


def _key_to_float(key):
    mag = jnp.abs(key)
    bits = jnp.where(mag > 0, mag + _DENORMAL_TOP, 0)
    return lax.bitcast_convert_type(jnp.where(key < 0, bits | jnp.int32(-2 ** 31), bits), F32)


def _upper_normal_quantile(p):
    pp = jnp.clip(jnp.minimum(p, 1.0 - p), 1e-30, 0.5)
    t = jnp.sqrt(-2.0 * jnp.log(pp))
    z = t - (2.515517 + t * (0.802853 + t * 0.010328)) / (1.0 + t * (1.432788 + t * (0.189269 + t * 0.001308)))
    return jnp.where(p <= 0.5, z, -z)


def _tree(op, x, group=8):
    parts = x.reshape(x.shape[0] // group, group, LANES)
    k = parts.shape[0]
    while k > 1:
        k //= 2
        parts = op(parts[:k], parts[k:2 * k])
    return parts[0]


def _dsa_kernel(q_ref, iq_ref, kwq_ref, k_ref, vt_ref, ik_ref, tab_ref, ltri_ref, o_ref,
                s_scr, iqt_scr, qpad_scr, bias_scr, mx_scr, brange_scr, *, kc, top_k):
    j = pl.program_id(1)
    per = kc // Q_BLOCK
    pad = kc - Q_BLOCK
    nch = j // per + 1
    q_pos = j * Q_BLOCK + lax.broadcasted_iota(jnp.int32, (1, LANES), 1)
    row_iota = lax.broadcasted_iota(jnp.int32, (kc, LANES), 0)

    def rows(i):
        return pl.ds(pl.multiple_of(j * Q_BLOCK - i * kc, Q_BLOCK), kc)

    def chunk_pairs(start, body, init, width=2, stop=None):
        count = jnp.maximum((nch if stop is None else stop) - start, 0)
        rest = count % width
        state = lax.fori_loop(0, rest, lambda r, st: body(start + r, st), init)

        def group(t, st):
            for u in range(width):
                st = body(start + rest + width * t + u, st)
            return st
        return lax.fori_loop(0, count // width, group, state)

    @pl.when((pl.program_id(0) == 0) & (j == 0))
    def _():
        r = lax.broadcasted_iota(jnp.int32, (2 * Q_BLOCK, LANES), 0)
        c = lax.broadcasted_iota(jnp.int32, (2 * Q_BLOCK, LANES), 1)
        dist = Q_BLOCK + c - r
        bucket = jnp.zeros_like(dist)
        for first in _BUCKET_STARTS:
            bucket = bucket + jnp.where(dist >= first, 1, 0)
        for h in range(DSA_HEADS):
            far = tab_ref[NUM_BUCKETS - 1, h]
            delta = jnp.zeros((2 * Q_BLOCK, LANES), F32)
            b_hi = jnp.float32(0.0)
            b_lo = jnp.float32(0.0)
            for b in range(NUM_BUCKETS - 1):
                delta = jnp.where(bucket == b, tab_ref[b, h] - far, delta)
                b_hi = jnp.maximum(b_hi, (tab_ref[b, h] - far) * LOG2E)
                b_lo = jnp.minimum(b_lo, (tab_ref[b, h] - far) * LOG2E)
            bias_scr[h] = jnp.where(dist >= 0, delta * LOG2E, 0.0)
            brange_scr[0, h] = b_hi
            brange_scr[1, h] = b_lo

    iqt = iq_ref[...].astype(F32).T
    zpad = jnp.zeros((LANES - IDX_DIM, LANES), F32)
    for h in range(IDX_HEADS):
        iqt_scr[:, h * LANES:(h + 1) * LANES] = jnp.concatenate(
            [iqt[h * IDX_DIM:(h + 1) * IDX_DIM], zpad], axis=0).astype(BF16)
    wt = kwq_ref[...].T[IDX_DIM:IDX_DIM + IDX_HEADS, :] * (IDX_HEADS ** -0.5)
    qt = q_ref[...].astype(F32).T
    half = lax.broadcasted_iota(jnp.int32, (LANES, LANES), 0) < HEAD_DIM
    for h in range(DSA_HEADS):
        pair = qt[(h // 2) * LANES:(h // 2 + 1) * LANES]
        keep = half if h % 2 == 0 else jnp.logical_not(half)
        qpad_scr[:, h * LANES:(h + 1) * LANES] = jnp.where(keep, pair, 0.0).astype(BF16)

    bound = tab_ref[NUM_BUCKETS, 0] * tab_ref[NUM_BUCKETS, 1]
    spread = jnp.float32(0.0)
    for h in range(DSA_HEADS):
        b_hi = brange_scr[0, h]
        b_lo = brange_scr[1, h]
        mx_scr[h:h + 1, :] = jnp.full((1, LANES), bound + b_hi, F32)
        spread = jnp.maximum(spread, 2.0 * bound + (b_hi - b_lo))
    bound_ok = spread <= MAX_LOGIT_SPREAD

    def score_chunk(i, carry, edge):
        d = jnp.dot(ik_ref[0, rows(i), :], iqt_scr[...], preferred_element_type=F32)
        acc = jnp.maximum(d[:, :LANES], 0.0) * wt[0:1, :]
        for h in range(1, IDX_HEADS):
            acc = acc + jnp.maximum(d[:, h * LANES:(h + 1) * LANES], 0.0) * wt[h:h + 1, :]
        sc = fin = acc
        if edge:
            key = row_iota + (j * Q_BLOCK - i * kc - pad)
            adm = jnp.where(key >= 0, key, q_pos + 1) <= q_pos
            sc = jnp.where(adm, acc, -jnp.inf)
            fin = jnp.where(adm, acc, 0.0)
        s_scr[rows(i), :] = sc
        tot, sq, top = carry
        return (tot + _tree(jnp.add, fin), sq + _tree(jnp.add, fin * fin),
                jnp.maximum(top, _tree(jnp.maximum, sc)))

    zero8 = jnp.zeros((8, LANES), F32)
    stats = score_chunk(0, (zero8, zero8, jnp.full((8, LANES), -jnp.inf, F32)), True)
    stats = lax.fori_loop(0, jnp.minimum(nch - 1, 1), lambda _, st: score_chunk(nch - 1, st, True), stats)
    tot, sq, top = chunk_pairs(1, lambda i, st: score_chunk(i, st, False), stats, width=4, stop=nch - 1)

    def count(pred):
        def body(i, cnt):
            return cnt + _tree(jnp.add, jnp.where(pred(s_scr[rows(i), :]), 1.0, 0.0))
        cnt = chunk_pairs(0, body, zero8, width=4)
        return jnp.sum(cnt, axis=0, keepdims=True)

    def max_below(t):
        def body(i, m):
            s = s_scr[rows(i), :]
            return jnp.maximum(m, _tree(jnp.maximum, jnp.where(s < t, s, -jnp.inf)))
        m = lax.fori_loop(0, nch, body, jnp.full((8, LANES), -jnp.inf, F32))
        return jnp.max(m, axis=0, keepdims=True)

    kf = float(top_k)
    n_adm = (q_pos + 1).astype(F32)
    mean = jnp.sum(tot, axis=0, keepdims=True) / n_adm
    std = jnp.sqrt(jnp.maximum(jnp.sum(sq, axis=0, keepdims=True) / n_adm - mean * mean, 0.0))
    first_guess = mean + _upper_normal_quantile(kf / n_adm) * std
    lowest = jnp.float32(-3.0e38)
    top1 = jnp.minimum(jnp.max(top, axis=0, keepdims=True), -lowest)

    def update(st, key):
        lo, hi, c_lo, c_hi, lo_set, hi_set, done, run, v_lo, v_hi, l_lo, l_hi = st
        key = jnp.clip(key, lo + 1, hi - 1)
        t = _key_to_float(key)
        c = count(lambda s: s >= t)
        lc = jnp.log(jnp.maximum(c, 0.5))
        active = done < 0.5
        up = active & (c >= kf)
        dn = active & (c < kf)
        lo, c_lo, lo_set = jnp.where(up, key, lo), jnp.where(up, c, c_lo), jnp.where(up, 1.0, lo_set)
        hi, c_hi, hi_set = jnp.where(dn, key, hi), jnp.where(dn, c, c_hi), jnp.where(dn, 1.0, hi_set)
        v_lo, l_lo = jnp.where(up, t, v_lo), jnp.where(up, lc, l_lo)
        v_hi, l_hi = jnp.where(dn, t, v_hi), jnp.where(dn, lc, l_hi)
        done = jnp.where((c_lo == kf) | (hi <= lo + 1), 1.0, done)
        return lo, hi, c_lo, c_hi, lo_set, hi_set, done, run, v_lo, v_hi, l_lo, l_hi

    def guess(st, midpoint):
        lo, hi, c_lo, c_hi, lo_set, hi_set, done, run, v_lo, v_hi, l_lo, l_hi = st
        both = (lo_set > 0.5) & (hi_set > 0.5)
        step = std * 0.25 * jnp.exp2(run)
        if midpoint:
            inner = 0.5 * v_lo + 0.5 * v_hi
        else:
            frac = jnp.clip((l_lo - math.log(kf - 0.5)) / jnp.maximum(l_lo - l_hi, 1e-6), 0.0, 1.0)
            inner = v_lo + (v_hi - v_lo) * frac
        t = jnp.where(both, inner,
                      jnp.where(hi_set > 0.5, v_hi - step, jnp.where(lo_set > 0.5, v_lo + step, first_guess)))
        t = jnp.where(t != t, 0.0, t)
        run = jnp.where(both, 0.0, run + 1.0)
        return _float_to_key(t), (lo, hi, c_lo, c_hi, lo_set, hi_set, done, run, v_lo, v_hi, l_lo, l_hi)

    def status(st):
        return jnp.max(jnp.where(st[6] < 0.5, STILL_SEARCHING, st[2]))

    flag0 = jnp.zeros((1, LANES), F32)
    hi0 = _float_to_key(top1) + 1
    st = (jnp.full((1, LANES), _float_to_key(lowest), jnp.int32), hi0,
          n_adm, flag0, flag0, flag0, jnp.where(n_adm <= kf, 1.0, 0.0), flag0,
          jnp.full((1, LANES), lowest, F32), _key_to_float(hi0), jnp.log(n_adm),
          jnp.full((1, LANES), math.log(0.5), F32))

    def probes(st):
        for r in range(PROBES_PER_ROUND):
            key, st = guess(st, midpoint=((r + 1) % MIDPOINT_EVERY == 0))
            st = update(st, key)
        return st

    st = lax.fori_loop(0, FAST_ROUNDS, lambda _, s: probes(s), st)

    def safe_round(carry):
        p, _, st = carry
        st = lax.fori_loop(0, jnp.minimum(p, 1),
                           lambda _, s: update(s, s[0] + lax.shift_right_logical(s[1] - s[0], 1)), st)
        lo, hi, done = st[0], st[1], st[6]
        below = _float_to_key(max_below(st[9]))
        hi = jnp.where(done < 0.5, jnp.clip(below + 1, lo + 1, hi), hi)
        done = jnp.where(hi <= lo + 1, 1.0, done)
        st = update((lo, hi) + st[2:6] + (done, st[7], st[8], _key_to_float(hi)) + st[10:], hi - 1)
        return p + 1, status(st), st

    _, most, st = lax.while_loop(lambda c: (c[1] >= STILL_SEARCHING) & (c[0] < 34), safe_round,
                                 (jnp.int32(0), status(st), st))

    thr = _key_to_float(st[0])
    has_ties = most > kf

    @pl.when(jnp.logical_not(has_ties))
    def _():
        def mask_chunk(i, carry):
            s_scr[rows(i), :] = jnp.where(s_scr[rows(i), :] >= thr, 0.0, NEG)
            return carry
        lax.fori_loop(0, nch, mask_chunk, 0)

    @pl.when(has_ties)
    def _():
        need = kf - count(lambda s: s > thr)

        def mask_chunk(t, run):
            i = nch - 1 - t
            s = s_scr[rows(i), :]
            tie = jnp.where(s == thr, 1.0, 0.0)
            pref = jnp.dot(ltri_ref[...], tie.astype(BF16), preferred_element_type=F32) + run
            tie_sel = jnp.where(pref <= need, tie, 0.0)
            sel = jnp.where(s > thr, 1.0, tie_sel)
            s_scr[rows(i), :] = jnp.where(sel > 0.5, 0.0, NEG)
            return run + jnp.sum(tie, axis=0, keepdims=True)
        lax.fori_loop(0, nch, mask_chunk, jnp.zeros((1, LANES), F32))

    def logits(i, with_bias):
        msk = s_scr[rows(i), :]
        out = []
        for pr in range(DSA_HEADS // 2):
            l2 = jnp.dot(k_ref[0, rows(i), pr * LANES:(pr + 1) * LANES],
                         qpad_scr[:, 2 * pr * LANES:(2 * pr + 2) * LANES],
                         preferred_element_type=F32)
            for e in range(2):
                l = l2[:, e * LANES:(e + 1) * LANES] + msk
                if with_bias:
                    far_rows = kc - 2 * Q_BLOCK
                    l = jnp.concatenate([l[:far_rows], l[far_rows:] + bias_scr[2 * pr + e]], axis=0)
                out.append(l)
        return out

    def max_step(i, ms, with_bias):
        return tuple(jnp.maximum(m, _tree(jnp.maximum, l)) for m, l in zip(ms, logits(i, with_bias)))

    @pl.when(jnp.logical_not(bound_ok))
    def _():
        ms = max_step(0, tuple(jnp.full((8, LANES), NEG, F32) for _ in range(DSA_HEADS)), True)
        ms = lax.fori_loop(1, nch, lambda i, m: max_step(i, m, False), ms)
        for h in range(DSA_HEADS):
            mx_scr[h:h + 1, :] = jnp.max(ms[h], axis=0, keepdims=True)

    mx = [mx_scr[h:h + 1, :] for h in range(DSA_HEADS)]

    def pv_step(i, accs, with_bias):
        return tuple(
            acc + jnp.dot(vt_ref[h, :, rows(i)], jnp.exp2(l - mx[h]).astype(BF16),
                          preferred_element_type=F32)
            for h, (acc, l) in enumerate(zip(accs, logits(i, with_bias))))

    accs = pv_step(0, tuple(jnp.zeros((V_ROWS, LANES), F32) for _ in range(DSA_HEADS)), True)
    accs = chunk_pairs(1, lambda i, a: pv_step(i, a, False), accs, width=4)
    outs = [acc[:HEAD_DIM] / acc[HEAD_DIM:HEAD_DIM + 1] for acc in accs]
    o_ref[...] = jnp.concatenate(outs, axis=0).T


def _dsa(q, iq, kw, k, vt, ikb, rel_bias, q_gain, k_gain, ltri, *, batch, seq, kc):
    nb = seq // Q_BLOCK
    top_k = min(TOPK_MAX, seq // 4)
    pad = kc - Q_BLOCK
    seqp = seq + pad
    kp = jnp.pad(k.reshape(batch, seq, DSA_W), ((0, 0), (pad, 0), (0, 0)))
    ikp = jnp.pad(ikb.reshape(batch, seq, LANES), ((0, 0), (pad, 0), (0, 0)))
    vtp = jnp.pad(vt.reshape(DSA_HEADS, V_ROWS, batch, seq),
                  ((0, 0), (0, 0), (0, 0), (pad, 0))).reshape(DSA_HEADS, V_ROWS, batch * seqp)
    kmax = (HEAD_DIM ** 0.5 * 1.01) * jnp.max(jnp.abs(k_gain.astype(F32)))
    qmax = (LOG2E * 1.01) * jnp.max(jnp.abs(q_gain.astype(F32)))
    norms = jnp.stack([kmax, qmax] + [jnp.zeros((), F32)] * (DSA_HEADS - 2)).reshape(1, DSA_HEADS)
    table = jnp.concatenate([rel_bias.astype(F32), norms], axis=0)
    qrow = lambda w: pl.BlockSpec((Q_BLOCK, w), lambda b, j: (b * nb + j, 0))
    return pl.pallas_call(
        functools.partial(_dsa_kernel, kc=kc, top_k=top_k),
        grid=(batch, nb),
        in_specs=[
            qrow(DSA_W), qrow(IDX_HEADS * IDX_DIM), qrow(LANES),
            pl.BlockSpec((1, seqp, DSA_W), lambda b, j: (b, 0, 0)),
            pl.BlockSpec((DSA_HEADS, V_ROWS, seqp), lambda b, j: (0, 0, b)),
            pl.BlockSpec((1, seqp, LANES), lambda b, j: (b, 0, 0)),
            pl.BlockSpec(memory_space=pltpu.SMEM),
            pl.BlockSpec(ltri.shape, lambda b, j: (0, 0)),
        ],
        out_specs=qrow(DSA_W),
        out_shape=jax.ShapeDtypeStruct((batch * seq, DSA_W), F32),
        scratch_shapes=[
            pltpu.VMEM((seqp, LANES), F32),
            pltpu.VMEM((LANES, IDX_HEADS * LANES), BF16),
            pltpu.VMEM((LANES, DSA_HEADS * LANES), BF16),
            pltpu.VMEM((DSA_HEADS, 2 * Q_BLOCK, LANES), F32),
            pltpu.VMEM((8, LANES), F32),
            pltpu.SMEM((2, DSA_HEADS), F32),
        ],
        compiler_params=_params("arbitrary", "arbitrary"),
        name="dsa",
    )(q, iq, kw, kp, vtp, ikp, table, ltri)


GDN_GROUP = 2


def _gdn_kernel(x_ref, ab_ref, cw_ref, ea_ref, eb_ref, alog_ref, dtb_ref, gain_ref, grp_ref,
                o_ref, xpad_scr, qkv_scr, gb_scr, gc_scr, gr_scr, pre_scr, st_scr, o_scr, *, tb):
    t = pl.program_id(1)
    cin = 3 * GDN_W
    n_chunks = tb // GDN_CHUNK
    pairs = range(GDN_HEADS // 2)
    lanes = lambda p: slice(p * LANES, (p + 1) * LANES)

    @pl.when(t == 0)
    def _():
        xpad_scr[0:8, :] = jnp.zeros((8, cin), F32)
        st_scr[...] = jnp.zeros_like(st_scr)

    x = x_ref[...]
    xpad_scr[8:8 + tb, :] = x[:, :cin]
    conv = jnp.zeros((tb, cin), F32)
    for jj in range(CONV_WIDTH):
        conv = conv + cw_ref[jj:jj + 1, :] * xpad_scr[pl.ds(8 - (CONV_WIDTH - 1) + jj, tb), :]
    xpad_scr[0:8, :] = x[tb - 8:tb, :cin]
    qkv = conv * _sigmoid(conv)
    grp = grp_ref[...]
    q = qkv[:, :GDN_W]
    k = qkv[:, GDN_W:2 * GDN_W]
    qkv_scr[:, :GDN_W] = q * lax.rsqrt(_group_sum(q * q, grp) + EPS) * (HEAD_DIM ** -0.5)
    qkv_scr[:, GDN_W:2 * GDN_W] = k * lax.rsqrt(_group_sum(k * k, grp) + EPS)
    qkv_scr[:, 2 * GDN_W:] = qkv[:, 2 * GDN_W:]

    ab = ab_ref[...]
    a_e = _dot_sel(ab, ea_ref[...], 3) + dtb_ref[...]
    b_e = _dot_sel(ab, eb_ref[...], 3)
    softplus = jnp.maximum(a_e, 0.0) + jnp.log(1.0 + jnp.exp(-jnp.abs(a_e)))
    gb_scr[:, :GDN_W] = -jnp.exp(alog_ref[...]) * softplus
    gb_scr[:, GDN_W:] = _sigmoid(b_e)

    r64 = lax.broadcasted_iota(jnp.int32, (GDN_CHUNK, LANES), 0)
    c64 = lax.broadcasted_iota(jnp.int32, (GDN_CHUNK, LANES), 1) % GDN_CHUNK
    causal = c64 <= r64
    strict = c64 < r64
    eye = jnp.where(c64 == r64, 1.0, 0.0)
    r128 = lax.broadcasted_iota(jnp.int32, (LANES, LANES), 0)
    c128 = lax.broadcasted_iota(jnp.int32, (LANES, LANES), 1)
    bdmask = (r128 // HEAD_DIM) == (c128 // HEAD_DIM)

    def bd(m):
        return jnp.where(bdmask, jnp.concatenate([m, m], axis=0), 0.0)

    lt_r = lax.broadcasted_iota(jnp.int32, (GDN_CHUNK, GDN_CHUNK), 0)
    lt_c = lax.broadcasted_iota(jnp.int32, (GDN_CHUNK, GDN_CHUNK), 1)
    ltri = jnp.where(lt_c <= lt_r, 1.0, 0.0)
    ones8 = jnp.ones((8, GDN_CHUNK), F32)
    up_r = lax.broadcasted_iota(jnp.int32, (GDN_CHUNK, GDN_W), 0)
    up_c = lax.broadcasted_iota(jnp.int32, (GDN_CHUNK, GDN_W), 1) % GDN_CHUNK
    upper = jnp.where(up_r <= up_c, 1.0, 0.0)
    for ci in range(n_chunks):
        g = gb_scr[ci * GDN_CHUNK:(ci + 1) * GDN_CHUNK, :GDN_W]
        gc_scr[ci * GDN_CHUNK:(ci + 1) * GDN_CHUNK, :] = _sel_dot(ltri, g, 3)
        gr_scr[ci * 8:(ci + 1) * 8, :] = _sel_dot(ones8, g * upper, 3)

    def par_body(gi, carry):
        inst = [(u, p) for u in range(GDN_GROUP) for p in pairs]
        rows = [pl.ds(pl.multiple_of((gi * GDN_GROUP + u) * GDN_CHUNK, GDN_CHUNK), GDN_CHUNK)
                for u in range(GDN_GROUP)]
        grow = [pl.ds(pl.multiple_of((gi * GDN_GROUP + u) * 8, 8), 8) for u in range(GDN_GROUP)]
        off = lambda s, p: slice(s * GDN_W + p * LANES, s * GDN_W + (p + 1) * LANES)
        qp = [qkv_scr[rows[u], off(0, p)] for u, p in inst]
        kp = [qkv_scr[rows[u], off(1, p)] for u, p in inst]
        vp = [qkv_scr[rows[u], off(2, p)] for u, p in inst]
        beta = [gb_scr[rows[u], off(1, p)] for u, p in inst]
        gc = [gc_scr[rows[u], lanes(p)] for u, p in inst]
        gr = [gr_scr[grow[u], lanes(p)][0:1, :] for u, p in inst]
        dmat = [jnp.exp(jnp.where(causal, a - b, NEG)) for a, b in zip(gc, gr)]
        kbd = [jnp.where(bdmask, jnp.concatenate([a, a], axis=0).T, 0.0) for a in kp]
        kb = [a * b for a, b in zip(kp, beta)]
        kk = [_bdot(a, b) for a, b in zip(kb, kbd)]
        qk = [_bdot(a, b) for a, b in zip(qp, kbd)]
        lm = [jnp.where(strict, a * d, 0.0) for a, d in zip(kk, dmat)]
        aintra = [a * d for a, d in zip(qk, dmat)]
        tinv = [eye - a for a in lm]
        lpow = lm
        for _ in range(5):
            lpow = [_dot3(a, bd(a)) for a in lpow]
            tinv = [a + _dot3(a, bd(b)) for a, b in zip(tinv, lpow)]
        egc = [jnp.exp(a) for a in gc]
        un = [_dot3(a, bd(v * b)) for a, v, b in zip(tinv, vp, beta)]
        wn = [_dot3(a, bd(b * e)) for a, b, e in zip(tinv, kb, egc)]
        for n, (u, p) in enumerate(inst):
            pre_scr[0, rows[u], lanes(p)] = un[n]
            pre_scr[1, rows[u], lanes(p)] = wn[n]
            pre_scr[2, rows[u], lanes(p)] = aintra[n]
            pre_scr[3, rows[u], lanes(p)] = qp[n] * egc[n]
            pre_scr[4, rows[u], lanes(p)] = kp[n] * jnp.exp(gc[n][GDN_CHUNK - 1:GDN_CHUNK, :] - gc[n])
        return carry

    lax.fori_loop(0, n_chunks // GDN_GROUP, par_body, 0)

    def rec_body(ci, carry):
        rows = pl.ds(pl.multiple_of(ci * GDN_CHUNK, GDN_CHUNK), GDN_CHUNK)
        tail = gc_scr[pl.ds(pl.multiple_of(ci * GDN_CHUNK + GDN_CHUNK - 8, 8), 8), :]
        egl = jnp.exp(tail[7:8, :])
        sbd = [st_scr[p] for p in pairs]
        un, wn, aintra, qe, kd = [[pre_scr[s, rows, lanes(p)] for p in pairs] for s in range(5)]
        ws = [_bdot(a, s) for a, s in zip(wn, sbd)]
        qs = [_bdot(a, s) for a, s in zip(qe, sbd)]
        v_new = [a - b for a, b in zip(un, ws)]
        o = [a + _bdot(b, bd(v)) for a, b, v in zip(qs, aintra, v_new)]
        upd = [_bdot(a.T, v) for a, v in zip(kd, v_new)]
        for p in pairs:
            st_scr[p] = sbd[p] * egl[:, lanes(p)] + jnp.where(bdmask, upd[p], 0.0)
            o_scr[rows, lanes(p)] = o[p]
        return carry

    lax.fori_loop(0, n_chunks, rec_body, 0)

    o = o_scr[...]
    oms = _group_sum(o * o, grp) * (1.0 / HEAD_DIM)
    z = x[:, cin:]
    o_ref[...] = o * lax.rsqrt(oms + EPS) * gain_ref[...] * (z * _sigmoid(z))


def _gdn(gx, ab, conv_w, ea, eb, alog, dtb, gain, grp, *, batch, seq, tb):
    nt = seq // tb
    cin = 3 * GDN_W
    row = lambda w: pl.BlockSpec((tb, w), lambda b, t: (b * nt + t, 0))
    full = lambda a: pl.BlockSpec(a.shape, lambda b, t: (0,) * a.ndim)
    return pl.pallas_call(
        functools.partial(_gdn_kernel, tb=tb),
        grid=(batch, nt),
        in_specs=[row(4 * GDN_W), row(LANES), full(conv_w), full(ea), full(eb), full(alog),
                  full(dtb), full(gain), full(grp)],
        out_specs=row(GDN_W),
        out_shape=jax.ShapeDtypeStruct((batch * seq, GDN_W), F32),
        scratch_shapes=[
            pltpu.VMEM((tb + 8, cin), F32),
            pltpu.VMEM((tb, cin), F32),
            pltpu.VMEM((tb, 2 * GDN_W), F32),
            pltpu.VMEM((tb, GDN_W), F32),
            pltpu.VMEM((tb // 8, GDN_W), F32),
            pltpu.VMEM((5, tb, GDN_W), F32),
            pltpu.VMEM((GDN_HEADS // 2, LANES, LANES), F32),
            pltpu.VMEM((tb, GDN_W), F32),
        ],
        compiler_params=_params("arbitrary", "arbitrary"),
        name="gdn",
    )(gx, ab, conv_w, ea, eb, alog, dtb, gain, grp)


def _memkv_kernel(mem_ref, g_ref, w_ref, kg_ref, grp_ref, kt_ref, vp_ref):
    x = mem_ref[0]
    ms = jnp.mean(x * x, axis=-1, keepdims=True)
    h = (x * lax.rsqrt(ms + EPS) * g_ref[...]).astype(BF16)
    kv = jnp.dot(h, w_ref[...], preferred_element_type=F32)
    km = kv[:, :MEM_W]
    vm = kv[:, MEM_W:]
    kms = _group_sum(km * km, grp_ref[...]) * (1.0 / HEAD_DIM)
    kt = (km * lax.rsqrt(kms + EPS) * kg_ref[...]).T
    n_mem = x.shape[0]
    top = lax.broadcasted_iota(jnp.int32, (LANES, n_mem), 0) < HEAD_DIM
    left = lax.broadcasted_iota(jnp.int32, (n_mem, LANES), 1) < HEAD_DIM
    for hh in range(MEM_HEADS):
        pr = slice((hh // 2) * LANES, (hh // 2 + 1) * LANES)
        keep_r = top if hh % 2 == 0 else jnp.logical_not(top)
        keep_c = left if hh % 2 == 0 else jnp.logical_not(left)
        kt_ref[0, hh] = jnp.where(keep_r, kt[pr, :], 0.0).astype(BF16)
        vp_ref[0, hh] = jnp.where(keep_c, vm[:, pr], 0.0).astype(BF16)


def _memkv(mem, gain, w, kg, grp):
    b, n_mem, d = mem.shape
    full = lambda a: pl.BlockSpec(a.shape, lambda i: (0,) * a.ndim)
    return pl.pallas_call(
        _memkv_kernel,
        grid=(b,),
        in_specs=[pl.BlockSpec((1, n_mem, d), lambda i: (i, 0, 0)), full(gain), full(w), full(kg),
                  full(grp)],
        out_specs=(pl.BlockSpec((1, MEM_HEADS, LANES, n_mem), lambda i: (i, 0, 0, 0)),
                   pl.BlockSpec((1, MEM_HEADS, n_mem, LANES), lambda i: (i, 0, 0, 0))),
        out_shape=(jax.ShapeDtypeStruct((b, MEM_HEADS, LANES, n_mem), BF16),
                   jax.ShapeDtypeStruct((b, MEM_HEADS, n_mem, LANES), BF16)),
        compiler_params=_params("parallel"),
        name="memkv",
    )(mem, gain, w, kg, grp)


def _memattn_kernel(q_ref, qg_ref, grp_ref, kt_ref, vp_ref, o_ref):
    q = q_ref[...]
    qms = _group_sum(q * q, grp_ref[...]) * (1.0 / HEAD_DIM)
    qn = (q * lax.rsqrt(qms + EPS) * qg_ref[...] * (HEAD_DIM ** -0.5)).astype(BF16)
    outs = []
    for pr in range(MEM_HEADS // 2):
        qp = qn[:, pr * LANES:(pr + 1) * LANES]
        acc = None
        for e in range(2):
            hh = 2 * pr + e
            l = jnp.dot(qp, kt_ref[0, hh], preferred_element_type=F32)
            l = l - jnp.max(l, axis=-1, keepdims=True)
            p = jnp.exp(l)
            p = p / jnp.sum(p, axis=-1, keepdims=True)
            o = jnp.dot(p.astype(BF16), vp_ref[0, hh], preferred_element_type=F32)
            acc = o if acc is None else acc + o
        outs.append(acc)
    o_ref[...] = jnp.concatenate(outs, axis=-1)


def _memattn(mq, qg, grp, kt, vp, *, batch, seq, tm):
    nt = seq // tm
    n_mem = kt.shape[-1]
    full = lambda a: pl.BlockSpec(a.shape, lambda b, t: (0,) * a.ndim)
    return pl.pallas_call(
        _memattn_kernel,
        grid=(batch, nt),
        in_specs=[
            pl.BlockSpec((tm, MEM_W), lambda b, t: (b * nt + t, 0)), full(qg), full(grp),
            pl.BlockSpec((1, MEM_HEADS, LANES, n_mem), lambda b, t: (b, 0, 0, 0)),
            pl.BlockSpec((1, MEM_HEADS, n_mem, LANES), lambda b, t: (b, 0, 0, 0)),
        ],
        out_specs=pl.BlockSpec((tm, MEM_W), lambda b, t: (b * nt + t, 0)),
        out_shape=jax.ShapeDtypeStruct((batch * seq, MEM_W), F32),
        compiler_params=_params("parallel", "parallel"),
        name="memattn",
    )(mq, qg, grp, kt, vp)


def _group_ones(width):
    idx = np.arange(width) // HEAD_DIM
    return jnp.asarray((idx[:, None] == idx[None, :]).astype(np.float32))


def _t5_bucket_np(n):
    max_exact = NUM_BUCKETS // 2
    nf = np.maximum(n, 1).astype(np.float32)
    large = max_exact + (np.log(nf / np.float32(max_exact)) / np.float32(math.log(MAX_DISTANCE / max_exact))
                         * (NUM_BUCKETS - max_exact)).astype(np.int32)
    large = np.minimum(large, NUM_BUCKETS - 1)
    return np.where(n < max_exact, n, large)


def _bucket_starts():
    dist = np.arange(2 * Q_BLOCK)
    bucket = _t5_bucket_np(dist)
    assert (np.diff(bucket) >= 0).all() and bucket[-1] == NUM_BUCKETS - 1
    return tuple(int(dist[bucket >= b].min()) for b in range(1, NUM_BUCKETS))


_BUCKET_STARTS = _bucket_starts()


def _cast_kernel(w_ref, o_ref):
    o_ref[...] = w_ref[...].astype(BF16)


def _to_bf16(w, *, rows):
    nl, nr, nc = w.shape
    spec = pl.BlockSpec((1, rows, nc), lambda l, r: (l, r, 0))
    return pl.pallas_call(
        _cast_kernel,
        grid=(nl, nr // rows),
        in_specs=[spec],
        out_specs=spec,
        out_shape=jax.ShapeDtypeStruct(w.shape, BF16),
        compiler_params=_params("parallel", "parallel"),
        name="cast",
    )(w)


def _pack_w_in(w):
    o = 3 * DSA_W
    iq = w[:, o:o + IDX_HEADS * IDX_DIM]
    o += IDX_HEADS * IDX_DIM
    kw = w[:, o:o + IDX_DIM + IDX_HEADS]
    o += IDX_DIM + IDX_HEADS
    g = w[:, o:o + 4 * GDN_W]
    o += 4 * GDN_W
    ab = w[:, o:o + 2 * GDN_HEADS]
    o += 2 * GDN_HEADS
    mq = w[:, o:o + MEM_W]
    pad = lambda a: jnp.pad(a, ((0, 0), (0, LANES - a.shape[1])))
    return jnp.concatenate([w[:, :3 * DSA_W], iq, pad(kw), g, pad(ab), mq], axis=1)


def _tile_heads(v, heads):
    return jnp.tile(v.astype(F32), heads).reshape(1, heads * HEAD_DIM)


def kernel(x, mem, ffn1_norm, ffn1_w_gate, ffn1_w_up, ffn1_w_down, mix_norm, w_in, dsa_q_norm, dsa_k_norm, rel_bias, gdn_conv, gdn_A_log, gdn_dt_bias, gdn_out_norm, mem_norm, w_mem_kv, mem_q_norm, mem_k_norm, w_out, ffn2_norm, ffn2_w_gate, ffn2_w_up, ffn2_w_down):
    batch, seq, d = x.shape
    depth = w_in.shape[0]
    m = batch * seq
    tm = min(512, seq)
    kc = min(512, seq)
    dff = ffn1_w_gate.shape[-1]
    tf = dff // 2 if (dff // 2) % LANES == 0 else dff

    grp_a = _group_ones(DSA_W)
    grp_m = _group_ones(MEM_W)
    ltri = jnp.asarray(np.tril(np.ones((kc, kc), np.float32))).astype(BF16)
    heads_of = np.arange(GDN_W) // HEAD_DIM
    ea = jnp.asarray((np.arange(LANES)[:, None] == heads_of[None, :]).astype(np.float32))
    eb = jnp.asarray((np.arange(LANES)[:, None] == heads_of[None, :] + GDN_HEADS).astype(np.float32))

    w_in_b = _to_bf16(w_in, rows=min(256, d))
    xf = x.reshape(m, d)
    for l in range(depth):
        xf = _ffn(xf, ffn1_norm[l], ffn1_w_gate[l].astype(BF16), ffn1_w_up[l].astype(BF16),
                  ffn1_w_down[l].astype(BF16), tm=tm, tf=tf)
        (q, k, vt, iq, ikb, kw, gx, ab, mq) = _inproj(
            xf, mix_norm[l].reshape(1, d), _pack_w_in(w_in_b[l]), grp_a,
            _tile_heads(dsa_q_norm[l], DSA_HEADS), _tile_heads(dsa_k_norm[l], DSA_HEADS), tm=tm)
        out_a = _dsa(q, iq, kw, k, vt, ikb, rel_bias, dsa_q_norm[l], dsa_k_norm[l], ltri,
                     batch=batch, seq=seq, kc=kc)
        out_b = _gdn(gx, ab, gdn_conv[l], ea, eb,
                     jnp.repeat(gdn_A_log[l].astype(F32), HEAD_DIM).reshape(1, GDN_W),
                     jnp.repeat(gdn_dt_bias[l].astype(F32), HEAD_DIM).reshape(1, GDN_W),
                     _tile_heads(gdn_out_norm[l], GDN_HEADS), grp_a, batch=batch, seq=seq, tb=tm)
        kt, vp = _memkv(mem, mem_norm[l].reshape(1, d), w_mem_kv[l].astype(BF16),
                        _tile_heads(mem_k_norm[l], MEM_HEADS), grp_m)
        out_c = _memattn(mq, _tile_heads(mem_q_norm[l], MEM_HEADS), grp_m, kt, vp,
                         batch=batch, seq=seq, tm=tm)
        wo = w_out[l].astype(BF16)
        xf = _ffn(xf, ffn2_norm[l], ffn2_w_gate[l].astype(BF16), ffn2_w_up[l].astype(BF16),
                  ffn2_w_down[l].astype(BF16), mix=(out_a, out_b, out_c),
                  mix_w=(wo[:DSA_W], wo[DSA_W:DSA_W + GDN_W], wo[DSA_W + GDN_W:]), tm=tm, tf=tf)
    return xf.reshape(batch, seq, d)
```

```python
import functools
import math

import jax
import jax.numpy as jnp
import numpy as np
from jax import lax
from jax.experimental import pallas as pl
from jax.experimental.pallas import tpu as pltpu

F32 = jnp.float32
BF16 = jnp.bfloat16

HEAD_DIM = 64
DSA_HEADS = 6
GDN_HEADS = 6
MEM_HEADS = 4
DSA_W = DSA_HEADS * HEAD_DIM
GDN_W = GDN_HEADS * HEAD_DIM
MEM_W = MEM_HEADS * HEAD_DIM
IDX_HEADS = 8
IDX_DIM = 32
TOPK_MAX = 256
Q_BLOCK = 128
GDN_CHUNK = 64
CONV_WIDTH = 4
NUM_BUCKETS = 32
MAX_DISTANCE = 128
EPS = 1e-6

LANES = 128
VMEM_LIMIT = 52 * 1024 * 1024
NEG = -1e30
LOG2E = math.log2(math.e)
V_ROWS = HEAD_DIM + 16
MAX_LOGIT_SPREAD = 80.0
PROBES_PER_ROUND = 4
MIDPOINT_EVERY = 4
FAST_ROUNDS = 3
STILL_SEARCHING = 1e9

SEG_A = 0
SEG_IQ = SEG_A + 3 * DSA_W
SEG_KW = SEG_IQ + IDX_HEADS * IDX_DIM
SEG_G = SEG_KW + LANES
SEG_AB = SEG_G + 4 * GDN_W
SEG_MQ = SEG_AB + LANES
IN_PACKED = SEG_MQ + MEM_W


def _bdot(a, b):
    return jnp.dot(a.astype(BF16), b.astype(BF16), preferred_element_type=F32)


def _split(x, terms):
    out = []
    for _ in range(terms - 1):
        hi = x.astype(BF16)
        out.append(hi)
        x = x - hi.astype(F32)
    out.append(x.astype(BF16))
    return out


def _dot3(a, b):
    ah, al = _split(a, 2)
    bh, bl = _split(b, 2)
    mm = lambda x, y: jnp.dot(x, y, preferred_element_type=F32)
    return mm(ah, bh) + (mm(ah, bl) + mm(al, bh))


def _dot_sel(a, sel, terms):
    selb = sel.astype(BF16)
    acc = None
    for piece in _split(a, terms):
        d = jnp.dot(piece, selb, preferred_element_type=F32)
        acc = d if acc is None else acc + d
    return acc


def _sel_dot(sel, b, terms):
    selb = sel.astype(BF16)
    acc = None
    for piece in _split(b, terms):
        d = jnp.dot(selb, piece, preferred_element_type=F32)
        acc = d if acc is None else acc + d
    return acc


def _group_sum(x, grp):
    return _dot_sel(x, grp, 2)


def _sigmoid(x):
    return 1.0 / (1.0 + jnp.exp(-x))


def _params(*sem):
    return pltpu.CompilerParams(dimension_semantics=sem, vmem_limit_bytes=VMEM_LIMIT)


def _ffn_kernel(*refs, n_mix):
    x_ref, g_ref, wg_ref, wu_ref, wd_ref = refs[:5]
    mix_refs = refs[5:5 + 2 * n_mix]
    o_ref, x_scr, h_scr, acc_scr = refs[5 + 2 * n_mix:]
    f = pl.program_id(1)

    @pl.when(f == 0)
    def _():
        x = x_ref[...]
        for k in range(n_mix):
            x = x + _bdot(mix_refs[k][...], mix_refs[n_mix + k][...])
        x_scr[...] = x
        ms = jnp.mean(x * x, axis=-1, keepdims=True)
        h_scr[...] = (x * lax.rsqrt(ms + EPS) * g_ref[...]).astype(BF16)
        acc_scr[...] = jnp.zeros_like(acc_scr)

    h = h_scr[...]
    a = jnp.dot(h, wg_ref[...], preferred_element_type=F32)
    u = jnp.dot(h, wu_ref[...], preferred_element_type=F32)
    z = (a * _sigmoid(a)) * u
    acc_scr[...] += jnp.dot(z.astype(BF16), wd_ref[...], preferred_element_type=F32)

    @pl.when(f == pl.num_programs(1) - 1)
    def _():
        o_ref[...] = x_scr[...] + 0.5 * acc_scr[...]


def _ffn(x, gain, wg, wu, wd, mix=(), mix_w=(), *, tm, tf):
    m, d = x.shape
    dff = wg.shape[1]
    row = lambda w: pl.BlockSpec((tm, w), lambda i, f: (i, 0))
    full = lambda a: pl.BlockSpec(a.shape, lambda i, f: (0,) * a.ndim)
    return pl.pallas_call(
        functools.partial(_ffn_kernel, n_mix=len(mix)),
        grid=(m // tm, dff // tf),
        in_specs=[
            row(d),
            pl.BlockSpec((1, d), lambda i, f: (0, 0)),
            pl.BlockSpec((d, tf), lambda i, f: (0, f)),
            pl.BlockSpec((d, tf), lambda i, f: (0, f)),
            pl.BlockSpec((tf, d), lambda i, f: (f, 0)),
        ] + [row(a.shape[1]) for a in mix] + [full(w) for w in mix_w],
        out_specs=row(d),
        out_shape=jax.ShapeDtypeStruct((m, d), F32),
        scratch_shapes=[pltpu.VMEM((tm, d), F32), pltpu.VMEM((tm, d), BF16), pltpu.VMEM((tm, d), F32)],
        compiler_params=_params("parallel", "arbitrary"),
        name="ffn",
    )(x, gain.reshape(1, d), wg, wu, wd, *mix, *mix_w)


def _inproj_kernel(x_ref, g_ref, w_ref, grp_ref, qg_ref, kg_ref,
                   q_ref, k_ref, vt_ref, iq_ref, ikb_ref, kw_ref, gx_ref, ab_ref, mq_ref):
    x = x_ref[...]
    ms = jnp.mean(x * x, axis=-1, keepdims=True)
    h = (x * lax.rsqrt(ms + EPS) * g_ref[...]).astype(BF16)
    p = jnp.dot(h, w_ref[...], preferred_element_type=F32)

    grp = grp_ref[...]
    dq = p[:, SEG_A:SEG_A + DSA_W]
    dk = p[:, SEG_A + DSA_W:SEG_A + 2 * DSA_W]
    dv = p[:, SEG_A + 2 * DSA_W:SEG_A + 3 * DSA_W]
    qms = _group_sum(dq * dq, grp) * (1.0 / HEAD_DIM)
    kms = _group_sum(dk * dk, grp) * (1.0 / HEAD_DIM)
    q_ref[...] = (dq * lax.rsqrt(qms + EPS) * qg_ref[...] * (HEAD_DIM ** -0.5 * LOG2E)).astype(BF16)
    k_ref[...] = (dk * lax.rsqrt(kms + EPS) * kg_ref[...]).astype(BF16)
    dvt = dv.T
    ones = jnp.ones((V_ROWS - HEAD_DIM, dvt.shape[1]), F32)
    for hd in range(DSA_HEADS):
        vt_ref[0, hd] = jnp.concatenate([dvt[hd * HEAD_DIM:(hd + 1) * HEAD_DIM], ones], axis=0).astype(BF16)

    iq_ref[...] = (p[:, SEG_IQ:SEG_IQ + IDX_HEADS * IDX_DIM] * (IDX_DIM ** -0.5)).astype(BF16)
    kw = p[:, SEG_KW:SEG_KW + LANES]
    kw_ref[...] = kw
    ikb_ref[...] = kw.astype(BF16)
    gx_ref[...] = p[:, SEG_G:SEG_G + 4 * GDN_W]
    ab_ref[...] = p[:, SEG_AB:SEG_AB + LANES]
    mq_ref[...] = p[:, SEG_MQ:SEG_MQ + MEM_W]


def _inproj(x, gain, w_packed, grp, qg, kg, *, batch, tm):
    m, d = x.shape
    row = lambda w: pl.BlockSpec((tm, w), lambda i: (i, 0))
    full = lambda a: pl.BlockSpec(a.shape, lambda i: (0,) * a.ndim)
    out_shape = (
        jax.ShapeDtypeStruct((m, DSA_W), BF16),
        jax.ShapeDtypeStruct((m, DSA_W), BF16),
        jax.ShapeDtypeStruct((batch, DSA_HEADS, V_ROWS, m // batch), BF16),
        jax.ShapeDtypeStruct((m, IDX_HEADS * IDX_DIM), BF16),
        jax.ShapeDtypeStruct((m, LANES), BF16),
        jax.ShapeDtypeStruct((m, LANES), F32),
        jax.ShapeDtypeStruct((m, 4 * GDN_W), F32),
        jax.ShapeDtypeStruct((m, LANES), F32),
        jax.ShapeDtypeStruct((m, MEM_W), F32),
    )
    nt = m // batch // tm
    out_specs = (row(DSA_W), row(DSA_W),
                 pl.BlockSpec((1, DSA_HEADS, V_ROWS, tm), lambda i: (i // nt, 0, 0, i % nt)),
                 row(IDX_HEADS * IDX_DIM), row(LANES), row(LANES), row(4 * GDN_W),
                 row(LANES), row(MEM_W))
    return pl.pallas_call(
        _inproj_kernel,
        grid=(m // tm,),
        in_specs=[row(d), full(gain), full(w_packed), full(grp), full(qg), full(kg)],
        out_specs=out_specs,
        out_shape=out_shape,
        compiler_params=_params("parallel"),
        name="inproj",
    )(x, gain, w_packed, grp, qg, kg)


_DENORMAL_TOP = 0x007FFFFF


def _float_to_key(f):
    bits = lax.bitcast_convert_type(f, jnp.int32)
    mag = jnp.maximum((bits & jnp.int32(0x7FFFFFFF)) - _DENORMAL_TOP, 0)
    return jnp.where(bits >= 0, mag, -mag)


def _key_to_float(key):
    mag = jnp.abs(key)
    bits = jnp.where(mag > 0, mag + _DENORMAL_TOP, 0)
    return lax.bitcast_convert_type(jnp.where(key < 0, bits | jnp.int32(-2 ** 31), bits), F32)


def _upper_normal_quantile(p):
    pp = jnp.clip(jnp.minimum(p, 1.0 - p), 1e-30, 0.5)
    t = jnp.sqrt(-2.0 * jnp.log(pp))
    z = t - (2.515517 + t * (0.802853 + t * 0.010328)) / (1.0 + t * (1.432788 + t * (0.189269 + t * 0.001308)))
    return jnp.where(p <= 0.5, z, -z)


def _tree(op, x, group=8):
    parts = x.reshape(x.shape[0] // group, group, LANES)
    k = parts.shape[0]
    while k > 1:
        k //= 2
        parts = op(parts[:k], parts[k:2 * k])
    return parts[0]


def _dsa_kernel(q_ref, iq_ref, kwq_ref, k_ref, vt_ref, ik_ref, tab_ref, ltri_ref, o_ref,
                s_scr, iqt_scr, qpad_scr, bias_scr, mx_scr, brange_scr, *, kc, top_k):
    j = pl.program_id(1)
    per = kc // Q_BLOCK
    pad = kc - Q_BLOCK
    nch = j // per + 1
    q_pos = j * Q_BLOCK + lax.broadcasted_iota(jnp.int32, (1, LANES), 1)
    row_iota = lax.broadcasted_iota(jnp.int32, (kc, LANES), 0)

    def rows(i):
        return pl.ds(pl.multiple_of(j * Q_BLOCK - i * kc, Q_BLOCK), kc)

    def chunk_pairs(start, body, init, width=2, stop=None):
        count = jnp.maximum((nch if stop is None else stop) - start, 0)
        rest = count % width
        state = lax.fori_loop(0, rest, lambda r, st: body(start + r, st), init)

        def group(t, st):
            for u in range(width):
                st = body(start + rest + width * t + u, st)
            return st
        return lax.fori_loop(0, count // width, group, state)

    @pl.when((pl.program_id(0) == 0) & (j == 0))
    def _():
        r = lax.broadcasted_iota(jnp.int32, (2 * Q_BLOCK, LANES), 0)
        c = lax.broadcasted_iota(jnp.int32, (2 * Q_BLOCK, LANES), 1)
        dist = Q_BLOCK + c - r
        bucket = jnp.zeros_like(dist)
        for first in _BUCKET_STARTS:
            bucket = bucket + jnp.where(dist >= first, 1, 0)
        for h in range(DSA_HEADS):
            far = tab_ref[NUM_BUCKETS - 1, h]
            delta = jnp.zeros((2 * Q_BLOCK, LANES), F32)
            b_hi = jnp.float32(0.0)
            b_lo = jnp.float32(0.0)
            for b in range(NUM_BUCKETS - 1):
                delta = jnp.where(bucket == b, tab_ref[b, h] - far, delta)
                b_hi = jnp.maximum(b_hi, (tab_ref[b, h] - far) * LOG2E)
                b_lo = jnp.minimum(b_lo, (tab_ref[b, h] - far) * LOG2E)
            bias_scr[h] = jnp.where(dist >= 0, delta * LOG2E, 0.0)
            brange_scr[0, h] = b_hi
            brange_scr[1, h] = b_lo

    iqt = iq_ref[...].astype(F32).T
    zpad = jnp.zeros((LANES - IDX_DIM, LANES), F32)
    for h in range(IDX_HEADS):
        iqt_scr[:, h * LANES:(h + 1) * LANES] = jnp.concatenate(
            [iqt[h * IDX_DIM:(h + 1) * IDX_DIM], zpad], axis=0).astype(BF16)
    wt = kwq_ref[...].T[IDX_DIM:IDX_DIM + IDX_HEADS, :] * (IDX_HEADS ** -0.5)
    qt = q_ref[...].astype(F32).T
    half = lax.broadcasted_iota(jnp.int32, (LANES, LANES), 0) < HEAD_DIM
    for h in range(DSA_HEADS):
        pair = qt[(h // 2) * LANES:(h // 2 + 1) * LANES]
        keep = half if h % 2 == 0 else jnp.logical_not(half)
        qpad_scr[:, h * LANES:(h + 1) * LANES] = jnp.where(keep, pair, 0.0).astype(BF16)

    bound = tab_ref[NUM_BUCKETS, 0] * tab_ref[NUM_BUCKETS, 1]
    spread = jnp.float32(0.0)
    for h in range(DSA_HEADS):
        b_hi = brange_scr[0, h]
        b_lo = brange_scr[1, h]
        mx_scr[h:h + 1, :] = jnp.full((1, LANES), bound + b_hi, F32)
        spread = jnp.maximum(spread, 2.0 * bound + (b_hi - b_lo))
    bound_ok = spread <= MAX_LOGIT_SPREAD

    def score_chunk(i, carry, edge):
        d = jnp.dot(ik_ref[0, rows(i), :], iqt_scr[...], preferred_element_type=F32)
        acc = jnp.maximum(d[:, :LANES], 0.0) * wt[0:1, :]
        for h in range(1, IDX_HEADS):
            acc = acc + jnp.maximum(d[:, h * LANES:(h + 1) * LANES], 0.0) * wt[h:h + 1, :]
        sc = fin = acc
        if edge:
            key = row_iota + (j * Q_BLOCK - i * kc - pad)
            adm = jnp.where(key >= 0, key, q_pos + 1) <= q_pos
            sc = jnp.where(adm, acc, -jnp.inf)
            fin = jnp.where(adm, acc, 0.0)
        s_scr[rows(i), :] = sc
        tot, sq, top = carry
        return (tot + _tree(jnp.add, fin), sq + _tree(jnp.add, fin * fin),
                jnp.maximum(top, _tree(jnp.maximum, sc)))

    zero8 = jnp.zeros((8, LANES), F32)
    stats = score_chunk(0, (zero8, zero8, jnp.full((8, LANES), -jnp.inf, F32)), True)
    stats = lax.fori_loop(0, jnp.minimum(nch - 1, 1), lambda _, st: score_chunk(nch - 1, st, True), stats)
    tot, sq, top = chunk_pairs(1, lambda i, st: score_chunk(i, st, False), stats, width=4, stop=nch - 1)

    def count(pred):
        def body(i, cnt):
            return cnt + _tree(jnp.add, jnp.where(pred(s_scr[rows(i), :]), 1.0, 0.0))
        cnt = chunk_pairs(0, body, zero8, width=4)
        return jnp.sum(cnt, axis=0, keepdims=True)

    def max_below(t):
        def body(i, m):
            s = s_scr[rows(i), :]
            return jnp.maximum(m, _tree(jnp.maximum, jnp.where(s < t, s, -jnp.inf)))
        m = lax.fori_loop(0, nch, body, jnp.full((8, LANES), -jnp.inf, F32))
        return jnp.max(m, axis=0, keepdims=True)

    kf = float(top_k)
    n_adm = (q_pos + 1).astype(F32)
    mean = jnp.sum(tot, axis=0, keepdims=True) / n_adm
    std = jnp.sqrt(jnp.maximum(jnp.sum(sq, axis=0, keepdims=True) / n_adm - mean * mean, 0.0))
    first_guess = mean + _upper_normal_quantile(kf / n_adm) * std
    lowest = jnp.float32(-3.0e38)
    top1 = jnp.minimum(jnp.max(top, axis=0, keepdims=True), -lowest)

    def update(st, key):
        lo, hi, c_lo, c_hi, lo_set, hi_set, done, run, v_lo, v_hi, l_lo, l_hi = st
        key = jnp.clip(key, lo + 1, hi - 1)
        t = _key_to_float(key)
        c = count(lambda s: s >= t)
        lc = jnp.log(jnp.maximum(c, 0.5))
        active = done < 0.5
        up = active & (c >= kf)
        dn = active & (c < kf)
        lo, c_lo, lo_set = jnp.where(up, key, lo), jnp.where(up, c, c_lo), jnp.where(up, 1.0, lo_set)
        hi, c_hi, hi_set = jnp.where(dn, key, hi), jnp.where(dn, c, c_hi), jnp.where(dn, 1.0, hi_set)
        v_lo, l_lo = jnp.where(up, t, v_lo), jnp.where(up, lc, l_lo)
        v_hi, l_hi = jnp.where(dn, t, v_hi), jnp.where(dn, lc, l_hi)
        done = jnp.where((c_lo == kf) | (hi <= lo + 1), 1.0, done)
        return lo, hi, c_lo, c_hi, lo_set, hi_set, done, run, v_lo, v_hi, l_lo, l_hi

    def guess(st, midpoint):
        lo, hi, c_lo, c_hi, lo_set, hi_set, done, run, v_lo, v_hi, l_lo, l_hi = st
        both = (lo_set > 0.5) & (hi_set > 0.5)
        step = std * 0.25 * jnp.exp2(run)
        if midpoint:
            inner = 0.5 * v_lo + 0.5 * v_hi
        else:
            frac = jnp.clip((l_lo - math.log(kf - 0.5)) / jnp.maximum(l_lo - l_hi, 1e-6), 0.0, 1.0)
            inner = v_lo + (v_hi - v_lo) * frac
        t = jnp.where(both, inner,
                      jnp.where(hi_set > 0.5, v_hi - step, jnp.where(lo_set > 0.5, v_lo + step, first_guess)))
        t = jnp.where(t != t, 0.0, t)
        run = jnp.where(both, 0.0, run + 1.0)
        return _float_to_key(t), (lo, hi, c_lo, c_hi, lo_set, hi_set, done, run, v_lo, v_hi, l_lo, l_hi)

    def status(st):
        return jnp.max(jnp.where(st[6] < 0.5, STILL_SEARCHING, st[2]))

    flag0 = jnp.zeros((1, LANES), F32)
    hi0 = _float_to_key(top1) + 1
    st = (jnp.full((1, LANES), _float_to_key(lowest), jnp.int32), hi0,
          n_adm, flag0, flag0, flag0, jnp.where(n_adm <= kf, 1.0, 0.0), flag0,
          jnp.full((1, LANES), lowest, F32), _key_to_float(hi0), jnp.log(n_adm),
          jnp.full((1, LANES), math.log(0.5), F32))

    def probes(st):
        for r in range(PROBES_PER_ROUND):
            key, st = guess(st, midpoint=((r + 1) % MIDPOINT_EVERY == 0))
            st = update(st, key)
        return st

    st = lax.fori_loop(0, FAST_ROUNDS, lambda _, s: probes(s), st)

    def safe_round(carry):
        p, _, st = carry
        st = update(st, st[0] + lax.shift_right_logical(st[1] - st[0], 1))
        lo, hi, done = st[0], st[1], st[6]
        below = _float_to_key(max_below(st[9]))
        hi = jnp.where(done < 0.5, jnp.clip(below + 1, lo + 1, hi), hi)
        done = jnp.where(hi <= lo + 1, 1.0, done)
        st = update((lo, hi) + st[2:6] + (done, st[7], st[8], _key_to_float(hi)) + st[10:], hi - 1)
        return p + 1, status(st), st

    _, most, st = lax.while_loop(lambda c: (c[1] >= STILL_SEARCHING) & (c[0] < 34), safe_round,
                                 (jnp.int32(0), status(st), st))

    thr = _key_to_float(st[0])
    has_ties = most > kf

    @pl.when(jnp.logical_not(has_ties))
    def _():
        def mask_chunk(i, carry):
            s_scr[rows(i), :] = jnp.where(s_scr[rows(i), :] >= thr, 0.0, NEG)
            return carry
        lax.fori_loop(0, nch, mask_chunk, 0)

    @pl.when(has_ties)
    def _():
        need = kf - count(lambda s: s > thr)

        def mask_chunk(t, run):
            i = nch - 1 - t
            s = s_scr[rows(i), :]
            tie = jnp.where(s == thr, 1.0, 0.0)
            pref = jnp.dot(ltri_ref[...], tie.astype(BF16), preferred_element_type=F32) + run
            tie_sel = jnp.where(pref <= need, tie, 0.0)
            sel = jnp.where(s > thr, 1.0, tie_sel)
            s_scr[rows(i), :] = jnp.where(sel > 0.5, 0.0, NEG)
            return run + jnp.sum(tie, axis=0, keepdims=True)
        lax.fori_loop(0, nch, mask_chunk, jnp.zeros((1, LANES), F32))

    def logits(i, with_bias):
        msk = s_scr[rows(i), :]
        out = []
        for pr in range(DSA_HEADS // 2):
            l2 = jnp.dot(k_ref[0, rows(i), pr * LANES:(pr + 1) * LANES],
                         qpad_scr[:, 2 * pr * LANES:(2 * pr + 2) * LANES],
                         preferred_element_type=F32)
            for e in range(2):
                l = l2[:, e * LANES:(e + 1) * LANES] + msk
                if with_bias:
                    far_rows = kc - 2 * Q_BLOCK
                    l = jnp.concatenate([l[:far_rows], l[far_rows:] + bias_scr[2 * pr + e]], axis=0)
                out.append(l)
        return out

    def max_step(i, ms, with_bias):
        return tuple(jnp.maximum(m, _tree(jnp.maximum, l)) for m, l in zip(ms, logits(i, with_bias)))

    @pl.when(jnp.logical_not(bound_ok))
    def _():
        ms = max_step(0, tuple(jnp.full((8, LANES), NEG, F32) for _ in range(DSA_HEADS)), True)
        ms = lax.fori_loop(1, nch, lambda i, m: max_step(i, m, False), ms)
        for h in range(DSA_HEADS):
            mx_scr[h:h + 1, :] = jnp.max(ms[h], axis=0, keepdims=True)

    mx = [mx_scr[h:h + 1, :] for h in range(DSA_HEADS)]

    def pv_step(i, accs, with_bias):
        return tuple(
            acc + jnp.dot(vt_ref[0, h, :, rows(i)], jnp.exp2(l - mx[h]).astype(BF16),
                          preferred_element_type=F32)
            for h, (acc, l) in enumerate(zip(accs, logits(i, with_bias))))

    accs = pv_step(0, tuple(jnp.zeros((V_ROWS, LANES), F32) for _ in range(DSA_HEADS)), True)
    accs = chunk_pairs(1, lambda i, a: pv_step(i, a, False), accs, width=4)
    outs = [acc[:HEAD_DIM] / acc[HEAD_DIM:HEAD_DIM + 1] for acc in accs]
    o_ref[...] = jnp.concatenate(outs, axis=0).T


def _dsa(q, iq, kw, k, vt, ikb, rel_bias, q_gain, k_gain, ltri, *, batch, seq, kc):
    nb = seq // Q_BLOCK
    top_k = min(TOPK_MAX, seq // 4)
    pad = kc - Q_BLOCK
    seqp = seq + pad
    kp = jnp.pad(k.reshape(batch, seq, DSA_W), ((0, 0), (pad, 0), (0, 0)))
    ikp = jnp.pad(ikb.reshape(batch, seq, LANES), ((0, 0), (pad, 0), (0, 0)))
    vtp = jnp.pad(vt, ((0, 0), (0, 0), (0, 0), (pad, 0)))
    kmax = (HEAD_DIM ** 0.5 * 1.01) * jnp.max(jnp.abs(k_gain.astype(F32)))
    qmax = (LOG2E * 1.01) * jnp.max(jnp.abs(q_gain.astype(F32)))
    norms = jnp.stack([kmax, qmax] + [jnp.zeros((), F32)] * (DSA_HEADS - 2)).reshape(1, DSA_HEADS)
    table = jnp.concatenate([rel_bias.astype(F32), norms], axis=0)
    qrow = lambda w: pl.BlockSpec((Q_BLOCK, w), lambda b, j: (b * nb + j, 0))
    return pl.pallas_call(
        functools.partial(_dsa_kernel, kc=kc, top_k=top_k),
        grid=(batch, nb),
        in_specs=[
            qrow(DSA_W), qrow(IDX_HEADS * IDX_DIM), qrow(LANES),
            pl.BlockSpec((1, seqp, DSA_W), lambda b, j: (b, 0, 0)),
            pl.BlockSpec((1, DSA_HEADS, V_ROWS, seqp), lambda b, j: (b, 0, 0, 0)),
            pl.BlockSpec((1, seqp, LANES), lambda b, j: (b, 0, 0)),
            pl.BlockSpec(memory_space=pltpu.SMEM),
            pl.BlockSpec(ltri.shape, lambda b, j: (0, 0)),
        ],
        out_specs=qrow(DSA_W),
        out_shape=jax.ShapeDtypeStruct((batch * seq, DSA_W), F32),
        scratch_shapes=[
            pltpu.VMEM((seqp, LANES), F32),
            pltpu.VMEM((LANES, IDX_HEADS * LANES), BF16),
            pltpu.VMEM((LANES, DSA_HEADS * LANES), BF16),
            pltpu.VMEM((DSA_HEADS, 2 * Q_BLOCK, LANES), F32),
            pltpu.VMEM((8, LANES), F32),
            pltpu.SMEM((2, DSA_HEADS), F32),
        ],
        compiler_params=_params("arbitrary", "arbitrary"),
        name="dsa",
    )(q, iq, kw, kp, vtp, ikp, table, ltri)


GDN_GROUP = 2


def _gdn_kernel(x_ref, ab_ref, cw_ref, ea_ref, eb_ref, alog_ref, dtb_ref, gain_ref, grp_ref,
                o_ref, xpad_scr, qkv_scr, gb_scr, gc_scr, gr_scr, pre_scr, st_scr, o_scr, *, tb):
    t = pl.program_id(1)
    cin = 3 * GDN_W
    n_chunks = tb // GDN_CHUNK
    pairs = range(GDN_HEADS // 2)
    lanes = lambda p: slice(p * LANES, (p + 1) * LANES)

    @pl.when(t == 0)
    def _():
        xpad_scr[0:8, :] = jnp.zeros((8, cin), F32)
        st_scr[...] = jnp.zeros_like(st_scr)

    x = x_ref[...]
    xpad_scr[8:8 + tb, :] = x[:, :cin]
    conv = jnp.zeros((tb, cin), F32)
    for jj in range(CONV_WIDTH):
        conv = conv + cw_ref[jj:jj + 1, :] * xpad_scr[pl.ds(8 - (CONV_WIDTH - 1) + jj, tb), :]
    xpad_scr[0:8, :] = x[tb - 8:tb, :cin]
    qkv = conv * _sigmoid(conv)
    grp = grp_ref[...]
    q = qkv[:, :GDN_W]
    k = qkv[:, GDN_W:2 * GDN_W]
    qkv_scr[:, :GDN_W] = q * lax.rsqrt(_group_sum(q * q, grp) + EPS) * (HEAD_DIM ** -0.5)
    qkv_scr[:, GDN_W:2 * GDN_W] = k * lax.rsqrt(_group_sum(k * k, grp) + EPS)
    qkv_scr[:, 2 * GDN_W:] = qkv[:, 2 * GDN_W:]

    ab = ab_ref[...]
    a_e = _dot_sel(ab, ea_ref[...], 3) + dtb_ref[...]
    b_e = _dot_sel(ab, eb_ref[...], 3)
    softplus = jnp.maximum(a_e, 0.0) + jnp.log(1.0 + jnp.exp(-jnp.abs(a_e)))
    gb_scr[:, :GDN_W] = -jnp.exp(alog_ref[...]) * softplus
    gb_scr[:, GDN_W:] = _sigmoid(b_e)

    r64 = lax.broadcasted_iota(jnp.int32, (GDN_CHUNK, LANES), 0)
    c64 = lax.broadcasted_iota(jnp.int32, (GDN_CHUNK, LANES), 1) % GDN_CHUNK
    causal = c64 <= r64
    strict = c64 < r64
    eye = jnp.where(c64 == r64, 1.0, 0.0)
    r128 = lax.broadcasted_iota(jnp.int32, (LANES, LANES), 0)
    c128 = lax.broadcasted_iota(jnp.int32, (LANES, LANES), 1)
    bdmask = (r128 // HEAD_DIM) == (c128 // HEAD_DIM)

    def bd(m):
        return jnp.where(bdmask, jnp.concatenate([m, m], axis=0), 0.0)

    lt_r = lax.broadcasted_iota(jnp.int32, (GDN_CHUNK, GDN_CHUNK), 0)
    lt_c = lax.broadcasted_iota(jnp.int32, (GDN_CHUNK, GDN_CHUNK), 1)
    ltri = jnp.where(lt_c <= lt_r, 1.0, 0.0)
    ones8 = jnp.ones((8, GDN_CHUNK), F32)
    up_r = lax.broadcasted_iota(jnp.int32, (GDN_CHUNK, GDN_W), 0)
    up_c = lax.broadcasted_iota(jnp.int32, (GDN_CHUNK, GDN_W), 1) % GDN_CHUNK
    upper = jnp.where(up_r <= up_c, 1.0, 0.0)
    for ci in range(n_chunks):
        g = gb_scr[ci * GDN_CHUNK:(ci + 1) * GDN_CHUNK, :GDN_W]
        gc_scr[ci * GDN_CHUNK:(ci + 1) * GDN_CHUNK, :] = _sel_dot(ltri, g, 3)
        gr_scr[ci * 8:(ci + 1) * 8, :] = _sel_dot(ones8, g * upper, 3)

    def par_body(gi, carry):
        inst = [(u, p) for u in range(GDN_GROUP) for p in pairs]
        rows = [pl.ds(pl.multiple_of((gi * GDN_GROUP + u) * GDN_CHUNK, GDN_CHUNK), GDN_CHUNK)
                for u in range(GDN_GROUP)]
        grow = [pl.ds(pl.multiple_of((gi * GDN_GROUP + u) * 8, 8), 8) for u in range(GDN_GROUP)]
        off = lambda s, p: slice(s * GDN_W + p * LANES, s * GDN_W + (p + 1) * LANES)
        qp = [qkv_scr[rows[u], off(0, p)] for u, p in inst]
        kp = [qkv_scr[rows[u], off(1, p)] for u, p in inst]
        vp = [qkv_scr[rows[u], off(2, p)] for u, p in inst]
        beta = [gb_scr[rows[u], off(1, p)] for u, p in inst]
        gc = [gc_scr[rows[u], lanes(p)] for u, p in inst]
        gr = [gr_scr[grow[u], lanes(p)][0:1, :] for u, p in inst]
        dmat = [jnp.exp(jnp.where(causal, a - b, NEG)) for a, b in zip(gc, gr)]
        kbd = [jnp.where(bdmask, jnp.concatenate([a, a], axis=0).T, 0.0) for a in kp]
        kb = [a * b for a, b in zip(kp, beta)]
        kk = [_bdot(a, b) for a, b in zip(kb, kbd)]
        qk = [_bdot(a, b) for a, b in zip(qp, kbd)]
        lm = [jnp.where(strict, a * d, 0.0) for a, d in zip(kk, dmat)]
        aintra = [a * d for a, d in zip(qk, dmat)]
        tinv = [eye - a for a in lm]
        lpow = lm
        for _ in range(5):
            lpow = [_dot3(a, bd(a)) for a in lpow]
            tinv = [a + _dot3(a, bd(b)) for a, b in zip(tinv, lpow)]
        egc = [jnp.exp(a) for a in gc]
        un = [_dot3(a, bd(v * b)) for a, v, b in zip(tinv, vp, beta)]
        wn = [_dot3(a, bd(b * e)) for a, b, e in zip(tinv, kb, egc)]
        for n, (u, p) in enumerate(inst):
            pre_scr[0, rows[u], lanes(p)] = un[n]
            pre_scr[1, rows[u], lanes(p)] = wn[n]
            pre_scr[2, rows[u], lanes(p)] = aintra[n]
            pre_scr[3, rows[u], lanes(p)] = qp[n] * egc[n]
            pre_scr[4, rows[u], lanes(p)] = kp[n] * jnp.exp(gc[n][GDN_CHUNK - 1:GDN_CHUNK, :] - gc[n])
        return carry

    lax.fori_loop(0, n_chunks // GDN_GROUP, par_body, 0)

    def rec_body(ci, carry):
        rows = pl.ds(pl.multiple_of(ci * GDN_CHUNK, GDN_CHUNK), GDN_CHUNK)
        tail = gc_scr[pl.ds(pl.multiple_of(ci * GDN_CHUNK + GDN_CHUNK - 8, 8), 8), :]
        egl = jnp.exp(tail[7:8, :])
        sbd = [st_scr[p] for p in pairs]
        un, wn, aintra, qe, kd = [[pre_scr[s, rows, lanes(p)] for p in pairs] for s in range(5)]
        ws = [_bdot(a, s) for a, s in zip(wn, sbd)]
        qs = [_bdot(a, s) for a, s in zip(qe, sbd)]
        v_new = [a - b for a, b in zip(un, ws)]
        o = [a + _bdot(b, bd(v)) for a, b, v in zip(qs, aintra, v_new)]
        upd = [_bdot(a.T, v) for a, v in zip(kd, v_new)]
        for p in pairs:
            st_scr[p] = sbd[p] * egl[:, lanes(p)] + jnp.where(bdmask, upd[p], 0.0)
            o_scr[rows, lanes(p)] = o[p]
        return carry

    lax.fori_loop(0, n_chunks, rec_body, 0)

    o = o_scr[...]
    oms = _group_sum(o * o, grp) * (1.0 / HEAD_DIM)
    z = x[:, cin:]
    o_ref[...] = o * lax.rsqrt(oms + EPS) * gain_ref[...] * (z * _sigmoid(z))


def _gdn(gx, ab, conv_w, ea, eb, alog, dtb, gain, grp, *, batch, seq, tb):
    nt = seq // tb
    cin = 3 * GDN_W
    row = lambda w: pl.BlockSpec((tb, w), lambda b, t: (b * nt + t, 0))
    full = lambda a: pl.BlockSpec(a.shape, lambda b, t: (0,) * a.ndim)
    return pl.pallas_call(
        functools.partial(_gdn_kernel, tb=tb),
        grid=(batch, nt),
        in_specs=[row(4 * GDN_W), row(LANES), full(conv_w), full(ea), full(eb), full(alog),
                  full(dtb), full(gain), full(grp)],
        out_specs=row(GDN_W),
        out_shape=jax.ShapeDtypeStruct((batch * seq, GDN_W), F32),
        scratch_shapes=[
            pltpu.VMEM((tb + 8, cin), F32),
            pltpu.VMEM((tb, cin), F32),
            pltpu.VMEM((tb, 2 * GDN_W), F32),
            pltpu.VMEM((tb, GDN_W), F32),
            pltpu.VMEM((tb // 8, GDN_W), F32),
            pltpu.VMEM((5, tb, GDN_W), F32),
            pltpu.VMEM((GDN_HEADS // 2, LANES, LANES), F32),
            pltpu.VMEM((tb, GDN_W), F32),
        ],
        compiler_params=_params("arbitrary", "arbitrary"),
        name="gdn",
    )(gx, ab, conv_w, ea, eb, alog, dtb, gain, grp)


def _memkv_kernel(mem_ref, g_ref, w_ref, kg_ref, grp_ref, kt_ref, vp_ref):
    x = mem_ref[0]
    ms = jnp.mean(x * x, axis=-1, keepdims=True)
    h = (x * lax.rsqrt(ms + EPS) * g_ref[...]).astype(BF16)
    kv = jnp.dot(h, w_ref[...], preferred_element_type=F32)
    km = kv[:, :MEM_W]
    vm = kv[:, MEM_W:]
    kms = _group_sum(km * km, grp_ref[...]) * (1.0 / HEAD_DIM)
    kt = (km * lax.rsqrt(kms + EPS) * kg_ref[...]).T
    n_mem = x.shape[0]
    top = lax.broadcasted_iota(jnp.int32, (LANES, n_mem), 0) < HEAD_DIM
    left = lax.broadcasted_iota(jnp.int32, (n_mem, LANES), 1) < HEAD_DIM
    for hh in range(MEM_HEADS):
        pr = slice((hh // 2) * LANES, (hh // 2 + 1) * LANES)
        keep_r = top if hh % 2 == 0 else jnp.logical_not(top)
        keep_c = left if hh % 2 == 0 else jnp.logical_not(left)
        kt_ref[0, hh] = jnp.where(keep_r, kt[pr, :], 0.0).astype(BF16)
        vp_ref[0, hh] = jnp.where(keep_c, vm[:, pr], 0.0).astype(BF16)


def _memkv(mem, gain, w, kg, grp):
    b, n_mem, d = mem.shape
    full = lambda a: pl.BlockSpec(a.shape, lambda i: (0,) * a.ndim)
    return pl.pallas_call(
        _memkv_kernel,
        grid=(b,),
        in_specs=[pl.BlockSpec((1, n_mem, d), lambda i: (i, 0, 0)), full(gain), full(w), full(kg),
                  full(grp)],
        out_specs=(pl.BlockSpec((1, MEM_HEADS, LANES, n_mem), lambda i: (i, 0, 0, 0)),
                   pl.BlockSpec((1, MEM_HEADS, n_mem, LANES), lambda i: (i, 0, 0, 0))),
        out_shape=(jax.ShapeDtypeStruct((b, MEM_HEADS, LANES, n_mem), BF16),
                   jax.ShapeDtypeStruct((b, MEM_HEADS, n_mem, LANES), BF16)),
        compiler_params=_params("parallel"),
        name="memkv",
    )(mem, gain, w, kg, grp)


def _memattn_kernel(q_ref, qg_ref, grp_ref, kt_ref, vp_ref, o_ref):
    q = q_ref[...]
    qms = _group_sum(q * q, grp_ref[...]) * (1.0 / HEAD_DIM)
    qn = (q * lax.rsqrt(qms + EPS) * qg_ref[...] * (HEAD_DIM ** -0.5)).astype(BF16)
    outs = []
    for pr in range(MEM_HEADS // 2):
        qp = qn[:, pr * LANES:(pr + 1) * LANES]
        acc = None
        for e in range(2):
            hh = 2 * pr + e
            l = jnp.dot(qp, kt_ref[0, hh], preferred_element_type=F32)
            l = l - jnp.max(l, axis=-1, keepdims=True)
            p = jnp.exp(l)
            p = p / jnp.sum(p, axis=-1, keepdims=True)
            o = jnp.dot(p.astype(BF16), vp_ref[0, hh], preferred_element_type=F32)
            acc = o if acc is None else acc + o
        outs.append(acc)
    o_ref[...] = jnp.concatenate(outs, axis=-1)


def _memattn(mq, qg, grp, kt, vp, *, batch, seq, tm):
    nt = seq // tm
    n_mem = kt.shape[-1]
    full = lambda a: pl.BlockSpec(a.shape, lambda b, t: (0,) * a.ndim)
    return pl.pallas_call(
        _memattn_kernel,
        grid=(batch, nt),
        in_specs=[
            pl.BlockSpec((tm, MEM_W), lambda b, t: (b * nt + t, 0)), full(qg), full(grp),
            pl.BlockSpec((1, MEM_HEADS, LANES, n_mem), lambda b, t: (b, 0, 0, 0)),
            pl.BlockSpec((1, MEM_HEADS, n_mem, LANES), lambda b, t: (b, 0, 0, 0)),
        ],
        out_specs=pl.BlockSpec((tm, MEM_W), lambda b, t: (b * nt + t, 0)),
        out_shape=jax.ShapeDtypeStruct((batch * seq, MEM_W), F32),
        compiler_params=_params("parallel", "parallel"),
        name="memattn",
    )(mq, qg, grp, kt, vp)


def _group_ones(width):
    idx = np.arange(width) // HEAD_DIM
    return jnp.asarray((idx[:, None] == idx[None, :]).astype(np.float32))


def _t5_bucket_np(n):
    max_exact = NUM_BUCKETS // 2
    nf = np.maximum(n, 1).astype(np.float32)
    large = max_exact + (np.log(nf / np.float32(max_exact)) / np.float32(math.log(MAX_DISTANCE / max_exact))
                         * (NUM_BUCKETS - max_exact)).astype(np.int32)
    large = np.minimum(large, NUM_BUCKETS - 1)
    return np.where(n < max_exact, n, large)


def _bucket_starts():
    dist = np.arange(2 * Q_BLOCK)
    bucket = _t5_bucket_np(dist)
    assert (np.diff(bucket) >= 0).all() and bucket[-1] == NUM_BUCKETS - 1
    return tuple(int(dist[bucket >= b].min()) for b in range(1, NUM_BUCKETS))


_BUCKET_STARTS = _bucket_starts()


def _cast_kernel(w_ref, o_ref):
    o_ref[...] = w_ref[...].astype(BF16)


def _to_bf16(w, *, rows):
    nl, nr, nc = w.shape
    spec = pl.BlockSpec((1, rows, nc), lambda l, r: (l, r, 0))
    return pl.pallas_call(
        _cast_kernel,
        grid=(nl, nr // rows),
        in_specs=[spec],
        out_specs=spec,
        out_shape=jax.ShapeDtypeStruct(w.shape, BF16),
        compiler_params=_params("parallel", "parallel"),
        name="cast",
    )(w)


def _pack_w_in(w):
    o = 3 * DSA_W
    iq = w[:, o:o + IDX_HEADS * IDX_DIM]
    o += IDX_HEADS * IDX_DIM
    kw = w[:, o:o + IDX_DIM + IDX_HEADS]
    o += IDX_DIM + IDX_HEADS
    g = w[:, o:o + 4 * GDN_W]
    o += 4 * GDN_W
    ab = w[:, o:o + 2 * GDN_HEADS]
    o += 2 * GDN_HEADS
    mq = w[:, o:o + MEM_W]
    pad = lambda a: jnp.pad(a, ((0, 0), (0, LANES - a.shape[1])))
    return jnp.concatenate([w[:, :3 * DSA_W], iq, pad(kw), g, pad(ab), mq], axis=1)


def _tile_heads(v, heads):
    return jnp.tile(v.astype(F32), heads).reshape(1, heads * HEAD_DIM)


def kernel(x, mem, ffn1_norm, ffn1_w_gate, ffn1_w_up, ffn1_w_down, mix_norm, w_in, dsa_q_norm, dsa_k_norm, rel_bias, gdn_conv, gdn_A_log, gdn_dt_bias, gdn_out_norm, mem_norm, w_mem_kv, mem_q_norm, mem_k_norm, w_out, ffn2_norm, ffn2_w_gate, ffn2_w_up, ffn2_w_down):
    batch, seq, d = x.shape
    depth = w_in.shape[0]
    m = batch * seq
    tm = min(512, seq)
    kc = min(512, seq)
    dff = ffn1_w_gate.shape[-1]
    tf = dff // 2 if (dff // 2) % LANES == 0 else dff

    grp_a = _group_ones(DSA_W)
    grp_m = _group_ones(MEM_W)
    ltri = jnp.asarray(np.tril(np.ones((kc, kc), np.float32))).astype(BF16)
    heads_of = np.arange(GDN_W) // HEAD_DIM
    ea = jnp.asarray((np.arange(LANES)[:, None] == heads_of[None, :]).astype(np.float32))
    eb = jnp.asarray((np.arange(LANES)[:, None] == heads_of[None, :] + GDN_HEADS).astype(np.float32))

    w_in_b = _to_bf16(w_in, rows=min(256, d))
    xf = x.reshape(m, d)
    for l in range(depth):
        xf = _ffn(xf, ffn1_norm[l], ffn1_w_gate[l].astype(BF16), ffn1_w_up[l].astype(BF16),
                  ffn1_w_down[l].astype(BF16), tm=tm, tf=tf)
        (q, k, vt, iq, ikb, kw, gx, ab, mq) = _inproj(
            xf, mix_norm[l].reshape(1, d), _pack_w_in(w_in_b[l]), grp_a,
            _tile_heads(dsa_q_norm[l], DSA_HEADS), _tile_heads(dsa_k_norm[l], DSA_HEADS),
            batch=batch, tm=tm)
        out_a = _dsa(q, iq, kw, k, vt, ikb, rel_bias, dsa_q_norm[l], dsa_k_norm[l], ltri,
                     batch=batch, seq=seq, kc=kc)
        out_b = _gdn(gx, ab, gdn_conv[l], ea, eb,
                     jnp.repeat(gdn_A_log[l].astype(F32), HEAD_DIM).reshape(1, GDN_W),
                     jnp.repeat(gdn_dt_bias[l].astype(F32), HEAD_DIM).reshape(1, GDN_W),
                     _tile_heads(gdn_out_norm[l], GDN_HEADS), grp_a, batch=batch, seq=seq, tb=tm)
        kt, vp = _memkv(mem, mem_norm[l].reshape(1, d), w_mem_kv[l].astype(BF16),
                        _tile_heads(mem_k_norm[l], MEM_HEADS), grp_m)
        out_c = _memattn(mq, _tile_heads(mem_q_norm[l], MEM_HEADS), grp_m, kt, vp,
                         batch=batch, seq=seq, tm=tm)
        wo = w_out[l].astype(BF16)
        xf = _ffn(xf, ffn2_norm[l], ffn2_w_gate[l].astype(BF16), ffn2_w_up[l].astype(BF16),
                  ffn2_w_down[l].astype(BF16), mix=(out_a, out_b, out_c),
                  mix_w=(wo[:DSA_W], wo[DSA_W:DSA_W + GDN_W], wo[DSA_W + GDN_W:]), tm=tm, tf=tf)
    return xf.reshape(batch, seq, d)
```

```python
import functools
import math

import jax
import jax.numpy as jnp
import numpy as np
from jax import lax
from jax.experimental import pallas as pl
from jax.experimental.pallas import tpu as pltpu

F32 = jnp.float32
BF16 = jnp.bfloat16

HEAD_DIM = 64
DSA_HEADS = 6
GDN_HEADS = 6
MEM_HEADS = 4
DSA_W = DSA_HEADS * HEAD_DIM
GDN_W = GDN_HEADS * HEAD_DIM
MEM_W = MEM_HEADS * HEAD_DIM
IDX_HEADS = 8
IDX_DIM = 32
TOPK_MAX = 256
Q_BLOCK = 128
GDN_CHUNK = 64
CONV_WIDTH = 4
NUM_BUCKETS = 32
MAX_DISTANCE = 128
EPS = 1e-6

LANES = 128
VMEM_LIMIT = 52 * 1024 * 1024
NEG = -1e30
LOG2E = math.log2(math.e)
V_ROWS = HEAD_DIM + 16
MAX_LOGIT_SPREAD = 80.0
PROBES_PER_ROUND = 4
MIDPOINT_EVERY = 4
FAST_ROUNDS = 3
STILL_SEARCHING = 1e9

SEG_A = 0
SEG_IQ = SEG_A + 3 * DSA_W
SEG_KW = SEG_IQ + IDX_HEADS * IDX_DIM
SEG_G = SEG_KW + LANES
SEG_AB = SEG_G + 4 * GDN_W
SEG_MQ = SEG_AB + LANES
IN_PACKED = SEG_MQ + MEM_W


def _bdot(a, b):
    return jnp.dot(a.astype(BF16), b.astype(BF16), preferred_element_type=F32)


def _split(x, terms):
    out = []
    for _ in range(terms - 1):
        hi = x.astype(BF16)
        out.append(hi)
        x = x - hi.astype(F32)
    out.append(x.astype(BF16))
    return out


def _dot3(a, b):
    ah, al = _split(a, 2)
    bh, bl = _split(b, 2)
    mm = lambda x, y: jnp.dot(x, y, preferred_element_type=F32)
    return mm(ah, bh) + (mm(ah, bl) + mm(al, bh))


def _dot_sel(a, sel, terms):
    selb = sel.astype(BF16)
    acc = None
    for piece in _split(a, terms):
        d = jnp.dot(piece, selb, preferred_element_type=F32)
        acc = d if acc is None else acc + d
    return acc


def _sel_dot(sel, b, terms):
    selb = sel.astype(BF16)
    acc = None
    for piece in _split(b, terms):
        d = jnp.dot(selb, piece, preferred_element_type=F32)
        acc = d if acc is None else acc + d
    return acc


def _group_sum(x, grp):
    return _dot_sel(x, grp, 2)


def _sigmoid(x):
    return 1.0 / (1.0 + jnp.exp(-x))


def _params(*sem):
    return pltpu.CompilerParams(dimension_semantics=sem, vmem_limit_bytes=VMEM_LIMIT)


def _ffn_kernel(*refs, n_mix):
    x_ref, g_ref, wg_ref, wu_ref, wd_ref = refs[:5]
    mix_refs = refs[5:5 + 2 * n_mix]
    o_ref, x_scr, h_scr, acc_scr = refs[5 + 2 * n_mix:]
    f = pl.program_id(1)

    @pl.when(f == 0)
    def _():
        x = x_ref[...]
        for k in range(n_mix):
            x = x + _bdot(mix_refs[k][...], mix_refs[n_mix + k][...])
        x_scr[...] = x
        ms = jnp.mean(x * x, axis=-1, keepdims=True)
        h_scr[...] = (x * lax.rsqrt(ms + EPS) * g_ref[...]).astype(BF16)
        acc_scr[...] = jnp.zeros_like(acc_scr)

    h = h_scr[...]
    a = jnp.dot(h, wg_ref[...], preferred_element_type=F32)
    u = jnp.dot(h, wu_ref[...], preferred_element_type=F32)
    z = (a * _sigmoid(a)) * u
    acc_scr[...] += jnp.dot(z.astype(BF16), wd_ref[...], preferred_element_type=F32)

    @pl.when(f == pl.num_programs(1) - 1)
    def _():
        o_ref[...] = x_scr[...] + 0.5 * acc_scr[...]


def _ffn(x, gain, wg, wu, wd, mix=(), mix_w=(), *, tm, tf):
    m, d = x.shape
    dff = wg.shape[1]
    row = lambda w: pl.BlockSpec((tm, w), lambda i, f: (i, 0))
    full = lambda a: pl.BlockSpec(a.shape, lambda i, f: (0,) * a.ndim)
    return pl.pallas_call(
        functools.partial(_ffn_kernel, n_mix=len(mix)),
        grid=(m // tm, dff // tf),
        in_specs=[
            row(d),
            pl.BlockSpec((1, d), lambda i, f: (0, 0)),
            pl.BlockSpec((d, tf), lambda i, f: (0, f)),
            pl.BlockSpec((d, tf), lambda i, f: (0, f)),
            pl.BlockSpec((tf, d), lambda i, f: (f, 0)),
        ] + [row(a.shape[1]) for a in mix] + [full(w) for w in mix_w],
        out_specs=row(d),
        out_shape=jax.ShapeDtypeStruct((m, d), F32),
        scratch_shapes=[pltpu.VMEM((tm, d), F32), pltpu.VMEM((tm, d), BF16), pltpu.VMEM((tm, d), F32)],
        compiler_params=_params("parallel", "arbitrary"),
        name="ffn",
    )(x, gain.reshape(1, d), wg, wu, wd, *mix, *mix_w)


def _inproj_kernel(x_ref, g_ref, w_ref, grp_ref, qg_ref, kg_ref,
                   q_ref, k_ref, vt_ref, iq_ref, ikb_ref, kw_ref, gx_ref, ab_ref, mq_ref):
    x = x_ref[...]
    ms = jnp.mean(x * x, axis=-1, keepdims=True)
    h = (x * lax.rsqrt(ms + EPS) * g_ref[...]).astype(BF16)
    p = jnp.dot(h, w_ref[...], preferred_element_type=F32)

    grp = grp_ref[...]
    dq = p[:, SEG_A:SEG_A + DSA_W]
    dk = p[:, SEG_A + DSA_W:SEG_A + 2 * DSA_W]
    dv = p[:, SEG_A + 2 * DSA_W:SEG_A + 3 * DSA_W]
    qms = _group_sum(dq * dq, grp) * (1.0 / HEAD_DIM)
    kms = _group_sum(dk * dk, grp) * (1.0 / HEAD_DIM)
    q_ref[...] = (dq * lax.rsqrt(qms + EPS) * qg_ref[...] * (HEAD_DIM ** -0.5 * LOG2E)).astype(BF16)
    k_ref[...] = (dk * lax.rsqrt(kms + EPS) * kg_ref[...]).astype(BF16)
    dvt = dv.T
    ones = jnp.ones((V_ROWS - HEAD_DIM, dvt.shape[1]), F32)
    for hd in range(DSA_HEADS):
        vt_ref[0, hd] = jnp.concatenate([dvt[hd * HEAD_DIM:(hd + 1) * HEAD_DIM], ones], axis=0).astype(BF16)

    iq_ref[...] = (p[:, SEG_IQ:SEG_IQ + IDX_HEADS * IDX_DIM] * (IDX_DIM ** -0.5)).astype(BF16)
    kw = p[:, SEG_KW:SEG_KW + LANES]
    kw_ref[...] = kw
    ikb_ref[...] = kw.astype(BF16)
    gx_ref[...] = p[:, SEG_G:SEG_G + 4 * GDN_W]
    ab_ref[...] = p[:, SEG_AB:SEG_AB + LANES]
    mq_ref[...] = p[:, SEG_MQ:SEG_MQ + MEM_W]


def _inproj(x, gain, w_packed, grp, qg, kg, *, batch, tm):
    m, d = x.shape
    row = lambda w: pl.BlockSpec((tm, w), lambda i: (i, 0))
    full = lambda a: pl.BlockSpec(a.shape, lambda i: (0,) * a.ndim)
    out_shape = (
        jax.ShapeDtypeStruct((m, DSA_W), BF16),
        jax.ShapeDtypeStruct((m, DSA_W), BF16),
        jax.ShapeDtypeStruct((batch, DSA_HEADS, V_ROWS, m // batch), BF16),
        jax.ShapeDtypeStruct((m, IDX_HEADS * IDX_DIM), BF16),
        jax.ShapeDtypeStruct((m, LANES), BF16),
        jax.ShapeDtypeStruct((m, LANES), F32),
        jax.ShapeDtypeStruct((m, 4 * GDN_W), F32),
        jax.ShapeDtypeStruct((m, LANES), F32),
        jax.ShapeDtypeStruct((m, MEM_W), F32),
    )
    nt = m // batch // tm
    out_specs = (row(DSA_W), row(DSA_W),
                 pl.BlockSpec((1, DSA_HEADS, V_ROWS, tm), lambda i: (i // nt, 0, 0, i % nt)),
                 row(IDX_HEADS * IDX_DIM), row(LANES), row(LANES), row(4 * GDN_W),
                 row(LANES), row(MEM_W))
    return pl.pallas_call(
        _inproj_kernel,
        grid=(m // tm,),
        in_specs=[row(d), full(gain), full(w_packed), full(grp), full(qg), full(kg)],
        out_specs=out_specs,
        out_shape=out_shape,
        compiler_params=_params("parallel"),
        name="inproj",
    )(x, gain, w_packed, grp, qg, kg)


_DENORMAL_TOP = 0x007FFFFF


def _float_to_key(f):
    bits = lax.bitcast_convert_type(f, jnp.int32)
    mag = jnp.maximum((bits & jnp.int32(0x7FFFFFFF)) - _DENORMAL_TOP, 0)
    return jnp.where(bits >= 0, mag, -mag)


def _key_to_float(key):
    mag = jnp.abs(key)
    bits = jnp.where(mag > 0, mag + _DENORMAL_TOP, 0)
    return lax.bitcast_convert_type(jnp.where(key < 0, bits | jnp.int32(-2 ** 31), bits), F32)


def _upper_normal_quantile(p):
    pp = jnp.clip(jnp.minimum(p, 1.0 - p), 1e-30, 0.5)
    t = jnp.sqrt(-2.0 * jnp.log(pp))
    z = t - (2.515517 + t * (0.802853 + t * 0.010328)) / (1.0 + t * (1.432788 + t * (0.189269 + t * 0.001308)))
    return jnp.where(p <= 0.5, z, -z)


def _tree(op, x, group=8):
    parts = x.reshape(x.shape[0] // group, group, LANES)
    k = parts.shape[0]
    while k > 1:
        k //= 2
        parts = op(parts[:k], parts[k:2 * k])
    return parts[0]


def _dsa_kernel(q_ref, iq_ref, kwq_ref, k_ref, vt_ref, ik_ref, tab_ref, ltri_ref, o_ref,
                s_scr, iqt_scr, qpad_scr, bias_scr, mx_scr, brange_scr, *, kc, top_k):
    j = pl.program_id(1)
    per = kc // Q_BLOCK
    pad = kc - Q_BLOCK
    nch = j // per + 1
    q_pos = j * Q_BLOCK + lax.broadcasted_iota(jnp.int32, (1, LANES), 1)
    row_iota = lax.broadcasted_iota(jnp.int32, (kc, LANES), 0)

    def rows(i):
        return pl.ds(pl.multiple_of(j * Q_BLOCK - i * kc, Q_BLOCK), kc)

    def chunk_pairs(start, body, init, width=2, stop=None):
        count = jnp.maximum((nch if stop is None else stop) - start, 0)
        rest = count % width
        state = lax.fori_loop(0, rest, lambda r, st: body(start + r, st), init)

        def group(t, st):
            for u in range(width):
                st = body(start + rest + width * t + u, st)
            return st
        return lax.fori_loop(0, count // width, group, state)

    @pl.when((pl.program_id(0) == 0) & (j == 0))
    def _():
        r = lax.broadcasted_iota(jnp.int32, (2 * Q_BLOCK, LANES), 0)
        c = lax.broadcasted_iota(jnp.int32, (2 * Q_BLOCK, LANES), 1)
        dist = Q_BLOCK + c - r
        bucket = jnp.zeros_like(dist)
        for first in _BUCKET_STARTS:
            bucket = bucket + jnp.where(dist >= first, 1, 0)
        for h in range(DSA_HEADS):
            far = tab_ref[NUM_BUCKETS - 1, h]
            delta = jnp.zeros((2 * Q_BLOCK, LANES), F32)
            b_hi = jnp.float32(0.0)
            b_lo = jnp.float32(0.0)
            for b in range(NUM_BUCKETS - 1):
                delta = jnp.where(bucket == b, tab_ref[b, h] - far, delta)
                b_hi = jnp.maximum(b_hi, (tab_ref[b, h] - far) * LOG2E)
                b_lo = jnp.minimum(b_lo, (tab_ref[b, h] - far) * LOG2E)
            bias_scr[h] = jnp.where(dist >= 0, delta * LOG2E, 0.0)
            brange_scr[0, h] = b_hi
            brange_scr[1, h] = b_lo

    iqt = iq_ref[...].astype(F32).T
    zpad = jnp.zeros((LANES - IDX_DIM, LANES), F32)
    for h in range(IDX_HEADS):
        iqt_scr[:, h * LANES:(h + 1) * LANES] = jnp.concatenate(
            [iqt[h * IDX_DIM:(h + 1) * IDX_DIM], zpad], axis=0).astype(BF16)
    wt = kwq_ref[...].T[IDX_DIM:IDX_DIM + IDX_HEADS, :] * (IDX_HEADS ** -0.5)
    qt = q_ref[...].astype(F32).T
    half = lax.broadcasted_iota(jnp.int32, (LANES, LANES), 0) < HEAD_DIM
    for h in range(DSA_HEADS):
        pair = qt[(h // 2) * LANES:(h // 2 + 1) * LANES]
        keep = half if h % 2 == 0 else jnp.logical_not(half)
        qpad_scr[:, h * LANES:(h + 1) * LANES] = jnp.where(keep, pair, 0.0).astype(BF16)

    bound = tab_ref[NUM_BUCKETS, 0] * tab_ref[NUM_BUCKETS, 1]
    spread = jnp.float32(0.0)
    for h in range(DSA_HEADS):
        b_hi = brange_scr[0, h]
        b_lo = brange_scr[1, h]
        mx_scr[h:h + 1, :] = jnp.full((1, LANES), bound + b_hi, F32)
        spread = jnp.maximum(spread, 2.0 * bound + (b_hi - b_lo))
    bound_ok = spread <= MAX_LOGIT_SPREAD

    def score_chunk(i, carry, edge):
        d = jnp.dot(ik_ref[0, rows(i), :], iqt_scr[...], preferred_element_type=F32)
        acc = jnp.maximum(d[:, :LANES], 0.0) * wt[0:1, :]
        for h in range(1, IDX_HEADS):
            acc = acc + jnp.maximum(d[:, h * LANES:(h + 1) * LANES], 0.0) * wt[h:h + 1, :]
        sc = fin = acc
        if edge:
            key = row_iota + (j * Q_BLOCK - i * kc - pad)
            adm = jnp.where(key >= 0, key, q_pos + 1) <= q_pos
            sc = jnp.where(adm, acc, -jnp.inf)
            fin = jnp.where(adm, acc, 0.0)
        s_scr[rows(i), :] = sc
        tot, sq, top = carry
        return (tot + _tree(jnp.add, fin), sq + _tree(jnp.add, fin * fin),
                jnp.maximum(top, _tree(jnp.maximum, sc)))

    zero8 = jnp.zeros((8, LANES), F32)
    stats = score_chunk(0, (zero8, zero8, jnp.full((8, LANES), -jnp.inf, F32)), True)
    stats = lax.fori_loop(0, jnp.minimum(nch - 1, 1), lambda _, st: score_chunk(nch - 1, st, True), stats)
    tot, sq, top = chunk_pairs(1, lambda i, st: score_chunk(i, st, False), stats, width=4, stop=nch - 1)

    def count(pred):
        def body(i, cnt):
            return cnt + _tree(jnp.add, jnp.where(pred(s_scr[rows(i), :]), 1.0, 0.0))
        cnt = chunk_pairs(0, body, zero8, width=4)
        return jnp.sum(cnt, axis=0, keepdims=True)

    def max_below(t):
        def body(i, m):
            s = s_scr[rows(i), :]
            return jnp.maximum(m, _tree(jnp.maximum, jnp.where(s < t, s, -jnp.inf)))
        m = lax.fori_loop(0, nch, body, jnp.full((8, LANES), -jnp.inf, F32))
        return jnp.max(m, axis=0, keepdims=True)

    kf = float(top_k)
    n_adm = (q_pos + 1).astype(F32)
    mean = jnp.sum(tot, axis=0, keepdims=True) / n_adm
    std = jnp.sqrt(jnp.maximum(jnp.sum(sq, axis=0, keepdims=True) / n_adm - mean * mean, 0.0))
    first_guess = mean + _upper_normal_quantile(kf / n_adm) * std
    lowest = jnp.float32(-3.0e38)
    top1 = jnp.minimum(jnp.max(top, axis=0, keepdims=True), -lowest)

    def update(st, key):
        lo, hi, c_lo, c_hi, lo_set, hi_set, done, run, v_lo, v_hi, l_lo, l_hi = st
        key = jnp.clip(key, lo + 1, hi - 1)
        t = _key_to_float(key)
        c = count(lambda s: s >= t)
        lc = jnp.log(jnp.maximum(c, 0.5))
        active = done < 0.5
        up = active & (c >= kf)
        dn = active & (c < kf)
        lo, c_lo, lo_set = jnp.where(up, key, lo), jnp.where(up, c, c_lo), jnp.where(up, 1.0, lo_set)
        hi, c_hi, hi_set = jnp.where(dn, key, hi), jnp.where(dn, c, c_hi), jnp.where(dn, 1.0, hi_set)
        v_lo, l_lo = jnp.where(up, t, v_lo), jnp.where(up, lc, l_lo)
        v_hi, l_hi = jnp.where(dn, t, v_hi), jnp.where(dn, lc, l_hi)
        done = jnp.where((c_lo == kf) | (hi <= lo + 1), 1.0, done)
        return lo, hi, c_lo, c_hi, lo_set, hi_set, done, run, v_lo, v_hi, l_lo, l_hi

    def guess(st, midpoint):
        lo, hi, c_lo, c_hi, lo_set, hi_set, done, run, v_lo, v_hi, l_lo, l_hi = st
        both = (lo_set > 0.5) & (hi_set > 0.5)
        step = std * 0.25 * jnp.exp2(run)
        if midpoint:
            inner = 0.5 * v_lo + 0.5 * v_hi
        else:
            frac = jnp.clip((l_lo - math.log(kf - 0.5)) / jnp.maximum(l_lo - l_hi, 1e-6), 0.0, 1.0)
            inner = v_lo + (v_hi - v_lo) * frac
        t = jnp.where(both, inner,
                      jnp.where(hi_set > 0.5, v_hi - step, jnp.where(lo_set > 0.5, v_lo + step, first_guess)))
        t = jnp.where(t != t, 0.0, t)
        run = jnp.where(both, 0.0, run + 1.0)
        return _float_to_key(t), (lo, hi, c_lo, c_hi, lo_set, hi_set, done, run, v_lo, v_hi, l_lo, l_hi)

    def status(st):
        return jnp.max(jnp.where(st[6] < 0.5, STILL_SEARCHING, st[2]))

    flag0 = jnp.zeros((1, LANES), F32)
    hi0 = _float_to_key(top1) + 1
    st = (jnp.full((1, LANES), _float_to_key(lowest), jnp.int32), hi0,
          n_adm, flag0, flag0, flag0, jnp.where(n_adm <= kf, 1.0, 0.0), flag0,
          jnp.full((1, LANES), lowest, F32), _key_to_float(hi0), jnp.log(n_adm),
          jnp.full((1, LANES), math.log(0.5), F32))

    def probes(st):
        for r in range(PROBES_PER_ROUND):
            key, st = guess(st, midpoint=((r + 1) % MIDPOINT_EVERY == 0))
            st = update(st, key)
        return st

    st = lax.fori_loop(0, FAST_ROUNDS, lambda _, s: probes(s), st)

    def safe_round(carry):
        p, _, st = carry
        st = update(st, st[0] + lax.shift_right_logical(st[1] - st[0], 1))
        lo, hi, done = st[0], st[1], st[6]
        below = _float_to_key(max_below(st[9]))
        hi = jnp.where(done < 0.5, jnp.clip(below + 1, lo + 1, hi), hi)
        done = jnp.where(hi <= lo + 1, 1.0, done)
        st = update((lo, hi) + st[2:6] + (done, st[7], st[8], _key_to_float(hi)) + st[10:], hi - 1)
        return p + 1, status(st), st

    _, most, st = lax.while_loop(lambda c: (c[1] >= STILL_SEARCHING) & (c[0] < 34), safe_round,
                                 (jnp.int32(0), status(st), st))

    thr = _key_to_float(st[0])
    has_ties = most > kf

    @pl.when(jnp.logical_not(has_ties))
    def _():
        def mask_chunk(i, carry):
            s_scr[rows(i), :] = jnp.where(s_scr[rows(i), :] >= thr, 0.0, NEG)
            return carry
        lax.fori_loop(0, nch, mask_chunk, 0)

    @pl.when(has_ties)
    def _():
        need = kf - count(lambda s: s > thr)

        def mask_chunk(t, run):
            i = nch - 1 - t
            s = s_scr[rows(i), :]
            tie = jnp.where(s == thr, 1.0, 0.0)
            pref = jnp.dot(ltri_ref[...], tie.astype(BF16), preferred_element_type=F32) + run
            tie_sel = jnp.where(pref <= need, tie, 0.0)
            sel = jnp.where(s > thr, 1.0, tie_sel)
            s_scr[rows(i), :] = jnp.where(sel > 0.5, 0.0, NEG)
            return run + jnp.sum(tie, axis=0, keepdims=True)
        lax.fori_loop(0, nch, mask_chunk, jnp.zeros((1, LANES), F32))

    def logits(i, with_bias):
        msk = s_scr[rows(i), :]
        out = []
        for pr in range(DSA_HEADS // 2):
            l2 = jnp.dot(k_ref[0, rows(i), pr * LANES:(pr + 1) * LANES],
                         qpad_scr[:, 2 * pr * LANES:(2 * pr + 2) * LANES],
                         preferred_element_type=F32)
            for e in range(2):
                l = l2[:, e * LANES:(e + 1) * LANES] + msk
                if with_bias:
                    far_rows = kc - 2 * Q_BLOCK
                    l = jnp.concatenate([l[:far_rows], l[far_rows:] + bias_scr[2 * pr + e]], axis=0)
                out.append(l)
        return out

    def max_step(i, ms, with_bias):
        return tuple(jnp.maximum(m, _tree(jnp.maximum, l)) for m, l in zip(ms, logits(i, with_bias)))

    @pl.when(jnp.logical_not(bound_ok))
    def _():
        ms = max_step(0, tuple(jnp.full((8, LANES), NEG, F32) for _ in range(DSA_HEADS)), True)
        ms = lax.fori_loop(1, nch, lambda i, m: max_step(i, m, False), ms)
        for h in range(DSA_HEADS):
            mx_scr[h:h + 1, :] = jnp.max(ms[h], axis=0, keepdims=True)

    mx = [mx_scr[h:h + 1, :] for h in range(DSA_HEADS)]

    def pv_step(i, accs, with_bias):
        return tuple(
            acc + jnp.dot(vt_ref[0, h, :, rows(i)], jnp.exp2(l - mx[h]).astype(BF16),
                          preferred_element_type=F32)
            for h, (acc, l) in enumerate(zip(accs, logits(i, with_bias))))

    accs = pv_step(0, tuple(jnp.zeros((V_ROWS, LANES), F32) for _ in range(DSA_HEADS)), True)
    accs = chunk_pairs(1, lambda i, a: pv_step(i, a, False), accs, width=4)
    outs = [acc[:HEAD_DIM] / acc[HEAD_DIM:HEAD_DIM + 1] for acc in accs]
    o_ref[...] = jnp.concatenate(outs, axis=0).T


def _dsa(q, iq, kw, k, vt, ikb, rel_bias, q_gain, k_gain, ltri, *, batch, seq, kc):
    nb = seq // Q_BLOCK
    top_k = min(TOPK_MAX, seq // 4)
    pad = kc - Q_BLOCK
    seqp = seq + pad
    kp = jnp.pad(k.reshape(batch, seq, DSA_W), ((0, 0), (pad, 0), (0, 0)))
    ikp = jnp.pad(ikb.reshape(batch, seq, LANES), ((0, 0), (pad, 0), (0, 0)))
    vtp = jnp.pad(vt, ((0, 0), (0, 0), (0, 0), (pad, 0)))
    kmax = (HEAD_DIM ** 0.5 * 1.01) * jnp.max(jnp.abs(k_gain.astype(F32)))
    qmax = (LOG2E * 1.01) * jnp.max(jnp.abs(q_gain.astype(F32)))
    norms = jnp.stack([kmax, qmax] + [jnp.zeros((), F32)] * (DSA_HEADS - 2)).reshape(1, DSA_HEADS)
    table = jnp.concatenate([rel_bias.astype(F32), norms], axis=0)
    qrow = lambda w: pl.BlockSpec((Q_BLOCK, w), lambda b, j: (b * nb + j, 0))
    return pl.pallas_call(
        functools.partial(_dsa_kernel, kc=kc, top_k=top_k),
        grid=(batch, nb),
        in_specs=[
            qrow(DSA_W), qrow(IDX_HEADS * IDX_DIM), qrow(LANES),
            pl.BlockSpec((1, seqp, DSA_W), lambda b, j: (b, 0, 0)),
            pl.BlockSpec((1, DSA_HEADS, V_ROWS, seqp), lambda b, j: (b, 0, 0, 0)),
            pl.BlockSpec((1, seqp, LANES), lambda b, j: (b, 0, 0)),
            pl.BlockSpec(memory_space=pltpu.SMEM),
            pl.BlockSpec(ltri.shape, lambda b, j: (0, 0)),
        ],
        out_specs=qrow(DSA_W),
        out_shape=jax.ShapeDtypeStruct((batch * seq, DSA_W), F32),
        scratch_shapes=[
            pltpu.VMEM((seqp, LANES), F32),
            pltpu.VMEM((LANES, IDX_HEADS * LANES), BF16),
            pltpu.VMEM((LANES, DSA_HEADS * LANES), BF16),
            pltpu.VMEM((DSA_HEADS, 2 * Q_BLOCK, LANES), F32),
            pltpu.VMEM((8, LANES), F32),
            pltpu.SMEM((2, DSA_HEADS), F32),
        ],
        compiler_params=_params("arbitrary", "arbitrary"),
        name="dsa",
    )(q, iq, kw, kp, vtp, ikp, table, ltri)


GDN_GROUP = 2


def _gdn_kernel(x_ref, ab_ref, cw_ref, ea_ref, eb_ref, alog_ref, dtb_ref, gain_ref, grp_ref,
                o_ref, xpad_scr, qkv_scr, gb_scr, gc_scr, gr_scr, pre_scr, st_scr, o_scr, *, tb, nbatch):
    t = pl.program_id(0)
    cin = 3 * GDN_W
    n_chunks = tb // GDN_CHUNK
    pairs = range(GDN_HEADS // 2)
    lanes = lambda p: slice(p * LANES, (p + 1) * LANES)

    @pl.when(t == 0)
    def _():
        for b in range(nbatch):
            xpad_scr[b * (tb + 8):b * (tb + 8) + 8, :] = jnp.zeros((8, cin), F32)
        st_scr[...] = jnp.zeros_like(st_scr)

    grp = grp_ref[...]
    for b in range(nbatch):
        r0, x0 = b * tb, b * (tb + 8)
        x = x_ref[b]
        xpad_scr[x0 + 8:x0 + 8 + tb, :] = x[:, :cin]
        conv = jnp.zeros((tb, cin), F32)
        for jj in range(CONV_WIDTH):
            conv = conv + cw_ref[jj:jj + 1, :] * xpad_scr[pl.ds(x0 + 8 - (CONV_WIDTH - 1) + jj, tb), :]
        xpad_scr[x0:x0 + 8, :] = x[tb - 8:tb, :cin]
        qkv = conv * _sigmoid(conv)
        q = qkv[:, :GDN_W]
        k = qkv[:, GDN_W:2 * GDN_W]
        qkv_scr[r0:r0 + tb, :GDN_W] = q * lax.rsqrt(_group_sum(q * q, grp) + EPS) * (HEAD_DIM ** -0.5)
        qkv_scr[r0:r0 + tb, GDN_W:2 * GDN_W] = k * lax.rsqrt(_group_sum(k * k, grp) + EPS)
        qkv_scr[r0:r0 + tb, 2 * GDN_W:] = qkv[:, 2 * GDN_W:]

        ab = ab_ref[b]
        a_e = _dot_sel(ab, ea_ref[...], 3) + dtb_ref[...]
        b_e = _dot_sel(ab, eb_ref[...], 3)
        softplus = jnp.maximum(a_e, 0.0) + jnp.log(1.0 + jnp.exp(-jnp.abs(a_e)))
        gb_scr[r0:r0 + tb, :GDN_W] = -jnp.exp(alog_ref[...]) * softplus
        gb_scr[r0:r0 + tb, GDN_W:] = _sigmoid(b_e)

    r64 = lax.broadcasted_iota(jnp.int32, (GDN_CHUNK, LANES), 0)
    c64 = lax.broadcasted_iota(jnp.int32, (GDN_CHUNK, LANES), 1) % GDN_CHUNK
    causal = c64 <= r64
    strict = c64 < r64
    eye = jnp.where(c64 == r64, 1.0, 0.0)
    r128 = lax.broadcasted_iota(jnp.int32, (LANES, LANES), 0)
    c128 = lax.broadcasted_iota(jnp.int32, (LANES, LANES), 1)
    bdmask = (r128 // HEAD_DIM) == (c128 // HEAD_DIM)

    def bd(m):
        return jnp.where(bdmask, jnp.concatenate([m, m], axis=0), 0.0)

    lt_r = lax.broadcasted_iota(jnp.int32, (GDN_CHUNK, GDN_CHUNK), 0)
    lt_c = lax.broadcasted_iota(jnp.int32, (GDN_CHUNK, GDN_CHUNK), 1)
    ltri = jnp.where(lt_c <= lt_r, 1.0, 0.0)
    ones8 = jnp.ones((8, GDN_CHUNK), F32)
    up_r = lax.broadcasted_iota(jnp.int32, (GDN_CHUNK, GDN_W), 0)
    up_c = lax.broadcasted_iota(jnp.int32, (GDN_CHUNK, GDN_W), 1) % GDN_CHUNK
    upper = jnp.where(up_r <= up_c, 1.0, 0.0)
    for ci in range(nbatch * n_chunks):
        g = gb_scr[ci * GDN_CHUNK:(ci + 1) * GDN_CHUNK, :GDN_W]
        gc_scr[ci * GDN_CHUNK:(ci + 1) * GDN_CHUNK, :] = _sel_dot(ltri, g, 3)
        gr_scr[ci * 8:(ci + 1) * 8, :] = _sel_dot(ones8, g * upper, 3)

    def par_body(gi, carry):
        inst = [(u, p) for u in range(GDN_GROUP) for p in pairs]
        rows = [pl.ds(pl.multiple_of((gi * GDN_GROUP + u) * GDN_CHUNK, GDN_CHUNK), GDN_CHUNK)
                for u in range(GDN_GROUP)]
        grow = [pl.ds(pl.multiple_of((gi * GDN_GROUP + u) * 8, 8), 8) for u in range(GDN_GROUP)]
        off = lambda s, p: slice(s * GDN_W + p * LANES, s * GDN_W + (p + 1) * LANES)
        qp = [qkv_scr[rows[u], off(0, p)] for u, p in inst]
        kp = [qkv_scr[rows[u], off(1, p)] for u, p in inst]
        vp = [qkv_scr[rows[u], off(2, p)] for u, p in inst]
        beta = [gb_scr[rows[u], off(1, p)] for u, p in inst]
        gc = [gc_scr[rows[u], lanes(p)] for u, p in inst]
        gr = [gr_scr[grow[u], lanes(p)][0:1, :] for u, p in inst]
        dmat = [jnp.exp(jnp.where(causal, a - b, NEG)) for a, b in zip(gc, gr)]
        kbd = [jnp.where(bdmask, jnp.concatenate([a, a], axis=0).T, 0.0) for a in kp]
        kb = [a * b for a, b in zip(kp, beta)]
        kk = [_bdot(a, b) for a, b in zip(kb, kbd)]
        qk = [_bdot(a, b) for a, b in zip(qp, kbd)]
        lm = [jnp.where(strict, a * d, 0.0) for a, d in zip(kk, dmat)]
        aintra = [a * d for a, d in zip(qk, dmat)]
        tinv = [eye - a for a in lm]
        lpow = lm
        for _ in range(5):
            lpow = [_dot3(a, bd(a)) for a in lpow]
            tinv = [a + _dot3(a, bd(b)) for a, b in zip(tinv, lpow)]
        egc = [jnp.exp(a) for a in gc]
        un = [_dot3(a, bd(v * b)) for a, v, b in zip(tinv, vp, beta)]
        wn = [_dot3(a, bd(b * e)) for a, b, e in zip(tinv, kb, egc)]
        for n, (u, p) in enumerate(inst):
            pre_scr[0, rows[u], lanes(p)] = un[n]
            pre_scr[1, rows[u], lanes(p)] = wn[n]
            pre_scr[2, rows[u], lanes(p)] = aintra[n]
            pre_scr[3, rows[u], lanes(p)] = qp[n] * egc[n]
            pre_scr[4, rows[u], lanes(p)] = kp[n] * jnp.exp(gc[n][GDN_CHUNK - 1:GDN_CHUNK, :] - gc[n])
        return carry

    lax.fori_loop(0, nbatch * n_chunks // GDN_GROUP, par_body, 0)

    chains = [(b, p) for b in range(nbatch) for p in pairs]

    def rec_body(ci, carry):
        rows = [pl.ds(pl.multiple_of(b * tb + ci * GDN_CHUNK, GDN_CHUNK), GDN_CHUNK) for b in range(nbatch)]
        egl = [jnp.exp(gc_scr[pl.ds(pl.multiple_of(b * tb + ci * GDN_CHUNK + GDN_CHUNK - 8, 8), 8), :][7:8, :])
               for b in range(nbatch)]
        sbd = [st_scr[b * len(pairs) + p] for b, p in chains]
        un, wn, aintra, qe, kd = [[pre_scr[s, rows[b], lanes(p)] for b, p in chains] for s in range(5)]
        ws = [_bdot(a, s) for a, s in zip(wn, sbd)]
        qs = [_bdot(a, s) for a, s in zip(qe, sbd)]
        v_new = [a - b_ for a, b_ in zip(un, ws)]
        o = [a + _bdot(b_, bd(v)) for a, b_, v in zip(qs, aintra, v_new)]
        upd = [_bdot(a.T, v) for a, v in zip(kd, v_new)]
        for n, (b, p) in enumerate(chains):
            st_scr[b * len(pairs) + p] = sbd[n] * egl[b][:, lanes(p)] + jnp.where(bdmask, upd[n], 0.0)
            o_scr[rows[b], lanes(p)] = o[n]
        return carry

    lax.fori_loop(0, n_chunks, rec_body, 0)

    for b in range(nbatch):
        o = o_scr[b * tb:(b + 1) * tb, :]
        oms = _group_sum(o * o, grp) * (1.0 / HEAD_DIM)
        z = x_ref[b][:, cin:]
        o_ref[b] = o * lax.rsqrt(oms + EPS) * gain_ref[...] * (z * _sigmoid(z))


def _gdn(gx, ab, conv_w, ea, eb, alog, dtb, gain, grp, *, batch, seq, tb):
    nt = seq // tb
    cin = 3 * GDN_W
    row = lambda w: pl.BlockSpec((batch, tb, w), lambda t: (0, t, 0))
    full = lambda a: pl.BlockSpec(a.shape, lambda t: (0,) * a.ndim)
    rows = batch * tb
    out = pl.pallas_call(
        functools.partial(_gdn_kernel, tb=tb, nbatch=batch),
        grid=(nt,),
        in_specs=[row(4 * GDN_W), row(LANES), full(conv_w), full(ea), full(eb), full(alog),
                  full(dtb), full(gain), full(grp)],
        out_specs=row(GDN_W),
        out_shape=jax.ShapeDtypeStruct((batch, seq, GDN_W), F32),
        scratch_shapes=[
            pltpu.VMEM((batch * (tb + 8), cin), F32),
            pltpu.VMEM((rows, cin), F32),
            pltpu.VMEM((rows, 2 * GDN_W), F32),
            pltpu.VMEM((rows, GDN_W), F32),
            pltpu.VMEM((rows // 8, GDN_W), F32),
            pltpu.VMEM((5, rows, GDN_W), F32),
            pltpu.VMEM((batch * (GDN_HEADS // 2), LANES, LANES), F32),
            pltpu.VMEM((rows, GDN_W), F32),
        ],
        compiler_params=_params("arbitrary"),
        name="gdn",
    )(gx.reshape(batch, seq, 4 * GDN_W), ab.reshape(batch, seq, LANES), conv_w, ea, eb, alog, dtb, gain, grp)
    return out.reshape(batch * seq, GDN_W)


def _memkv_kernel(mem_ref, g_ref, w_ref, kg_ref, grp_ref, kt_ref, vp_ref):
    x = mem_ref[0]
    ms = jnp.mean(x * x, axis=-1, keepdims=True)
    h = (x * lax.rsqrt(ms + EPS) * g_ref[...]).astype(BF16)
    kv = jnp.dot(h, w_ref[...], preferred_element_type=F32)
    km = kv[:, :MEM_W]
    vm = kv[:, MEM_W:]
    kms = _group_sum(km * km, grp_ref[...]) * (1.0 / HEAD_DIM)
    kt = (km * lax.rsqrt(kms + EPS) * kg_ref[...]).T
    n_mem = x.shape[0]
    top = lax.broadcasted_iota(jnp.int32, (LANES, n_mem), 0) < HEAD_DIM
    left = lax.broadcasted_iota(jnp.int32, (n_mem, LANES), 1) < HEAD_DIM
    for hh in range(MEM_HEADS):
        pr = slice((hh // 2) * LANES, (hh // 2 + 1) * LANES)
        keep_r = top if hh % 2 == 0 else jnp.logical_not(top)
        keep_c = left if hh % 2 == 0 else jnp.logical_not(left)
        kt_ref[0, hh] = jnp.where(keep_r, kt[pr, :], 0.0).astype(BF16)
        vp_ref[0, hh] = jnp.where(keep_c, vm[:, pr], 0.0).astype(BF16)


def _memkv(mem, gain, w, kg, grp):
    b, n_mem, d = mem.shape
    full = lambda a: pl.BlockSpec(a.shape, lambda i: (0,) * a.ndim)
    return pl.pallas_call(
        _memkv_kernel,
        grid=(b,),
        in_specs=[pl.BlockSpec((1, n_mem, d), lambda i: (i, 0, 0)), full(gain), full(w), full(kg),
                  full(grp)],
        out_specs=(pl.BlockSpec((1, MEM_HEADS, LANES, n_mem), lambda i: (i, 0, 0, 0)),
                   pl.BlockSpec((1, MEM_HEADS, n_mem, LANES), lambda i: (i, 0, 0, 0))),
        out_shape=(jax.ShapeDtypeStruct((b, MEM_HEADS, LANES, n_mem), BF16),
                   jax.ShapeDtypeStruct((b, MEM_HEADS, n_mem, LANES), BF16)),
        compiler_params=_params("parallel"),
        name="memkv",
    )(mem, gain, w, kg, grp)


def _memattn_kernel(q_ref, qg_ref, grp_ref, kt_ref, vp_ref, o_ref):
    q = q_ref[...]
    qms = _group_sum(q * q, grp_ref[...]) * (1.0 / HEAD_DIM)
    qn = (q * lax.rsqrt(qms + EPS) * qg_ref[...] * (HEAD_DIM ** -0.5)).astype(BF16)
    outs = []
    for pr in range(MEM_HEADS // 2):
        qp = qn[:, pr * LANES:(pr + 1) * LANES]
        acc = None
        for e in range(2):
            hh = 2 * pr + e
            l = jnp.dot(qp, kt_ref[0, hh], preferred_element_type=F32)
            l = l - jnp.max(l, axis=-1, keepdims=True)
            p = jnp.exp(l)
            p = p / jnp.sum(p, axis=-1, keepdims=True)
            o = jnp.dot(p.astype(BF16), vp_ref[0, hh], preferred_element_type=F32)
            acc = o if acc is None else acc + o
        outs.append(acc)
    o_ref[...] = jnp.concatenate(outs, axis=-1)


def _memattn(mq, qg, grp, kt, vp, *, batch, seq, tm):
    nt = seq // tm
    n_mem = kt.shape[-1]
    full = lambda a: pl.BlockSpec(a.shape, lambda b, t: (0,) * a.ndim)
    return pl.pallas_call(
        _memattn_kernel,
        grid=(batch, nt),
        in_specs=[
            pl.BlockSpec((tm, MEM_W), lambda b, t: (b * nt + t, 0)), full(qg), full(grp),
            pl.BlockSpec((1, MEM_HEADS, LANES, n_mem), lambda b, t: (b, 0, 0, 0)),
            pl.BlockSpec((1, MEM_HEADS, n_mem, LANES), lambda b, t: (b, 0, 0, 0)),
        ],
        out_specs=pl.BlockSpec((tm, MEM_W), lambda b, t: (b * nt + t, 0)),
        out_shape=jax.ShapeDtypeStruct((batch * seq, MEM_W), F32),
        compiler_params=_params("parallel", "parallel"),
        name="memattn",
    )(mq, qg, grp, kt, vp)


def _group_ones(width):
    idx = np.arange(width) // HEAD_DIM
    return jnp.asarray((idx[:, None] == idx[None, :]).astype(np.float32))


def _t5_bucket_np(n):
    max_exact = NUM_BUCKETS // 2
    nf = np.maximum(n, 1).astype(np.float32)
    large = max_exact + (np.log(nf / np.float32(max_exact)) / np.float32(math.log(MAX_DISTANCE / max_exact))
                         * (NUM_BUCKETS - max_exact)).astype(np.int32)
    large = np.minimum(large, NUM_BUCKETS - 1)
    return np.where(n < max_exact, n, large)


def _bucket_starts():
    dist = np.arange(2 * Q_BLOCK)
    bucket = _t5_bucket_np(dist)
    assert (np.diff(bucket) >= 0).all() and bucket[-1] == NUM_BUCKETS - 1
    return tuple(int(dist[bucket >= b].min()) for b in range(1, NUM_BUCKETS))


_BUCKET_STARTS = _bucket_starts()


def _cast_kernel(w_ref, o_ref):
    o_ref[...] = w_ref[...].astype(BF16)


def _to_bf16(w, *, rows):
    nl, nr, nc = w.shape
    spec = pl.BlockSpec((1, rows, nc), lambda l, r: (l, r, 0))
    return pl.pallas_call(
        _cast_kernel,
        grid=(nl, nr // rows),
        in_specs=[spec],
        out_specs=spec,
        out_shape=jax.ShapeDtypeStruct(w.shape, BF16),
        compiler_params=_params("parallel", "parallel"),
        name="cast",
    )(w)


def _pack_w_in(w):
    o = 3 * DSA_W
    iq = w[:, o:o + IDX_HEADS * IDX_DIM]
    o += IDX_HEADS * IDX_DIM
    kw = w[:, o:o + IDX_DIM + IDX_HEADS]
    o += IDX_DIM + IDX_HEADS
    g = w[:, o:o + 4 * GDN_W]
    o += 4 * GDN_W
    ab = w[:, o:o + 2 * GDN_HEADS]
    o += 2 * GDN_HEADS
    mq = w[:, o:o + MEM_W]
    pad = lambda a: jnp.pad(a, ((0, 0), (0, LANES - a.shape[1])))
    return jnp.concatenate([w[:, :3 * DSA_W], iq, pad(kw), g, pad(ab), mq], axis=1)


def _tile_heads(v, heads):
    return jnp.tile(v.astype(F32), heads).reshape(1, heads * HEAD_DIM)


def kernel(x, mem, ffn1_norm, ffn1_w_gate, ffn1_w_up, ffn1_w_down, mix_norm, w_in, dsa_q_norm, dsa_k_norm, rel_bias, gdn_conv, gdn_A_log, gdn_dt_bias, gdn_out_norm, mem_norm, w_mem_kv, mem_q_norm, mem_k_norm, w_out, ffn2_norm, ffn2_w_gate, ffn2_w_up, ffn2_w_down):
    batch, seq, d = x.shape
    depth = w_in.shape[0]
    m = batch * seq
    tm = min(512, seq)
    kc = min(512, seq)
    dff = ffn1_w_gate.shape[-1]
    tf = dff // 2 if (dff // 2) % LANES == 0 else dff

    grp_a = _group_ones(DSA_W)
    grp_m = _group_ones(MEM_W)
    ltri = jnp.asarray(np.tril(np.ones((kc, kc), np.float32))).astype(BF16)
    heads_of = np.arange(GDN_W) // HEAD_DIM
    ea = jnp.asarray((np.arange(LANES)[:, None] == heads_of[None, :]).astype(np.float32))
    eb = jnp.asarray((np.arange(LANES)[:, None] == heads_of[None, :] + GDN_HEADS).astype(np.float32))

    w_in_b = _to_bf16(w_in, rows=min(256, d))
    xf = x.reshape(m, d)
    for l in range(depth):
        xf = _ffn(xf, ffn1_norm[l], ffn1_w_gate[l].astype(BF16), ffn1_w_up[l].astype(BF16),
                  ffn1_w_down[l].astype(BF16), tm=tm, tf=tf)
        (q, k, vt, iq, ikb, kw, gx, ab, mq) = _inproj(
            xf, mix_norm[l].reshape(1, d), _pack_w_in(w_in_b[l]), grp_a,
            _tile_heads(dsa_q_norm[l], DSA_HEADS), _tile_heads(dsa_k_norm[l], DSA_HEADS),
            batch=batch, tm=tm)
        out_a = _dsa(q, iq, kw, k, vt, ikb, rel_bias, dsa_q_norm[l], dsa_k_norm[l], ltri,
                     batch=batch, seq=seq, kc=kc)
        out_b = _gdn(gx, ab, gdn_conv[l], ea, eb,
                     jnp.repeat(gdn_A_log[l].astype(F32), HEAD_DIM).reshape(1, GDN_W),
                     jnp.repeat(gdn_dt_bias[l].astype(F32), HEAD_DIM).reshape(1, GDN_W),
                     _tile_heads(gdn_out_norm[l], GDN_HEADS), grp_a, batch=batch, seq=seq, tb=tm)
        kt, vp = _memkv(mem, mem_norm[l].reshape(1, d), w_mem_kv[l].astype(BF16),
                        _tile_heads(mem_k_norm[l], MEM_HEADS), grp_m)
        out_c = _memattn(mq, _tile_heads(mem_q_norm[l], MEM_HEADS), grp_m, kt, vp,
                         batch=batch, seq=seq, tm=tm)
        wo = w_out[l].astype(BF16)
        xf = _ffn(xf, ffn2_norm[l], ffn2_w_gate[l].astype(BF16), ffn2_w_up[l].astype(BF16),
                  ffn2_w_down[l].astype(BF16), mix=(out_a, out_b, out_c),
                  mix_w=(wo[:DSA_W], wo[DSA_W:DSA_W + GDN_W], wo[DSA_W + GDN_W:]), tm=tm, tf=tf)
    return xf.reshape(batch, seq, d)
```
